```python
import jax
import jax.numpy as jnp
from jax import lax
import numpy as np

D_MODEL = 2048
BATCH = 16
SEQ = 256
DEPTH = 4
DEC_BATCH = 8
DEC_SEQ = 1024
PAST_LEN = 512

GRID_W = 64
N_EVEN = (DEPTH + 1) // 2
N_ODD = DEPTH // 2
N_MOD = 9
D_FF = 5632
EPS = 1e-6
Q_BLOCK = 128
ROPE_THETA = 10000.0

LRU_W = D_MODEL // 2
LRU_BLOCKS = 16
LRU_BW = LRU_W // LRU_BLOCKS
CONV_W = 4
LRU_C = 8.0
NA_HEADS = 16
NA_DH = 64
NA_W = NA_HEADS * NA_DH
NA_KR = 8
NA_KC = 16
GQA_HEADS = 16
GQA_KV_HEADS = 4
GQA_DH = 64
GQA_GROUP = GQA_HEADS // GQA_KV_HEADS
GQA_Q_W = GQA_HEADS * GQA_DH
GQA_KV_W = GQA_KV_HEADS * GQA_DH
MLA_HEADS = 8
MLA_Q_RANK = 512
MLA_KV_RANK = 512
MLA_NOPE = 128
MLA_ROPE = 64
MLA_V = 128
MLA_QK = MLA_NOPE + MLA_ROPE

AB_IN = 2 * LRU_W + 3 * NA_W
AB_OUT = LRU_W + NA_W
CD_IN = GQA_Q_W + 2 * GQA_KV_W + MLA_Q_RANK + MLA_KV_RANK + MLA_ROPE
CD_OUT = GQA_Q_W + MLA_HEADS * MLA_V

kernel_name = 'hybrid_diffusion_trunk_ctx_and_denoise_step'


def rmsnorm(x, gain):
    xf = x.astype(jnp.float32)
    y = xf * lax.rsqrt(jnp.mean(xf * xf, axis=-1, keepdims=True) + EPS)
    return (y * gain.astype(jnp.float32)).astype(x.dtype)


def adaln(x, gain, shift, scale):
    return rmsnorm(x, gain) * (1 + scale) + shift


def modulation(cond, w_mod, b_mod):
    m = jax.nn.silu(cond) @ w_mod + b_mod
    m = m.reshape(cond.shape[0], 1, N_MOD, D_MODEL)
    return tuple(m[:, :, j] for j in range(N_MOD))


def swiglu(h, w_in, w_out):
    g, u = jnp.split(h @ w_in, 2, axis=-1)
    return (jax.nn.silu(g) * u) @ w_out


def axial_rope(x):
    seq_len, dim = x.shape[1], x.shape[-1]
    half = dim // 2
    nf = half // 2
    t = jnp.arange(seq_len)
    inv_freq = 1.0 / (ROPE_THETA ** (jnp.arange(nf, dtype=jnp.float32) / nf))

    def rotate(xs, pos):
        ang = pos.astype(jnp.float32)[:, None] * inv_freq[None, :]
        cos = jnp.cos(ang)[None, :, None, :]
        sin = jnp.sin(ang)[None, :, None, :]
        xf = xs.astype(jnp.float32)
        x1, x2 = xf[..., :nf], xf[..., nf:]
        return jnp.concatenate([x1 * cos - x2 * sin, x2 * cos + x1 * sin], axis=-1)

    out = jnp.concatenate([rotate(x[..., :half], t // GRID_W), rotate(x[..., half:], t % GRID_W)], axis=-1)
    return out.astype(x.dtype)


def block_attention(q, k, v, scale):
    b, lq, hk, g, dq = q.shape
    nb = lq // Q_BLOCK
    qb = q.reshape(b, nb, Q_BLOCK, hk, g, dq).transpose(1, 0, 2, 3, 4, 5)

    def one_block(qblk):
        s = jnp.einsum('bqhgd,bkhd->bhgqk', qblk, k).astype(jnp.float32) * scale
        p = jax.nn.softmax(s, axis=-1).astype(v.dtype)
        return jnp.einsum('bhgqk,bkhd->bqhgd', p, v)

    o = lax.map(one_block, qb)
    return o.transpose(1, 0, 2, 3, 4, 5).reshape(b, lq, hk * g * v.shape[-1])


def centred_dwconv(x, w, b):
    seq_len = x.shape[1]
    left = CONV_W // 2
    xp = jnp.pad(x, ((0, 0), (left, CONV_W - 1 - left), (0, 0)))
    y = b
    for j in range(CONV_W):
        y = y + xp[:, j:j + seq_len] * w[j]
    return y


def rglru(xc, h0, wa, ba, wx, bx, lam, reverse):
    b, l, _ = xc.shape
    xb = xc.reshape(b, l, LRU_BLOCKS, LRU_BW)
    r = jax.nn.sigmoid(jnp.einsum('blni,nij->blnj', xb, wa).reshape(b, l, LRU_W) + ba)
    gi = jax.nn.sigmoid(jnp.einsum('blni,nij->blnj', xb, wx).reshape(b, l, LRU_W) + bx)
    log_a = -LRU_C * r.astype(jnp.float32) * jax.nn.softplus(-lam.astype(jnp.float32))
    a = jnp.exp(log_a)
    u = jnp.sqrt(-jnp.expm1(2.0 * log_a)) * (gi * xc).astype(jnp.float32)

    def combine(left, right):
        return (left[0] * right[0], right[0] * left[1] + right[1])

    a_cum, u_cum = lax.associative_scan(combine, (a, u), axis=1, reverse=reverse)
    h = a_cum * h0.astype(jnp.float32)[:, None, :] + u_cum
    last = h[:, 0] if reverse else h[:, -1]
    return h.astype(xc.dtype), last.astype(xc.dtype)


def lru_mixer(xa, ga, h0f, h0b, lp):
    xc = centred_dwconv(xa, lp['conv_w'], lp['conv_b'])
    hf, lf = rglru(xc, h0f, lp['wa'][0], lp['ba'][0], lp['wx'][0], lp['bx'][0], lp['lam'][0], False)
    hb, lb = rglru(xc, h0b, lp['wa'][1], lp['ba'][1], lp['wx'][1], lp['bx'][1], lp['lam'][1], True)
    return (hf + hb) * jax.nn.gelu(ga), lf, lb


def neighbourhood_attention(q, k, v, k_ctx, v_ctx, bias_tab):
    b, l, h, dh = q.shape
    rows_n = l // GRID_W
    kr = min(NA_KR, rows_n)
    scale = dh ** -0.5
    rows = jnp.arange(rows_n)
    r_start = jnp.clip(rows - kr // 2, 0, rows_n - kr)
    row_idx = r_start[:, None] + jnp.arange(kr)[None, :]
    qg = q.reshape(b, rows_n, GRID_W, h, dh)
    kg = k.reshape(b, rows_n, GRID_W, h, dh)[:, row_idx]
    vg = v.reshape(b, rows_n, GRID_W, h, dh)[:, row_idx]
    cols = jnp.arange(GRID_W)
    c_start = jnp.clip(cols - NA_KC // 2, 0, GRID_W - NA_KC)
    col_ok = (cols[None, :] >= c_start[:, None]) & (cols[None, :] < c_start[:, None] + NA_KC)
    rel_r = row_idx - rows[:, None] + (NA_KR - 1)
    rel_c = jnp.clip(cols[None, :] - cols[:, None] + (NA_KC - 1), 0, 2 * NA_KC - 2)
    bias = bias_tab[:, rel_r[:, None, :, None], rel_c[None, :, None, :]].astype(jnp.float32)
    s_loc = jnp.einsum('brqhd,brikhd->bhrqik', qg, kg).astype(jnp.float32) * scale + bias[None]
    s_loc = jnp.where(col_ok[None, None, None, :, None, :], s_loc, -jnp.inf)
    s_loc = s_loc.reshape(b, h, rows_n, GRID_W, kr * GRID_W)
    s_ctx = jnp.einsum('brqhd,bchd->bhrqc', qg, k_ctx).astype(jnp.float32) * scale
    p = jax.nn.softmax(jnp.concatenate([s_loc, s_ctx], axis=-1), axis=-1).astype(v.dtype)
    p_loc = p[..., :kr * GRID_W].reshape(b, h, rows_n, GRID_W, kr, GRID_W)
    p_ctx = p[..., kr * GRID_W:]
    o = jnp.einsum('bhrqik,brikhd->brqhd', p_loc, vg) + jnp.einsum('bhrqc,bchd->brqhd', p_ctx, v_ctx)
    return o.reshape(b, l, h * dh)


def ab_split(h, w_in):
    b, l, _ = h.shape
    xa, ga, q, k, v = jnp.split(h @ w_in, [LRU_W, 2 * LRU_W, 2 * LRU_W + NA_W, 2 * LRU_W + 2 * NA_W], axis=-1)
    return (xa, ga, q.reshape(b, l, NA_HEADS, NA_DH), k.reshape(b, l, NA_HEADS, NA_DH),
            v.reshape(b, l, NA_HEADS, NA_DH))


def ab_context(h, lp):
    xa, ga, q, k, v = ab_split(h, lp['w_in'])
    zeros = jnp.zeros((h.shape[0], LRU_W), h.dtype)
    ya, sf, sb = lru_mixer(xa, ga, zeros, zeros, lp)
    yb = block_attention(q[:, :, :, None], k, v, NA_DH ** -0.5)
    y = jnp.concatenate([ya, yb], axis=-1) @ lp['w_out']
    return y, (sf, sb, k, v)


def ab_latent(h, lp, sf, sb, k_ctx, v_ctx):
    xa, ga, q, k, v = ab_split(h, lp['w_in'])
    ya, _, _ = lru_mixer(xa, ga, sf, sb, lp)
    yb = neighbourhood_attention(q, k, v, k_ctx, v_ctx, lp['na_bias'])
    return jnp.concatenate([ya, yb], axis=-1) @ lp['w_out'], None


def cd_split(h, lp):
    b, l, _ = h.shape
    idx = [GQA_Q_W, GQA_Q_W + GQA_KV_W, GQA_Q_W + 2 * GQA_KV_W, GQA_Q_W + 2 * GQA_KV_W + MLA_Q_RANK,
           GQA_Q_W + 2 * GQA_KV_W + MLA_Q_RANK + MLA_KV_RANK]
    qc, kc, vc, qa, ckv, kr = jnp.split(h @ lp['w_in'], idx, axis=-1)
    qc = rmsnorm(qc.reshape(b, l, GQA_HEADS, GQA_DH), lp['q_gain'])
    kc = rmsnorm(kc.reshape(b, l, GQA_KV_HEADS, GQA_DH), lp['k_gain'])
    vc = vc.reshape(b, l, GQA_KV_HEADS, GQA_DH)
    qd = (rmsnorm(qa, lp['mla_q_gain']) @ lp['w_uq']).reshape(b, l, MLA_HEADS, MLA_QK)
    ckv = rmsnorm(ckv, lp['mla_kv_gain'])
    return qc, kc, vc, qd, ckv, kr


def mla_kv(ckv, kr, lp):
    b, l, _ = ckv.shape
    k_nope = (ckv @ lp['w_uk']).reshape(b, l, MLA_HEADS, MLA_NOPE)
    v = (ckv @ lp['w_uv']).reshape(b, l, MLA_HEADS, MLA_V)
    k = jnp.concatenate([k_nope, jnp.broadcast_to(kr[:, :, None, :], (b, l, MLA_HEADS, MLA_ROPE))], axis=-1)
    return k, v


def cd_context(h, lp):
    qc, kc, vc, qd, ckv, kr = cd_split(h, lp)
    b, l = h.shape[0], h.shape[1]
    yc = block_attention(qc.reshape(b, l, GQA_KV_HEADS, GQA_GROUP, GQA_DH), kc, vc, GQA_DH ** -0.5)
    kd, vd = mla_kv(ckv, kr, lp)
    yd = block_attention(qd[:, :, :, None], kd, vd, MLA_QK ** -0.5)
    y = jnp.concatenate([yc, yd], axis=-1) @ lp['w_out']
    return y, (kc, vc, ckv, kr)


def cd_latent(h, lp, kc_ctx, vc_ctx, ckv_ctx, kr_ctx):
    qc, kc, vc, qd, ckv, kr = cd_split(h, lp)
    b, l = h.shape[0], h.shape[1]
    qc = axial_rope(qc)
    kc = axial_rope(kc)
    qd = jnp.concatenate([qd[..., :MLA_NOPE], axial_rope(qd[..., MLA_NOPE:])], axis=-1)
    kr = axial_rope(kr[:, :, None, :])[:, :, 0]
    k_all = jnp.concatenate([kc, kc_ctx], axis=1)
    v_all = jnp.concatenate([vc, vc_ctx], axis=1)
    yc = block_attention(qc.reshape(b, l, GQA_KV_HEADS, GQA_GROUP, GQA_DH), k_all, v_all, GQA_DH ** -0.5)
    kd, vd = mla_kv(jnp.concatenate([ckv, ckv_ctx], axis=1), jnp.concatenate([kr, kr_ctx], axis=1), lp)
    yd = block_attention(qd[:, :, :, None], kd, vd, MLA_QK ** -0.5)
    return jnp.concatenate([yc, yd], axis=-1) @ lp['w_out'], None


def trunk_layer(x, mods, gains, ffn_in, ffn_out, mixer):
    sh1, sc1, g1, sh2, sc2, g2, sh3, sc3, g3 = mods
    x = x + 0.5 * g1 * swiglu(adaln(x, gains[0], sh1, sc1), ffn_in[0], ffn_out[0])
    y, ctx_tensors = mixer(adaln(x, gains[1], sh2, sc2))
    x = x + g2 * y
    x = x + 0.5 * g3 * swiglu(adaln(x, gains[2], sh3, sc3), ffn_in[1], ffn_out[1])
    return x, ctx_tensors


def setup_inputs(seed: int = 0) -> dict:
    key = jax.random.key(seed)
    keys = jax.random.split(key, 64)
    counter = [0]

    def nxt():
        k = keys[counter[0]]
        counter[0] += 1
        return k

    def nrm(shape, std):
        return jax.random.normal(nxt(), shape, jnp.float32) * std

    def gain(shape):
        return 1.0 + nrm(shape, 0.05)

    u = jax.random.uniform(nxt(), (N_EVEN, 2, LRU_W), jnp.float32, 0.9, 0.999)
    base = u ** (1.0 / LRU_C)
    lru_lambda = jnp.log(base) - jnp.log1p(-base)
    return {
        'x_prompt': nrm((BATCH, SEQ, D_MODEL), 1.0),
        'x_sample': nrm((DEC_BATCH, DEC_SEQ, D_MODEL), 1.0),
        'state_lru_fwd': nrm((DEC_BATCH, N_EVEN, LRU_W), 0.5),
        'state_lru_bwd': nrm((DEC_BATCH, N_EVEN, LRU_W), 0.5),
        'cache_na_k': nrm((DEC_BATCH, N_EVEN, PAST_LEN, NA_HEADS, NA_DH), 1.0),
        'cache_na_v': nrm((DEC_BATCH, N_EVEN, PAST_LEN, NA_HEADS, NA_DH), 1.0),
        'cache_gqa_k': nrm((DEC_BATCH, N_ODD, PAST_LEN, GQA_KV_HEADS, GQA_DH), 1.0),
        'cache_gqa_v': nrm((DEC_BATCH, N_ODD, PAST_LEN, GQA_KV_HEADS, GQA_DH), 1.0),
        'cache_mla_ckv': nrm((DEC_BATCH, N_ODD, PAST_LEN, MLA_KV_RANK), 1.0),
        'cache_mla_krope': nrm((DEC_BATCH, N_ODD, PAST_LEN, MLA_ROPE), 1.0),
        'c': nrm((DEC_BATCH, D_MODEL), 1.0),
        'c_ctx': nrm((D_MODEL,), 1.0),
        'w_mod': nrm((DEPTH, D_MODEL, N_MOD * D_MODEL), 0.5 * D_MODEL ** -0.5),
        'b_mod': nrm((DEPTH, N_MOD * D_MODEL), 0.02),
        'norm_gain': gain((DEPTH, 3, D_MODEL)),
        'w_ffn_in': nrm((DEPTH, 2, D_MODEL, 2 * D_FF), D_MODEL ** -0.5),
        'w_ffn_out': nrm((DEPTH, 2, D_FF, D_MODEL), D_FF ** -0.5),
        'w_in_ab': nrm((N_EVEN, D_MODEL, AB_IN), D_MODEL ** -0.5),
        'conv_w': nrm((N_EVEN, CONV_W, LRU_W), CONV_W ** -0.5),
        'conv_b': nrm((N_EVEN, LRU_W), 0.02),
        'lru_wa': nrm((N_EVEN, 2, LRU_BLOCKS, LRU_BW, LRU_BW), LRU_BW ** -0.5),
        'lru_ba': nrm((N_EVEN, 2, LRU_W), 0.1),
        'lru_wx': nrm((N_EVEN, 2, LRU_BLOCKS, LRU_BW, LRU_BW), LRU_BW ** -0.5),
        'lru_bx': nrm((N_EVEN, 2, LRU_W), 0.1),
        'lru_lambda': lru_lambda,
        'na_bias': nrm((N_EVEN, NA_HEADS, 2 * NA_KR - 1, 2 * NA_KC - 1), 0.5),
        'w_out_ab': nrm((N_EVEN, AB_OUT, D_MODEL), AB_OUT ** -0.5),
        'w_in_cd': nrm((N_ODD, D_MODEL, CD_IN), D_MODEL ** -0.5),
        'gqa_q_gain': gain((N_ODD, GQA_DH)),
        'gqa_k_gain': gain((N_ODD, GQA_DH)),
        'mla_q_gain': gain((N_ODD, MLA_Q_RANK)),
        'mla_kv_gain': gain((N_ODD, MLA_KV_RANK)),
        'mla_w_uq': nrm((N_ODD, MLA_Q_RANK, MLA_HEADS * MLA_QK), MLA_Q_RANK ** -0.5),
        'mla_w_uk': nrm((N_ODD, MLA_KV_RANK, MLA_HEADS * MLA_NOPE), MLA_KV_RANK ** -0.5),
        'mla_w_uv': nrm((N_ODD, MLA_KV_RANK, MLA_HEADS * MLA_V), MLA_KV_RANK ** -0.5),
        'w_out_cd': nrm((N_ODD, CD_OUT, D_MODEL), CD_OUT ** -0.5),
        'final_gain': gain((D_MODEL,)),
    }


def reference(x_prompt, x_sample, state_lru_fwd, state_lru_bwd, cache_na_k, cache_na_v, cache_gqa_k,
              cache_gqa_v, cache_mla_ckv, cache_mla_krope, c, c_ctx, w_mod, b_mod, norm_gain, w_ffn_in,
              w_ffn_out, w_in_ab, conv_w, conv_b, lru_wa, lru_ba, lru_wx, lru_bx, lru_lambda, na_bias,
              w_out_ab, w_in_cd, gqa_q_gain, gqa_k_gain, mla_q_gain, mla_kv_gain, mla_w_uq, mla_w_uk,
              mla_w_uv, w_out_cd, final_gain):
    xp, xs = x_prompt, x_sample
    st_f, st_b, na_k, na_v, gq_k, gq_v, ml_c, ml_r = [], [], [], [], [], [], [], []
    for layer in range(DEPTH):
        mod_p = modulation(c_ctx[None, :], w_mod[layer], b_mod[layer])
        mod_s = modulation(c, w_mod[layer], b_mod[layer])
        gains, f_in, f_out = norm_gain[layer], w_ffn_in[layer], w_ffn_out[layer]
        j = layer // 2
        if layer % 2 == 0:
            lp = {'w_in': w_in_ab[j], 'conv_w': conv_w[j], 'conv_b': conv_b[j], 'wa': lru_wa[j],
                  'ba': lru_ba[j], 'wx': lru_wx[j], 'bx': lru_bx[j], 'lam': lru_lambda[j],
                  'na_bias': na_bias[j], 'w_out': w_out_ab[j]}
            xp, ctx = trunk_layer(xp, mod_p, gains, f_in, f_out, lambda h: ab_context(h, lp))
            xs, _ = trunk_layer(xs, mod_s, gains, f_in, f_out,
                                lambda h: ab_latent(h, lp, state_lru_fwd[:, j], state_lru_bwd[:, j],
                                                    cache_na_k[:, j], cache_na_v[:, j]))
            st_f.append(ctx[0])
            st_b.append(ctx[1])
            na_k.append(ctx[2])
            na_v.append(ctx[3])
        else:
            lp = {'w_in': w_in_cd[j], 'q_gain': gqa_q_gain[j], 'k_gain': gqa_k_gain[j],
                  'mla_q_gain': mla_q_gain[j], 'mla_kv_gain': mla_kv_gain[j], 'w_uq': mla_w_uq[j],
                  'w_uk': mla_w_uk[j], 'w_uv': mla_w_uv[j], 'w_out': w_out_cd[j]}
            xp, ctx = trunk_layer(xp, mod_p, gains, f_in, f_out, lambda h: cd_context(h, lp))
            xs, _ = trunk_layer(xs, mod_s, gains, f_in, f_out,
                                lambda h: cd_latent(h, lp, cache_gqa_k[:, j], cache_gqa_v[:, j],
                                                    cache_mla_ckv[:, j], cache_mla_krope[:, j]))
            gq_k.append(ctx[0])
            gq_v.append(ctx[1])
            ml_c.append(ctx[2])
            ml_r.append(ctx[3])
    y_prompt = rmsnorm(xp, final_gain)
    y_sample = rmsnorm(xs, final_gain)
    new_state_lru_fwd = jnp.stack(st_f, axis=1)
    new_state_lru_bwd = jnp.stack(st_b, axis=1)
    new_cache_na_k = jnp.stack(na_k, axis=1)
    new_cache_na_v = jnp.stack(na_v, axis=1)
    new_cache_gqa_k = jnp.stack(gq_k, axis=1)
    new_cache_gqa_v = jnp.stack(gq_v, axis=1)
    new_cache_mla_ckv = jnp.stack(ml_c, axis=1)
    new_cache_mla_krope = jnp.stack(ml_r, axis=1)
    return (y_prompt, y_sample, new_state_lru_fwd, new_state_lru_bwd, new_cache_na_k, new_cache_na_v,
            new_cache_gqa_k, new_cache_gqa_v, new_cache_mla_ckv, new_cache_mla_krope)
```

```python
import functools

import jax
import jax.numpy as jnp
from jax import lax
from jax.experimental import pallas as pl
from jax.experimental.pallas import tpu as pltpu

F32 = jnp.float32
BF16 = jnp.bfloat16

EPS = 1e-6
N_MOD = 9
GRID_W = 64
NA_KR = 8
NA_KC = 16
ROPE_THETA = 10000.0
LRU_C = 8.0
CONV_W = 4
MLA_NOPE = 128
MLA_V = 128
NEG_BIAS = -1e30

V7X_VMEM_BYTES = 64 * 1024 * 1024
V7X_LANES = 128
V7X_SUBLANES = 8
MOD_ROWS = 16
TOKEN_TILE = 512
FF_TILE = 512
Q_CHUNK = 256


def _params(sem, vmem_mib):
    return pltpu.CompilerParams(dimension_semantics=sem, vmem_limit_bytes=vmem_mib * 1024 * 1024)


def _silu(x):
    return x * jax.nn.sigmoid(x)


def _rms(x, gain):
    ms = jnp.mean(x * x, axis=-1, keepdims=True)
    return x * lax.rsqrt(ms + EPS) * gain


def _mod_row(i, n_prompt_tiles, tiles_per_seq):
    return jnp.where(i < n_prompt_tiles, 0, 1 + (i - n_prompt_tiles) // tiles_per_seq)


def _mod_kernel(cond_ref, w_ref, b_ref, o_ref):
    s = _silu(cond_ref[...]).astype(BF16)
    o_ref[...] = jnp.dot(s, w_ref[...].astype(BF16), preferred_element_type=F32) + b_ref[...]


def _modulation(cond, w_mod, b_mod):
    depth, d, n = w_mod.shape
    tn = 1024
    return pl.pallas_call(
        _mod_kernel,
        grid=(depth, n // tn),
        in_specs=[
            pl.BlockSpec((MOD_ROWS, d), lambda l, j: (0, 0)),
            pl.BlockSpec((None, d, tn), lambda l, j: (l, 0, j)),
            pl.BlockSpec((None, 1, tn), lambda l, j: (l, 0, j)),
        ],
        out_specs=pl.BlockSpec((None, MOD_ROWS, tn), lambda l, j: (l, 0, j)),
        out_shape=jax.ShapeDtypeStruct((depth, MOD_ROWS, n), F32),
        compiler_params=_params(("parallel", "parallel"), 40),
        name="modulation",
    )(cond, w_mod, b_mod.reshape(depth, 1, n))


def _ffn_kernel(x_ref, mod_ref, gain_ref, wg_ref, wu_ref, wo_ref, o_ref, h_ref, *,
                mod_base, n_prompt_tiles, tiles_per_seq):
    i = pl.program_id(0)
    j = pl.program_id(1)
    d = x_ref.shape[1]
    row = _mod_row(i, n_prompt_tiles, tiles_per_seq)

    @pl.when(j == 0)
    def _():
        shift = mod_ref[pl.ds(row, 1), pl.ds(mod_base * d, d)]
        scale = mod_ref[pl.ds(row, 1), pl.ds((mod_base + 1) * d, d)]
        y = _rms(x_ref[...], gain_ref[...])
        h_ref[...] = (y * (1 + scale) + shift).astype(BF16)
        o_ref[...] = jnp.zeros_like(o_ref)

    h = h_ref[...]
    g = jnp.dot(h, wg_ref[...], preferred_element_type=F32)
    u = jnp.dot(h, wu_ref[...], preferred_element_type=F32)
    act = (_silu(g) * u).astype(BF16)
    o_ref[...] += jnp.dot(act, wo_ref[...], preferred_element_type=F32)

    @pl.when(j == pl.num_programs(1) - 1)
    def _():
        gate = mod_ref[pl.ds(row, 1), pl.ds((mod_base + 2) * d, d)]
        o_ref[...] = x_ref[...] + (0.5 * gate) * o_ref[...]


def _ffn(x, mods, layer, gain, w_in, w_out, mod_base, n_prompt_tiles, tiles_per_seq):
    t, d = x.shape
    dff = w_out.shape[0]
    nj = dff // FF_TILE
    kern = functools.partial(_ffn_kernel, mod_base=mod_base, n_prompt_tiles=n_prompt_tiles,
                             tiles_per_seq=tiles_per_seq)
    return pl.pallas_call(
        kern,
        grid=(t // TOKEN_TILE, nj),
        in_specs=[
            pl.BlockSpec((TOKEN_TILE, d), lambda i, j: (i, 0)),
            pl.BlockSpec((None, MOD_ROWS, N_MOD * d), lambda i, j: (layer, 0, 0)),
            pl.BlockSpec((1, d), lambda i, j: (0, 0)),
            pl.BlockSpec((d, FF_TILE), lambda i, j: (0, j)),
            pl.BlockSpec((d, FF_TILE), lambda i, j: (0, j + nj)),
            pl.BlockSpec((FF_TILE, d), lambda i, j: (j, 0)),
        ],
        out_specs=pl.BlockSpec((TOKEN_TILE, d), lambda i, j: (i, 0)),
        out_shape=jax.ShapeDtypeStruct((t, d), F32),
        scratch_shapes=[pltpu.VMEM((TOKEN_TILE, d), BF16)],
        compiler_params=_params(("parallel", "arbitrary"), 48),
        name="ffn",
    )(x, mods, gain.reshape(1, d), w_in, w_in, w_out)


def _inproj_kernel(x_ref, mod_ref, gain_ref, w_ref, o_ref, h_ref, *, mod_base, n_prompt_tiles,
                   tiles_per_seq):
    i = pl.program_id(0)
    d = x_ref.shape[1]

    @pl.when(pl.program_id(1) == 0)
    def _():
        row = _mod_row(i, n_prompt_tiles, tiles_per_seq)
        shift = mod_ref[pl.ds(row, 1), pl.ds(mod_base * d, d)]
        scale = mod_ref[pl.ds(row, 1), pl.ds((mod_base + 1) * d, d)]
        y = _rms(x_ref[...], gain_ref[...])
        h_ref[...] = (y * (1 + scale) + shift).astype(BF16)

    o_ref[...] = jnp.dot(h_ref[...], w_ref[...], preferred_element_type=F32)


def _inproj(x, mods, layer, gain, w, tn, n_prompt_tiles, tiles_per_seq):
    t, d = x.shape
    n = w.shape[1]
    kern = functools.partial(_inproj_kernel, mod_base=3, n_prompt_tiles=n_prompt_tiles,
                             tiles_per_seq=tiles_per_seq)
    return pl.pallas_call(
        kern,
        grid=(t // TOKEN_TILE, n // tn),
        in_specs=[
            pl.BlockSpec((TOKEN_TILE, d), lambda i, j: (i, 0)),
            pl.BlockSpec((None, MOD_ROWS, N_MOD * d), lambda i, j: (layer, 0, 0)),
            pl.BlockSpec((1, d), lambda i, j: (0, 0)),
            pl.BlockSpec((d, tn), lambda i, j: (0, j)),
        ],
        out_specs=pl.BlockSpec((TOKEN_TILE, tn), lambda i, j: (i, j)),
        out_shape=jax.ShapeDtypeStruct((t, n), F32),
        scratch_shapes=[pltpu.VMEM((TOKEN_TILE, d), BF16)],
        compiler_params=_params(("parallel", "arbitrary"), 48),
        name="inproj",
    )(x, mods, gain.reshape(1, d), w)


def _outproj_kernel(x_ref, mod_ref, ap_ref, as_ref, bp_ref, bs_ref, w_ref, o_ref, *,
                    n_prompt_tiles, tiles_per_seq):
    i = pl.program_id(0)
    d = x_ref.shape[1]
    half = ap_ref.shape[1]
    is_prompt = i < n_prompt_tiles
    ya = jnp.where(is_prompt, ap_ref[...], as_ref[...])
    yb = jnp.where(is_prompt, bp_ref[...], bs_ref[...])
    y = jnp.dot(ya, w_ref[pl.ds(0, half), :], preferred_element_type=F32)
    y = y + jnp.dot(yb, w_ref[pl.ds(half, half), :], preferred_element_type=F32)
    row = _mod_row(i, n_prompt_tiles, tiles_per_seq)
    gate = mod_ref[pl.ds(row, 1), pl.ds(5 * d, d)]
    o_ref[...] = x_ref[...] + gate * y


def _outproj(x, mods, layer, ya_p, ya_s, yb_p, yb_s, w, n_prompt_tiles, tiles_per_seq):
    t, d = x.shape
    half = ya_p.shape[1]
    npt = n_prompt_tiles
    kern = functools.partial(_outproj_kernel, n_prompt_tiles=npt, tiles_per_seq=tiles_per_seq)
    p_map = lambda i: (jnp.minimum(i, npt - 1), 0)
    s_map = lambda i: (jnp.maximum(i - npt, 0), 0)
    return pl.pallas_call(
        kern,
        grid=(t // TOKEN_TILE,),
        in_specs=[
            pl.BlockSpec((TOKEN_TILE, d), lambda i: (i, 0)),
            pl.BlockSpec((None, MOD_ROWS, N_MOD * d), lambda i: (layer, 0, 0)),
            pl.BlockSpec((TOKEN_TILE, half), p_map),
            pl.BlockSpec((TOKEN_TILE, half), s_map),
            pl.BlockSpec((TOKEN_TILE, half), p_map),
            pl.BlockSpec((TOKEN_TILE, half), s_map),
            pl.BlockSpec((2 * half, d), lambda i: (0, 0)),
        ],
        out_specs=pl.BlockSpec((TOKEN_TILE, d), lambda i: (i, 0)),
        out_shape=jax.ShapeDtypeStruct((t, d), F32),
        compiler_params=_params(("parallel",), 48),
        name="outproj",
    )(x, mods, ya_p, ya_s, yb_p, yb_s, w)


def _final_norm_kernel(x_ref, g_ref, o_ref):
    o_ref[...] = _rms(x_ref[...], g_ref[...])


def _final_norm(x, gain, row_start, rows):
    d = x.shape[1]
    off = row_start // TOKEN_TILE
    return pl.pallas_call(
        _final_norm_kernel,
        grid=(rows // TOKEN_TILE,),
        in_specs=[pl.BlockSpec((TOKEN_TILE, d), lambda i: (i + off, 0)),
                  pl.BlockSpec((1, d), lambda i: (0, 0))],
        out_specs=pl.BlockSpec((TOKEN_TILE, d), lambda i: (i, 0)),
        out_shape=jax.ShapeDtypeStruct((rows, d), F32),
        compiler_params=_params(("parallel",), 32),
        name="final_norm",
    )(x, gain.reshape(1, d))


def _lru_kernel(xa_ref, ga_ref, h0f_ref, h0b_ref, cw_ref, cb_ref, wa_ref, ba_ref, wx_ref, bx_ref,
                lam_ref, y_ref, lf_ref, lb_ref, a_scr, u_scr, hf_scr, hb_scr):
    seq, w = xa_ref.shape
    groups = seq // V7X_SUBLANES
    xa = xa_ref[...]
    row = lax.broadcasted_iota(jnp.int32, (seq, w), 0)
    sub = row & (V7X_SUBLANES - 1)

    def tap(offset):
        if offset == 0:
            return xa
        shifted = pltpu.roll(xa, (-offset) % seq, axis=0)
        valid = (row + offset >= 0) & (row + offset < seq)
        return jnp.where(valid, shifted, 0.0)

    xc = cb_ref[...]
    for j in range(CONV_W):
        xc = xc + tap(j - CONV_W // 2) * cw_ref[pl.ds(j, 1), :]
    xcb = xc.astype(BF16)

    for direction, (h0_ref, h_scr, last_ref) in enumerate(((h0f_ref, hf_scr, lf_ref),
                                                            (h0b_ref, hb_scr, lb_ref))):
        reverse = direction == 1
        r = jax.nn.sigmoid(jnp.dot(xcb, wa_ref[direction], preferred_element_type=F32) + ba_ref[direction])
        gi = jax.nn.sigmoid(jnp.dot(xcb, wx_ref[direction], preferred_element_type=F32) + bx_ref[direction])
        neg_lam = -lam_ref[direction]
        softplus = jnp.maximum(neg_lam, 0.0) + jnp.log1p(jnp.exp(-jnp.abs(neg_lam)))
        log_a = -LRU_C * r * softplus
        a = jnp.exp(log_a)
        u = jnp.sqrt(-jnp.tanh(log_a) * (a * a + 1.0)) * (gi * xc)
        for step in (1, 2, 4):
            if reverse:
                a_nb = pltpu.roll(a, seq - step, axis=0)
                u_nb = pltpu.roll(u, seq - step, axis=0)
                valid = sub < V7X_SUBLANES - step
            else:
                a_nb = pltpu.roll(a, step, axis=0)
                u_nb = pltpu.roll(u, step, axis=0)
                valid = sub >= step
            u = a * jnp.where(valid, u_nb, 0.0) + u
            a = a * jnp.where(valid, a_nb, 1.0)
        a_scr[...] = a
        u_scr[...] = u

        def carry_step(g, carry, reverse=reverse, h_scr=h_scr):
            gg = groups - 1 - g if reverse else g
            off = pl.multiple_of(gg * V7X_SUBLANES, V7X_SUBLANES)
            h = a_scr[pl.ds(off, V7X_SUBLANES), :] * carry + u_scr[pl.ds(off, V7X_SUBLANES), :]
            h_scr[pl.ds(off, V7X_SUBLANES), :] = h
            return h[0:1] if reverse else h[V7X_SUBLANES - 1:V7X_SUBLANES]

        last_ref[...] = lax.fori_loop(0, groups, carry_step, h0_ref[...], unroll=4)

    y_ref[...] = ((hf_scr[...] + hb_scr[...]) * jax.nn.gelu(ga_ref[...])).astype(BF16)


def _lru(p_ab, h0f, h0b, prm, seq_len, n_seq, row_start):
    lru_w = h0f.shape[-1]
    cw = 256
    ncb = lru_w // cw
    sb = row_start // seq_len
    grid = (n_seq, ncb)
    vec = lambda: pl.BlockSpec((None, 1, cw), lambda s, c: (s, 0, c))
    par2 = lambda: pl.BlockSpec((2, 1, cw), lambda s, c: (0, 0, c))
    return pl.pallas_call(
        _lru_kernel,
        grid=grid,
        in_specs=[
            pl.BlockSpec((seq_len, cw), lambda s, c: (s + sb, c)),
            pl.BlockSpec((seq_len, cw), lambda s, c: (s + sb, c + ncb)),
            vec(), vec(),
            pl.BlockSpec((CONV_W, cw), lambda s, c: (0, c)),
            pl.BlockSpec((1, cw), lambda s, c: (0, c)),
            pl.BlockSpec((2, None, cw, cw), lambda s, c: (0, c, 0, 0)),
            par2(),
            pl.BlockSpec((2, None, cw, cw), lambda s, c: (0, c, 0, 0)),
            par2(), par2(),
        ],
        out_specs=[
            pl.BlockSpec((seq_len, cw), lambda s, c: (s, c)),
            vec(), vec(),
        ],
        out_shape=[
            jax.ShapeDtypeStruct((n_seq * seq_len, lru_w), BF16),
            jax.ShapeDtypeStruct((n_seq, 1, lru_w), F32),
            jax.ShapeDtypeStruct((n_seq, 1, lru_w), F32),
        ],
        scratch_shapes=[pltpu.VMEM((seq_len, cw), F32)] * 4,
        compiler_params=_params(("parallel", "parallel"), 40),
        name="lru",
    )(p_ab, p_ab, h0f.reshape(n_seq, 1, lru_w), h0b.reshape(n_seq, 1, lru_w), prm["conv_w"],
      prm["conv_b"], prm["wa"], prm["ba"], prm["wx"], prm["bx"], prm["lam"])


def _block_diag_tiles(w, tile):
    two, nb, bw, _ = w.shape
    per = tile // bw
    w = w.reshape(two, nb // per, per, bw, bw)
    eye = jnp.eye(per, dtype=w.dtype)
    out = jnp.einsum("dgpij,pq->dgpiqj", w, eye)
    return out.reshape(two, nb // per, tile, tile)


def _softmax_attend(q, ks, vs, biases, scale):
    scores = []
    for k, b in zip(ks, biases):
        s = lax.dot_general(q, k, (((1,), (1,)), ((), ())), preferred_element_type=F32) * scale
        scores.append(s if b is None else s + b)
    m = scores[0].max(axis=-1, keepdims=True)
    for s in scores[1:]:
        m = jnp.maximum(m, s.max(axis=-1, keepdims=True))
    denom = None
    out = None
    for s, v in zip(scores, vs):
        e = jnp.exp(s - m)
        part = e.sum(axis=-1, keepdims=True)
        denom = part if denom is None else denom + part
        o = jnp.dot(e.astype(BF16), v, preferred_element_type=F32)
        out = o if out is None else out + o
    return out / denom


def _lane_half(shape, half):
    lane = lax.broadcasted_iota(jnp.int32, shape, len(shape) - 1)
    return (lane % V7X_LANES >= V7X_LANES // 2) == (half == 1)


def _pair_attention(q_ref, ks, vs, o_ref, scale, biases=None):
    lq = q_ref.shape[0]
    kb = [k.astype(BF16) for k in ks]
    vb = [v.astype(BF16) for v in vs]

    def chunk(c, carry):
        off = pl.multiple_of(c * Q_CHUNK, Q_CHUNK)
        q = q_ref[pl.ds(off, Q_CHUNK), :].astype(F32)
        outs = []
        for half in (0, 1):
            sel = _lane_half(q.shape, half)
            qm = jnp.where(sel, q, 0.0).astype(BF16)
            outs.append(_softmax_attend(qm, kb, vb, [None] * len(kb), scale))
        o_ref[pl.ds(off, Q_CHUNK), :] = jnp.where(_lane_half(q.shape, 0), outs[0], outs[1]).astype(o_ref.dtype)
        return carry

    lax.fori_loop(0, lq // Q_CHUNK, chunk, 0)


def _na_ctx_kernel(q_ref, k_ref, v_ref, o_ref, *, scale):
    _pair_attention(q_ref, [k_ref[...]], [v_ref[...]], o_ref, scale)


def _na_ctx(p_ab, n_seq, seq_len, q_col, k_col, v_col, n_pairs, scale):
    blk = lambda col: pl.BlockSpec((seq_len, V7X_LANES), lambda b, h: (b, col + h))
    return pl.pallas_call(
        functools.partial(_na_ctx_kernel, scale=scale),
        grid=(n_seq, n_pairs),
        in_specs=[blk(q_col), blk(k_col), blk(v_col)],
        out_specs=pl.BlockSpec((seq_len, V7X_LANES), lambda b, h: (b, h)),
        out_shape=jax.ShapeDtypeStruct((n_seq * seq_len, n_pairs * V7X_LANES), BF16),
        compiler_params=_params(("parallel", "parallel"), 32),
        name="na_ctx",
    )(p_ab, p_ab, p_ab)


NA_WIN = 768


def _na_lat_kernel(q_ref, k_ref, v_ref, kc_ref, vc_ref, bias_ref, o_ref, *, scale):
    lq = q_ref.shape[0]
    n_chunks = lq // Q_CHUNK
    kcb = kc_ref[...].astype(BF16)
    vcb = vc_ref[...].astype(BF16)
    for c in range(n_chunks):
        win = 0 if c < n_chunks // 2 else lq - NA_WIN
        kb = k_ref[pl.ds(win, NA_WIN), :].astype(BF16)
        vb = v_ref[pl.ds(win, NA_WIN), :].astype(BF16)
        q = q_ref[pl.ds(c * Q_CHUNK, Q_CHUNK), :]
        outs = []
        for half in (0, 1):
            qm = jnp.where(_lane_half(q.shape, half), q, 0.0).astype(BF16)
            outs.append(_softmax_attend(qm, [kb, kcb], [vb, vcb], [bias_ref[half, c], None], scale))
        o_ref[pl.ds(c * Q_CHUNK, Q_CHUNK), :] = jnp.where(_lane_half(q.shape, 0), outs[0], outs[1]).astype(o_ref.dtype)


def _na_lat(p_ab, cache_k, cache_v, bias, n_seq, seq_len, row_start, q_col, k_col, v_col, n_pairs, scale):
    sb = row_start // seq_len
    past = cache_k.shape[1]
    blk = lambda col: pl.BlockSpec((seq_len, V7X_LANES), lambda h, b: (b + sb, col + h))
    cblk = lambda: pl.BlockSpec((None, past, V7X_LANES), lambda h, b: (b, 0, h))
    n_chunks = seq_len // Q_CHUNK
    return pl.pallas_call(
        functools.partial(_na_lat_kernel, scale=scale),
        grid=(n_pairs, n_seq),
        in_specs=[blk(q_col), blk(k_col), blk(v_col), cblk(), cblk(),
                  pl.BlockSpec((2, n_chunks, Q_CHUNK, NA_WIN), lambda h, b: (h, 0, 0, 0))],
        out_specs=pl.BlockSpec((seq_len, V7X_LANES), lambda h, b: (b, h)),
        out_shape=jax.ShapeDtypeStruct((n_seq * seq_len, n_pairs * V7X_LANES), BF16),
        compiler_params=_params(("parallel", "arbitrary"), 48),
        name="na_lat",
    )(p_ab, p_ab, p_ab, cache_k, cache_v, bias)


def _na_bias_blocks(tab, seq_len):
    n_heads = tab.shape[0]
    rows_n = seq_len // GRID_W
    kr = min(NA_KR, rows_n)
    cols = jnp.arange(GRID_W)
    c_start = jnp.clip(cols - NA_KC // 2, 0, GRID_W - NA_KC)
    col_ok = (cols[None, :] >= c_start[:, None]) & (cols[None, :] < c_start[:, None] + NA_KC)
    rel_c = jnp.clip(cols[None, :] - cols[:, None] + (NA_KC - 1), 0, 2 * NA_KC - 2)
    tiles = jnp.where(col_ok[None, None], tab[:, :, rel_c], NEG_BIAS)
    tiles = jnp.concatenate([tiles, jnp.full((n_heads, 1, GRID_W, GRID_W), NEG_BIAS, F32)], axis=1)
    rows = jnp.arange(rows_n)
    r_start = jnp.clip(rows - kr // 2, 0, rows_n - kr)
    rows_per_chunk = Q_CHUNK // GRID_W
    n_chunks = rows_n // rows_per_chunk
    win_rows = NA_WIN // GRID_W
    base = jnp.where(rows // rows_per_chunk < n_chunks // 2, 0, rows_n - win_rows)
    k_row = base[:, None] + jnp.arange(win_rows)[None, :]
    row_ok = (k_row >= r_start[:, None]) & (k_row < r_start[:, None] + kr)
    idx = jnp.where(row_ok, k_row - rows[:, None] + (NA_KR - 1), 2 * NA_KR - 1)
    g = tiles[:, idx]
    g = g.transpose(0, 1, 3, 2, 4)
    return g.reshape(n_heads, n_chunks, Q_CHUNK, NA_WIN)


def _head_rms(x, ones_bd, gain, head_dim):
    sq = x * x
    hi = sq.astype(BF16)
    lo = (sq - hi.astype(F32)).astype(BF16)
    ss = jnp.dot(hi, ones_bd, preferred_element_type=F32) + jnp.dot(lo, ones_bd, preferred_element_type=F32)
    return x * lax.rsqrt(ss * (1.0 / head_dim) + EPS) * gain


def _rope(x, cos, sin_lo, sin_hi):
    w = x.shape[1]
    reps = w // V7X_LANES
    tile = lambda t: t if reps == 1 else jnp.concatenate([t] * reps, axis=1)
    quarter = 16
    return (x * tile(cos) + pltpu.roll(x, w - quarter, axis=1) * tile(sin_lo)
            + pltpu.roll(x, quarter, axis=1) * tile(sin_hi))


def _cd_post_kernel(p_ref, ones_ref, qg_ref, kg_ref, mqg_ref, mkg_ref, wuq_ref, wukv_ref, *rest,
                    rope, head_dim, q_w, kv_w, rank):
    if rope:
        cos_ref, slo_ref, shi_ref = rest[:3]
        rest = rest[3:]
        tabs = (cos_ref[...], slo_ref[...], shi_ref[...])
    q_ref, k_ref, qn_ref, qr_ref, ckv_ref, kn_ref, vm_ref, kr_ref = rest
    c0 = 0
    qc = p_ref[:, pl.ds(c0, q_w)]
    c0 += q_w
    kc = p_ref[:, pl.ds(c0, kv_w)]
    c0 += 2 * kv_w
    qa = p_ref[:, pl.ds(c0, rank)]
    c0 += rank
    ckv = p_ref[:, pl.ds(c0, rank)]
    c0 += rank
    kr = p_ref[:, pl.ds(c0, V7X_LANES)]

    qc = _head_rms(qc, ones_ref[...], qg_ref[...], head_dim)
    kc = _head_rms(kc, ones_ref[pl.ds(0, kv_w), pl.ds(0, kv_w)], kg_ref[...], head_dim)
    qd = jnp.dot(_rms(qa, mqg_ref[...]).astype(BF16), wuq_ref[...], preferred_element_type=F32)
    n_nope = qn_ref.shape[1]
    qn = qd[:, :n_nope]
    qr = qd[:, n_nope:]
    ckv = _rms(ckv, mkg_ref[...])
    if rope:
        qc = _rope(qc, *tabs)
        kc = _rope(kc, *tabs)
        qr = _rope(qr, *tabs)
        kr = _rope(kr, *tabs)
    q_ref[...] = qc.astype(BF16)
    k_ref[...] = kc
    qn_ref[...] = qn.astype(BF16)
    qr_ref[...] = qr.astype(BF16)
    ckv_ref[...] = ckv
    kv = jnp.dot(ckv.astype(BF16), wukv_ref[...], preferred_element_type=F32)
    kn_ref[...] = kv[:, :n_nope].astype(BF16)
    vm_ref[...] = kv[:, n_nope:].astype(BF16)
    kr_ref[...] = kr


def _cd_post(p_cd, prm, row_start, rows, rope_tabs, dims):
    q_w, kv_w, rank, head_dim, n_nope, n_rope, n_v = dims
    tm = 256
    off = row_start // tm
    n_in = p_cd.shape[1]
    rope = rope_tabs is not None
    const = lambda shape: pl.BlockSpec(shape, lambda i: (0,) * len(shape))
    in_specs = [
        pl.BlockSpec((tm, n_in), lambda i: (i + off, 0)),
        const((q_w, q_w)), const((1, q_w)), const((1, kv_w)), const((1, rank)), const((1, rank)),
        const((rank, n_nope + n_rope)), const((rank, n_nope + n_v)),
    ]
    args = [p_cd, prm["ones_bd"], prm["q_gain"], prm["k_gain"], prm["mla_q_gain"], prm["mla_kv_gain"],
            prm["w_uq"], prm["w_ukv"]]
    if rope:
        seq_tiles = rope_tabs[0].shape[0] // tm
        in_specs += [pl.BlockSpec((tm, V7X_LANES), lambda i: (i % seq_tiles, 0))] * 3
        args += list(rope_tabs)
    widths = [(q_w, BF16), (kv_w, F32), (n_nope, BF16), (n_rope, BF16), (rank, F32), (n_nope, BF16),
              (n_v, BF16), (V7X_LANES, F32)]
    kern = functools.partial(_cd_post_kernel, rope=rope, head_dim=head_dim, q_w=q_w, kv_w=kv_w, rank=rank)
    return pl.pallas_call(
        kern,
        grid=(rows // tm,),
        in_specs=in_specs,
        out_specs=[pl.BlockSpec((tm, w), lambda i: (i, 0)) for w, _ in widths],
        out_shape=[jax.ShapeDtypeStruct((rows, w), dt) for w, dt in widths],
        compiler_params=_params(("parallel",), 48),
        name="cd_post",
    )(*args)


def _rope_tables(seq_len, head_dim):
    half = head_dim // 2
    nf = half // 2
    t = jnp.arange(seq_len)
    inv_freq = 1.0 / (ROPE_THETA ** (jnp.arange(nf, dtype=F32) / nf))
    zeros = jnp.zeros((seq_len, nf), F32)
    cos, slo, shi = [], [], []
    for pos in (t // GRID_W, t % GRID_W):
        ang = pos.astype(F32)[:, None] * inv_freq[None, :]
        c, s = jnp.cos(ang), jnp.sin(ang)
        cos += [c, c]
        slo += [-s, zeros]
        shi += [zeros, s]
    reps = V7X_LANES // head_dim
    return tuple(jnp.tile(jnp.concatenate(x, axis=1), (1, reps)) for x in (cos, slo, shi))


def _ckv_up_kernel(c_ref, w_ref, kn_ref, vm_ref):
    kv = jnp.dot(c_ref[...].astype(BF16), w_ref[...], preferred_element_type=F32)
    n = kn_ref.shape[1]
    kn_ref[...] = kv[:, :n].astype(BF16)
    vm_ref[...] = kv[:, n:].astype(BF16)


def _ckv_up(ckv, w_ukv, n_nope):
    rows, rank = ckv.shape
    n = w_ukv.shape[1]
    tm = TOKEN_TILE
    return pl.pallas_call(
        _ckv_up_kernel,
        grid=(rows // tm,),
        in_specs=[pl.BlockSpec((tm, rank), lambda i: (i, 0)), pl.BlockSpec((rank, n), lambda i: (0, 0))],
        out_specs=[pl.BlockSpec((tm, n_nope), lambda i: (i, 0)), pl.BlockSpec((tm, n - n_nope), lambda i: (i, 0))],
        out_shape=[jax.ShapeDtypeStruct((rows, n_nope), BF16), jax.ShapeDtypeStruct((rows, n - n_nope), BF16)],
        compiler_params=_params(("parallel",), 32),
        name="ckv_up",
    )(ckv, w_ukv)


def _dup_head(block, use_high):
    swapped = pltpu.roll(block, V7X_LANES // 2, axis=1)
    lane = lax.broadcasted_iota(jnp.int32, block.shape, 1)
    keep = (lane >= V7X_LANES // 2) == use_high
    return jnp.where(keep, block, swapped)


def _gqa_kernel(q_ref, k_ref, v_ref, *rest, scale, has_cache):
    if has_cache:
        kc_ref, vc_ref, o_ref = rest
    else:
        (o_ref,) = rest
    use_high = (pl.program_id(1) % 2) == 1
    ks = [_dup_head(k_ref[...], use_high).astype(BF16)]
    vs = [_dup_head(v_ref[...], use_high).astype(BF16)]
    if has_cache:
        ks.append(_dup_head(kc_ref[...], use_high).astype(BF16))
        vs.append(_dup_head(vc_ref[...], use_high).astype(BF16))
    lq = q_ref.shape[0]
    n_pairs = q_ref.shape[1] // V7X_LANES

    def chunk(c, carry):
        off = pl.multiple_of(c * Q_CHUNK, Q_CHUNK)
        for pair in range(n_pairs):
            q = q_ref[pl.ds(off, Q_CHUNK), pl.ds(pair * V7X_LANES, V7X_LANES)].astype(F32)
            outs = []
            for half in (0, 1):
                qm = jnp.where(_lane_half(q.shape, half), q, 0.0).astype(BF16)
                outs.append(_softmax_attend(qm, ks, vs, [None] * len(ks), scale))
            o = jnp.where(_lane_half(q.shape, 0), outs[0], outs[1])
            o_ref[pl.ds(off, Q_CHUNK), pl.ds(pair * V7X_LANES, V7X_LANES)] = o.astype(o_ref.dtype)
        return carry

    lax.fori_loop(0, lq // Q_CHUNK, chunk, 0)


def _gqa(q, k, p_cd, v_col, cache_k, cache_v, n_seq, seq_len, row_start, n_kv, group_w, scale):
    sb = row_start // seq_len
    has_cache = cache_k is not None
    in_specs = [
        pl.BlockSpec((seq_len, group_w), lambda b, h: (b, h)),
        pl.BlockSpec((seq_len, V7X_LANES), lambda b, h: (b, h // 2)),
        pl.BlockSpec((seq_len, V7X_LANES), lambda b, h: (b + sb, v_col + h // 2)),
    ]
    args = [q, k, p_cd]
    if has_cache:
        past = cache_k.shape[1]
        in_specs += [pl.BlockSpec((None, past, V7X_LANES), lambda b, h: (b, 0, h // 2))] * 2
        args += [cache_k, cache_v]
    return pl.pallas_call(
        functools.partial(_gqa_kernel, scale=scale, has_cache=has_cache),
        grid=(n_seq, n_kv),
        in_specs=in_specs,
        out_specs=pl.BlockSpec((seq_len, group_w), lambda b, h: (b, h)),
        out_shape=jax.ShapeDtypeStruct((n_seq * seq_len, n_kv * group_w), BF16),
        compiler_params=_params(("parallel", "parallel"), 48),
        name="gqa",
    )(*args)


def _mla_kernel(qn_ref, qr_ref, kn_ref, kr_ref, v_ref, *rest, scale, has_cache):
    if has_cache:
        knc_ref, krc_ref, vc_ref, o_ref = rest
    else:
        (o_ref,) = rest
    use_high = (pl.program_id(1) % 2) == 1
    ks = [jnp.concatenate([kn_ref[...], kr_ref[...].astype(BF16)], axis=1)]
    vs = [v_ref[...]]
    if has_cache:
        ks.append(jnp.concatenate([knc_ref[...], krc_ref[...].astype(BF16)], axis=1))
        vs.append(vc_ref[...])
    lq = qn_ref.shape[0]

    def chunk(c, carry):
        off = pl.multiple_of(c * Q_CHUNK, Q_CHUNK)
        qr = qr_ref[pl.ds(off, Q_CHUNK), :].astype(F32)
        lane = lax.broadcasted_iota(jnp.int32, qr.shape, 1)
        qr = jnp.where((lane >= V7X_LANES // 2) == use_high, qr, 0.0).astype(BF16)
        q = jnp.concatenate([qn_ref[pl.ds(off, Q_CHUNK), :], qr], axis=1)
        o = _softmax_attend(q, ks, vs, [None] * len(ks), scale)
        o_ref[pl.ds(off, Q_CHUNK), :] = o.astype(o_ref.dtype)
        return carry

    lax.fori_loop(0, lq // Q_CHUNK, chunk, 0)


def _mla(qn, qr, kn, kr2, vm, cache, n_seq, seq_len, n_heads, scale):
    has_cache = cache is not None
    head = lambda: pl.BlockSpec((seq_len, V7X_LANES), lambda b, h: (b, h))
    in_specs = [head(),
                pl.BlockSpec((seq_len, V7X_LANES), lambda b, h: (b, h // 2)),
                head(),
                pl.BlockSpec((seq_len, V7X_LANES), lambda b, h: (b, 0)),
                head()]
    args = [qn, qr, kn, kr2, vm]
    if has_cache:
        knc, krc, vmc, past = cache
        in_specs += [pl.BlockSpec((past, V7X_LANES), lambda b, h: (b, h)),
                     pl.BlockSpec((past, V7X_LANES), lambda b, h: (b, 0)),
                     pl.BlockSpec((past, V7X_LANES), lambda b, h: (b, h))]
        args += [knc, krc, vmc]
    return pl.pallas_call(
        functools.partial(_mla_kernel, scale=scale, has_cache=has_cache),
        grid=(n_seq, n_heads),
        in_specs=in_specs,
        out_specs=head(),
        out_shape=jax.ShapeDtypeStruct((n_seq * seq_len, n_heads * V7X_LANES), BF16),
        compiler_params=_params(("parallel", "parallel"), 48),
        name="mla",
    )(*args)


def kernel(x_prompt, x_sample, state_lru_fwd, state_lru_bwd, cache_na_k, cache_na_v, cache_gqa_k, cache_gqa_v, cache_mla_ckv, cache_mla_krope, c, c_ctx, w_mod, b_mod, norm_gain, w_ffn_in, w_ffn_out, w_in_ab, conv_w, conv_b, lru_wa, lru_ba, lru_wx, lru_bx, lru_lambda, na_bias, w_out_ab, w_in_cd, gqa_q_gain, gqa_k_gain, mla_q_gain, mla_kv_gain, mla_w_uq, mla_w_uk, mla_w_uv, w_out_cd, final_gain):
    batch, seq, d = x_prompt.shape
    dec_batch, dec_seq, _ = x_sample.shape
    depth = w_mod.shape[0]
    n_p = batch * seq
    n_s = dec_batch * dec_seq
    past = cache_na_k.shape[2]
    lru_w = state_lru_fwd.shape[-1]
    na_heads, na_dh = cache_na_k.shape[3], cache_na_k.shape[4]
    na_w = na_heads * na_dh
    gqa_kv, gqa_dh = cache_gqa_k.shape[3], cache_gqa_k.shape[4]
    kv_w = gqa_kv * gqa_dh
    rank = cache_mla_ckv.shape[-1]
    rope_w = cache_mla_krope.shape[-1]
    q_w = w_in_cd.shape[2] - 2 * kv_w - 2 * rank - rope_w
    assert mla_q_gain.shape[-1] == rank and 2 * rope_w == V7X_LANES
    mla_heads = mla_w_uk.shape[2] // MLA_NOPE
    n_nope = mla_heads * MLA_NOPE
    n_rope = mla_heads * rope_w
    n_v = mla_heads * MLA_V
    mla_qk = MLA_NOPE + rope_w
    assert n_p % dec_seq == 0 and dec_seq % TOKEN_TILE == 0 and seq % Q_CHUNK == 0
    assert dec_seq // GRID_W == 2 * NA_KR and na_dh == GRID_W and gqa_dh == GRID_W and rope_w == GRID_W
    assert w_in_ab.shape[2] == 2 * lru_w + 3 * na_w and (q_w // gqa_kv) % V7X_LANES == 0

    npt = n_p // TOKEN_TILE
    tps = dec_seq // TOKEN_TILE

    x = jnp.concatenate([x_prompt.reshape(n_p, d), x_sample.reshape(n_s, d)], axis=0)
    cond = jnp.concatenate([c_ctx[None, :], c, jnp.zeros((MOD_ROWS - 1 - dec_batch, d), F32)], axis=0)
    mods = _modulation(cond, w_mod, b_mod)

    zeros_state = jnp.zeros((batch, lru_w), F32)
    ones_bd = jnp.kron(jnp.eye(q_w // gqa_dh, dtype=F32), jnp.ones((gqa_dh, gqa_dh), F32)).astype(BF16)
    rope_tabs = _rope_tables(dec_seq, gqa_dh)

    st_f, st_b, na_k, na_v, gq_k, gq_v, ml_c, ml_r = [], [], [], [], [], [], [], []
    for layer in range(depth):
        jdx = layer // 2
        w_in_bf = w_ffn_in[layer].astype(BF16)
        w_out_bf = w_ffn_out[layer].astype(BF16)
        x = _ffn(x, mods, layer, norm_gain[layer, 0], w_in_bf[0], w_out_bf[0], 0, npt, tps)
        if layer % 2 == 0:
            p_ab = _inproj(x, mods, layer, norm_gain[layer, 1], w_in_ab[jdx].astype(BF16), 1280, npt, tps)
            lane_w = 256
            prm = {
                "conv_w": conv_w[jdx], "conv_b": conv_b[jdx].reshape(1, lru_w),
                "wa": _block_diag_tiles(lru_wa[jdx], lane_w).astype(BF16),
                "wx": _block_diag_tiles(lru_wx[jdx], lane_w).astype(BF16),
                "ba": lru_ba[jdx].reshape(2, 1, lru_w), "bx": lru_bx[jdx].reshape(2, 1, lru_w),
                "lam": lru_lambda[jdx].reshape(2, 1, lru_w),
            }
            ya_p, lf, lb = _lru(p_ab, zeros_state, zeros_state, prm, seq, batch, 0)
            ya_s, _, _ = _lru(p_ab, state_lru_fwd[:, jdx], state_lru_bwd[:, jdx], prm, dec_seq, dec_batch, n_p)
            q_col = 2 * lru_w // V7X_LANES
            k_col = q_col + na_w // V7X_LANES
            v_col = k_col + na_w // V7X_LANES
            n_pairs = na_w // V7X_LANES
            scale = na_dh ** -0.5
            yb_p = _na_ctx(p_ab, batch, seq, q_col, k_col, v_col, n_pairs, scale)
            bias = _na_bias_blocks(na_bias[jdx], dec_seq)
            yb_s = _na_lat(p_ab, cache_na_k[:, jdx].reshape(dec_batch, past, na_w),
                           cache_na_v[:, jdx].reshape(dec_batch, past, na_w), bias, dec_batch, dec_seq, n_p,
                           q_col, k_col, v_col, n_pairs, scale)
            x = _outproj(x, mods, layer, ya_p, ya_s, yb_p, yb_s, w_out_ab[jdx].astype(BF16), npt, tps)
            st_f.append(lf.reshape(batch, lru_w))
            st_b.append(lb.reshape(batch, lru_w))
            na_k.append(p_ab[:n_p, 2 * lru_w + na_w:2 * lru_w + 2 * na_w].reshape(batch, seq, na_heads, na_dh))
            na_v.append(p_ab[:n_p, 2 * lru_w + 2 * na_w:].reshape(batch, seq, na_heads, na_dh))
        else:
            w_cd = w_in_cd[jdx]
            w_cd = jnp.concatenate([w_cd, w_cd[:, -rope_w:]], axis=1).astype(BF16)
            p_cd = _inproj(x, mods, layer, norm_gain[layer, 1], w_cd, w_cd.shape[1] // 3, npt, tps)
            w_uq = mla_w_uq[jdx].reshape(rank, mla_heads, mla_qk)
            w_uq = jnp.concatenate([w_uq[:, :, :MLA_NOPE].reshape(rank, n_nope),
                                    w_uq[:, :, MLA_NOPE:].reshape(rank, n_rope)], axis=1).astype(BF16)
            w_ukv = jnp.concatenate([mla_w_uk[jdx], mla_w_uv[jdx]], axis=1).astype(BF16)
            prm = {
                "ones_bd": ones_bd,
                "q_gain": jnp.tile(gqa_q_gain[jdx], q_w // gqa_dh).reshape(1, q_w),
                "k_gain": jnp.tile(gqa_k_gain[jdx], gqa_kv).reshape(1, kv_w),
                "mla_q_gain": mla_q_gain[jdx].reshape(1, rank),
                "mla_kv_gain": mla_kv_gain[jdx].reshape(1, rank),
                "w_uq": w_uq, "w_ukv": w_ukv,
            }
            dims = (q_w, kv_w, rank, gqa_dh, n_nope, n_rope, n_v)
            qp, kp, qnp_, qrp, ckvp, knp_, vmp, krp = _cd_post(p_cd, prm, 0, n_p, None, dims)
            qs, ks_, qns, qrs, _, kns, vms, krs = _cd_post(p_cd, prm, n_p, n_s, rope_tabs, dims)
            v_col = (q_w + kv_w) // V7X_LANES
            group_w = q_w // gqa_kv
            yc_p = _gqa(qp, kp, p_cd, v_col, None, None, batch, seq, 0, gqa_kv, group_w, gqa_dh ** -0.5)
            yc_s = _gqa(qs, ks_, p_cd, v_col, cache_gqa_k[:, jdx].reshape(dec_batch, past, kv_w),
                        cache_gqa_v[:, jdx].reshape(dec_batch, past, kv_w), dec_batch, dec_seq, n_p, gqa_kv,
                        group_w, gqa_dh ** -0.5)
            knc, vmc = _ckv_up(cache_mla_ckv[:, jdx].reshape(dec_batch * past, rank), w_ukv, n_nope)
            krc = cache_mla_krope[:, jdx].reshape(dec_batch * past, rope_w)
            krc = jnp.concatenate([krc, krc], axis=1)
            yd_p = _mla(qnp_, qrp, knp_, krp, vmp, None, batch, seq, mla_heads, mla_qk ** -0.5)
            yd_s = _mla(qns, qrs, kns, krs, vms, (knc, krc, vmc, past), dec_batch, dec_seq, mla_heads,
                        mla_qk ** -0.5)
            x = _outproj(x, mods, layer, yc_p, yc_s, yd_p, yd_s, w_out_cd[jdx].astype(BF16), npt, tps)
            gq_k.append(kp.reshape(batch, seq, gqa_kv, gqa_dh))
            gq_v.append(p_cd[:n_p, q_w + kv_w:q_w + 2 * kv_w].reshape(batch, seq, gqa_kv, gqa_dh))
            ml_c.append(ckvp.reshape(batch, seq, rank))
            ml_r.append(krp[:, :rope_w].reshape(batch, seq, rope_w))
        x = _ffn(x, mods, layer, norm_gain[layer, 2], w_in_bf[1], w_out_bf[1], 6, npt, tps)

    y_prompt = _final_norm(x, final_gain, 0, n_p).reshape(batch, seq, d)
    y_sample = _final_norm(x, final_gain, n_p, n_s).reshape(dec_batch, dec_seq, d)
    stack = lambda xs: jnp.stack(xs, axis=1)
    return (y_prompt, y_sample, stack(st_f), stack(st_b), stack(na_k), stack(na_v), stack(gq_k), stack(gq_v),
            stack(ml_c), stack(ml_r))
```

```python
import functools
import math

import numpy as np

import jax
import jax.numpy as jnp
from jax import lax
from jax.experimental import pallas as pl
from jax.experimental.pallas import tpu as pltpu

F32 = jnp.float32
BF16 = jnp.bfloat16

EPS = 1e-6
N_MOD = 9
GRID_W = 64
NA_KR = 8
NA_KC = 16
ROPE_THETA = 10000.0
LRU_C = 8.0
CONV_W = 4
MLA_NOPE = 128
MLA_V = 128
NEG_BIAS = -1e30

V7X_LANES = 128
V7X_SUBLANES = 8
MOD_ROWS = 16
TOKEN_TILE = 512
PROJ_TILE = 1024
FF_TILE = 512
Q_CHUNK = 256
NA_Q_CHUNK = 512
NA_WIN = 768
LRU_COLS = 256


def _params(sem, vmem_mib):
    return pltpu.CompilerParams(dimension_semantics=sem, vmem_limit_bytes=vmem_mib * 1024 * 1024)


def _silu(x):
    return x * jax.nn.sigmoid(x)


def _rms(x, gain):
    ms = jnp.mean(x * x, axis=-1, keepdims=True)
    return x * lax.rsqrt(ms + EPS) * gain


def _mod_row(i, n_prompt_tiles, tiles_per_seq):
    return jnp.where(i < n_prompt_tiles, 0, 1 + (i - n_prompt_tiles) // tiles_per_seq)


def _adaln(x_ref, mod_ref, gain_ref, row, mod_base):
    d = x_ref.shape[1]
    shift = mod_ref[pl.ds(row, 1), pl.ds(mod_base * d, d)]
    scale = mod_ref[pl.ds(row, 1), pl.ds((mod_base + 1) * d, d)]
    return (_rms(x_ref[...], gain_ref[...]) * (1 + scale) + shift).astype(BF16)


def _is_pow2(v):
    return math.frexp(v)[0] == 0.5


def _mod_kernel(cond_ref, w_ref, b_ref, o_ref):
    s = _silu(cond_ref[...]).astype(BF16)
    o_ref[...] = jnp.dot(s, w_ref[...].astype(BF16), preferred_element_type=F32) + b_ref[...]


def _modulation(cond, w_mod, b_mod):
    depth, d, n = w_mod.shape
    tn = 1024
    return pl.pallas_call(
        _mod_kernel,
        grid=(depth, n // tn),
        in_specs=[
            pl.BlockSpec((MOD_ROWS, d), lambda l, j: (0, 0)),
            pl.BlockSpec((None, d, tn), lambda l, j: (l, 0, j)),
            pl.BlockSpec((None, 1, tn), lambda l, j: (l, 0, j)),
        ],
        out_specs=pl.BlockSpec((None, MOD_ROWS, tn), lambda l, j: (l, 0, j)),
        out_shape=jax.ShapeDtypeStruct((depth, MOD_ROWS, n), F32),
        compiler_params=_params(("parallel", "parallel"), 40),
        name="modulation",
    )(cond, w_mod, b_mod.reshape(depth, 1, n))


def _ffn_kernel(x_ref, mod_ref, gain_ref, wg_ref, wu_ref, wo_ref, o_ref, h_ref, *,
                mod_base, n_prompt_tiles, tiles_per_seq):
    i = pl.program_id(0)
    j = pl.program_id(1)
    d = x_ref.shape[1]
    row = _mod_row(i, n_prompt_tiles, tiles_per_seq)

    @pl.when(j == 0)
    def _():
        h_ref[...] = _adaln(x_ref, mod_ref, gain_ref, row, mod_base)
        o_ref[...] = jnp.zeros_like(o_ref)

    h = h_ref[...]
    g = jnp.dot(h, wg_ref[...], preferred_element_type=F32)
    u = jnp.dot(h, wu_ref[...], preferred_element_type=F32)
    act = (_silu(g) * u).astype(BF16)
    o_ref[...] += jnp.dot(act, wo_ref[...], preferred_element_type=F32)

    @pl.when(j == pl.num_programs(1) - 1)
    def _():
        gate = mod_ref[pl.ds(row, 1), pl.ds((mod_base + 2) * d, d)]
        o_ref[...] = x_ref[...] + (0.5 * gate) * o_ref[...]


def _ffn(x, mods, gains, w_in, w_out, layer, which, n_prompt_rows, seq_rows):
    t, d = x.shape
    dff = w_out.shape[2]
    nj = dff // FF_TILE
    mod_base = 6 * which
    kern = functools.partial(_ffn_kernel, mod_base=mod_base, n_prompt_tiles=n_prompt_rows // TOKEN_TILE,
                             tiles_per_seq=seq_rows // TOKEN_TILE)
    return pl.pallas_call(
        kern,
        grid=(t // TOKEN_TILE, nj),
        in_specs=[
            pl.BlockSpec((TOKEN_TILE, d), lambda i, j: (i, 0)),
            pl.BlockSpec((None, MOD_ROWS, N_MOD * d), lambda i, j: (layer, 0, 0)),
            pl.BlockSpec((None, 1, d), lambda i, j: (3 * layer + 2 * which, 0, 0)),
            pl.BlockSpec((None, None, d, FF_TILE), lambda i, j: (layer, which, 0, j)),
            pl.BlockSpec((None, None, d, FF_TILE), lambda i, j: (layer, which, 0, j + nj)),
            pl.BlockSpec((None, None, FF_TILE, d), lambda i, j: (layer, which, j, 0)),
        ],
        out_specs=pl.BlockSpec((TOKEN_TILE, d), lambda i, j: (i, 0)),
        out_shape=jax.ShapeDtypeStruct((t, d), F32),
        scratch_shapes=[pltpu.VMEM((TOKEN_TILE, d), BF16)],
        compiler_params=_params(("parallel", "arbitrary"), 48),
        name="ffn",
    )(x, mods, gains, w_in, w_in, w_out)


def _inproj_kernel(x_ref, mod_ref, gain_ref, w_ref, o_ref, h_ref, *, n_prompt_tiles, tiles_per_seq):
    @pl.when(pl.program_id(1) == 0)
    def _():
        row = _mod_row(pl.program_id(0), n_prompt_tiles, tiles_per_seq)
        h_ref[...] = _adaln(x_ref, mod_ref, gain_ref, row, 3)

    o_ref[...] = jnp.dot(h_ref[...], w_ref[...], preferred_element_type=F32)


def _inproj(x, mods, gains, w, layer, w_idx, tn, n_prompt_rows, seq_rows):
    t, d = x.shape
    n = w.shape[2]
    tm = PROJ_TILE
    kern = functools.partial(_inproj_kernel, n_prompt_tiles=n_prompt_rows // tm, tiles_per_seq=seq_rows // tm)
    return pl.pallas_call(
        kern,
        grid=(t // tm, n // tn),
        in_specs=[
            pl.BlockSpec((tm, d), lambda i, j: (i, 0)),
            pl.BlockSpec((None, MOD_ROWS, N_MOD * d), lambda i, j: (layer, 0, 0)),
            pl.BlockSpec((None, 1, d), lambda i, j: (3 * layer + 1, 0, 0)),
            pl.BlockSpec((None, d, tn), lambda i, j: (w_idx, 0, j)),
        ],
        out_specs=pl.BlockSpec((tm, tn), lambda i, j: (i, j)),
        out_shape=jax.ShapeDtypeStruct((t, n), F32),
        scratch_shapes=[pltpu.VMEM((tm, d), BF16)],
        compiler_params=_params(("parallel", "arbitrary"), 52),
        name="inproj",
    )(x, mods, gains, w)


def _outproj_kernel(x_ref, mod_ref, ap_ref, as_ref, bp_ref, bs_ref, w_ref, o_ref, *,
                    n_prompt_tiles, tiles_per_seq):
    i = pl.program_id(0)
    d = x_ref.shape[1]
    half = ap_ref.shape[1]
    is_prompt = i < n_prompt_tiles
    ya = jnp.where(is_prompt, ap_ref[...], as_ref[...])
    yb = jnp.where(is_prompt, bp_ref[...], bs_ref[...])
    y = jnp.dot(ya, w_ref[pl.ds(0, half), :], preferred_element_type=F32)
    y = y + jnp.dot(yb, w_ref[pl.ds(half, half), :], preferred_element_type=F32)
    row = _mod_row(i, n_prompt_tiles, tiles_per_seq)
    gate = mod_ref[pl.ds(row, 1), pl.ds(5 * d, d)]
    o_ref[...] = x_ref[...] + gate * y


def _outproj(x, mods, layer, ya_p, ya_s, yb_p, yb_s, w, w_idx, n_prompt_rows, seq_rows):
    t, d = x.shape
    half = ya_p.shape[1]
    npt = n_prompt_rows // TOKEN_TILE
    kern = functools.partial(_outproj_kernel, n_prompt_tiles=npt, tiles_per_seq=seq_rows // TOKEN_TILE)
    p_map = lambda i: (jnp.minimum(i, npt - 1), 0)
    s_map = lambda i: (jnp.maximum(i - npt, 0), 0)
    return pl.pallas_call(
        kern,
        grid=(t // TOKEN_TILE,),
        in_specs=[
            pl.BlockSpec((TOKEN_TILE, d), lambda i: (i, 0)),
            pl.BlockSpec((None, MOD_ROWS, N_MOD * d), lambda i: (layer, 0, 0)),
            pl.BlockSpec((TOKEN_TILE, half), p_map),
            pl.BlockSpec((TOKEN_TILE, half), s_map),
            pl.BlockSpec((TOKEN_TILE, half), p_map),
            pl.BlockSpec((TOKEN_TILE, half), s_map),
            pl.BlockSpec((None, 2 * half, d), lambda i: (w_idx, 0, 0)),
        ],
        out_specs=pl.BlockSpec((TOKEN_TILE, d), lambda i: (i, 0)),
        out_shape=jax.ShapeDtypeStruct((t, d), F32),
        compiler_params=_params(("parallel",), 48),
        name="outproj",
    )(x, mods, ya_p, ya_s, yb_p, yb_s, w)


def _final_norm_kernel(x_ref, g_ref, o_ref):
    o_ref[...] = _rms(x_ref[...], g_ref[...])


def _final_norm(x, gain, row_start, rows):
    d = x.shape[1]
    off = row_start // TOKEN_TILE
    return pl.pallas_call(
        _final_norm_kernel,
        grid=(rows // TOKEN_TILE,),
        in_specs=[pl.BlockSpec((TOKEN_TILE, d), lambda i: (i + off, 0)),
                  pl.BlockSpec((1, d), lambda i: (0, 0))],
        out_specs=pl.BlockSpec((TOKEN_TILE, d), lambda i: (i, 0)),
        out_shape=jax.ShapeDtypeStruct((rows, d), F32),
        compiler_params=_params(("parallel",), 32),
        name="final_norm",
    )(x, gain.reshape(1, d))


def _group_roll(x, step):
    rows, w = x.shape
    x3 = x.reshape(rows // V7X_SUBLANES, V7X_SUBLANES, w)
    return pltpu.roll(x3, step, axis=1).reshape(rows, w)


def _lru_kernel(xa_ref, ga_ref, h0f_ref, h0b_ref, cw_ref, cb_ref, wa_ref, ba_ref, wx_ref, bx_ref,
                lam_ref, y_ref, lf_ref, lb_ref, a_scr, u_scr, hf_scr, hb_scr):
    seq, w = xa_ref.shape
    groups = seq // V7X_SUBLANES
    xa = xa_ref[...]
    row = lax.broadcasted_iota(jnp.int32, (seq, w), 0)
    sub = row & (V7X_SUBLANES - 1)

    def tap(offset):
        if offset == 0:
            return xa
        shifted = pltpu.roll(xa, (-offset) % seq, axis=0)
        valid = (row + offset >= 0) & (row + offset < seq)
        return jnp.where(valid, shifted, 0.0)

    xc = cb_ref[...]
    for j in range(CONV_W):
        xc = xc + tap(j - CONV_W // 2) * cw_ref[pl.ds(j, 1), :]
    xcb = xc.astype(BF16)

    for direction, (h0_ref, h_scr, last_ref) in enumerate(((h0f_ref, hf_scr, lf_ref),
                                                            (h0b_ref, hb_scr, lb_ref))):
        reverse = direction == 1
        r = jax.nn.sigmoid(jnp.dot(xcb, wa_ref[direction], preferred_element_type=F32) + ba_ref[direction])
        gi = jax.nn.sigmoid(jnp.dot(xcb, wx_ref[direction], preferred_element_type=F32) + bx_ref[direction])
        neg_lam = -lam_ref[direction]
        softplus = jnp.maximum(neg_lam, 0.0) + jnp.log1p(jnp.exp(-jnp.abs(neg_lam)))
        log_a = -LRU_C * r * softplus
        a = jnp.exp(log_a)
        m2 = -jnp.tanh(log_a) * (a * a + 1.0)
        u = jnp.where(m2 > 0.0, m2 * lax.rsqrt(m2), 0.0) * (gi * xc)
        for step in (1, 2, 4):
            if reverse:
                a_nb = _group_roll(a, V7X_SUBLANES - step)
                u_nb = _group_roll(u, V7X_SUBLANES - step)
                valid = sub < V7X_SUBLANES - step
            else:
                a_nb = _group_roll(a, step)
                u_nb = _group_roll(u, step)
                valid = sub >= step
            u = a * jnp.where(valid, u_nb, 0.0) + u
            a = a * jnp.where(valid, a_nb, 1.0)
        a_scr[...] = a
        u_scr[...] = u

        def carry_step(g, carry, reverse=reverse, h_scr=h_scr):
            gg = groups - 1 - g if reverse else g
            off = pl.multiple_of(gg * V7X_SUBLANES, V7X_SUBLANES)
            h = a_scr[pl.ds(off, V7X_SUBLANES), :] * carry + u_scr[pl.ds(off, V7X_SUBLANES), :]
            h_scr[pl.ds(off, V7X_SUBLANES), :] = h
            return h[0:1] if reverse else h[V7X_SUBLANES - 1:V7X_SUBLANES]

        last_ref[...] = lax.fori_loop(0, groups, carry_step, h0_ref[...], unroll=4)

    y_ref[...] = ((hf_scr[...] + hb_scr[...]) * jax.nn.gelu(ga_ref[...])).astype(BF16)


def _lru(p_ab, h0f, h0b, prm, seq_len, n_seq, row_start):
    lru_w = h0f.shape[-1]
    cw = LRU_COLS
    ncb = lru_w // cw
    sb = row_start // seq_len
    vec = lambda: pl.BlockSpec((None, 1, cw), lambda s, c: (s, 0, c))
    par2 = lambda: pl.BlockSpec((None, 2, 1, cw), lambda s, c: (prm["idx"], 0, 0, c))
    gate = lambda: pl.BlockSpec((None, 2, None, cw, cw), lambda s, c: (prm["idx"], 0, c, 0, 0))
    return pl.pallas_call(
        _lru_kernel,
        grid=(n_seq, ncb),
        in_specs=[
            pl.BlockSpec((seq_len, cw), lambda s, c: (s + sb, c)),
            pl.BlockSpec((seq_len, cw), lambda s, c: (s + sb, c + ncb)),
            vec(), vec(),
            pl.BlockSpec((None, CONV_W, cw), lambda s, c: (prm["idx"], 0, c)),
            pl.BlockSpec((None, 1, cw), lambda s, c: (prm["idx"], 0, c)),
            gate(), par2(), gate(), par2(), par2(),
        ],
        out_specs=[
            pl.BlockSpec((seq_len, cw), lambda s, c: (s, c)),
            vec(), vec(),
        ],
        out_shape=[
            jax.ShapeDtypeStruct((n_seq * seq_len, lru_w), BF16),
            jax.ShapeDtypeStruct((n_seq, 1, lru_w), F32),
            jax.ShapeDtypeStruct((n_seq, 1, lru_w), F32),
        ],
        scratch_shapes=[pltpu.VMEM((seq_len, cw), F32)] * 4,
        compiler_params=_params(("parallel", "parallel"), 40),
        name="lru",
    )(p_ab, p_ab, h0f.reshape(n_seq, 1, lru_w), h0b.reshape(n_seq, 1, lru_w), prm["conv_w"],
      prm["conv_b"], prm["wa"], prm["ba"], prm["wx"], prm["bx"], prm["lam"])


def _block_diag_tiles(w, tile):
    n, two, nb, bw, _ = w.shape
    per = tile // bw
    w = w.reshape(n, two, nb // per, per, bw, bw)
    eye = jnp.eye(per, dtype=w.dtype)
    out = jnp.einsum("ndgpij,pq->ndgpiqj", w, eye)
    return out.reshape(n, two, nb // per, tile, tile)


def _softmax_attend(q, ks, vs, biases, scale):
    scores = []
    for k, b in zip(ks, biases):
        s = lax.dot_general(q, k, (((1,), (1,)), ((), ())), preferred_element_type=F32)
        if scale is not None:
            s = s * scale
        scores.append(s if b is None else s + b)
    m = scores[0].max(axis=-1, keepdims=True)
    for s in scores[1:]:
        m = jnp.maximum(m, s.max(axis=-1, keepdims=True))
    denom = None
    out = None
    for s, v in zip(scores, vs):
        e = jnp.exp(s - m)
        part = e.sum(axis=-1, keepdims=True)
        denom = part if denom is None else denom + part
        o = jnp.dot(e.astype(BF16), v, preferred_element_type=F32)
        out = o if out is None else out + o
    return out / denom


def _upper_half(shape):
    lane = lax.broadcasted_iota(jnp.int32, shape, len(shape) - 1)
    return lane % V7X_LANES >= V7X_LANES // 2


def _split_pair(q):
    up = _upper_half(q.shape)
    zero = jnp.zeros_like(q)
    return jnp.concatenate([jnp.where(up, zero, q), jnp.where(up, q, zero)], axis=0)


def _merge_pair(o):
    m = o.shape[0] // 2
    return jnp.where(_upper_half((m, o.shape[1])), o[m:], o[:m])


def _dup_head(block, use_high):
    swapped = pltpu.roll(block, V7X_LANES // 2, axis=1)
    keep = _upper_half(block.shape) == use_high
    return jnp.where(keep, block, swapped)


def _fold_scale(q, scale):
    return (q * scale, None) if _is_pow2(scale) else (q, scale)


def _na_ctx_kernel(q_ref, k_ref, v_ref, o_ref, *, scale):
    for pair in range(q_ref.shape[1] // V7X_LANES):
        sl = pl.ds(pair * V7X_LANES, V7X_LANES)
        q, sc = _fold_scale(q_ref[:, sl], scale)
        o = _softmax_attend(_split_pair(q.astype(BF16)), [k_ref[:, sl].astype(BF16)],
                            [v_ref[:, sl].astype(BF16)], [None], sc)
        o_ref[:, sl] = _merge_pair(o).astype(o_ref.dtype)


def _na_ctx(p_ab, n_seq, seq_len, q_blk, na_w, scale):
    blk = lambda col: pl.BlockSpec((seq_len, na_w), lambda b: (b, col))
    return pl.pallas_call(
        functools.partial(_na_ctx_kernel, scale=scale),
        grid=(n_seq,),
        in_specs=[blk(q_blk), blk(q_blk + 1), blk(q_blk + 2)],
        out_specs=pl.BlockSpec((seq_len, na_w), lambda b: (b, 0)),
        out_shape=jax.ShapeDtypeStruct((n_seq * seq_len, na_w), BF16),
        compiler_params=_params(("parallel",), 40),
        name="na_ctx",
    )(p_ab, p_ab, p_ab)


def _na_lat_kernel(q_ref, k_ref, v_ref, kc_ref, vc_ref, bias_ref, o_ref, *, scale):
    lq = q_ref.shape[0]
    n_chunks = lq // NA_Q_CHUNK
    kcb = kc_ref[...].astype(BF16)
    vcb = vc_ref[...].astype(BF16)
    for c in range(n_chunks):
        win = 0 if c < n_chunks // 2 else lq - NA_WIN
        kb = k_ref[pl.ds(win, NA_WIN), :].astype(BF16)
        vb = v_ref[pl.ds(win, NA_WIN), :].astype(BF16)
        rows = pl.ds(c * NA_Q_CHUNK, NA_Q_CHUNK)
        q, sc = _fold_scale(q_ref[rows, :], scale)
        o = _softmax_attend(_split_pair(q.astype(BF16)), [kb, kcb], [vb, vcb], [bias_ref[c], None], sc)
        o_ref[rows, :] = _merge_pair(o).astype(o_ref.dtype)


def _na_lat(p_ab, cache_k, cache_v, idx, bias, n_seq, seq_len, row_start, q_col, n_pairs, scale):
    sb = row_start // seq_len
    past = cache_k.shape[2]
    blk = lambda col: pl.BlockSpec((seq_len, V7X_LANES), lambda h, b: (b + sb, col + h))
    cblk = lambda: pl.BlockSpec((None, None, past, V7X_LANES), lambda h, b: (b, idx, 0, h))
    n_chunks = seq_len // NA_Q_CHUNK
    return pl.pallas_call(
        functools.partial(_na_lat_kernel, scale=scale),
        grid=(n_pairs, n_seq),
        in_specs=[blk(q_col), blk(q_col + n_pairs), blk(q_col + 2 * n_pairs), cblk(), cblk(),
                  pl.BlockSpec((None, n_chunks, 2 * NA_Q_CHUNK, NA_WIN), lambda h, b: (h, 0, 0, 0))],
        out_specs=pl.BlockSpec((seq_len, V7X_LANES), lambda h, b: (b, h)),
        out_shape=jax.ShapeDtypeStruct((n_seq * seq_len, n_pairs * V7X_LANES), BF16),
        compiler_params=_params(("parallel", "arbitrary"), 56),
        name="na_lat",
    )(p_ab, p_ab, p_ab, cache_k, cache_v, bias)


def _na_bias_blocks(tab, seq_len):
    n_heads = tab.shape[0]
    rows_n = seq_len // GRID_W
    kr = min(NA_KR, rows_n)
    cols = jnp.arange(GRID_W)
    c_start = jnp.clip(cols - NA_KC // 2, 0, GRID_W - NA_KC)
    col_ok = (cols[None, :] >= c_start[:, None]) & (cols[None, :] < c_start[:, None] + NA_KC)
    rel_c = jnp.clip(cols[None, :] - cols[:, None] + (NA_KC - 1), 0, 2 * NA_KC - 2)
    tiles = jnp.where(col_ok[None, None], tab[:, :, rel_c], NEG_BIAS)
    tiles = jnp.concatenate([tiles, jnp.full((n_heads, 1, GRID_W, GRID_W), NEG_BIAS, F32)], axis=1)
    rows = jnp.arange(rows_n)
    r_start = jnp.clip(rows - kr // 2, 0, rows_n - kr)
    rows_per_chunk = NA_Q_CHUNK // GRID_W
    n_chunks = rows_n // rows_per_chunk
    win_rows = NA_WIN // GRID_W
    base = jnp.where(rows // rows_per_chunk < n_chunks // 2, 0, rows_n - win_rows)
    k_row = base[:, None] + jnp.arange(win_rows)[None, :]
    row_ok = (k_row >= r_start[:, None]) & (k_row < r_start[:, None] + kr)
    idx = jnp.where(row_ok, k_row - rows[:, None] + (NA_KR - 1), 2 * NA_KR - 1)
    g = tiles[:, idx]
    g = g.reshape(n_heads // 2, 2, n_chunks, rows_per_chunk, win_rows, GRID_W, GRID_W)
    g = g.transpose(0, 2, 1, 3, 5, 4, 6)
    return g.reshape(n_heads // 2, n_chunks, 2 * NA_Q_CHUNK, NA_WIN)


def _head_rms(x, ones_bd, gain, head_dim):
    sq = x * x
    hi = sq.astype(BF16)
    lo = (sq - hi.astype(F32)).astype(BF16)
    ss = jnp.dot(hi, ones_bd, preferred_element_type=F32) + jnp.dot(lo, ones_bd, preferred_element_type=F32)
    return x * lax.rsqrt(ss * (1.0 / head_dim) + EPS) * gain


def _rope(x, cos, sin_lo, sin_hi):
    w = x.shape[1]
    reps = w // V7X_LANES
    tile = lambda t: t if reps == 1 else jnp.concatenate([t] * reps, axis=1)
    quarter = 16
    return (x * tile(cos) + pltpu.roll(x, w - quarter, axis=1) * tile(sin_lo)
            + pltpu.roll(x, quarter, axis=1) * tile(sin_hi))


def _cd_post_kernel(p_ref, ones_ref, qg_ref, kg_ref, mqg_ref, mkg_ref, wuq_ref, wukv_ref, *rest,
                    rope, head_dim, q_w, kv_w, rank, q_scale):
    if rope:
        cos_ref, slo_ref, shi_ref = rest[:3]
        rest = rest[3:]
        tabs = (cos_ref[...], slo_ref[...], shi_ref[...])
    q_ref, k_ref, qn_ref, qr_ref, ckv_ref, kn_ref, vm_ref, kr_ref = rest
    c0 = 0
    qc = p_ref[:, pl.ds(c0, q_w)]
    c0 += q_w
    kc = p_ref[:, pl.ds(c0, kv_w)]
    c0 += 2 * kv_w
    qa = p_ref[:, pl.ds(c0, rank)]
    c0 += rank
    ckv = p_ref[:, pl.ds(c0, rank)]
    c0 += rank
    kr = p_ref[:, pl.ds(c0, V7X_LANES)]

    qc = _head_rms(qc, ones_ref[...], qg_ref[...], head_dim)
    kc = _head_rms(kc, ones_ref[pl.ds(0, kv_w), pl.ds(0, kv_w)], kg_ref[...], head_dim)
    qd = jnp.dot(_rms(qa, mqg_ref[...]).astype(BF16), wuq_ref[...], preferred_element_type=F32)
    n_nope = qn_ref.shape[1]
    qn = qd[:, :n_nope]
    qr = qd[:, n_nope:]
    ckv = _rms(ckv, mkg_ref[...])
    if rope:
        qc = _rope(qc, *tabs)
        kc = _rope(kc, *tabs)
        qr = _rope(qr, *tabs)
        kr = _rope(kr, *tabs)
    q_ref[...] = (qc * q_scale).astype(BF16)
    k_ref[...] = kc
    qn_ref[...] = qn.astype(BF16)
    qr_ref[...] = qr.astype(BF16)
    ckv_ref[...] = ckv
    kv = jnp.dot(ckv.astype(BF16), wukv_ref[...], preferred_element_type=F32)
    kn_ref[...] = kv[:, :n_nope].astype(BF16)
    vm_ref[...] = kv[:, n_nope:].astype(BF16)
    kr_ref[...] = kr


def _cd_post(p_cd, prm, row_start, rows, rope_tabs, dims, q_scale):
    q_w, kv_w, rank, head_dim, n_nope, n_rope, n_v = dims
    tm = 256
    off = row_start // tm
    n_in = p_cd.shape[1]
    idx = prm["idx"]
    rope = rope_tabs is not None
    const = lambda shape: pl.BlockSpec(shape, lambda i: (0,) * len(shape))
    layer = lambda shape: pl.BlockSpec((None,) + shape, lambda i: (idx,) + (0,) * len(shape))
    in_specs = [
        pl.BlockSpec((tm, n_in), lambda i: (i + off, 0)),
        const((q_w, q_w)), layer((1, q_w)), layer((1, kv_w)), layer((1, rank)), layer((1, rank)),
        layer((rank, n_nope + n_rope)), layer((rank, n_nope + n_v)),
    ]
    args = [p_cd, prm["ones_bd"], prm["q_gain"], prm["k_gain"], prm["mla_q_gain"], prm["mla_kv_gain"],
            prm["w_uq"], prm["w_ukv"]]
    if rope:
        seq_tiles = rope_tabs[0].shape[0] // tm
        in_specs += [pl.BlockSpec((tm, V7X_LANES), lambda i: (i % seq_tiles, 0))] * 3
        args += list(rope_tabs)
    widths = [(q_w, BF16), (kv_w, F32), (n_nope, BF16), (n_rope, BF16), (rank, F32), (n_nope, BF16),
              (n_v, BF16), (V7X_LANES, F32)]
    kern = functools.partial(_cd_post_kernel, rope=rope, head_dim=head_dim, q_w=q_w, kv_w=kv_w, rank=rank,
                             q_scale=q_scale)
    return pl.pallas_call(
        kern,
        grid=(rows // tm,),
        in_specs=in_specs,
        out_specs=[pl.BlockSpec((tm, w), lambda i: (i, 0)) for w, _ in widths],
        out_shape=[jax.ShapeDtypeStruct((rows, w), dt) for w, dt in widths],
        compiler_params=_params(("parallel",), 48),
        name="cd_post",
    )(*args)


def _rope_tables(seq_len, head_dim):
    half = head_dim // 2
    nf = half // 2
    t = np.arange(seq_len)
    inv_freq = (1.0 / (ROPE_THETA ** (np.arange(nf, dtype=np.float32) / nf))).astype(np.float32)
    zeros = np.zeros((seq_len, nf), np.float32)
    cos, slo, shi = [], [], []
    for pos in (t // GRID_W, t % GRID_W):
        ang = pos.astype(np.float32)[:, None] * inv_freq[None, :]
        c, s = np.cos(ang).astype(np.float32), np.sin(ang).astype(np.float32)
        cos += [c, c]
        slo += [-s, zeros]
        shi += [zeros, s]
    reps = V7X_LANES // head_dim
    return tuple(jnp.asarray(np.tile(np.concatenate(x, axis=1), (1, reps))) for x in (cos, slo, shi))


def _ckv_up_kernel(c_ref, w_ref, kn_ref, vm_ref):
    kv = jnp.dot(c_ref[...].astype(BF16), w_ref[...], preferred_element_type=F32)
    n = kn_ref.shape[1]
    kn_ref[...] = kv[:, :n].astype(BF16)
    vm_ref[...] = kv[:, n:].astype(BF16)


def _ckv_up(ckv, w_ukv, idx, n_nope):
    n_seq, _, tm, rank = ckv.shape
    rows = n_seq * tm
    n = w_ukv.shape[2]
    return pl.pallas_call(
        _ckv_up_kernel,
        grid=(rows // tm,),
        in_specs=[pl.BlockSpec((None, None, tm, rank), lambda i: (i, idx, 0, 0)),
                  pl.BlockSpec((None, rank, n), lambda i: (idx, 0, 0))],
        out_specs=[pl.BlockSpec((tm, n_nope), lambda i: (i, 0)), pl.BlockSpec((tm, n - n_nope), lambda i: (i, 0))],
        out_shape=[jax.ShapeDtypeStruct((rows, n_nope), BF16), jax.ShapeDtypeStruct((rows, n - n_nope), BF16)],
        compiler_params=_params(("parallel",), 32),
        name="ckv_up",
    )(ckv, w_ukv)


def _gqa_kernel(q_ref, k_ref, v_ref, *rest, scale, has_cache, n_kv):
    if has_cache:
        kc_ref, vc_ref, o_ref = rest
    else:
        (o_ref,) = rest
    lq = q_ref.shape[0]
    group_w = q_ref.shape[1] // n_kv
    n_pairs = group_w // V7X_LANES
    for h in range(n_kv):
        blk = pl.ds((h // 2) * V7X_LANES, V7X_LANES)
        high = h % 2 == 1
        ks = [_dup_head(k_ref[:, blk], high).astype(BF16)]
        vs = [_dup_head(v_ref[:, blk], high).astype(BF16)]
        if has_cache:
            ks.append(_dup_head(kc_ref[:, blk], high).astype(BF16))
            vs.append(_dup_head(vc_ref[:, blk], high).astype(BF16))

        def chunk(c, carry, h=h, ks=ks, vs=vs):
            rows = pl.ds(pl.multiple_of(c * Q_CHUNK, Q_CHUNK), Q_CHUNK)
            cols = [pl.ds(h * group_w + p * V7X_LANES, V7X_LANES) for p in range(n_pairs)]
            q = jnp.concatenate([_split_pair(q_ref[rows, cl]) for cl in cols], axis=0)
            o = _softmax_attend(q, ks, vs, [None] * len(ks), scale)
            for p, cl in enumerate(cols):
                o_ref[rows, cl] = _merge_pair(o[2 * p * Q_CHUNK:2 * (p + 1) * Q_CHUNK]).astype(o_ref.dtype)
            return carry

        lax.fori_loop(0, lq // Q_CHUNK, chunk, 0)


def _gqa(q, k, p_cd, v_blk, cache_k, cache_v, idx, n_seq, seq_len, row_start, n_kv, scale):
    sb = row_start // seq_len
    q_w, kv_w = q.shape[1], k.shape[1]
    has_cache = cache_k is not None
    in_specs = [
        pl.BlockSpec((seq_len, q_w), lambda b: (b, 0)),
        pl.BlockSpec((seq_len, kv_w), lambda b: (b, 0)),
        pl.BlockSpec((seq_len, kv_w), lambda b: (b + sb, v_blk)),
    ]
    args = [q, k, p_cd]
    if has_cache:
        past = cache_k.shape[2]
        in_specs += [pl.BlockSpec((None, None, past, kv_w), lambda b: (b, idx, 0, 0))] * 2
        args += [cache_k, cache_v]
    return pl.pallas_call(
        functools.partial(_gqa_kernel, scale=scale, has_cache=has_cache, n_kv=n_kv),
        grid=(n_seq,),
        in_specs=in_specs,
        out_specs=pl.BlockSpec((seq_len, q_w), lambda b: (b, 0)),
        out_shape=jax.ShapeDtypeStruct((n_seq * seq_len, q_w), BF16),
        compiler_params=_params(("parallel",), 56),
        name="gqa",
    )(*args)


def _mla_kernel(qn_ref, qr_ref, kn_ref, kr_ref, v_ref, *rest, scale, has_cache):
    if has_cache:
        knc_ref, krc_ref, vc_ref, o_ref = rest
        krc = krc_ref[...].astype(BF16)
    else:
        (o_ref,) = rest
    lq = qn_ref.shape[0]
    qc = min(2 * Q_CHUNK, lq)
    kr = kr_ref[...].astype(BF16)
    for h in range(qn_ref.shape[1] // V7X_LANES):
        sl = pl.ds(h * V7X_LANES, V7X_LANES)
        pair = pl.ds((h // 2) * V7X_LANES, V7X_LANES)
        high = h % 2 == 1
        ks = [jnp.concatenate([kn_ref[:, sl], kr], axis=1)]
        vs = [v_ref[:, sl]]
        if has_cache:
            ks.append(jnp.concatenate([knc_ref[:, sl], krc], axis=1))
            vs.append(vc_ref[:, sl])

        def chunk(c, carry, sl=sl, pair=pair, high=high, ks=ks, vs=vs):
            rows = pl.ds(pl.multiple_of(c * qc, qc), qc)
            qr = qr_ref[rows, pair]
            qr = jnp.where(_upper_half(qr.shape) == high, qr, jnp.zeros_like(qr))
            q = jnp.concatenate([qn_ref[rows, sl], qr], axis=1)
            o_ref[rows, sl] = _softmax_attend(q, ks, vs, [None] * len(ks), scale).astype(o_ref.dtype)
            return carry

        lax.fori_loop(0, lq // qc, chunk, 0)


def _mla(qn, qr, kn, kr2, vm, cache, n_seq, seq_len, scale):
    has_cache = cache is not None
    full = lambda a, rows: pl.BlockSpec((rows, a.shape[1]), lambda b: (b, 0))
    in_specs = [full(a, seq_len) for a in (qn, qr, kn, kr2, vm)]
    args = [qn, qr, kn, kr2, vm]
    if has_cache:
        knc, krc, vmc, idx = cache
        past = krc.shape[2]
        in_specs += [full(knc, past),
                     pl.BlockSpec((None, None, past, krc.shape[3]), lambda b: (b, idx, 0, 0)),
                     full(vmc, past)]
        args += [knc, krc, vmc]
    return pl.pallas_call(
        functools.partial(_mla_kernel, scale=scale, has_cache=has_cache),
        grid=(n_seq,),
        in_specs=in_specs,
        out_specs=full(vm, seq_len),
        out_shape=jax.ShapeDtypeStruct((n_seq * seq_len, vm.shape[1]), BF16),
        compiler_params=_params(("parallel",), 56),
        name="mla",
    )(*args)


def kernel(x_prompt, x_sample, state_lru_fwd, state_lru_bwd, cache_na_k, cache_na_v, cache_gqa_k, cache_gqa_v, cache_mla_ckv, cache_mla_krope, c, c_ctx, w_mod, b_mod, norm_gain, w_ffn_in, w_ffn_out, w_in_ab, conv_w, conv_b, lru_wa, lru_ba, lru_wx, lru_bx, lru_lambda, na_bias, w_out_ab, w_in_cd, gqa_q_gain, gqa_k_gain, mla_q_gain, mla_kv_gain, mla_w_uq, mla_w_uk, mla_w_uv, w_out_cd, final_gain):
    batch, seq, d = x_prompt.shape
    dec_batch, dec_seq, _ = x_sample.shape
    depth = w_mod.shape[0]
    n_even, n_odd = w_in_ab.shape[0], w_in_cd.shape[0]
    n_p = batch * seq
    n_s = dec_batch * dec_seq
    past = cache_na_k.shape[2]
    lru_w = state_lru_fwd.shape[-1]
    na_heads, na_dh = cache_na_k.shape[3], cache_na_k.shape[4]
    na_w = na_heads * na_dh
    gqa_kv, gqa_dh = cache_gqa_k.shape[3], cache_gqa_k.shape[4]
    kv_w = gqa_kv * gqa_dh
    rank = cache_mla_ckv.shape[-1]
    rope_w = cache_mla_krope.shape[-1]
    q_w = w_in_cd.shape[2] - 2 * kv_w - 2 * rank - rope_w
    mla_heads = mla_w_uk.shape[2] // MLA_NOPE
    n_nope = mla_heads * MLA_NOPE
    n_rope = mla_heads * rope_w
    n_v = mla_heads * MLA_V
    mla_qk = MLA_NOPE + rope_w
    assert mla_q_gain.shape[-1] == rank and 2 * rope_w == V7X_LANES
    assert n_p % dec_seq == 0 and dec_seq % PROJ_TILE == 0 and seq % Q_CHUNK == 0
    assert dec_seq // GRID_W == 2 * NA_KR and na_dh == GRID_W and gqa_dh == GRID_W
    assert w_in_ab.shape[2] == 2 * lru_w + 3 * na_w and lru_w == na_w
    assert (q_w // gqa_kv) % V7X_LANES == 0 and q_w % kv_w == 0

    x = jnp.concatenate([x_prompt.reshape(n_p, d), x_sample.reshape(n_s, d)], axis=0)
    cond = jnp.concatenate([c_ctx[None, :], c, jnp.zeros((MOD_ROWS - 1 - dec_batch, d), F32)], axis=0)
    mods = _modulation(cond, w_mod, b_mod)
    gains = norm_gain.reshape(depth * 3, 1, d)

    w_ffn_in_bf = w_ffn_in.astype(BF16)
    w_ffn_out_bf = w_ffn_out.astype(BF16)
    w_in_ab_bf = w_in_ab.astype(BF16)
    w_out_ab_bf = w_out_ab.astype(BF16)
    w_in_cd_bf = jnp.concatenate([w_in_cd, w_in_cd[:, :, -rope_w:]], axis=2).astype(BF16)
    w_out_cd_bf = w_out_cd.astype(BF16)

    lru_prm = {
        "conv_w": conv_w, "conv_b": conv_b.reshape(n_even, 1, lru_w),
        "wa": _block_diag_tiles(lru_wa, LRU_COLS).astype(BF16),
        "wx": _block_diag_tiles(lru_wx, LRU_COLS).astype(BF16),
        "ba": lru_ba.reshape(n_even, 2, 1, lru_w), "bx": lru_bx.reshape(n_even, 2, 1, lru_w),
        "lam": lru_lambda.reshape(n_even, 2, 1, lru_w),
    }
    zeros_state = jnp.zeros((batch, lru_w), F32)
    na_k_ctx = cache_na_k.reshape(dec_batch, n_even, past, na_w)
    na_v_ctx = cache_na_v.reshape(dec_batch, n_even, past, na_w)

    w_uq = mla_w_uq.reshape(n_odd, rank, mla_heads, mla_qk)
    w_uq = jnp.concatenate([w_uq[..., :MLA_NOPE].reshape(n_odd, rank, n_nope),
                            w_uq[..., MLA_NOPE:].reshape(n_odd, rank, n_rope)], axis=2).astype(BF16)
    gqa_scale = gqa_dh ** -0.5
    q_scale = gqa_scale if _is_pow2(gqa_scale) else 1.0
    cd_prm = {
        "ones_bd": jnp.asarray(np.kron(np.eye(q_w // gqa_dh, dtype=np.float32),
                                       np.ones((gqa_dh, gqa_dh), np.float32)), BF16),
        "q_gain": jnp.tile(gqa_q_gain, (1, q_w // gqa_dh)).reshape(n_odd, 1, q_w),
        "k_gain": jnp.tile(gqa_k_gain, (1, gqa_kv)).reshape(n_odd, 1, kv_w),
        "mla_q_gain": mla_q_gain.reshape(n_odd, 1, rank),
        "mla_kv_gain": mla_kv_gain.reshape(n_odd, 1, rank),
        "w_uq": w_uq,
        "w_ukv": jnp.concatenate([mla_w_uk, mla_w_uv], axis=2).astype(BF16),
    }
    rope_tabs = _rope_tables(dec_seq, gqa_dh)
    gqa_k_ctx = cache_gqa_k.reshape(dec_batch, n_odd, past, kv_w)
    gqa_v_ctx = cache_gqa_v.reshape(dec_batch, n_odd, past, kv_w)
    krope_ctx = jnp.concatenate([cache_mla_krope, cache_mla_krope], axis=-1)

    st_f, st_b, na_k, na_v, gq_k, gq_v, ml_c, ml_r = [], [], [], [], [], [], [], []
    for layer in range(depth):
        jdx = layer // 2
        x = _ffn(x, mods, gains, w_ffn_in_bf, w_ffn_out_bf, layer, 0, n_p, dec_seq)
        if layer % 2 == 0:
            p_ab = _inproj(x, mods, gains, w_in_ab_bf, layer, jdx, 1024, n_p, dec_seq)
            prm = dict(lru_prm, idx=jdx)
            ya_p, lf, lb = _lru(p_ab, zeros_state, zeros_state, prm, seq, batch, 0)
            ya_s, _, _ = _lru(p_ab, state_lru_fwd[:, jdx], state_lru_bwd[:, jdx], prm, dec_seq, dec_batch, n_p)
            scale = na_dh ** -0.5
            yb_p = _na_ctx(p_ab, batch, seq, 2 * lru_w // na_w, na_w, scale)
            bias = _na_bias_blocks(na_bias[jdx], dec_seq)
            yb_s = _na_lat(p_ab, na_k_ctx, na_v_ctx, jdx, bias, dec_batch, dec_seq, n_p,
                           2 * lru_w // V7X_LANES, na_w // V7X_LANES, scale)
            x = _outproj(x, mods, layer, ya_p, ya_s, yb_p, yb_s, w_out_ab_bf, jdx, n_p, dec_seq)
            st_f.append(lf.reshape(batch, lru_w))
            st_b.append(lb.reshape(batch, lru_w))
            na_k.append(p_ab[:n_p, 2 * lru_w + na_w:2 * lru_w + 2 * na_w].reshape(batch, seq, na_heads, na_dh))
            na_v.append(p_ab[:n_p, 2 * lru_w + 2 * na_w:].reshape(batch, seq, na_heads, na_dh))
        else:
            p_cd = _inproj(x, mods, gains, w_in_cd_bf, layer, jdx, w_in_cd_bf.shape[2] // 3, n_p, dec_seq)
            prm = dict(cd_prm, idx=jdx)
            dims = (q_w, kv_w, rank, gqa_dh, n_nope, n_rope, n_v)
            qp, kp, qnp_, qrp, ckvp, knp_, vmp, krp = _cd_post(p_cd, prm, 0, n_p, None, dims, q_scale)
            qs, ks_, qns, qrs, _, kns, vms, krs = _cd_post(p_cd, prm, n_p, n_s, rope_tabs, dims, q_scale)
            v_blk = (q_w + kv_w) // kv_w
            att_scale = None if q_scale != 1.0 else gqa_scale
            yc_p = _gqa(qp, kp, p_cd, v_blk, None, None, jdx, batch, seq, 0, gqa_kv, att_scale)
            yc_s = _gqa(qs, ks_, p_cd, v_blk, gqa_k_ctx, gqa_v_ctx, jdx, dec_batch, dec_seq, n_p,
                        gqa_kv, att_scale)
            knc, vmc = _ckv_up(cache_mla_ckv, cd_prm["w_ukv"], jdx, n_nope)
            yd_p = _mla(qnp_, qrp, knp_, krp, vmp, None, batch, seq, mla_qk ** -0.5)
            yd_s = _mla(qns, qrs, kns, krs, vms, (knc, krope_ctx, vmc, jdx), dec_batch, dec_seq, mla_qk ** -0.5)
            x = _outproj(x, mods, layer, yc_p, yc_s, yd_p, yd_s, w_out_cd_bf, jdx, n_p, dec_seq)
            gq_k.append(kp.reshape(batch, seq, gqa_kv, gqa_dh))
            gq_v.append(p_cd[:n_p, q_w + kv_w:q_w + 2 * kv_w].reshape(batch, seq, gqa_kv, gqa_dh))
            ml_c.append(ckvp.reshape(batch, seq, rank))
            ml_r.append(krp[:, :rope_w].reshape(batch, seq, rope_w))
        x = _ffn(x, mods, gains, w_ffn_in_bf, w_ffn_out_bf, layer, 1, n_p, dec_seq)

    y_prompt = _final_norm(x, final_gain, 0, n_p).reshape(batch, seq, d)
    y_sample = _final_norm(x, final_gain, n_p, n_s).reshape(dec_batch, dec_seq, d)
    stack = lambda xs: jnp.stack(xs, axis=1)
    return (y_prompt, y_sample, stack(st_f), stack(st_b), stack(na_k), stack(na_v), stack(gq_k), stack(gq_v),
            stack(ml_c), stack(ml_r))
```

```python
import functools
import math

import numpy as np

import jax
import jax.numpy as jnp
from jax import lax
from jax.experimental import pallas as pl
from jax.experimental.pallas import tpu as pltpu

F32 = jnp.float32
BF16 = jnp.bfloat16

EPS = 1e-6
N_MOD = 9
GRID_W = 64
NA_KR = 8
NA_KC = 16
ROPE_THETA = 10000.0
LRU_C = 8.0
CONV_W = 4
MLA_NOPE = 128
MLA_V = 128
NEG_BIAS = -1e30

V7X_LANES = 128
V7X_SUBLANES = 8
MOD_ROWS = 16
TOKEN_TILE = 512
PROJ_TILE = 1024
FF_TILE = 512
Q_CHUNK = 256
NA_Q_CHUNK = 256
NA_WIN = 768
LRU_COLS = 256


def _params(sem, vmem_mib):
    return pltpu.CompilerParams(dimension_semantics=sem, vmem_limit_bytes=vmem_mib * 1024 * 1024)


def _silu(x):
    return x * jax.nn.sigmoid(x)


def _rms(x, gain):
    ms = jnp.mean(x * x, axis=-1, keepdims=True)
    return x * lax.rsqrt(ms + EPS) * gain


def _mod_row(i, n_prompt_tiles, tiles_per_seq):
    return jnp.where(i < n_prompt_tiles, 0, 1 + (i - n_prompt_tiles) // tiles_per_seq)


def _adaln(x_ref, mod_ref, gain_ref, row, mod_base):
    d = x_ref.shape[1]
    shift = mod_ref[pl.ds(row, 1), pl.ds(mod_base * d, d)]
    scale = mod_ref[pl.ds(row, 1), pl.ds((mod_base + 1) * d, d)]
    return (_rms(x_ref[...], gain_ref[...]) * (1 + scale) + shift).astype(BF16)


def _is_pow2(v):
    return math.frexp(v)[0] == 0.5


def _mod_kernel(cond_ref, w_ref, b_ref, o_ref):
    s = _silu(cond_ref[...]).astype(BF16)
    o_ref[...] = jnp.dot(s, w_ref[...].astype(BF16), preferred_element_type=F32) + b_ref[...]


def _modulation(cond, w_mod, b_mod):
    depth, d, n = w_mod.shape
    tn = 1024
    return pl.pallas_call(
        _mod_kernel,
        grid=(depth, n // tn),
        in_specs=[
            pl.BlockSpec((MOD_ROWS, d), lambda l, j: (0, 0)),
            pl.BlockSpec((None, d, tn), lambda l, j: (l, 0, j)),
            pl.BlockSpec((None, 1, tn), lambda l, j: (l, 0, j)),
        ],
        out_specs=pl.BlockSpec((None, MOD_ROWS, tn), lambda l, j: (l, 0, j)),
        out_shape=jax.ShapeDtypeStruct((depth, MOD_ROWS, n), F32),
        compiler_params=_params(("parallel", "parallel"), 40),
        name="modulation",
    )(cond, w_mod, b_mod.reshape(depth, 1, n))


def _ffn_kernel(x_ref, mod_ref, gain_ref, wg_ref, wu_ref, wo_ref, o_ref, h_ref, *,
                mod_base, n_prompt_tiles, tiles_per_seq):
    i = pl.program_id(0)
    j = pl.program_id(1)
    d = x_ref.shape[1]
    row = _mod_row(i, n_prompt_tiles, tiles_per_seq)

    @pl.when(j == 0)
    def _():
        h_ref[...] = _adaln(x_ref, mod_ref, gain_ref, row, mod_base)
        o_ref[...] = jnp.zeros_like(o_ref)

    h = h_ref[...]
    g = jnp.dot(h, wg_ref[...], preferred_element_type=F32)
    u = jnp.dot(h, wu_ref[...], preferred_element_type=F32)
    act = (_silu(g) * u).astype(BF16)
    o_ref[...] += jnp.dot(act, wo_ref[...], preferred_element_type=F32)

    @pl.when(j == pl.num_programs(1) - 1)
    def _():
        gate = mod_ref[pl.ds(row, 1), pl.ds((mod_base + 2) * d, d)]
        o_ref[...] = x_ref[...] + (0.5 * gate) * o_ref[...]


def _ffn(x, mods, gains, w_in, w_out, layer, which, n_prompt_rows, seq_rows):
    t, d = x.shape
    dff = w_out.shape[2]
    nj = dff // FF_TILE
    mod_base = 6 * which
    kern = functools.partial(_ffn_kernel, mod_base=mod_base, n_prompt_tiles=n_prompt_rows // TOKEN_TILE,
                             tiles_per_seq=seq_rows // TOKEN_TILE)
    return pl.pallas_call(
        kern,
        grid=(t // TOKEN_TILE, nj),
        in_specs=[
            pl.BlockSpec((TOKEN_TILE, d), lambda i, j: (i, 0)),
            pl.BlockSpec((None, MOD_ROWS, N_MOD * d), lambda i, j: (layer, 0, 0)),
            pl.BlockSpec((None, 1, d), lambda i, j: (3 * layer + 2 * which, 0, 0)),
            pl.BlockSpec((None, None, d, FF_TILE), lambda i, j: (layer, which, 0, j)),
            pl.BlockSpec((None, None, d, FF_TILE), lambda i, j: (layer, which, 0, j + nj)),
            pl.BlockSpec((None, None, FF_TILE, d), lambda i, j: (layer, which, j, 0)),
        ],
        out_specs=pl.BlockSpec((TOKEN_TILE, d), lambda i, j: (i, 0)),
        out_shape=jax.ShapeDtypeStruct((t, d), F32),
        scratch_shapes=[pltpu.VMEM((TOKEN_TILE, d), BF16)],
        compiler_params=_params(("parallel", "arbitrary"), 48),
        name="ffn",
    )(x, mods, gains, w_in, w_in, w_out)


def _inproj_kernel(x_ref, mod_ref, gain_ref, w_ref, o_ref, h_ref, *, n_prompt_tiles, tiles_per_seq):
    @pl.when(pl.program_id(1) == 0)
    def _():
        row = _mod_row(pl.program_id(0), n_prompt_tiles, tiles_per_seq)
        h_ref[...] = _adaln(x_ref, mod_ref, gain_ref, row, 3)

    o_ref[...] = jnp.dot(h_ref[...], w_ref[...], preferred_element_type=F32)


def _inproj(x, mods, gains, w, layer, w_idx, tn, n_prompt_rows, seq_rows):
    t, d = x.shape
    n = w.shape[2]
    tm = PROJ_TILE
    kern = functools.partial(_inproj_kernel, n_prompt_tiles=n_prompt_rows // tm, tiles_per_seq=seq_rows // tm)
    return pl.pallas_call(
        kern,
        grid=(t // tm, n // tn),
        in_specs=[
            pl.BlockSpec((tm, d), lambda i, j: (i, 0)),
            pl.BlockSpec((None, MOD_ROWS, N_MOD * d), lambda i, j: (layer, 0, 0)),
            pl.BlockSpec((None, 1, d), lambda i, j: (3 * layer + 1, 0, 0)),
            pl.BlockSpec((None, d, tn), lambda i, j: (w_idx, 0, j)),
        ],
        out_specs=pl.BlockSpec((tm, tn), lambda i, j: (i, j)),
        out_shape=jax.ShapeDtypeStruct((t, n), F32),
        scratch_shapes=[pltpu.VMEM((tm, d), BF16)],
        compiler_params=_params(("parallel", "arbitrary"), 52),
        name="inproj",
    )(x, mods, gains, w)


def _outproj_kernel(x_ref, mod_ref, ap_ref, as_ref, bp_ref, bs_ref, w_ref, o_ref, *,
                    n_prompt_tiles, tiles_per_seq):
    i = pl.program_id(0)
    d = x_ref.shape[1]
    half = ap_ref.shape[1]
    is_prompt = i < n_prompt_tiles
    ya = jnp.where(is_prompt, ap_ref[...], as_ref[...])
    yb = jnp.where(is_prompt, bp_ref[...], bs_ref[...])
    y = jnp.dot(ya, w_ref[pl.ds(0, half), :], preferred_element_type=F32)
    y = y + jnp.dot(yb, w_ref[pl.ds(half, half), :], preferred_element_type=F32)
    row = _mod_row(i, n_prompt_tiles, tiles_per_seq)
    gate = mod_ref[pl.ds(row, 1), pl.ds(5 * d, d)]
    o_ref[...] = x_ref[...] + gate * y


def _outproj(x, mods, layer, ya_p, ya_s, yb_p, yb_s, w, w_idx, n_prompt_rows, seq_rows):
    t, d = x.shape
    half = ya_p.shape[1]
    npt = n_prompt_rows // TOKEN_TILE
    kern = functools.partial(_outproj_kernel, n_prompt_tiles=npt, tiles_per_seq=seq_rows // TOKEN_TILE)
    p_map = lambda i: (jnp.minimum(i, npt - 1), 0)
    s_map = lambda i: (jnp.maximum(i - npt, 0), 0)
    return pl.pallas_call(
        kern,
        grid=(t // TOKEN_TILE,),
        in_specs=[
            pl.BlockSpec((TOKEN_TILE, d), lambda i: (i, 0)),
            pl.BlockSpec((None, MOD_ROWS, N_MOD * d), lambda i: (layer, 0, 0)),
            pl.BlockSpec((TOKEN_TILE, half), p_map),
            pl.BlockSpec((TOKEN_TILE, half), s_map),
            pl.BlockSpec((TOKEN_TILE, half), p_map),
            pl.BlockSpec((TOKEN_TILE, half), s_map),
            pl.BlockSpec((None, 2 * half, d), lambda i: (w_idx, 0, 0)),
        ],
        out_specs=pl.BlockSpec((TOKEN_TILE, d), lambda i: (i, 0)),
        out_shape=jax.ShapeDtypeStruct((t, d), F32),
        compiler_params=_params(("parallel",), 48),
        name="outproj",
    )(x, mods, ya_p, ya_s, yb_p, yb_s, w)


def _final_norm_kernel(x_ref, g_ref, o_ref):
    o_ref[...] = _rms(x_ref[...], g_ref[...])


def _final_norm(x, gain, row_start, rows):
    d = x.shape[1]
    off = row_start // TOKEN_TILE
    return pl.pallas_call(
        _final_norm_kernel,
        grid=(rows // TOKEN_TILE,),
        in_specs=[pl.BlockSpec((TOKEN_TILE, d), lambda i: (i + off, 0)),
                  pl.BlockSpec((1, d), lambda i: (0, 0))],
        out_specs=pl.BlockSpec((TOKEN_TILE, d), lambda i: (i, 0)),
        out_shape=jax.ShapeDtypeStruct((rows, d), F32),
        compiler_params=_params(("parallel",), 32),
        name="final_norm",
    )(x, gain.reshape(1, d))


def _group_roll(x, step):
    rows, w = x.shape
    x3 = x.reshape(rows // V7X_SUBLANES, V7X_SUBLANES, w)
    return pltpu.roll(x3, step, axis=1).reshape(rows, w)


def _lru_kernel(xa_ref, ga_ref, h0f_ref, h0b_ref, cw_ref, cb_ref, wa_ref, ba_ref, wx_ref, bx_ref,
                lam_ref, y_ref, lf_ref, lb_ref, a_scr, u_scr, hf_scr, hb_scr):
    seq, w = xa_ref.shape
    groups = seq // V7X_SUBLANES
    xa = xa_ref[...]
    row = lax.broadcasted_iota(jnp.int32, (seq, w), 0)
    sub = row & (V7X_SUBLANES - 1)

    def tap(offset):
        if offset == 0:
            return xa
        shifted = pltpu.roll(xa, (-offset) % seq, axis=0)
        valid = (row + offset >= 0) & (row + offset < seq)
        return jnp.where(valid, shifted, 0.0)

    xc = cb_ref[...]
    for j in range(CONV_W):
        xc = xc + tap(j - CONV_W // 2) * cw_ref[pl.ds(j, 1), :]
    xcb = xc.astype(BF16)

    for direction, (h0_ref, h_scr, last_ref) in enumerate(((h0f_ref, hf_scr, lf_ref),
                                                            (h0b_ref, hb_scr, lb_ref))):
        reverse = direction == 1
        r = jax.nn.sigmoid(jnp.dot(xcb, wa_ref[direction], preferred_element_type=F32) + ba_ref[direction])
        gi = jax.nn.sigmoid(jnp.dot(xcb, wx_ref[direction], preferred_element_type=F32) + bx_ref[direction])
        neg_lam = -lam_ref[direction]
        softplus = jnp.maximum(neg_lam, 0.0) + jnp.log1p(jnp.exp(-jnp.abs(neg_lam)))
        log_a = -LRU_C * r * softplus
        a = jnp.exp(log_a)
        m2 = -jnp.tanh(log_a) * (a * a + 1.0)
        u = jnp.where(m2 > 0.0, m2 * lax.rsqrt(m2), 0.0) * (gi * xc)
        for step in (1, 2, 4):
            if reverse:
                a_nb = _group_roll(a, V7X_SUBLANES - step)
                u_nb = _group_roll(u, V7X_SUBLANES - step)
                valid = sub < V7X_SUBLANES - step
            else:
                a_nb = _group_roll(a, step)
                u_nb = _group_roll(u, step)
                valid = sub >= step
            u = a * jnp.where(valid, u_nb, 0.0) + u
            a = a * jnp.where(valid, a_nb, 1.0)
        a_scr[...] = a
        u_scr[...] = u

        def carry_step(g, carry, reverse=reverse, h_scr=h_scr):
            gg = groups - 1 - g if reverse else g
            off = pl.multiple_of(gg * V7X_SUBLANES, V7X_SUBLANES)
            h = a_scr[pl.ds(off, V7X_SUBLANES), :] * carry + u_scr[pl.ds(off, V7X_SUBLANES), :]
            h_scr[pl.ds(off, V7X_SUBLANES), :] = h
            return h[0:1] if reverse else h[V7X_SUBLANES - 1:V7X_SUBLANES]

        last_ref[...] = lax.fori_loop(0, groups, carry_step, h0_ref[...], unroll=4)

    y_ref[...] = ((hf_scr[...] + hb_scr[...]) * jax.nn.gelu(ga_ref[...])).astype(BF16)


def _lru(p_ab, h0f, h0b, prm, seq_len, n_seq, row_start):
    lru_w = h0f.shape[-1]
    cw = LRU_COLS
    ncb = lru_w // cw
    sb = row_start // seq_len
    vec = lambda: pl.BlockSpec((None, 1, cw), lambda s, c: (s, 0, c))
    par2 = lambda: pl.BlockSpec((None, 2, 1, cw), lambda s, c: (prm["idx"], 0, 0, c))
    gate = lambda: pl.BlockSpec((None, 2, None, cw, cw), lambda s, c: (prm["idx"], 0, c, 0, 0))
    return pl.pallas_call(
        _lru_kernel,
        grid=(n_seq, ncb),
        in_specs=[
            pl.BlockSpec((seq_len, cw), lambda s, c: (s + sb, c)),
            pl.BlockSpec((seq_len, cw), lambda s, c: (s + sb, c + ncb)),
            vec(), vec(),
            pl.BlockSpec((None, CONV_W, cw), lambda s, c: (prm["idx"], 0, c)),
            pl.BlockSpec((None, 1, cw), lambda s, c: (prm["idx"], 0, c)),
            gate(), par2(), gate(), par2(), par2(),
        ],
        out_specs=[
            pl.BlockSpec((seq_len, cw), lambda s, c: (s, c)),
            vec(), vec(),
        ],
        out_shape=[
            jax.ShapeDtypeStruct((n_seq * seq_len, lru_w), BF16),
            jax.ShapeDtypeStruct((n_seq, 1, lru_w), F32),
            jax.ShapeDtypeStruct((n_seq, 1, lru_w), F32),
        ],
        scratch_shapes=[pltpu.VMEM((seq_len, cw), F32)] * 4,
        compiler_params=_params(("parallel", "parallel"), 40),
        name="lru",
    )(p_ab, p_ab, h0f.reshape(n_seq, 1, lru_w), h0b.reshape(n_seq, 1, lru_w), prm["conv_w"],
      prm["conv_b"], prm["wa"], prm["ba"], prm["wx"], prm["bx"], prm["lam"])


def _block_diag_tiles(w, tile):
    n, two, nb, bw, _ = w.shape
    per = tile // bw
    w = w.reshape(n, two, nb // per, per, bw, bw)
    eye = jnp.eye(per, dtype=w.dtype)
    out = jnp.einsum("ndgpij,pq->ndgpiqj", w, eye)
    return out.reshape(n, two, nb // per, tile, tile)


def _softmax_attend(q, ks, vs, biases, scale):
    scores = []
    for k, b in zip(ks, biases):
        s = lax.dot_general(q, k, (((1,), (1,)), ((), ())), preferred_element_type=F32)
        if scale is not None:
            s = s * scale
        scores.append(s if b is None else s + b)
    m = scores[0].max(axis=-1, keepdims=True)
    for s in scores[1:]:
        m = jnp.maximum(m, s.max(axis=-1, keepdims=True))
    denom = None
    out = None
    for s, v in zip(scores, vs):
        e = jnp.exp(s - m)
        part = e.sum(axis=-1, keepdims=True)
        denom = part if denom is None else denom + part
        o = jnp.dot(e.astype(BF16), v, preferred_element_type=F32)
        out = o if out is None else out + o
    return out / denom


def _upper_half(shape):
    lane = lax.broadcasted_iota(jnp.int32, shape, len(shape) - 1)
    return lane % V7X_LANES >= V7X_LANES // 2


def _split_pair(q):
    up = _upper_half(q.shape)
    zero = jnp.zeros_like(q)
    return jnp.concatenate([jnp.where(up, zero, q), jnp.where(up, q, zero)], axis=0)


def _merge_pair(o):
    m = o.shape[0] // 2
    return jnp.where(_upper_half((m, o.shape[1])), o[m:], o[:m])


def _dup_head(block, use_high):
    swapped = pltpu.roll(block, V7X_LANES // 2, axis=1)
    keep = _upper_half(block.shape) == use_high
    return jnp.where(keep, block, swapped)


def _fold_scale(q, scale):
    return (q * scale, None) if _is_pow2(scale) else (q, scale)


def _na_ctx_kernel(q_ref, k_ref, v_ref, o_ref, *, scale):
    for pair in range(q_ref.shape[1] // V7X_LANES):
        sl = pl.ds(pair * V7X_LANES, V7X_LANES)
        q, sc = _fold_scale(q_ref[:, sl], scale)
        o = _softmax_attend(_split_pair(q.astype(BF16)), [k_ref[:, sl].astype(BF16)],
                            [v_ref[:, sl].astype(BF16)], [None], sc)
        o_ref[:, sl] = _merge_pair(o).astype(o_ref.dtype)


def _na_ctx(p_ab, n_seq, seq_len, q_blk, na_w, scale):
    blk = lambda col: pl.BlockSpec((seq_len, na_w), lambda b: (b, col))
    return pl.pallas_call(
        functools.partial(_na_ctx_kernel, scale=scale),
        grid=(n_seq,),
        in_specs=[blk(q_blk), blk(q_blk + 1), blk(q_blk + 2)],
        out_specs=pl.BlockSpec((seq_len, na_w), lambda b: (b, 0)),
        out_shape=jax.ShapeDtypeStruct((n_seq * seq_len, na_w), BF16),
        compiler_params=_params(("parallel",), 40),
        name="na_ctx",
    )(p_ab, p_ab, p_ab)


def _na_lat_kernel(q_ref, k_ref, v_ref, kc_ref, vc_ref, bias_ref, o_ref, *, scale):
    lq = q_ref.shape[0]
    n_chunks = lq // NA_Q_CHUNK
    kcb = kc_ref[...].astype(BF16)
    vcb = vc_ref[...].astype(BF16)
    for c in range(n_chunks):
        win = 0 if c < n_chunks // 2 else lq - NA_WIN
        kb = k_ref[pl.ds(win, NA_WIN), :].astype(BF16)
        vb = v_ref[pl.ds(win, NA_WIN), :].astype(BF16)
        rows = pl.ds(c * NA_Q_CHUNK, NA_Q_CHUNK)
        q, sc = _fold_scale(q_ref[rows, :], scale)
        o = _softmax_attend(_split_pair(q.astype(BF16)), [kb, kcb], [vb, vcb], [bias_ref[c], None], sc)
        o_ref[rows, :] = _merge_pair(o).astype(o_ref.dtype)


def _na_lat(p_ab, cache_k, cache_v, idx, bias, n_seq, seq_len, row_start, q_col, n_pairs, scale):
    sb = row_start // seq_len
    past = cache_k.shape[2]
    blk = lambda col: pl.BlockSpec((seq_len, V7X_LANES), lambda h, b: (b + sb, col + h))
    cblk = lambda: pl.BlockSpec((None, None, past, V7X_LANES), lambda h, b: (b, idx, 0, h))
    n_chunks = seq_len // NA_Q_CHUNK
    return pl.pallas_call(
        functools.partial(_na_lat_kernel, scale=scale),
        grid=(n_pairs, n_seq),
        in_specs=[blk(q_col), blk(q_col + n_pairs), blk(q_col + 2 * n_pairs), cblk(), cblk(),
                  pl.BlockSpec((None, n_chunks, 2 * NA_Q_CHUNK, NA_WIN), lambda h, b: (h, 0, 0, 0))],
        out_specs=pl.BlockSpec((seq_len, V7X_LANES), lambda h, b: (b, h)),
        out_shape=jax.ShapeDtypeStruct((n_seq * seq_len, n_pairs * V7X_LANES), BF16),
        compiler_params=_params(("parallel", "arbitrary"), 56),
        name="na_lat",
    )(p_ab, p_ab, p_ab, cache_k, cache_v, bias)


def _na_bias_blocks(tab, seq_len):
    n_heads = tab.shape[0]
    rows_n = seq_len // GRID_W
    kr = min(NA_KR, rows_n)
    cols = jnp.arange(GRID_W)
    c_start = jnp.clip(cols - NA_KC // 2, 0, GRID_W - NA_KC)
    col_ok = (cols[None, :] >= c_start[:, None]) & (cols[None, :] < c_start[:, None] + NA_KC)
    rel_c = jnp.clip(cols[None, :] - cols[:, None] + (NA_KC - 1), 0, 2 * NA_KC - 2)
    tiles = jnp.where(col_ok[None, None], tab[:, :, rel_c], NEG_BIAS)
    tiles = jnp.concatenate([tiles, jnp.full((n_heads, 1, GRID_W, GRID_W), NEG_BIAS, F32)], axis=1)
    rows = jnp.arange(rows_n)
    r_start = jnp.clip(rows - kr // 2, 0, rows_n - kr)
    rows_per_chunk = NA_Q_CHUNK // GRID_W
    n_chunks = rows_n // rows_per_chunk
    win_rows = NA_WIN // GRID_W
    base = jnp.where(rows // rows_per_chunk < n_chunks // 2, 0, rows_n - win_rows)
    k_row = base[:, None] + jnp.arange(win_rows)[None, :]
    row_ok = (k_row >= r_start[:, None]) & (k_row < r_start[:, None] + kr)
    idx = jnp.where(row_ok, k_row - rows[:, None] + (NA_KR - 1), 2 * NA_KR - 1)
    g = tiles[:, idx]
    g = g.reshape(n_heads // 2, 2, n_chunks, rows_per_chunk, win_rows, GRID_W, GRID_W)
    g = g.transpose(0, 2, 1, 3, 5, 4, 6)
    return g.reshape(n_heads // 2, n_chunks, 2 * NA_Q_CHUNK, NA_WIN)


def _head_rms(x, ones_bd, gain, head_dim):
    sq = x * x
    hi = sq.astype(BF16)
    lo = (sq - hi.astype(F32)).astype(BF16)
    ss = jnp.dot(hi, ones_bd, preferred_element_type=F32) + jnp.dot(lo, ones_bd, preferred_element_type=F32)
    return x * lax.rsqrt(ss * (1.0 / head_dim) + EPS) * gain


def _rope(x, cos, sin_lo, sin_hi):
    w = x.shape[1]
    reps = w // V7X_LANES
    tile = lambda t: t if reps == 1 else jnp.concatenate([t] * reps, axis=1)
    quarter = 16
    return (x * tile(cos) + pltpu.roll(x, w - quarter, axis=1) * tile(sin_lo)
            + pltpu.roll(x, quarter, axis=1) * tile(sin_hi))


def _cd_post_kernel(p_ref, ones_ref, qg_ref, kg_ref, mqg_ref, mkg_ref, wuq_ref, wukv_ref, *rest,
                    rope, head_dim, q_w, kv_w, rank, q_scale):
    if rope:
        cos_ref, slo_ref, shi_ref = rest[:3]
        rest = rest[3:]
        tabs = (cos_ref[...], slo_ref[...], shi_ref[...])
    q_ref, k_ref, qn_ref, qr_ref, ckv_ref, kn_ref, vm_ref, kr_ref = rest
    c0 = 0
    qc = p_ref[:, pl.ds(c0, q_w)]
    c0 += q_w
    kc = p_ref[:, pl.ds(c0, kv_w)]
    c0 += 2 * kv_w
    qa = p_ref[:, pl.ds(c0, rank)]
    c0 += rank
    ckv = p_ref[:, pl.ds(c0, rank)]
    c0 += rank
    kr = p_ref[:, pl.ds(c0, V7X_LANES)]

    qc = _head_rms(qc, ones_ref[...], qg_ref[...], head_dim)
    kc = _head_rms(kc, ones_ref[pl.ds(0, kv_w), pl.ds(0, kv_w)], kg_ref[...], head_dim)
    qd = jnp.dot(_rms(qa, mqg_ref[...]).astype(BF16), wuq_ref[...], preferred_element_type=F32)
    n_nope = qn_ref.shape[1]
    qn = qd[:, :n_nope]
    qr = qd[:, n_nope:]
    ckv = _rms(ckv, mkg_ref[...])
    if rope:
        qc = _rope(qc, *tabs)
        kc = _rope(kc, *tabs)
        qr = _rope(qr, *tabs)
        kr = _rope(kr, *tabs)
    q_ref[...] = (qc * q_scale).astype(BF16)
    k_ref[...] = kc
    qn_ref[...] = qn.astype(BF16)
    qr_ref[...] = qr.astype(BF16)
    ckv_ref[...] = ckv
    kv = jnp.dot(ckv.astype(BF16), wukv_ref[...], preferred_element_type=F32)
    kn_ref[...] = kv[:, :n_nope].astype(BF16)
    vm_ref[...] = kv[:, n_nope:].astype(BF16)
    kr_ref[...] = kr


def _cd_post(p_cd, prm, row_start, rows, rope_tabs, dims, q_scale):
    q_w, kv_w, rank, head_dim, n_nope, n_rope, n_v = dims
    tm = 256
    off = row_start // tm
    n_in = p_cd.shape[1]
    idx = prm["idx"]
    rope = rope_tabs is not None
    const = lambda shape: pl.BlockSpec(shape, lambda i: (0,) * len(shape))
    layer = lambda shape: pl.BlockSpec((None,) + shape, lambda i: (idx,) + (0,) * len(shape))
    in_specs = [
        pl.BlockSpec((tm, n_in), lambda i: (i + off, 0)),
        const((q_w, q_w)), layer((1, q_w)), layer((1, kv_w)), layer((1, rank)), layer((1, rank)),
        layer((rank, n_nope + n_rope)), layer((rank, n_nope + n_v)),
    ]
    args = [p_cd, prm["ones_bd"], prm["q_gain"], prm["k_gain"], prm["mla_q_gain"], prm["mla_kv_gain"],
            prm["w_uq"], prm["w_ukv"]]
    if rope:
        seq_tiles = rope_tabs[0].shape[0] // tm
        in_specs += [pl.BlockSpec((tm, V7X_LANES), lambda i: (i % seq_tiles, 0))] * 3
        args += list(rope_tabs)
    widths = [(q_w, BF16), (kv_w, F32), (n_nope, BF16), (n_rope, BF16), (rank, F32), (n_nope, BF16),
              (n_v, BF16), (V7X_LANES, F32)]
    kern = functools.partial(_cd_post_kernel, rope=rope, head_dim=head_dim, q_w=q_w, kv_w=kv_w, rank=rank,
                             q_scale=q_scale)
    return pl.pallas_call(
        kern,
        grid=(rows // tm,),
        in_specs=in_specs,
        out_specs=[pl.BlockSpec((tm, w), lambda i: (i, 0)) for w, _ in widths],
        out_shape=[jax.ShapeDtypeStruct((rows, w), dt) for w, dt in widths],
        compiler_params=_params(("parallel",), 48),
        name="cd_post",
    )(*args)


def _rope_tables(seq_len, head_dim):
    half = head_dim // 2
    nf = half // 2
    t = np.arange(seq_len)
    inv_freq = (1.0 / (ROPE_THETA ** (np.arange(nf, dtype=np.float32) / nf))).astype(np.float32)
    zeros = np.zeros((seq_len, nf), np.float32)
    cos, slo, shi = [], [], []
    for pos in (t // GRID_W, t % GRID_W):
        ang = pos.astype(np.float32)[:, None] * inv_freq[None, :]
        c, s = np.cos(ang).astype(np.float32), np.sin(ang).astype(np.float32)
        cos += [c, c]
        slo += [-s, zeros]
        shi += [zeros, s]
    reps = V7X_LANES // head_dim
    return tuple(jnp.asarray(np.tile(np.concatenate(x, axis=1), (1, reps))) for x in (cos, slo, shi))


def _ckv_up_kernel(c_ref, w_ref, kn_ref, vm_ref):
    kv = jnp.dot(c_ref[...].astype(BF16), w_ref[...], preferred_element_type=F32)
    n = kn_ref.shape[1]
    kn_ref[...] = kv[:, :n].astype(BF16)
    vm_ref[...] = kv[:, n:].astype(BF16)


def _ckv_up(ckv, w_ukv, idx, n_nope):
    n_seq, _, tm, rank = ckv.shape
    rows = n_seq * tm
    n = w_ukv.shape[2]
    return pl.pallas_call(
        _ckv_up_kernel,
        grid=(rows // tm,),
        in_specs=[pl.BlockSpec((None, None, tm, rank), lambda i: (i, idx, 0, 0)),
                  pl.BlockSpec((None, rank, n), lambda i: (idx, 0, 0))],
        out_specs=[pl.BlockSpec((tm, n_nope), lambda i: (i, 0)), pl.BlockSpec((tm, n - n_nope), lambda i: (i, 0))],
        out_shape=[jax.ShapeDtypeStruct((rows, n_nope), BF16), jax.ShapeDtypeStruct((rows, n - n_nope), BF16)],
        compiler_params=_params(("parallel",), 32),
        name="ckv_up",
    )(ckv, w_ukv)


def _gqa_kernel(q_ref, k_ref, v_ref, *rest, scale, has_cache):
    if has_cache:
        kc_ref, vc_ref, o_ref = rest
    else:
        (o_ref,) = rest
    lq = q_ref.shape[0]
    n_pairs = q_ref.shape[1] // V7X_LANES
    high = (pl.program_id(1) % 2) == 1
    ks = [_dup_head(k_ref[...], high).astype(BF16)]
    vs = [_dup_head(v_ref[...], high).astype(BF16)]
    if has_cache:
        ks.append(_dup_head(kc_ref[...], high).astype(BF16))
        vs.append(_dup_head(vc_ref[...], high).astype(BF16))

    def chunk(c, carry):
        rows = pl.ds(pl.multiple_of(c * Q_CHUNK, Q_CHUNK), Q_CHUNK)
        cols = [pl.ds(p * V7X_LANES, V7X_LANES) for p in range(n_pairs)]
        q = jnp.concatenate([_split_pair(q_ref[rows, cl]) for cl in cols], axis=0)
        o = _softmax_attend(q, ks, vs, [None] * len(ks), scale)
        for p, cl in enumerate(cols):
            o_ref[rows, cl] = _merge_pair(o[2 * p * Q_CHUNK:2 * (p + 1) * Q_CHUNK]).astype(o_ref.dtype)
        return carry

    lax.fori_loop(0, lq // Q_CHUNK, chunk, 0, unroll=True)


def _gqa(q, k, p_cd, v_col, cache_k, cache_v, idx, n_seq, seq_len, row_start, n_kv, scale):
    sb = row_start // seq_len
    group_w = q.shape[1] // n_kv
    has_cache = cache_k is not None
    in_specs = [
        pl.BlockSpec((seq_len, group_w), lambda b, h: (b, h)),
        pl.BlockSpec((seq_len, V7X_LANES), lambda b, h: (b, h // 2)),
        pl.BlockSpec((seq_len, V7X_LANES), lambda b, h: (b + sb, v_col + h // 2)),
    ]
    args = [q, k, p_cd]
    if has_cache:
        past = cache_k.shape[2]
        in_specs += [pl.BlockSpec((None, None, past, V7X_LANES), lambda b, h: (b, idx, 0, h // 2))] * 2
        args += [cache_k, cache_v]
    return pl.pallas_call(
        functools.partial(_gqa_kernel, scale=scale, has_cache=has_cache),
        grid=(n_seq, n_kv),
        in_specs=in_specs,
        out_specs=pl.BlockSpec((seq_len, group_w), lambda b, h: (b, h)),
        out_shape=jax.ShapeDtypeStruct(q.shape, BF16),
        compiler_params=_params(("parallel", "parallel"), 56),
        name="gqa",
    )(*args)


def _mla_kernel(qn_ref, qr_ref, kn_ref, kr_ref, v_ref, *rest, scale, has_cache):
    if has_cache:
        knc_ref, krc_ref, vc_ref, o_ref = rest
        krc = krc_ref[...].astype(BF16)
    else:
        (o_ref,) = rest
    lq = qn_ref.shape[0]
    qc = min(2 * Q_CHUNK, lq)
    high = (pl.program_id(1) % 2) == 1
    ks = [jnp.concatenate([kn_ref[...], kr_ref[...].astype(BF16)], axis=1)]
    vs = [v_ref[...]]
    if has_cache:
        ks.append(jnp.concatenate([knc_ref[...], krc], axis=1))
        vs.append(vc_ref[...])
    for c in range(lq // qc):
        rows = pl.ds(c * qc, qc)
        qr = qr_ref[rows, :]
        qr = jnp.where(_upper_half(qr.shape) == high, qr, jnp.zeros_like(qr))
        q = jnp.concatenate([qn_ref[rows, :], qr], axis=1)
        o_ref[rows, :] = _softmax_attend(q, ks, vs, [None] * len(ks), scale).astype(o_ref.dtype)


def _mla(qn, qr, kn, kr2, vm, cache, n_seq, seq_len, scale):
    has_cache = cache is not None
    n_heads = vm.shape[1] // V7X_LANES
    head = lambda rows: pl.BlockSpec((rows, V7X_LANES), lambda b, h: (b, h))
    in_specs = [head(seq_len),
                pl.BlockSpec((seq_len, V7X_LANES), lambda b, h: (b, h // 2)),
                head(seq_len),
                pl.BlockSpec((seq_len, V7X_LANES), lambda b, h: (b, 0)),
                head(seq_len)]
    args = [qn, qr, kn, kr2, vm]
    if has_cache:
        knc, krc, vmc, idx = cache
        past = krc.shape[2]
        in_specs += [head(past),
                     pl.BlockSpec((None, None, past, V7X_LANES), lambda b, h: (b, idx, 0, 0)),
                     head(past)]
        args += [knc, krc, vmc]
    return pl.pallas_call(
        functools.partial(_mla_kernel, scale=scale, has_cache=has_cache),
        grid=(n_seq, n_heads),
        in_specs=in_specs,
        out_specs=head(seq_len),
        out_shape=jax.ShapeDtypeStruct(vm.shape, BF16),
        compiler_params=_params(("parallel", "parallel"), 48),
        name="mla",
    )(*args)


def kernel(x_prompt, x_sample, state_lru_fwd, state_lru_bwd, cache_na_k, cache_na_v, cache_gqa_k, cache_gqa_v, cache_mla_ckv, cache_mla_krope, c, c_ctx, w_mod, b_mod, norm_gain, w_ffn_in, w_ffn_out, w_in_ab, conv_w, conv_b, lru_wa, lru_ba, lru_wx, lru_bx, lru_lambda, na_bias, w_out_ab, w_in_cd, gqa_q_gain, gqa_k_gain, mla_q_gain, mla_kv_gain, mla_w_uq, mla_w_uk, mla_w_uv, w_out_cd, final_gain):
    batch, seq, d = x_prompt.shape
    dec_batch, dec_seq, _ = x_sample.shape
    depth = w_mod.shape[0]
    n_even, n_odd = w_in_ab.shape[0], w_in_cd.shape[0]
    n_p = batch * seq
    n_s = dec_batch * dec_seq
    past = cache_na_k.shape[2]
    lru_w = state_lru_fwd.shape[-1]
    na_heads, na_dh = cache_na_k.shape[3], cache_na_k.shape[4]
    na_w = na_heads * na_dh
    gqa_kv, gqa_dh = cache_gqa_k.shape[3], cache_gqa_k.shape[4]
    kv_w = gqa_kv * gqa_dh
    rank = cache_mla_ckv.shape[-1]
    rope_w = cache_mla_krope.shape[-1]
    q_w = w_in_cd.shape[2] - 2 * kv_w - 2 * rank - rope_w
    mla_heads = mla_w_uk.shape[2] // MLA_NOPE
    n_nope = mla_heads * MLA_NOPE
    n_rope = mla_heads * rope_w
    n_v = mla_heads * MLA_V
    mla_qk = MLA_NOPE + rope_w
    assert mla_q_gain.shape[-1] == rank and 2 * rope_w == V7X_LANES
    assert n_p % dec_seq == 0 and dec_seq % PROJ_TILE == 0 and seq % Q_CHUNK == 0
    assert dec_seq // GRID_W == 2 * NA_KR and na_dh == GRID_W and gqa_dh == GRID_W
    assert w_in_ab.shape[2] == 2 * lru_w + 3 * na_w and lru_w == na_w
    assert (q_w // gqa_kv) % V7X_LANES == 0 and q_w % kv_w == 0

    x = jnp.concatenate([x_prompt.reshape(n_p, d), x_sample.reshape(n_s, d)], axis=0)
    cond = jnp.concatenate([c_ctx[None, :], c, jnp.zeros((MOD_ROWS - 1 - dec_batch, d), F32)], axis=0)
    mods = _modulation(cond, w_mod, b_mod)
    gains = norm_gain.reshape(depth * 3, 1, d)

    w_ffn_in_bf = w_ffn_in.astype(BF16)
    w_ffn_out_bf = w_ffn_out.astype(BF16)
    w_in_ab_bf = w_in_ab.astype(BF16)
    w_out_ab_bf = w_out_ab.astype(BF16)
    w_in_cd_bf = jnp.concatenate([w_in_cd, w_in_cd[:, :, -rope_w:]], axis=2).astype(BF16)
    w_out_cd_bf = w_out_cd.astype(BF16)

    lru_prm = {
        "conv_w": conv_w, "conv_b": conv_b.reshape(n_even, 1, lru_w),
        "wa": _block_diag_tiles(lru_wa, LRU_COLS).astype(BF16),
        "wx": _block_diag_tiles(lru_wx, LRU_COLS).astype(BF16),
        "ba": lru_ba.reshape(n_even, 2, 1, lru_w), "bx": lru_bx.reshape(n_even, 2, 1, lru_w),
        "lam": lru_lambda.reshape(n_even, 2, 1, lru_w),
    }
    zeros_state = jnp.zeros((batch, lru_w), F32)
    na_k_ctx = cache_na_k.reshape(dec_batch, n_even, past, na_w)
    na_v_ctx = cache_na_v.reshape(dec_batch, n_even, past, na_w)

    w_uq = mla_w_uq.reshape(n_odd, rank, mla_heads, mla_qk)
    w_uq = jnp.concatenate([w_uq[..., :MLA_NOPE].reshape(n_odd, rank, n_nope),
                            w_uq[..., MLA_NOPE:].reshape(n_odd, rank, n_rope)], axis=2).astype(BF16)
    gqa_scale = gqa_dh ** -0.5
    q_scale = gqa_scale if _is_pow2(gqa_scale) else 1.0
    cd_prm = {
        "ones_bd": jnp.asarray(np.kron(np.eye(q_w // gqa_dh, dtype=np.float32),
                                       np.ones((gqa_dh, gqa_dh), np.float32)), BF16),
        "q_gain": jnp.tile(gqa_q_gain, (1, q_w // gqa_dh)).reshape(n_odd, 1, q_w),
        "k_gain": jnp.tile(gqa_k_gain, (1, gqa_kv)).reshape(n_odd, 1, kv_w),
        "mla_q_gain": mla_q_gain.reshape(n_odd, 1, rank),
        "mla_kv_gain": mla_kv_gain.reshape(n_odd, 1, rank),
        "w_uq": w_uq,
        "w_ukv": jnp.concatenate([mla_w_uk, mla_w_uv], axis=2).astype(BF16),
    }
    rope_tabs = _rope_tables(dec_seq, gqa_dh)
    gqa_k_ctx = cache_gqa_k.reshape(dec_batch, n_odd, past, kv_w)
    gqa_v_ctx = cache_gqa_v.reshape(dec_batch, n_odd, past, kv_w)
    krope_ctx = jnp.concatenate([cache_mla_krope, cache_mla_krope], axis=-1)

    st_f, st_b, na_k, na_v, gq_k, gq_v, ml_c, ml_r = [], [], [], [], [], [], [], []
    for layer in range(depth):
        jdx = layer // 2
        x = _ffn(x, mods, gains, w_ffn_in_bf, w_ffn_out_bf, layer, 0, n_p, dec_seq)
        if layer % 2 == 0:
            p_ab = _inproj(x, mods, gains, w_in_ab_bf, layer, jdx, 1024, n_p, dec_seq)
            prm = dict(lru_prm, idx=jdx)
            ya_p, lf, lb = _lru(p_ab, zeros_state, zeros_state, prm, seq, batch, 0)
            ya_s, _, _ = _lru(p_ab, state_lru_fwd[:, jdx], state_lru_bwd[:, jdx], prm, dec_seq, dec_batch, n_p)
            scale = na_dh ** -0.5
            yb_p = _na_ctx(p_ab, batch, seq, 2 * lru_w // na_w, na_w, scale)
            bias = _na_bias_blocks(na_bias[jdx], dec_seq)
            yb_s = _na_lat(p_ab, na_k_ctx, na_v_ctx, jdx, bias, dec_batch, dec_seq, n_p,
                           2 * lru_w // V7X_LANES, na_w // V7X_LANES, scale)
            x = _outproj(x, mods, layer, ya_p, ya_s, yb_p, yb_s, w_out_ab_bf, jdx, n_p, dec_seq)
            st_f.append(lf.reshape(batch, lru_w))
            st_b.append(lb.reshape(batch, lru_w))
            na_k.append(p_ab[:n_p, 2 * lru_w + na_w:2 * lru_w + 2 * na_w].reshape(batch, seq, na_heads, na_dh))
            na_v.append(p_ab[:n_p, 2 * lru_w + 2 * na_w:].reshape(batch, seq, na_heads, na_dh))
        else:
            p_cd = _inproj(x, mods, gains, w_in_cd_bf, layer, jdx, w_in_cd_bf.shape[2] // 3, n_p, dec_seq)
            prm = dict(cd_prm, idx=jdx)
            dims = (q_w, kv_w, rank, gqa_dh, n_nope, n_rope, n_v)
            qp, kp, qnp_, qrp, ckvp, knp_, vmp, krp = _cd_post(p_cd, prm, 0, n_p, None, dims, q_scale)
            qs, ks_, qns, qrs, _, kns, vms, krs = _cd_post(p_cd, prm, n_p, n_s, rope_tabs, dims, q_scale)
            v_col = (q_w + kv_w) // V7X_LANES
            att_scale = None if q_scale != 1.0 else gqa_scale
            yc_p = _gqa(qp, kp, p_cd, v_col, None, None, jdx, batch, seq, 0, gqa_kv, att_scale)
            yc_s = _gqa(qs, ks_, p_cd, v_col, gqa_k_ctx, gqa_v_ctx, jdx, dec_batch, dec_seq, n_p,
                        gqa_kv, att_scale)
            knc, vmc = _ckv_up(cache_mla_ckv, cd_prm["w_ukv"], jdx, n_nope)
            yd_p = _mla(qnp_, qrp, knp_, krp, vmp, None, batch, seq, mla_qk ** -0.5)
            yd_s = _mla(qns, qrs, kns, krs, vms, (knc, krope_ctx, vmc, jdx), dec_batch, dec_seq, mla_qk ** -0.5)
            x = _outproj(x, mods, layer, yc_p, yc_s, yd_p, yd_s, w_out_cd_bf, jdx, n_p, dec_seq)
            gq_k.append(kp.reshape(batch, seq, gqa_kv, gqa_dh))
            gq_v.append(p_cd[:n_p, q_w + kv_w:q_w + 2 * kv_w].reshape(batch, seq, gqa_kv, gqa_dh))
            ml_c.append(ckvp.reshape(batch, seq, rank))
            ml_r.append(krp[:, :rope_w].reshape(batch, seq, rope_w))
        x = _ffn(x, mods, gains, w_ffn_in_bf, w_ffn_out_bf, layer, 1, n_p, dec_seq)

    y_prompt = _final_norm(x, final_gain, 0, n_p).reshape(batch, seq, d)
    y_sample = _final_norm(x, final_gain, n_p, n_s).reshape(dec_batch, dec_seq, d)
    stack = lambda xs: jnp.stack(xs, axis=1)
    return (y_prompt, y_sample, stack(st_f), stack(st_b), stack(na_k), stack(na_v), stack(gq_k), stack(gq_v),
            stack(ml_c), stack(ml_r))
```

```python
import functools
import math

import numpy as np

import jax
import jax.numpy as jnp
from jax import lax
from jax.experimental import pallas as pl
from jax.experimental.pallas import tpu as pltpu

F32 = jnp.float32
BF16 = jnp.bfloat16

EPS = 1e-6
N_MOD = 9
GRID_W = 64
NA_KR = 8
NA_KC = 16
ROPE_THETA = 10000.0
LRU_C = 8.0
CONV_W = 4
MLA_NOPE = 128
MLA_V = 128
NEG_BIAS = -1e30

V7X_LANES = 128
V7X_SUBLANES = 8
MOD_ROWS = 16
TOKEN_TILE = 512
PROJ_TILE = 1024
FF_TILE = 512
Q_CHUNK = 256
NA_Q_CHUNK = 256
NA_WIN = 768
LRU_COLS = 256


def _params(sem, vmem_mib):
    return pltpu.CompilerParams(dimension_semantics=sem, vmem_limit_bytes=vmem_mib * 1024 * 1024)


def _silu(x):
    return x * jax.nn.sigmoid(x)


def _rms(x, gain):
    ms = jnp.mean(x * x, axis=-1, keepdims=True)
    return x * lax.rsqrt(ms + EPS) * gain


def _mod_row(i, n_prompt_tiles, tiles_per_seq):
    return jnp.where(i < n_prompt_tiles, 0, 1 + (i - n_prompt_tiles) // tiles_per_seq)


def _adaln(x_ref, mod_ref, gain_ref, row, mod_base):
    d = x_ref.shape[1]
    shift = mod_ref[pl.ds(row, 1), pl.ds(mod_base * d, d)]
    scale = mod_ref[pl.ds(row, 1), pl.ds((mod_base + 1) * d, d)]
    return (_rms(x_ref[...], gain_ref[...]) * (1 + scale) + shift).astype(BF16)


def _is_pow2(v):
    return math.frexp(v)[0] == 0.5


def _mod_kernel(cond_ref, w_ref, b_ref, o_ref):
    s = _silu(cond_ref[...]).astype(BF16)
    o_ref[...] = jnp.dot(s, w_ref[...].astype(BF16), preferred_element_type=F32) + b_ref[...]


def _modulation(cond, w_mod, b_mod):
    depth, d, n = w_mod.shape
    tn = 1024
    return pl.pallas_call(
        _mod_kernel,
        grid=(depth, n // tn),
        in_specs=[
            pl.BlockSpec((MOD_ROWS, d), lambda l, j: (0, 0)),
            pl.BlockSpec((None, d, tn), lambda l, j: (l, 0, j)),
            pl.BlockSpec((None, 1, tn), lambda l, j: (l, 0, j)),
        ],
        out_specs=pl.BlockSpec((None, MOD_ROWS, tn), lambda l, j: (l, 0, j)),
        out_shape=jax.ShapeDtypeStruct((depth, MOD_ROWS, n), F32),
        compiler_params=_params(("parallel", "parallel"), 40),
        name="modulation",
    )(cond, w_mod, b_mod.reshape(depth, 1, n))


def _ffn_kernel(x_ref, mod_ref, gain_ref, wg_ref, wu_ref, wo_ref, o_ref, h_ref, *,
                mod_base, n_prompt_tiles, tiles_per_seq):
    i = pl.program_id(0)
    j = pl.program_id(1)
    d = x_ref.shape[1]
    row = _mod_row(i, n_prompt_tiles, tiles_per_seq)

    @pl.when(j == 0)
    def _():
        h_ref[...] = _adaln(x_ref, mod_ref, gain_ref, row, mod_base)
        o_ref[...] = jnp.zeros_like(o_ref)

    h = h_ref[...]
    g = jnp.dot(h, wg_ref[...], preferred_element_type=F32)
    u = jnp.dot(h, wu_ref[...], preferred_element_type=F32)
    act = (_silu(g) * u).astype(BF16)
    o_ref[...] += jnp.dot(act, wo_ref[...], preferred_element_type=F32)

    @pl.when(j == pl.num_programs(1) - 1)
    def _():
        gate = mod_ref[pl.ds(row, 1), pl.ds((mod_base + 2) * d, d)]
        o_ref[...] = x_ref[...] + (0.5 * gate) * o_ref[...]


def _ffn(x, mods, gains, w_in, w_out, layer, which, n_prompt_rows, seq_rows):
    t, d = x.shape
    dff = w_out.shape[2]
    nj = dff // FF_TILE
    mod_base = 6 * which
    kern = functools.partial(_ffn_kernel, mod_base=mod_base, n_prompt_tiles=n_prompt_rows // TOKEN_TILE,
                             tiles_per_seq=seq_rows // TOKEN_TILE)
    return pl.pallas_call(
        kern,
        grid=(t // TOKEN_TILE, nj),
        in_specs=[
            pl.BlockSpec((TOKEN_TILE, d), lambda i, j: (i, 0)),
            pl.BlockSpec((None, MOD_ROWS, N_MOD * d), lambda i, j: (layer, 0, 0)),
            pl.BlockSpec((None, 1, d), lambda i, j: (3 * layer + 2 * which, 0, 0)),
            pl.BlockSpec((None, None, d, FF_TILE), lambda i, j: (layer, which, 0, j)),
            pl.BlockSpec((None, None, d, FF_TILE), lambda i, j: (layer, which, 0, j + nj)),
            pl.BlockSpec((None, None, FF_TILE, d), lambda i, j: (layer, which, j, 0)),
        ],
        out_specs=pl.BlockSpec((TOKEN_TILE, d), lambda i, j: (i, 0)),
        out_shape=jax.ShapeDtypeStruct((t, d), F32),
        scratch_shapes=[pltpu.VMEM((TOKEN_TILE, d), BF16)],
        compiler_params=_params(("parallel", "arbitrary"), 48),
        name="ffn",
    )(x, mods, gains, w_in, w_in, w_out)


def _inproj_kernel(x_ref, mod_ref, gain_ref, w_ref, o_ref, h_ref, *, n_prompt_tiles, tiles_per_seq):
    @pl.when(pl.program_id(1) == 0)
    def _():
        row = _mod_row(pl.program_id(0), n_prompt_tiles, tiles_per_seq)
        h_ref[...] = _adaln(x_ref, mod_ref, gain_ref, row, 3)

    o_ref[...] = jnp.dot(h_ref[...], w_ref[...], preferred_element_type=F32)


def _inproj(x, mods, gains, w, layer, w_idx, tn, n_prompt_rows, seq_rows):
    t, d = x.shape
    n = w.shape[2]
    tm = PROJ_TILE
    kern = functools.partial(_inproj_kernel, n_prompt_tiles=n_prompt_rows // tm, tiles_per_seq=seq_rows // tm)
    return pl.pallas_call(
        kern,
        grid=(t // tm, n // tn),
        in_specs=[
            pl.BlockSpec((tm, d), lambda i, j: (i, 0)),
            pl.BlockSpec((None, MOD_ROWS, N_MOD * d), lambda i, j: (layer, 0, 0)),
            pl.BlockSpec((None, 1, d), lambda i, j: (3 * layer + 1, 0, 0)),
            pl.BlockSpec((None, d, tn), lambda i, j: (w_idx, 0, j)),
        ],
        out_specs=pl.BlockSpec((tm, tn), lambda i, j: (i, j)),
        out_shape=jax.ShapeDtypeStruct((t, n), F32),
        scratch_shapes=[pltpu.VMEM((tm, d), BF16)],
        compiler_params=_params(("parallel", "arbitrary"), 52),
        name="inproj",
    )(x, mods, gains, w)


def _outproj_kernel(x_ref, mod_ref, ap_ref, as_ref, bp_ref, bs_ref, w_ref, o_ref, *,
                    n_prompt_tiles, tiles_per_seq):
    i = pl.program_id(0)
    d = x_ref.shape[1]
    half = ap_ref.shape[1]
    is_prompt = i < n_prompt_tiles
    ya = jnp.where(is_prompt, ap_ref[...], as_ref[...])
    yb = jnp.where(is_prompt, bp_ref[...], bs_ref[...])
    y = jnp.dot(ya, w_ref[pl.ds(0, half), :], preferred_element_type=F32)
    y = y + jnp.dot(yb, w_ref[pl.ds(half, half), :], preferred_element_type=F32)
    row = _mod_row(i, n_prompt_tiles, tiles_per_seq)
    gate = mod_ref[pl.ds(row, 1), pl.ds(5 * d, d)]
    o_ref[...] = x_ref[...] + gate * y


def _outproj(x, mods, layer, ya_p, ya_s, yb_p, yb_s, w, w_idx, n_prompt_rows, seq_rows):
    t, d = x.shape
    half = ya_p.shape[1]
    npt = n_prompt_rows // TOKEN_TILE
    kern = functools.partial(_outproj_kernel, n_prompt_tiles=npt, tiles_per_seq=seq_rows // TOKEN_TILE)
    p_map = lambda i: (jnp.minimum(i, npt - 1), 0)
    s_map = lambda i: (jnp.maximum(i - npt, 0), 0)
    return pl.pallas_call(
        kern,
        grid=(t // TOKEN_TILE,),
        in_specs=[
            pl.BlockSpec((TOKEN_TILE, d), lambda i: (i, 0)),
            pl.BlockSpec((None, MOD_ROWS, N_MOD * d), lambda i: (layer, 0, 0)),
            pl.BlockSpec((TOKEN_TILE, half), p_map),
            pl.BlockSpec((TOKEN_TILE, half), s_map),
            pl.BlockSpec((TOKEN_TILE, half), p_map),
            pl.BlockSpec((TOKEN_TILE, half), s_map),
            pl.BlockSpec((None, 2 * half, d), lambda i: (w_idx, 0, 0)),
        ],
        out_specs=pl.BlockSpec((TOKEN_TILE, d), lambda i: (i, 0)),
        out_shape=jax.ShapeDtypeStruct((t, d), F32),
        compiler_params=_params(("parallel",), 48),
        name="outproj",
    )(x, mods, ya_p, ya_s, yb_p, yb_s, w)


def _final_norm_kernel(x_ref, g_ref, o_ref):
    o_ref[...] = _rms(x_ref[...], g_ref[...])


def _final_norm(x, gain, row_start, rows):
    d = x.shape[1]
    off = row_start // TOKEN_TILE
    return pl.pallas_call(
        _final_norm_kernel,
        grid=(rows // TOKEN_TILE,),
        in_specs=[pl.BlockSpec((TOKEN_TILE, d), lambda i: (i + off, 0)),
                  pl.BlockSpec((1, d), lambda i: (0, 0))],
        out_specs=pl.BlockSpec((TOKEN_TILE, d), lambda i: (i, 0)),
        out_shape=jax.ShapeDtypeStruct((rows, d), F32),
        compiler_params=_params(("parallel",), 32),
        name="final_norm",
    )(x, gain.reshape(1, d))


def _group_roll(x, step):
    rows, w = x.shape
    x3 = x.reshape(rows // V7X_SUBLANES, V7X_SUBLANES, w)
    return pltpu.roll(x3, step, axis=1).reshape(rows, w)


def _lru_kernel(xa_ref, ga_ref, h0f_ref, h0b_ref, cw_ref, cb_ref, wa_ref, ba_ref, wx_ref, bx_ref,
                lam_ref, y_ref, lf_ref, lb_ref, a_scr, u_scr, hf_scr, hb_scr):
    seq, w = xa_ref.shape
    groups = seq // V7X_SUBLANES
    xa = xa_ref[...]
    row = lax.broadcasted_iota(jnp.int32, (seq, w), 0)
    sub = row & (V7X_SUBLANES - 1)

    def tap(offset):
        if offset == 0:
            return xa
        shifted = pltpu.roll(xa, (-offset) % seq, axis=0)
        valid = (row + offset >= 0) & (row + offset < seq)
        return jnp.where(valid, shifted, 0.0)

    xc = cb_ref[...]
    for j in range(CONV_W):
        xc = xc + tap(j - CONV_W // 2) * cw_ref[pl.ds(j, 1), :]
    xcb = xc.astype(BF16)

    for direction, (h0_ref, h_scr, last_ref) in enumerate(((h0f_ref, hf_scr, lf_ref),
                                                            (h0b_ref, hb_scr, lb_ref))):
        reverse = direction == 1
        r = jax.nn.sigmoid(jnp.dot(xcb, wa_ref[direction], preferred_element_type=F32) + ba_ref[direction])
        gi = jax.nn.sigmoid(jnp.dot(xcb, wx_ref[direction], preferred_element_type=F32) + bx_ref[direction])
        neg_lam = -lam_ref[direction]
        softplus = jnp.maximum(neg_lam, 0.0) + jnp.log1p(jnp.exp(-jnp.abs(neg_lam)))
        log_a = -LRU_C * r * softplus
        a = jnp.exp(log_a)
        m2 = -jnp.tanh(log_a) * (a * a + 1.0)
        u = jnp.where(m2 > 0.0, m2 * lax.rsqrt(m2), 0.0) * (gi * xc)
        for step in (1, 2, 4):
            if reverse:
                a_nb = _group_roll(a, V7X_SUBLANES - step)
                u_nb = _group_roll(u, V7X_SUBLANES - step)
                valid = sub < V7X_SUBLANES - step
            else:
                a_nb = _group_roll(a, step)
                u_nb = _group_roll(u, step)
                valid = sub >= step
            u = a * jnp.where(valid, u_nb, 0.0) + u
            a = a * jnp.where(valid, a_nb, 1.0)
        a_scr[...] = a
        u_scr[...] = u

        def carry_step(g, carry, reverse=reverse, h_scr=h_scr):
            gg = groups - 1 - g if reverse else g
            off = pl.multiple_of(gg * V7X_SUBLANES, V7X_SUBLANES)
            h = a_scr[pl.ds(off, V7X_SUBLANES), :] * carry + u_scr[pl.ds(off, V7X_SUBLANES), :]
            h_scr[pl.ds(off, V7X_SUBLANES), :] = h
            return h[0:1] if reverse else h[V7X_SUBLANES - 1:V7X_SUBLANES]

        last_ref[...] = lax.fori_loop(0, groups, carry_step, h0_ref[...], unroll=4)

    y_ref[...] = ((hf_scr[...] + hb_scr[...]) * jax.nn.gelu(ga_ref[...])).astype(BF16)


def _lru(p_ab, h0f, h0b, prm, seq_len, n_seq, row_start):
    lru_w = h0f.shape[-1]
    cw = LRU_COLS
    ncb = lru_w // cw
    sb = row_start // seq_len
    vec = lambda: pl.BlockSpec((None, 1, cw), lambda s, c: (s, 0, c))
    par2 = lambda: pl.BlockSpec((None, 2, 1, cw), lambda s, c: (prm["idx"], 0, 0, c))
    gate = lambda: pl.BlockSpec((None, 2, None, cw, cw), lambda s, c: (prm["idx"], 0, c, 0, 0))
    return pl.pallas_call(
        _lru_kernel,
        grid=(n_seq, ncb),
        in_specs=[
            pl.BlockSpec((seq_len, cw), lambda s, c: (s + sb, c)),
            pl.BlockSpec((seq_len, cw), lambda s, c: (s + sb, c + ncb)),
            vec(), vec(),
            pl.BlockSpec((None, CONV_W, cw), lambda s, c: (prm["idx"], 0, c)),
            pl.BlockSpec((None, 1, cw), lambda s, c: (prm["idx"], 0, c)),
            gate(), par2(), gate(), par2(), par2(),
        ],
        out_specs=[
            pl.BlockSpec((seq_len, cw), lambda s, c: (s, c)),
            vec(), vec(),
        ],
        out_shape=[
            jax.ShapeDtypeStruct((n_seq * seq_len, lru_w), BF16),
            jax.ShapeDtypeStruct((n_seq, 1, lru_w), F32),
            jax.ShapeDtypeStruct((n_seq, 1, lru_w), F32),
        ],
        scratch_shapes=[pltpu.VMEM((seq_len, cw), F32)] * 4,
        compiler_params=_params(("parallel", "parallel"), 40),
        name="lru",
    )(p_ab, p_ab, h0f.reshape(n_seq, 1, lru_w), h0b.reshape(n_seq, 1, lru_w), prm["conv_w"],
      prm["conv_b"], prm["wa"], prm["ba"], prm["wx"], prm["bx"], prm["lam"])


def _block_diag_tiles(w, tile):
    n, two, nb, bw, _ = w.shape
    per = tile // bw
    w = w.reshape(n, two, nb // per, per, bw, bw)
    eye = jnp.eye(per, dtype=w.dtype)
    out = jnp.einsum("ndgpij,pq->ndgpiqj", w, eye)
    return out.reshape(n, two, nb // per, tile, tile)


def _softmax_attend(q, ks, vs, biases, scale):
    scores = []
    for k, b in zip(ks, biases):
        s = lax.dot_general(q, k, (((1,), (1,)), ((), ())), preferred_element_type=F32)
        if scale is not None:
            s = s * scale
        scores.append(s if b is None else s + b)
    m = scores[0].max(axis=-1, keepdims=True)
    for s in scores[1:]:
        m = jnp.maximum(m, s.max(axis=-1, keepdims=True))
    denom = None
    out = None
    for s, v in zip(scores, vs):
        e = jnp.exp(s - m)
        part = e.sum(axis=-1, keepdims=True)
        denom = part if denom is None else denom + part
        o = jnp.dot(e.astype(BF16), v, preferred_element_type=F32)
        out = o if out is None else out + o
    return out / denom


def _upper_half(shape):
    lane = lax.broadcasted_iota(jnp.int32, shape, len(shape) - 1)
    return lane % V7X_LANES >= V7X_LANES // 2


def _split_pair(q):
    up = _upper_half(q.shape)
    zero = jnp.zeros_like(q)
    return jnp.concatenate([jnp.where(up, zero, q), jnp.where(up, q, zero)], axis=0)


def _merge_pair(o):
    m = o.shape[0] // 2
    return jnp.where(_upper_half((m, o.shape[1])), o[m:], o[:m])


def _dup_head(block, use_high):
    swapped = pltpu.roll(block, V7X_LANES // 2, axis=1)
    keep = _upper_half(block.shape) == use_high
    return jnp.where(keep, block, swapped)


def _fold_scale(q, scale):
    return (q * scale, None) if _is_pow2(scale) else (q, scale)


def _na_ctx_kernel(q_ref, k_ref, v_ref, o_ref, *, scale):
    for pair in range(q_ref.shape[1] // V7X_LANES):
        sl = pl.ds(pair * V7X_LANES, V7X_LANES)
        q, sc = _fold_scale(q_ref[:, sl], scale)
        o = _softmax_attend(_split_pair(q.astype(BF16)), [k_ref[:, sl].astype(BF16)],
                            [v_ref[:, sl].astype(BF16)], [None], sc)
        o_ref[:, sl] = _merge_pair(o).astype(o_ref.dtype)


def _na_ctx(p_ab, n_seq, seq_len, q_blk, na_w, scale):
    blk = lambda col: pl.BlockSpec((seq_len, na_w), lambda b: (b, col))
    return pl.pallas_call(
        functools.partial(_na_ctx_kernel, scale=scale),
        grid=(n_seq,),
        in_specs=[blk(q_blk), blk(q_blk + 1), blk(q_blk + 2)],
        out_specs=pl.BlockSpec((seq_len, na_w), lambda b: (b, 0)),
        out_shape=jax.ShapeDtypeStruct((n_seq * seq_len, na_w), BF16),
        compiler_params=_params(("parallel",), 40),
        name="na_ctx",
    )(p_ab, p_ab, p_ab)


def _na_tile_index(seq_len):
    rows_n = seq_len // GRID_W
    kr = min(NA_KR, rows_n)
    rows_per_chunk = NA_Q_CHUNK // GRID_W
    n_chunks = rows_n // rows_per_chunk
    win_rows = NA_WIN // GRID_W
    table = []
    for r in range(rows_n):
        r_start = min(max(r - kr // 2, 0), rows_n - kr)
        base = 0 if r // rows_per_chunk < n_chunks // 2 else rows_n - win_rows
        table.append([k - r + NA_KR - 1 if r_start <= k < r_start + kr else 2 * NA_KR - 1
                      for k in range(base, base + win_rows)])
    return table


def _na_lat_kernel(q_ref, k_ref, v_ref, kc_ref, vc_ref, tiles_ref, o_ref, bias_ref, *, scale):
    lq = q_ref.shape[0]
    n_chunks = lq // NA_Q_CHUNK
    rows_per_chunk = NA_Q_CHUNK // GRID_W

    @pl.when(pl.program_id(1) == 0)
    def _():
        low = lax.broadcasted_iota(jnp.int32, (GRID_W, V7X_LANES), 1) < GRID_W
        for half in (0, 1):
            for r, slots in enumerate(_na_tile_index(lq)):
                c, rr = divmod(r, rows_per_chunk)
                rows = pl.ds(half * NA_Q_CHUNK + rr * GRID_W, GRID_W)
                for s in range(0, len(slots), 2):
                    blk = jnp.where(low, tiles_ref[half, slots[s]], tiles_ref[half, slots[s + 1]])
                    bias_ref[c, rows, pl.ds(s * GRID_W, V7X_LANES)] = blk

    kcb = kc_ref[...].astype(BF16)
    vcb = vc_ref[...].astype(BF16)
    for c in range(n_chunks):
        win = 0 if c < n_chunks // 2 else lq - NA_WIN
        kb = k_ref[pl.ds(win, NA_WIN), :].astype(BF16)
        vb = v_ref[pl.ds(win, NA_WIN), :].astype(BF16)
        rows = pl.ds(c * NA_Q_CHUNK, NA_Q_CHUNK)
        q, sc = _fold_scale(q_ref[rows, :], scale)
        o = _softmax_attend(_split_pair(q.astype(BF16)), [kb, kcb], [vb, vcb], [bias_ref[c], None], sc)
        o_ref[rows, :] = _merge_pair(o).astype(o_ref.dtype)


def _na_lat(p_ab, cache_k, cache_v, tiles, idx, n_seq, seq_len, row_start, q_col, n_pairs, scale):
    sb = row_start // seq_len
    past = cache_k.shape[2]
    blk = lambda col: pl.BlockSpec((seq_len, V7X_LANES), lambda h, b: (b + sb, col + h))
    cblk = lambda: pl.BlockSpec((None, None, past, V7X_LANES), lambda h, b: (b, idx, 0, h))
    n_chunks = seq_len // NA_Q_CHUNK
    return pl.pallas_call(
        functools.partial(_na_lat_kernel, scale=scale),
        grid=(n_pairs, n_seq),
        in_specs=[blk(q_col), blk(q_col + n_pairs), blk(q_col + 2 * n_pairs), cblk(), cblk(),
                  pl.BlockSpec((None, 2, 2 * NA_KR, GRID_W, V7X_LANES), lambda h, b: (idx, h, 0, 0, 0))],
        out_specs=pl.BlockSpec((seq_len, V7X_LANES), lambda h, b: (b, h)),
        out_shape=jax.ShapeDtypeStruct((n_seq * seq_len, n_pairs * V7X_LANES), BF16),
        scratch_shapes=[pltpu.VMEM((n_chunks, 2 * NA_Q_CHUNK, NA_WIN), F32)],
        compiler_params=_params(("parallel", "arbitrary"), 56),
        name="na_lat",
    )(p_ab, p_ab, p_ab, cache_k, cache_v, tiles)


def _na_bias_tiles(tab):
    lead = tab.shape[:-1]
    edge = GRID_W - NA_KC
    vec = jnp.pad(tab, [(0, 0)] * len(lead) + [(edge, edge + 1)], mode="edge")
    skew = jnp.broadcast_to(vec[..., None, :], lead + (GRID_W, 2 * GRID_W))
    skew = skew.reshape(lead + (2 * GRID_W * GRID_W,))[..., :GRID_W * (2 * GRID_W - 1)]
    toep = skew.reshape(lead + (GRID_W, 2 * GRID_W - 1))[..., GRID_W - 1:]
    cols = np.arange(GRID_W)
    c_start = np.clip(cols - NA_KC // 2, 0, GRID_W - NA_KC)
    col_ok = (cols[None, :] >= c_start[:, None]) & (cols[None, :] < c_start[:, None] + NA_KC)
    tiles = jnp.where(col_ok, toep, NEG_BIAS)
    masked = jnp.full(lead[:-1] + (1, GRID_W, GRID_W), NEG_BIAS, F32)
    tiles = jnp.concatenate([tiles, masked], axis=-3)
    return jnp.concatenate([tiles, tiles], axis=-1)


def _head_rms(x, ones_bd, gain, head_dim):
    sq = x * x
    hi = sq.astype(BF16)
    lo = (sq - hi.astype(F32)).astype(BF16)
    ss = jnp.dot(hi, ones_bd, preferred_element_type=F32) + jnp.dot(lo, ones_bd, preferred_element_type=F32)
    return x * lax.rsqrt(ss * (1.0 / head_dim) + EPS) * gain


def _rope(x, cos, sin_lo, sin_hi):
    w = x.shape[1]
    reps = w // V7X_LANES
    tile = lambda t: t if reps == 1 else jnp.concatenate([t] * reps, axis=1)
    quarter = 16
    return (x * tile(cos) + pltpu.roll(x, w - quarter, axis=1) * tile(sin_lo)
            + pltpu.roll(x, quarter, axis=1) * tile(sin_hi))


def _cd_post_kernel(p_ref, ones_ref, qg_ref, kg_ref, mqg_ref, mkg_ref, wuq_ref, wukv_ref, *rest,
                    rope, head_dim, q_w, kv_w, rank, q_scale):
    if rope:
        cos_ref, slo_ref, shi_ref = rest[:3]
        rest = rest[3:]
        tabs = (cos_ref[...], slo_ref[...], shi_ref[...])
    q_ref, k_ref, qn_ref, qr_ref, ckv_ref, kn_ref, vm_ref, kr_ref = rest
    c0 = 0
    qc = p_ref[:, pl.ds(c0, q_w)]
    c0 += q_w
    kc = p_ref[:, pl.ds(c0, kv_w)]
    c0 += 2 * kv_w
    qa = p_ref[:, pl.ds(c0, rank)]
    c0 += rank
    ckv = p_ref[:, pl.ds(c0, rank)]
    c0 += rank
    kr = p_ref[:, pl.ds(c0, V7X_LANES)]

    qc = _head_rms(qc, ones_ref[...], qg_ref[...], head_dim)
    kc = _head_rms(kc, ones_ref[pl.ds(0, kv_w), pl.ds(0, kv_w)], kg_ref[...], head_dim)
    qd = jnp.dot(_rms(qa, mqg_ref[...]).astype(BF16), wuq_ref[...], preferred_element_type=F32)
    n_nope = qn_ref.shape[1]
    qn = qd[:, :n_nope]
    qr = qd[:, n_nope:]
    ckv = _rms(ckv, mkg_ref[...])
    if rope:
        qc = _rope(qc, *tabs)
        kc = _rope(kc, *tabs)
        qr = _rope(qr, *tabs)
        kr = _rope(kr, *tabs)
    q_ref[...] = (qc * q_scale).astype(BF16)
    k_ref[...] = kc
    qn_ref[...] = qn.astype(BF16)
    qr_ref[...] = qr.astype(BF16)
    ckv_ref[...] = ckv
    kv = jnp.dot(ckv.astype(BF16), wukv_ref[...], preferred_element_type=F32)
    kn_ref[...] = kv[:, :n_nope].astype(BF16)
    vm_ref[...] = kv[:, n_nope:].astype(BF16)
    kr_ref[...] = kr


def _cd_post(p_cd, prm, row_start, rows, rope_tabs, dims, q_scale):
    q_w, kv_w, rank, head_dim, n_nope, n_rope, n_v = dims
    tm = 256
    off = row_start // tm
    n_in = p_cd.shape[1]
    idx = prm["idx"]
    rope = rope_tabs is not None
    const = lambda shape: pl.BlockSpec(shape, lambda i: (0,) * len(shape))
    layer = lambda shape: pl.BlockSpec((None,) + shape, lambda i: (idx,) + (0,) * len(shape))
    in_specs = [
        pl.BlockSpec((tm, n_in), lambda i: (i + off, 0)),
        const((q_w, q_w)), layer((1, q_w)), layer((1, kv_w)), layer((1, rank)), layer((1, rank)),
        layer((rank, n_nope + n_rope)), layer((rank, n_nope + n_v)),
    ]
    args = [p_cd, prm["ones_bd"], prm["q_gain"], prm["k_gain"], prm["mla_q_gain"], prm["mla_kv_gain"],
            prm["w_uq"], prm["w_ukv"]]
    if rope:
        seq_tiles = rope_tabs[0].shape[0] // tm
        in_specs += [pl.BlockSpec((tm, V7X_LANES), lambda i: (i % seq_tiles, 0))] * 3
        args += list(rope_tabs)
    widths = [(q_w, BF16), (kv_w, F32), (n_nope, BF16), (n_rope, BF16), (rank, F32), (n_nope, BF16),
              (n_v, BF16), (V7X_LANES, F32)]
    kern = functools.partial(_cd_post_kernel, rope=rope, head_dim=head_dim, q_w=q_w, kv_w=kv_w, rank=rank,
                             q_scale=q_scale)
    return pl.pallas_call(
        kern,
        grid=(rows // tm,),
        in_specs=in_specs,
        out_specs=[pl.BlockSpec((tm, w), lambda i: (i, 0)) for w, _ in widths],
        out_shape=[jax.ShapeDtypeStruct((rows, w), dt) for w, dt in widths],
        compiler_params=_params(("parallel",), 48),
        name="cd_post",
    )(*args)


def _rope_tables(seq_len, head_dim):
    half = head_dim // 2
    nf = half // 2
    t = np.arange(seq_len)
    inv_freq = (1.0 / (ROPE_THETA ** (np.arange(nf, dtype=np.float32) / nf))).astype(np.float32)
    zeros = np.zeros((seq_len, nf), np.float32)
    cos, slo, shi = [], [], []
    for pos in (t // GRID_W, t % GRID_W):
        ang = pos.astype(np.float32)[:, None] * inv_freq[None, :]
        c, s = np.cos(ang).astype(np.float32), np.sin(ang).astype(np.float32)
        cos += [c, c]
        slo += [-s, zeros]
        shi += [zeros, s]
    reps = V7X_LANES // head_dim
    return tuple(jnp.asarray(np.tile(np.concatenate(x, axis=1), (1, reps))) for x in (cos, slo, shi))


def _ckv_up_kernel(c_ref, w_ref, kn_ref, vm_ref):
    kv = jnp.dot(c_ref[...].astype(BF16), w_ref[...], preferred_element_type=F32)
    n = kn_ref.shape[1]
    kn_ref[...] = kv[:, :n].astype(BF16)
    vm_ref[...] = kv[:, n:].astype(BF16)


def _ckv_up(ckv, w_ukv, idx, n_nope):
    n_seq, _, tm, rank = ckv.shape
    rows = n_seq * tm
    n = w_ukv.shape[2]
    return pl.pallas_call(
        _ckv_up_kernel,
        grid=(rows // tm,),
        in_specs=[pl.BlockSpec((None, None, tm, rank), lambda i: (i, idx, 0, 0)),
                  pl.BlockSpec((None, rank, n), lambda i: (idx, 0, 0))],
        out_specs=[pl.BlockSpec((tm, n_nope), lambda i: (i, 0)), pl.BlockSpec((tm, n - n_nope), lambda i: (i, 0))],
        out_shape=[jax.ShapeDtypeStruct((rows, n_nope), BF16), jax.ShapeDtypeStruct((rows, n - n_nope), BF16)],
        compiler_params=_params(("parallel",), 32),
        name="ckv_up",
    )(ckv, w_ukv)


def _gqa_kernel(q_ref, k_ref, v_ref, *rest, scale, has_cache, group_w):
    if has_cache:
        kc_ref, vc_ref, o_ref = rest
    else:
        (o_ref,) = rest
    lq = q_ref.shape[0]
    n_local = q_ref.shape[1] // group_w
    n_pairs = group_w // V7X_LANES
    for hl in range(n_local):
        if n_local == 1:
            high = (pl.program_id(1) % 2) == 1
            pick = lambda ref: ref[...]
        else:
            high = hl % 2 == 1
            pick = lambda ref, hl=hl: ref[:, pl.ds((hl // 2) * V7X_LANES, V7X_LANES)]
        ks = [_dup_head(pick(k_ref), high).astype(BF16)]
        vs = [_dup_head(pick(v_ref), high).astype(BF16)]
        if has_cache:
            ks.append(_dup_head(pick(kc_ref), high).astype(BF16))
            vs.append(_dup_head(pick(vc_ref), high).astype(BF16))
        for c in range(lq // Q_CHUNK):
            rows = pl.ds(c * Q_CHUNK, Q_CHUNK)
            cols = [pl.ds(hl * group_w + p * V7X_LANES, V7X_LANES) for p in range(n_pairs)]
            q = jnp.concatenate([_split_pair(q_ref[rows, cl]) for cl in cols], axis=0)
            o = _softmax_attend(q, ks, vs, [None] * len(ks), scale)
            for p, cl in enumerate(cols):
                o_ref[rows, cl] = _merge_pair(o[2 * p * Q_CHUNK:2 * (p + 1) * Q_CHUNK]).astype(o_ref.dtype)


def _gqa(q, k, p_cd, v_col, cache_k, cache_v, idx, n_seq, seq_len, row_start, n_kv, heads_per_step, scale):
    sb = row_start // seq_len
    group_w = q.shape[1] // n_kv
    has_cache = cache_k is not None
    if heads_per_step == 1:
        kv_w, kv_blk, v_blk = V7X_LANES, (lambda h: h // 2), (lambda h: v_col + h // 2)
    else:
        kv_w = k.shape[1]
        kv_blk, v_blk = (lambda h: 0), (lambda h: v_col * V7X_LANES // kv_w)
    in_specs = [
        pl.BlockSpec((seq_len, group_w * heads_per_step), lambda b, h: (b, h)),
        pl.BlockSpec((seq_len, kv_w), lambda b, h: (b, kv_blk(h))),
        pl.BlockSpec((seq_len, kv_w), lambda b, h: (b + sb, v_blk(h))),
    ]
    args = [q, k, p_cd]
    if has_cache:
        past = cache_k.shape[2]
        in_specs += [pl.BlockSpec((None, None, past, kv_w), lambda b, h: (b, idx, 0, kv_blk(h)))] * 2
        args += [cache_k, cache_v]
    return pl.pallas_call(
        functools.partial(_gqa_kernel, scale=scale, has_cache=has_cache, group_w=group_w),
        grid=(n_seq, n_kv // heads_per_step),
        in_specs=in_specs,
        out_specs=pl.BlockSpec((seq_len, group_w * heads_per_step), lambda b, h: (b, h)),
        out_shape=jax.ShapeDtypeStruct(q.shape, BF16),
        compiler_params=_params(("parallel", "parallel"), 56),
        name="gqa",
    )(*args)


def _mla_kernel(qn_ref, qr_ref, kn_ref, kr_ref, v_ref, *rest, scale, has_cache):
    if has_cache:
        knc_ref, krc_ref, vc_ref, o_ref = rest
        krc = krc_ref[...].astype(BF16)
    else:
        (o_ref,) = rest
    lq = qn_ref.shape[0]
    qc = min(2 * Q_CHUNK, lq)
    n_local = qn_ref.shape[1] // V7X_LANES
    kr = kr_ref[...].astype(BF16)
    for hl in range(n_local):
        sl = pl.ds(hl * V7X_LANES, V7X_LANES)
        if n_local == 1:
            high = (pl.program_id(1) % 2) == 1
            pair = pl.ds(0, V7X_LANES)
        else:
            high = hl % 2 == 1
            pair = pl.ds((hl // 2) * V7X_LANES, V7X_LANES)
        ks = [jnp.concatenate([kn_ref[:, sl], kr], axis=1)]
        vs = [v_ref[:, sl]]
        if has_cache:
            ks.append(jnp.concatenate([knc_ref[:, sl], krc], axis=1))
            vs.append(vc_ref[:, sl])
        for c in range(lq // qc):
            rows = pl.ds(c * qc, qc)
            qr = qr_ref[rows, pair]
            qr = jnp.where(_upper_half(qr.shape) == high, qr, jnp.zeros_like(qr))
            q = jnp.concatenate([qn_ref[rows, sl], qr], axis=1)
            o_ref[rows, sl] = _softmax_attend(q, ks, vs, [None] * len(ks), scale).astype(o_ref.dtype)


def _mla(qn, qr, kn, kr2, vm, cache, n_seq, seq_len, heads_per_step, scale):
    has_cache = cache is not None
    n_heads = vm.shape[1] // V7X_LANES
    hw = heads_per_step * V7X_LANES
    head = lambda rows: pl.BlockSpec((rows, hw), lambda b, h: (b, h))
    if heads_per_step == 1:
        rope_q = pl.BlockSpec((seq_len, V7X_LANES), lambda b, h: (b, h // 2))
    else:
        rope_q = pl.BlockSpec((seq_len, qr.shape[1]), lambda b, h: (b, 0))
    in_specs = [head(seq_len), rope_q, head(seq_len),
                pl.BlockSpec((seq_len, V7X_LANES), lambda b, h: (b, 0)),
                head(seq_len)]
    args = [qn, qr, kn, kr2, vm]
    if has_cache:
        knc, krc, vmc, idx = cache
        past = krc.shape[2]
        in_specs += [head(past),
                     pl.BlockSpec((None, None, past, V7X_LANES), lambda b, h: (b, idx, 0, 0)),
                     head(past)]
        args += [knc, krc, vmc]
    return pl.pallas_call(
        functools.partial(_mla_kernel, scale=scale, has_cache=has_cache),
        grid=(n_seq, n_heads // heads_per_step),
        in_specs=in_specs,
        out_specs=head(seq_len),
        out_shape=jax.ShapeDtypeStruct(vm.shape, BF16),
        compiler_params=_params(("parallel", "parallel"), 48),
        name="mla",
    )(*args)


def kernel(x_prompt, x_sample, state_lru_fwd, state_lru_bwd, cache_na_k, cache_na_v, cache_gqa_k, cache_gqa_v, cache_mla_ckv, cache_mla_krope, c, c_ctx, w_mod, b_mod, norm_gain, w_ffn_in, w_ffn_out, w_in_ab, conv_w, conv_b, lru_wa, lru_ba, lru_wx, lru_bx, lru_lambda, na_bias, w_out_ab, w_in_cd, gqa_q_gain, gqa_k_gain, mla_q_gain, mla_kv_gain, mla_w_uq, mla_w_uk, mla_w_uv, w_out_cd, final_gain):
    batch, seq, d = x_prompt.shape
    dec_batch, dec_seq, _ = x_sample.shape
    depth = w_mod.shape[0]
    n_even, n_odd = w_in_ab.shape[0], w_in_cd.shape[0]
    n_p = batch * seq
    n_s = dec_batch * dec_seq
    past = cache_na_k.shape[2]
    lru_w = state_lru_fwd.shape[-1]
    na_heads, na_dh = cache_na_k.shape[3], cache_na_k.shape[4]
    na_w = na_heads * na_dh
    gqa_kv, gqa_dh = cache_gqa_k.shape[3], cache_gqa_k.shape[4]
    kv_w = gqa_kv * gqa_dh
    rank = cache_mla_ckv.shape[-1]
    rope_w = cache_mla_krope.shape[-1]
    q_w = w_in_cd.shape[2] - 2 * kv_w - 2 * rank - rope_w
    mla_heads = mla_w_uk.shape[2] // MLA_NOPE
    n_nope = mla_heads * MLA_NOPE
    n_rope = mla_heads * rope_w
    n_v = mla_heads * MLA_V
    mla_qk = MLA_NOPE + rope_w
    assert mla_q_gain.shape[-1] == rank and 2 * rope_w == V7X_LANES
    assert n_p % dec_seq == 0 and dec_seq % PROJ_TILE == 0 and seq % Q_CHUNK == 0
    assert dec_seq // GRID_W == 2 * NA_KR and na_dh == GRID_W and gqa_dh == GRID_W
    assert w_in_ab.shape[2] == 2 * lru_w + 3 * na_w and lru_w == na_w
    assert (q_w // gqa_kv) % V7X_LANES == 0 and q_w % kv_w == 0

    x = jnp.concatenate([x_prompt.reshape(n_p, d), x_sample.reshape(n_s, d)], axis=0)
    cond = jnp.concatenate([c_ctx[None, :], c, jnp.zeros((MOD_ROWS - 1 - dec_batch, d), F32)], axis=0)
    mods = _modulation(cond, w_mod, b_mod)
    gains = norm_gain.reshape(depth * 3, 1, d)

    w_ffn_in_bf = w_ffn_in.astype(BF16)
    w_ffn_out_bf = w_ffn_out.astype(BF16)
    w_in_ab_bf = w_in_ab.astype(BF16)
    w_out_ab_bf = w_out_ab.astype(BF16)
    w_in_cd_bf = jnp.concatenate([w_in_cd, w_in_cd[:, :, -rope_w:]], axis=2).astype(BF16)
    w_out_cd_bf = w_out_cd.astype(BF16)

    lru_prm = {
        "conv_w": conv_w, "conv_b": conv_b.reshape(n_even, 1, lru_w),
        "wa": _block_diag_tiles(lru_wa, LRU_COLS).astype(BF16),
        "wx": _block_diag_tiles(lru_wx, LRU_COLS).astype(BF16),
        "ba": lru_ba.reshape(n_even, 2, 1, lru_w), "bx": lru_bx.reshape(n_even, 2, 1, lru_w),
        "lam": lru_lambda.reshape(n_even, 2, 1, lru_w),
    }
    zeros_state = jnp.zeros((batch, lru_w), F32)
    na_k_ctx = cache_na_k.reshape(dec_batch, n_even, past, na_w)
    na_v_ctx = cache_na_v.reshape(dec_batch, n_even, past, na_w)
    na_tiles = _na_bias_tiles(na_bias)

    w_uq = mla_w_uq.reshape(n_odd, rank, mla_heads, mla_qk)
    w_uq = jnp.concatenate([w_uq[..., :MLA_NOPE].reshape(n_odd, rank, n_nope),
                            w_uq[..., MLA_NOPE:].reshape(n_odd, rank, n_rope)], axis=2).astype(BF16)
    gqa_scale = gqa_dh ** -0.5
    q_scale = gqa_scale if _is_pow2(gqa_scale) else 1.0
    cd_prm = {
        "ones_bd": jnp.asarray(np.kron(np.eye(q_w // gqa_dh, dtype=np.float32),
                                       np.ones((gqa_dh, gqa_dh), np.float32)), BF16),
        "q_gain": jnp.tile(gqa_q_gain, (1, q_w // gqa_dh)).reshape(n_odd, 1, q_w),
        "k_gain": jnp.tile(gqa_k_gain, (1, gqa_kv)).reshape(n_odd, 1, kv_w),
        "mla_q_gain": mla_q_gain.reshape(n_odd, 1, rank),
        "mla_kv_gain": mla_kv_gain.reshape(n_odd, 1, rank),
        "w_uq": w_uq,
        "w_ukv": jnp.concatenate([mla_w_uk, mla_w_uv], axis=2).astype(BF16),
    }
    rope_tabs = _rope_tables(dec_seq, gqa_dh)
    gqa_k_ctx = cache_gqa_k.reshape(dec_batch, n_odd, past, kv_w)
    gqa_v_ctx = cache_gqa_v.reshape(dec_batch, n_odd, past, kv_w)
    krope_ctx = jnp.concatenate([cache_mla_krope, cache_mla_krope], axis=-1)

    st_f, st_b, na_k, na_v, gq_k, gq_v, ml_c, ml_r = [], [], [], [], [], [], [], []
    for layer in range(depth):
        jdx = layer // 2
        x = _ffn(x, mods, gains, w_ffn_in_bf, w_ffn_out_bf, layer, 0, n_p, dec_seq)
        if layer % 2 == 0:
            p_ab = _inproj(x, mods, gains, w_in_ab_bf, layer, jdx, 1024, n_p, dec_seq)
            prm = dict(lru_prm, idx=jdx)
            ya_p, lf, lb = _lru(p_ab, zeros_state, zeros_state, prm, seq, batch, 0)
            ya_s, _, _ = _lru(p_ab, state_lru_fwd[:, jdx], state_lru_bwd[:, jdx], prm, dec_seq, dec_batch, n_p)
            scale = na_dh ** -0.5
            yb_p = _na_ctx(p_ab, batch, seq, 2 * lru_w // na_w, na_w, scale)
            yb_s = _na_lat(p_ab, na_k_ctx, na_v_ctx, na_tiles, jdx, dec_batch, dec_seq, n_p,
                           2 * lru_w // V7X_LANES, na_w // V7X_LANES, scale)
            x = _outproj(x, mods, layer, ya_p, ya_s, yb_p, yb_s, w_out_ab_bf, jdx, n_p, dec_seq)
            st_f.append(lf.reshape(batch, lru_w))
            st_b.append(lb.reshape(batch, lru_w))
            na_k.append(p_ab[:n_p, 2 * lru_w + na_w:2 * lru_w + 2 * na_w].reshape(batch, seq, na_heads, na_dh))
            na_v.append(p_ab[:n_p, 2 * lru_w + 2 * na_w:].reshape(batch, seq, na_heads, na_dh))
        else:
            p_cd = _inproj(x, mods, gains, w_in_cd_bf, layer, jdx, w_in_cd_bf.shape[2] // 3, n_p, dec_seq)
            prm = dict(cd_prm, idx=jdx)
            dims = (q_w, kv_w, rank, gqa_dh, n_nope, n_rope, n_v)
            qp, kp, qnp_, qrp, ckvp, knp_, vmp, krp = _cd_post(p_cd, prm, 0, n_p, None, dims, q_scale)
            qs, ks_, qns, qrs, _, kns, vms, krs = _cd_post(p_cd, prm, n_p, n_s, rope_tabs, dims, q_scale)
            v_col = (q_w + kv_w) // V7X_LANES
            att_scale = None if q_scale != 1.0 else gqa_scale
            yc_p = _gqa(qp, kp, p_cd, v_col, None, None, jdx, batch, seq, 0, gqa_kv, gqa_kv, att_scale)
            yc_s = _gqa(qs, ks_, p_cd, v_col, gqa_k_ctx, gqa_v_ctx, jdx, dec_batch, dec_seq, n_p,
                        gqa_kv, 1, att_scale)
            knc, vmc = _ckv_up(cache_mla_ckv, cd_prm["w_ukv"], jdx, n_nope)
            yd_p = _mla(qnp_, qrp, knp_, krp, vmp, None, batch, seq, mla_heads, mla_qk ** -0.5)
            yd_s = _mla(qns, qrs, kns, krs, vms, (knc, krope_ctx, vmc, jdx), dec_batch, dec_seq, 1,
                        mla_qk ** -0.5)
            x = _outproj(x, mods, layer, yc_p, yc_s, yd_p, yd_s, w_out_cd_bf, jdx, n_p, dec_seq)
            gq_k.append(kp.reshape(batch, seq, gqa_kv, gqa_dh))
            gq_v.append(p_cd[:n_p, q_w + kv_w:q_w + 2 * kv_w].reshape(batch, seq, gqa_kv, gqa_dh))
            ml_c.append(ckvp.reshape(batch, seq, rank))
            ml_r.append(krp[:, :rope_w].reshape(batch, seq, rope_w))
        x = _ffn(x, mods, gains, w_ffn_in_bf, w_ffn_out_bf, layer, 1, n_p, dec_seq)

    y_prompt = _final_norm(x, final_gain, 0, n_p).reshape(batch, seq, d)
    y_sample = _final_norm(x, final_gain, n_p, n_s).reshape(dec_batch, dec_seq, d)
    stack = lambda xs: jnp.stack(xs, axis=1)
    return (y_prompt, y_sample, stack(st_f), stack(st_b), stack(na_k), stack(na_v), stack(gq_k), stack(gq_v),
            stack(ml_c), stack(ml_r))
```

```python
import functools
import math

import numpy as np

import jax
import jax.numpy as jnp
from jax import lax
from jax.experimental import pallas as pl
from jax.experimental.pallas import tpu as pltpu

F32 = jnp.float32
BF16 = jnp.bfloat16

EPS = 1e-6
N_MOD = 9
GRID_W = 64
NA_KR = 8
NA_KC = 16
ROPE_THETA = 10000.0
LRU_C = 8.0
CONV_W = 4
MLA_NOPE = 128
MLA_V = 128
NEG_BIAS = -1e30

V7X_LANES = 128
V7X_SUBLANES = 8
MOD_ROWS = 16
TOKEN_TILE = 512
PROJ_TILE = 1024
FF_TILE = 512
Q_CHUNK = 256
NA_Q_CHUNK = 256
NA_WIN = 768
LRU_COLS = 256


def _params(sem, vmem_mib):
    return pltpu.CompilerParams(dimension_semantics=sem, vmem_limit_bytes=vmem_mib * 1024 * 1024)


def _silu(x):
    return x * jax.nn.sigmoid(x)


def _rms(x, gain):
    ms = jnp.mean(x * x, axis=-1, keepdims=True)
    return x * lax.rsqrt(ms + EPS) * gain


def _mod_row(i, n_prompt_tiles, tiles_per_seq):
    return jnp.where(i < n_prompt_tiles, 0, 1 + (i - n_prompt_tiles) // tiles_per_seq)


def _adaln(x, mod_ref, gain_ref, row, mod_base):
    d = x.shape[1]
    shift = mod_ref[pl.ds(row, 1), pl.ds(mod_base * d, d)]
    scale = mod_ref[pl.ds(row, 1), pl.ds((mod_base + 1) * d, d)]
    return (_rms(x, gain_ref[...]) * (1 + scale) + shift).astype(BF16)


def _is_pow2(v):
    return math.frexp(v)[0] == 0.5


def _mod_kernel(cond_ref, w_ref, b_ref, o_ref):
    s = _silu(cond_ref[...]).astype(BF16)
    o_ref[...] = jnp.dot(s, w_ref[...].astype(BF16), preferred_element_type=F32) + b_ref[...]


def _modulation(cond, w_mod, b_mod):
    depth, d, n = w_mod.shape
    tn = 1024
    return pl.pallas_call(
        _mod_kernel,
        grid=(depth, n // tn),
        in_specs=[
            pl.BlockSpec((MOD_ROWS, d), lambda l, j: (0, 0)),
            pl.BlockSpec((None, d, tn), lambda l, j: (l, 0, j)),
            pl.BlockSpec((None, 1, tn), lambda l, j: (l, 0, j)),
        ],
        out_specs=pl.BlockSpec((None, MOD_ROWS, tn), lambda l, j: (l, 0, j)),
        out_shape=jax.ShapeDtypeStruct((depth, MOD_ROWS, n), F32),
        compiler_params=_params(("parallel", "parallel"), 40),
        name="modulation",
    )(cond, w_mod, b_mod.reshape(depth, 1, n))


def _ffn_kernel(x_ref, mod_ref, gain_ref, wg_ref, wu_ref, wo_ref, *rest, mod_base, n_prompt_tiles,
                tiles_per_seq, cast_next):
    if cast_next:
        nin_ref, nout_ref, o_ref, cin_ref, cout_ref, h_ref = rest
    else:
        o_ref, h_ref = rest
    i = pl.program_id(0)
    j = pl.program_id(1)
    d = x_ref.shape[1]
    row = _mod_row(i, n_prompt_tiles, tiles_per_seq)

    @pl.when(j == 0)
    def _():
        h_ref[...] = _adaln(x_ref[...], mod_ref, gain_ref, row, mod_base)
        o_ref[...] = jnp.zeros_like(o_ref)

    h = h_ref[...]
    g = jnp.dot(h, wg_ref[...], preferred_element_type=F32)
    u = jnp.dot(h, wu_ref[...], preferred_element_type=F32)
    if cast_next:
        cin_ref[...] = nin_ref[...].astype(BF16)
        cout_ref[...] = nout_ref[...].astype(BF16)
    act = (_silu(g) * u).astype(BF16)
    o_ref[...] += jnp.dot(act, wo_ref[...], preferred_element_type=F32)

    @pl.when(j == pl.num_programs(1) - 1)
    def _():
        gate = mod_ref[pl.ds(row, 1), pl.ds((mod_base + 2) * d, d)]
        o_ref[...] = x_ref[...] + (0.5 * gate) * o_ref[...]


def _ffn(x, mods, gains, w_in, w_out, layer, which, nxt, n_prompt_rows, seq_rows):
    t, d = x.shape
    dff = w_out.shape[0]
    nj = dff // FF_TILE
    ni = t // TOKEN_TILE
    mod_base = 6 * which
    kern = functools.partial(_ffn_kernel, mod_base=mod_base, n_prompt_tiles=n_prompt_rows // TOKEN_TILE,
                             tiles_per_seq=seq_rows // TOKEN_TILE, cast_next=nxt is not None)
    in_specs = [
        pl.BlockSpec((TOKEN_TILE, d), lambda i, j: (i, 0)),
        pl.BlockSpec((None, MOD_ROWS, N_MOD * d), lambda i, j: (layer, 0, 0)),
        pl.BlockSpec((None, 1, d), lambda i, j: (3 * layer + 2 * which, 0, 0)),
        pl.BlockSpec((d, FF_TILE), lambda i, j: (0, j)),
        pl.BlockSpec((d, FF_TILE), lambda i, j: (0, j + nj)),
        pl.BlockSpec((FF_TILE, d), lambda i, j: (j, 0)),
    ]
    args = [x, mods, gains, w_in, w_in, w_out]
    out_specs = [pl.BlockSpec((TOKEN_TILE, d), lambda i, j: (i, 0))]
    out_shape = [jax.ShapeDtypeStruct((t, d), F32)]
    if nxt is not None:
        f_in, f_out, nl, nw = nxt
        cast_rows = min(r for r in (16, 32, 64, 128, 256, 512)
                        if d % r == 0 and dff % r == 0 and d // r + dff // r <= ni * nj)
        n_in, n_out = d // cast_rows, dff // cast_rows
        in_blk = lambda i, j: jnp.minimum(i * nj + j, n_in - 1)
        out_blk = lambda i, j: jnp.clip(i * nj + j - n_in, 0, n_out - 1)
        in_specs += [pl.BlockSpec((None, None, cast_rows, 2 * dff), lambda i, j: (nl, nw, in_blk(i, j), 0)),
                     pl.BlockSpec((None, None, cast_rows, d), lambda i, j: (nl, nw, out_blk(i, j), 0))]
        args += [f_in, f_out]
        out_specs += [pl.BlockSpec((cast_rows, 2 * dff), lambda i, j: (in_blk(i, j), 0)),
                      pl.BlockSpec((cast_rows, d), lambda i, j: (out_blk(i, j), 0))]
        out_shape += [jax.ShapeDtypeStruct((d, 2 * dff), BF16), jax.ShapeDtypeStruct((dff, d), BF16)]
    return pl.pallas_call(
        kern,
        grid=(ni, nj),
        in_specs=in_specs,
        out_specs=out_specs,
        out_shape=out_shape,
        scratch_shapes=[pltpu.VMEM((TOKEN_TILE, d), BF16)],
        compiler_params=_params(("arbitrary", "arbitrary"), 56),
        name="ffn",
    )(*args)


def _inproj_kernel(x_ref, mod_ref, gain_ref, w_ref, o_ref, h_ref, *, n_prompt_tiles, tiles_per_seq):
    @pl.when(pl.program_id(1) == 0)
    def _():
        row = _mod_row(pl.program_id(0), n_prompt_tiles, tiles_per_seq)
        h_ref[...] = _adaln(x_ref[...], mod_ref, gain_ref, row, 3)

    o_ref[...] = jnp.dot(h_ref[...], w_ref[...], preferred_element_type=F32)


def _inproj(x, mods, gains, w, layer, w_idx, tn, n_prompt_rows, seq_rows):
    t, d = x.shape
    n = w.shape[2]
    tm = PROJ_TILE
    kern = functools.partial(_inproj_kernel, n_prompt_tiles=n_prompt_rows // tm, tiles_per_seq=seq_rows // tm)
    return pl.pallas_call(
        kern,
        grid=(t // tm, n // tn),
        in_specs=[
            pl.BlockSpec((tm, d), lambda i, j: (i, 0)),
            pl.BlockSpec((None, MOD_ROWS, N_MOD * d), lambda i, j: (layer, 0, 0)),
            pl.BlockSpec((None, 1, d), lambda i, j: (3 * layer + 1, 0, 0)),
            pl.BlockSpec((None, d, tn), lambda i, j: (w_idx, 0, j)),
        ],
        out_specs=pl.BlockSpec((tm, tn), lambda i, j: (i, j)),
        out_shape=jax.ShapeDtypeStruct((t, n), F32),
        scratch_shapes=[pltpu.VMEM((tm, d), BF16)],
        compiler_params=_params(("parallel", "arbitrary"), 52),
        name="inproj",
    )(x, mods, gains, w)


def _outproj_kernel(x_ref, mod_ref, ap_ref, as_ref, bp_ref, bs_ref, w_ref, o_ref, *,
                    n_prompt_tiles, tiles_per_seq):
    i = pl.program_id(0)
    d = x_ref.shape[1]
    half = ap_ref.shape[1]
    is_prompt = i < n_prompt_tiles
    ya = jnp.where(is_prompt, ap_ref[...], as_ref[...])
    yb = jnp.where(is_prompt, bp_ref[...], bs_ref[...])
    y = jnp.dot(ya, w_ref[pl.ds(0, half), :], preferred_element_type=F32)
    y = y + jnp.dot(yb, w_ref[pl.ds(half, half), :], preferred_element_type=F32)
    row = _mod_row(i, n_prompt_tiles, tiles_per_seq)
    gate = mod_ref[pl.ds(row, 1), pl.ds(5 * d, d)]
    o_ref[...] = x_ref[...] + gate * y


def _outproj(x, mods, layer, ya_p, ya_s, yb_p, yb_s, w, w_idx, n_prompt_rows, seq_rows):
    t, d = x.shape
    half = ya_p.shape[1]
    npt = n_prompt_rows // TOKEN_TILE
    kern = functools.partial(_outproj_kernel, n_prompt_tiles=npt, tiles_per_seq=seq_rows // TOKEN_TILE)
    p_map = lambda i: (jnp.minimum(i, npt - 1), 0)
    s_map = lambda i: (jnp.maximum(i - npt, 0), 0)
    return pl.pallas_call(
        kern,
        grid=(t // TOKEN_TILE,),
        in_specs=[
            pl.BlockSpec((TOKEN_TILE, d), lambda i: (i, 0)),
            pl.BlockSpec((None, MOD_ROWS, N_MOD * d), lambda i: (layer, 0, 0)),
            pl.BlockSpec((TOKEN_TILE, half), p_map),
            pl.BlockSpec((TOKEN_TILE, half), s_map),
            pl.BlockSpec((TOKEN_TILE, half), p_map),
            pl.BlockSpec((TOKEN_TILE, half), s_map),
            pl.BlockSpec((None, 2 * half, d), lambda i: (w_idx, 0, 0)),
        ],
        out_specs=pl.BlockSpec((TOKEN_TILE, d), lambda i: (i, 0)),
        out_shape=jax.ShapeDtypeStruct((t, d), F32),
        compiler_params=_params(("parallel",), 48),
        name="outproj",
    )(x, mods, ya_p, ya_s, yb_p, yb_s, w)


def _final_norm_kernel(x_ref, g_ref, o_ref):
    o_ref[...] = _rms(x_ref[...], g_ref[...])


def _final_norm(x, gain, row_start, rows):
    d = x.shape[1]
    off = row_start // TOKEN_TILE
    return pl.pallas_call(
        _final_norm_kernel,
        grid=(rows // TOKEN_TILE,),
        in_specs=[pl.BlockSpec((TOKEN_TILE, d), lambda i: (i + off, 0)),
                  pl.BlockSpec((1, d), lambda i: (0, 0))],
        out_specs=pl.BlockSpec((TOKEN_TILE, d), lambda i: (i, 0)),
        out_shape=jax.ShapeDtypeStruct((rows, d), F32),
        compiler_params=_params(("parallel",), 32),
        name="final_norm",
    )(x, gain.reshape(1, d))


def _group_roll(x, step):
    rows, w = x.shape
    x3 = x.reshape(rows // V7X_SUBLANES, V7X_SUBLANES, w)
    return pltpu.roll(x3, step, axis=1).reshape(rows, w)


def _lru_kernel(xa_ref, ga_ref, h0f_ref, h0b_ref, cw_ref, cb_ref, wa_ref, ba_ref, wx_ref, bx_ref,
                lam_ref, y_ref, lf_ref, lb_ref, a_scr, u_scr, hf_scr, hb_scr):
    seq, w = xa_ref.shape
    groups = seq // V7X_SUBLANES
    xa = xa_ref[...]
    row = lax.broadcasted_iota(jnp.int32, (seq, w), 0)
    sub = row & (V7X_SUBLANES - 1)

    def tap(offset):
        if offset == 0:
            return xa
        shifted = pltpu.roll(xa, (-offset) % seq, axis=0)
        valid = (row + offset >= 0) & (row + offset < seq)
        return jnp.where(valid, shifted, 0.0)

    xc = cb_ref[...]
    for j in range(CONV_W):
        xc = xc + tap(j - CONV_W // 2) * cw_ref[pl.ds(j, 1), :]
    xcb = xc.astype(BF16)

    for direction, (h0_ref, h_scr, last_ref) in enumerate(((h0f_ref, hf_scr, lf_ref),
                                                            (h0b_ref, hb_scr, lb_ref))):
        reverse = direction == 1
        r = jax.nn.sigmoid(jnp.dot(xcb, wa_ref[direction], preferred_element_type=F32) + ba_ref[direction])
        gi = jax.nn.sigmoid(jnp.dot(xcb, wx_ref[direction], preferred_element_type=F32) + bx_ref[direction])
        neg_lam = -lam_ref[direction]
        softplus = jnp.maximum(neg_lam, 0.0) + jnp.log1p(jnp.exp(-jnp.abs(neg_lam)))
        log_a = -LRU_C * r * softplus
        a = jnp.exp(log_a)
        m2 = -jnp.tanh(log_a) * (a * a + 1.0)
        u = jnp.where(m2 > 0.0, m2 * lax.rsqrt(m2), 0.0) * (gi * xc)
        for step in (1, 2, 4):
            if reverse:
                a_nb = _group_roll(a, V7X_SUBLANES - step)
                u_nb = _group_roll(u, V7X_SUBLANES - step)
                valid = sub < V7X_SUBLANES - step
            else:
                a_nb = _group_roll(a, step)
                u_nb = _group_roll(u, step)
                valid = sub >= step
            u = a * jnp.where(valid, u_nb, 0.0) + u
            a = a * jnp.where(valid, a_nb, 1.0)
        a_scr[...] = a
        u_scr[...] = u

        def carry_step(g, carry, reverse=reverse, h_scr=h_scr):
            gg = groups - 1 - g if reverse else g
            off = pl.multiple_of(gg * V7X_SUBLANES, V7X_SUBLANES)
            h = a_scr[pl.ds(off, V7X_SUBLANES), :] * carry + u_scr[pl.ds(off, V7X_SUBLANES), :]
            h_scr[pl.ds(off, V7X_SUBLANES), :] = h
            return h[0:1] if reverse else h[V7X_SUBLANES - 1:V7X_SUBLANES]

        last_ref[...] = lax.fori_loop(0, groups, carry_step, h0_ref[...], unroll=4)

    y_ref[...] = ((hf_scr[...] + hb_scr[...]) * jax.nn.gelu(ga_ref[...])).astype(BF16)


def _lru(p_ab, h0f, h0b, prm, seq_len, n_seq, row_start):
    lru_w = h0f.shape[-1]
    cw = LRU_COLS
    ncb = lru_w // cw
    sb = row_start // seq_len
    vec = lambda: pl.BlockSpec((None, 1, cw), lambda s, c: (s, 0, c))
    par2 = lambda: pl.BlockSpec((None, 2, 1, cw), lambda s, c: (prm["idx"], 0, 0, c))
    gate = lambda: pl.BlockSpec((None, 2, None, cw, cw), lambda s, c: (prm["idx"], 0, c, 0, 0))
    return pl.pallas_call(
        _lru_kernel,
        grid=(n_seq, ncb),
        in_specs=[
            pl.BlockSpec((seq_len, cw), lambda s, c: (s + sb, c)),
            pl.BlockSpec((seq_len, cw), lambda s, c: (s + sb, c + ncb)),
            vec(), vec(),
            pl.BlockSpec((None, CONV_W, cw), lambda s, c: (prm["idx"], 0, c)),
            pl.BlockSpec((None, 1, cw), lambda s, c: (prm["idx"], 0, c)),
            gate(), par2(), gate(), par2(), par2(),
        ],
        out_specs=[
            pl.BlockSpec((seq_len, cw), lambda s, c: (s, c)),
            vec(), vec(),
        ],
        out_shape=[
            jax.ShapeDtypeStruct((n_seq * seq_len, lru_w), BF16),
            jax.ShapeDtypeStruct((n_seq, 1, lru_w), F32),
            jax.ShapeDtypeStruct((n_seq, 1, lru_w), F32),
        ],
        scratch_shapes=[pltpu.VMEM((seq_len, cw), F32)] * 4,
        compiler_params=_params(("parallel", "parallel"), 40),
        name="lru",
    )(p_ab, p_ab, h0f.reshape(n_seq, 1, lru_w), h0b.reshape(n_seq, 1, lru_w), prm["conv_w"],
      prm["conv_b"], prm["wa"], prm["ba"], prm["wx"], prm["bx"], prm["lam"])


def _block_diag_tiles(w, tile):
    n, two, nb, bw, _ = w.shape
    per = tile // bw
    w = w.reshape(n, two, nb // per, per, bw, bw)
    eye = jnp.eye(per, dtype=w.dtype)
    out = jnp.einsum("ndgpij,pq->ndgpiqj", w, eye)
    return out.reshape(n, two, nb // per, tile, tile)


def _softmax_attend(q, ks, vs, biases, scale):
    scores = []
    for k, b in zip(ks, biases):
        s = lax.dot_general(q, k, (((1,), (1,)), ((), ())), preferred_element_type=F32)
        if scale is not None:
            s = s * scale
        scores.append(s if b is None else s + b)
    m = scores[0].max(axis=-1, keepdims=True)
    for s in scores[1:]:
        m = jnp.maximum(m, s.max(axis=-1, keepdims=True))
    denom = None
    out = None
    for s, v in zip(scores, vs):
        e = jnp.exp(s - m)
        part = e.sum(axis=-1, keepdims=True)
        denom = part if denom is None else denom + part
        o = jnp.dot(e.astype(BF16), v, preferred_element_type=F32)
        out = o if out is None else out + o
    return out / denom


def _upper_half(shape):
    lane = lax.broadcasted_iota(jnp.int32, shape, len(shape) - 1)
    return lane % V7X_LANES >= V7X_LANES // 2


def _split_pair(q):
    up = _upper_half(q.shape)
    zero = jnp.zeros_like(q)
    return jnp.concatenate([jnp.where(up, zero, q), jnp.where(up, q, zero)], axis=0)


def _merge_pair(o):
    m = o.shape[0] // 2
    return jnp.where(_upper_half((m, o.shape[1])), o[m:], o[:m])


def _dup_head(block, use_high):
    swapped = pltpu.roll(block, V7X_LANES // 2, axis=1)
    keep = _upper_half(block.shape) == use_high
    return jnp.where(keep, block, swapped)


def _fold_scale(q, scale):
    return (q * scale, None) if _is_pow2(scale) else (q, scale)


def _na_ctx_kernel(q_ref, k_ref, v_ref, o_ref, *, scale):
    for pair in range(q_ref.shape[1] // V7X_LANES):
        sl = pl.ds(pair * V7X_LANES, V7X_LANES)
        q, sc = _fold_scale(q_ref[:, sl], scale)
        o = _softmax_attend(_split_pair(q.astype(BF16)), [k_ref[:, sl].astype(BF16)],
                            [v_ref[:, sl].astype(BF16)], [None], sc)
        o_ref[:, sl] = _merge_pair(o).astype(o_ref.dtype)


def _na_ctx(p_ab, n_seq, seq_len, q_blk, na_w, scale):
    blk = lambda col: pl.BlockSpec((seq_len, na_w), lambda b: (b, col))
    return pl.pallas_call(
        functools.partial(_na_ctx_kernel, scale=scale),
        grid=(n_seq,),
        in_specs=[blk(q_blk), blk(q_blk + 1), blk(q_blk + 2)],
        out_specs=pl.BlockSpec((seq_len, na_w), lambda b: (b, 0)),
        out_shape=jax.ShapeDtypeStruct((n_seq * seq_len, na_w), BF16),
        compiler_params=_params(("parallel",), 40),
        name="na_ctx",
    )(p_ab, p_ab, p_ab)


def _na_tile_index(seq_len):
    rows_n = seq_len // GRID_W
    kr = min(NA_KR, rows_n)
    rows_per_chunk = NA_Q_CHUNK // GRID_W
    n_chunks = rows_n // rows_per_chunk
    win_rows = NA_WIN // GRID_W
    table = []
    for r in range(rows_n):
        r_start = min(max(r - kr // 2, 0), rows_n - kr)
        base = 0 if r // rows_per_chunk < n_chunks // 2 else rows_n - win_rows
        table.append([k - r + NA_KR - 1 if r_start <= k < r_start + kr else 2 * NA_KR - 1
                      for k in range(base, base + win_rows)])
    return table


def _na_lat_kernel(q_ref, k_ref, v_ref, kc_ref, vc_ref, tiles_ref, o_ref, bias_ref, *, scale):
    lq = q_ref.shape[0]
    n_chunks = lq // NA_Q_CHUNK
    rows_per_chunk = NA_Q_CHUNK // GRID_W

    @pl.when(pl.program_id(1) == 0)
    def _():
        low = lax.broadcasted_iota(jnp.int32, (GRID_W, V7X_LANES), 1) < GRID_W
        for half in (0, 1):
            for r, slots in enumerate(_na_tile_index(lq)):
                c, rr = divmod(r, rows_per_chunk)
                rows = pl.ds(half * NA_Q_CHUNK + rr * GRID_W, GRID_W)
                for s in range(0, len(slots), 2):
                    blk = jnp.where(low, tiles_ref[half, slots[s]], tiles_ref[half, slots[s + 1]])
                    bias_ref[c, rows, pl.ds(s * GRID_W, V7X_LANES)] = blk

    kcb = kc_ref[...].astype(BF16)
    vcb = vc_ref[...].astype(BF16)
    for c in range(n_chunks):
        win = 0 if c < n_chunks // 2 else lq - NA_WIN
        kb = k_ref[pl.ds(win, NA_WIN), :].astype(BF16)
        vb = v_ref[pl.ds(win, NA_WIN), :].astype(BF16)
        rows = pl.ds(c * NA_Q_CHUNK, NA_Q_CHUNK)
        q, sc = _fold_scale(q_ref[rows, :], scale)
        o = _softmax_attend(_split_pair(q.astype(BF16)), [kb, kcb], [vb, vcb], [bias_ref[c], None], sc)
        o_ref[rows, :] = _merge_pair(o).astype(o_ref.dtype)


def _na_lat(p_ab, cache_k, cache_v, tiles, idx, n_seq, seq_len, row_start, q_col, n_pairs, scale):
    sb = row_start // seq_len
    past = cache_k.shape[2]
    blk = lambda col: pl.BlockSpec((seq_len, V7X_LANES), lambda h, b: (b + sb, col + h))
    cblk = lambda: pl.BlockSpec((None, None, past, V7X_LANES), lambda h, b: (b, idx, 0, h))
    n_chunks = seq_len // NA_Q_CHUNK
    return pl.pallas_call(
        functools.partial(_na_lat_kernel, scale=scale),
        grid=(n_pairs, n_seq),
        in_specs=[blk(q_col), blk(q_col + n_pairs), blk(q_col + 2 * n_pairs), cblk(), cblk(),
                  pl.BlockSpec((None, 2, 2 * NA_KR, GRID_W, V7X_LANES), lambda h, b: (idx, h, 0, 0, 0))],
        out_specs=pl.BlockSpec((seq_len, V7X_LANES), lambda h, b: (b, h)),
        out_shape=jax.ShapeDtypeStruct((n_seq * seq_len, n_pairs * V7X_LANES), BF16),
        scratch_shapes=[pltpu.VMEM((n_chunks, 2 * NA_Q_CHUNK, NA_WIN), F32)],
        compiler_params=_params(("parallel", "arbitrary"), 56),
        name="na_lat",
    )(p_ab, p_ab, p_ab, cache_k, cache_v, tiles)


def _na_bias_tiles(tab):
    lead = tab.shape[:-1]
    edge = GRID_W - NA_KC
    vec = jnp.pad(tab, [(0, 0)] * len(lead) + [(edge, edge + 1)], mode="edge")
    skew = jnp.broadcast_to(vec[..., None, :], lead + (GRID_W, 2 * GRID_W))
    skew = skew.reshape(lead + (2 * GRID_W * GRID_W,))[..., :GRID_W * (2 * GRID_W - 1)]
    toep = skew.reshape(lead + (GRID_W, 2 * GRID_W - 1))[..., GRID_W - 1:]
    cols = np.arange(GRID_W)
    c_start = np.clip(cols - NA_KC // 2, 0, GRID_W - NA_KC)
    col_ok = (cols[None, :] >= c_start[:, None]) & (cols[None, :] < c_start[:, None] + NA_KC)
    tiles = jnp.where(col_ok, toep, NEG_BIAS)
    masked = jnp.full(lead[:-1] + (1, GRID_W, GRID_W), NEG_BIAS, F32)
    tiles = jnp.concatenate([tiles, masked], axis=-3)
    return jnp.concatenate([tiles, tiles], axis=-1)


def _head_rms(x, ones_bd, gain, head_dim):
    sq = x * x
    hi = sq.astype(BF16)
    lo = (sq - hi.astype(F32)).astype(BF16)
    ss = jnp.dot(hi, ones_bd, preferred_element_type=F32) + jnp.dot(lo, ones_bd, preferred_element_type=F32)
    return x * lax.rsqrt(ss * (1.0 / head_dim) + EPS) * gain


def _rope(x, cos, sin_lo, sin_hi):
    w = x.shape[1]
    reps = w // V7X_LANES
    tile = lambda t: t if reps == 1 else jnp.concatenate([t] * reps, axis=1)
    quarter = 16
    return (x * tile(cos) + pltpu.roll(x, w - quarter, axis=1) * tile(sin_lo)
            + pltpu.roll(x, quarter, axis=1) * tile(sin_hi))


def _cd_post_kernel(p_ref, ones_ref, qg_ref, kg_ref, mqg_ref, mkg_ref, wuq_ref, wukv_ref, *rest,
                    rope, head_dim, q_w, kv_w, rank, q_scale):
    if rope:
        cos_ref, slo_ref, shi_ref = rest[:3]
        rest = rest[3:]
        tabs = (cos_ref[...], slo_ref[...], shi_ref[...])
    q_ref, k_ref, qn_ref, qr_ref, ckv_ref, kn_ref, vm_ref, kr_ref = rest
    c0 = 0
    qc = p_ref[:, pl.ds(c0, q_w)]
    c0 += q_w
    kc = p_ref[:, pl.ds(c0, kv_w)]
    c0 += 2 * kv_w
    qa = p_ref[:, pl.ds(c0, rank)]
    c0 += rank
    ckv = p_ref[:, pl.ds(c0, rank)]
    c0 += rank
    kr = p_ref[:, pl.ds(c0, V7X_LANES)]

    qc = _head_rms(qc, ones_ref[...], qg_ref[...], head_dim)
    kc = _head_rms(kc, ones_ref[pl.ds(0, kv_w), pl.ds(0, kv_w)], kg_ref[...], head_dim)
    qd = jnp.dot(_rms(qa, mqg_ref[...]).astype(BF16), wuq_ref[...], preferred_element_type=F32)
    n_nope = qn_ref.shape[1]
    qn = qd[:, :n_nope]
    qr = qd[:, n_nope:]
    ckv = _rms(ckv, mkg_ref[...])
    if rope:
        qc = _rope(qc, *tabs)
        kc = _rope(kc, *tabs)
        qr = _rope(qr, *tabs)
        kr = _rope(kr, *tabs)
    q_ref[...] = (qc * q_scale).astype(BF16)
    k_ref[...] = kc
    qn_ref[...] = qn.astype(BF16)
    qr_ref[...] = qr.astype(BF16)
    ckv_ref[...] = ckv
    kv = jnp.dot(ckv.astype(BF16), wukv_ref[...], preferred_element_type=F32)
    kn_ref[...] = kv[:, :n_nope].astype(BF16)
    vm_ref[...] = kv[:, n_nope:].astype(BF16)
    kr_ref[...] = kr


def _cd_post(p_cd, prm, row_start, rows, rope_tabs, dims, q_scale):
    q_w, kv_w, rank, head_dim, n_nope, n_rope, n_v = dims
    tm = 256
    off = row_start // tm
    n_in = p_cd.shape[1]
    idx = prm["idx"]
    rope = rope_tabs is not None
    const = lambda shape: pl.BlockSpec(shape, lambda i: (0,) * len(shape))
    layer = lambda shape: pl.BlockSpec((None,) + shape, lambda i: (idx,) + (0,) * len(shape))
    in_specs = [
        pl.BlockSpec((tm, n_in), lambda i: (i + off, 0)),
        const((q_w, q_w)), layer((1, q_w)), layer((1, kv_w)), layer((1, rank)), layer((1, rank)),
        layer((rank, n_nope + n_rope)), layer((rank, n_nope + n_v)),
    ]
    args = [p_cd, prm["ones_bd"], prm["q_gain"], prm["k_gain"], prm["mla_q_gain"], prm["mla_kv_gain"],
            prm["w_uq"], prm["w_ukv"]]
    if rope:
        seq_tiles = rope_tabs[0].shape[0] // tm
        in_specs += [pl.BlockSpec((tm, V7X_LANES), lambda i: (i % seq_tiles, 0))] * 3
        args += list(rope_tabs)
    widths = [(q_w, BF16), (kv_w, F32), (n_nope, BF16), (n_rope, BF16), (rank, F32), (n_nope, BF16),
              (n_v, BF16), (V7X_LANES, F32)]
    kern = functools.partial(_cd_post_kernel, rope=rope, head_dim=head_dim, q_w=q_w, kv_w=kv_w, rank=rank,
                             q_scale=q_scale)
    return pl.pallas_call(
        kern,
        grid=(rows // tm,),
        in_specs=in_specs,
        out_specs=[pl.BlockSpec((tm, w), lambda i: (i, 0)) for w, _ in widths],
        out_shape=[jax.ShapeDtypeStruct((rows, w), dt) for w, dt in widths],
        compiler_params=_params(("parallel",), 48),
        name="cd_post",
    )(*args)


def _rope_tables(seq_len, head_dim):
    half = head_dim // 2
    nf = half // 2
    t = np.arange(seq_len)
    inv_freq = (1.0 / (ROPE_THETA ** (np.arange(nf, dtype=np.float32) / nf))).astype(np.float32)
    zeros = np.zeros((seq_len, nf), np.float32)
    cos, slo, shi = [], [], []
    for pos in (t // GRID_W, t % GRID_W):
        ang = pos.astype(np.float32)[:, None] * inv_freq[None, :]
        c, s = np.cos(ang).astype(np.float32), np.sin(ang).astype(np.float32)
        cos += [c, c]
        slo += [-s, zeros]
        shi += [zeros, s]
    reps = V7X_LANES // head_dim
    return tuple(jnp.asarray(np.tile(np.concatenate(x, axis=1), (1, reps))) for x in (cos, slo, shi))


def _ckv_up_kernel(c_ref, w_ref, kn_ref, vm_ref):
    kv = jnp.dot(c_ref[...].astype(BF16), w_ref[...], preferred_element_type=F32)
    n = kn_ref.shape[1]
    kn_ref[...] = kv[:, :n].astype(BF16)
    vm_ref[...] = kv[:, n:].astype(BF16)


def _ckv_up(ckv, w_ukv, idx, n_nope):
    n_seq, _, tm, rank = ckv.shape
    rows = n_seq * tm
    n = w_ukv.shape[2]
    return pl.pallas_call(
        _ckv_up_kernel,
        grid=(rows // tm,),
        in_specs=[pl.BlockSpec((None, None, tm, rank), lambda i: (i, idx, 0, 0)),
                  pl.BlockSpec((None, rank, n), lambda i: (idx, 0, 0))],
        out_specs=[pl.BlockSpec((tm, n_nope), lambda i: (i, 0)), pl.BlockSpec((tm, n - n_nope), lambda i: (i, 0))],
        out_shape=[jax.ShapeDtypeStruct((rows, n_nope), BF16), jax.ShapeDtypeStruct((rows, n - n_nope), BF16)],
        compiler_params=_params(("parallel",), 32),
        name="ckv_up",
    )(ckv, w_ukv)


def _gqa_kernel(q_ref, k_ref, v_ref, *rest, scale, has_cache, group_w):
    if has_cache:
        kc_ref, vc_ref, o_ref = rest
    else:
        (o_ref,) = rest
    lq = q_ref.shape[0]
    n_local = q_ref.shape[1] // group_w
    n_pairs = group_w // V7X_LANES
    for hl in range(n_local):
        if n_local == 1:
            high = (pl.program_id(1) % 2) == 1
            pick = lambda ref: ref[...]
        else:
            high = hl % 2 == 1
            pick = lambda ref, hl=hl: ref[:, pl.ds((hl // 2) * V7X_LANES, V7X_LANES)]
        ks = [_dup_head(pick(k_ref), high).astype(BF16)]
        vs = [_dup_head(pick(v_ref), high).astype(BF16)]
        if has_cache:
            ks.append(_dup_head(pick(kc_ref), high).astype(BF16))
            vs.append(_dup_head(pick(vc_ref), high).astype(BF16))
        for c in range(lq // Q_CHUNK):
            rows = pl.ds(c * Q_CHUNK, Q_CHUNK)
            cols = [pl.ds(hl * group_w + p * V7X_LANES, V7X_LANES) for p in range(n_pairs)]
            q = jnp.concatenate([_split_pair(q_ref[rows, cl]) for cl in cols], axis=0)
            o = _softmax_attend(q, ks, vs, [None] * len(ks), scale)
            for p, cl in enumerate(cols):
                o_ref[rows, cl] = _merge_pair(o[2 * p * Q_CHUNK:2 * (p + 1) * Q_CHUNK]).astype(o_ref.dtype)


def _gqa(q, k, p_cd, v_col, cache_k, cache_v, idx, n_seq, seq_len, row_start, n_kv, heads_per_step, scale):
    sb = row_start // seq_len
    group_w = q.shape[1] // n_kv
    has_cache = cache_k is not None
    if heads_per_step == 1:
        kv_w, kv_blk, v_blk = V7X_LANES, (lambda h: h // 2), (lambda h: v_col + h // 2)
    else:
        kv_w = k.shape[1]
        kv_blk, v_blk = (lambda h: 0), (lambda h: v_col * V7X_LANES // kv_w)
    in_specs = [
        pl.BlockSpec((seq_len, group_w * heads_per_step), lambda b, h: (b, h)),
        pl.BlockSpec((seq_len, kv_w), lambda b, h: (b, kv_blk(h))),
        pl.BlockSpec((seq_len, kv_w), lambda b, h: (b + sb, v_blk(h))),
    ]
    args = [q, k, p_cd]
    if has_cache:
        past = cache_k.shape[2]
        in_specs += [pl.BlockSpec((None, None, past, kv_w), lambda b, h: (b, idx, 0, kv_blk(h)))] * 2
        args += [cache_k, cache_v]
    return pl.pallas_call(
        functools.partial(_gqa_kernel, scale=scale, has_cache=has_cache, group_w=group_w),
        grid=(n_seq, n_kv // heads_per_step),
        in_specs=in_specs,
        out_specs=pl.BlockSpec((seq_len, group_w * heads_per_step), lambda b, h: (b, h)),
        out_shape=jax.ShapeDtypeStruct(q.shape, BF16),
        compiler_params=_params(("parallel", "parallel"), 56),
        name="gqa",
    )(*args)


def _mla_kernel(qn_ref, qr_ref, kn_ref, kr_ref, v_ref, *rest, scale, has_cache):
    if has_cache:
        knc_ref, krc_ref, vc_ref, o_ref = rest
        krc = krc_ref[...].astype(BF16)
    else:
        (o_ref,) = rest
    lq = qn_ref.shape[0]
    qc = min(2 * Q_CHUNK, lq)
    n_local = qn_ref.shape[1] // V7X_LANES
    kr = kr_ref[...].astype(BF16)
    for hl in range(n_local):
        sl = pl.ds(hl * V7X_LANES, V7X_LANES)
        if n_local == 1:
            high = (pl.program_id(1) % 2) == 1
            pair = pl.ds(0, V7X_LANES)
        else:
            high = hl % 2 == 1
            pair = pl.ds((hl // 2) * V7X_LANES, V7X_LANES)
        ks = [jnp.concatenate([kn_ref[:, sl], kr], axis=1)]
        vs = [v_ref[:, sl]]
        if has_cache:
            ks.append(jnp.concatenate([knc_ref[:, sl], krc], axis=1))
            vs.append(vc_ref[:, sl])
        for c in range(lq // qc):
            rows = pl.ds(c * qc, qc)
            qr = qr_ref[rows, pair]
            qr = jnp.where(_upper_half(qr.shape) == high, qr, jnp.zeros_like(qr))
            q = jnp.concatenate([qn_ref[rows, sl], qr], axis=1)
            o_ref[rows, sl] = _softmax_attend(q, ks, vs, [None] * len(ks), scale).astype(o_ref.dtype)


def _mla(qn, qr, kn, kr2, vm, cache, n_seq, seq_len, heads_per_step, scale):
    has_cache = cache is not None
    n_heads = vm.shape[1] // V7X_LANES
    hw = heads_per_step * V7X_LANES
    head = lambda rows: pl.BlockSpec((rows, hw), lambda b, h: (b, h))
    if heads_per_step == 1:
        rope_q = pl.BlockSpec((seq_len, V7X_LANES), lambda b, h: (b, h // 2))
    else:
        rope_q = pl.BlockSpec((seq_len, qr.shape[1]), lambda b, h: (b, 0))
    in_specs = [head(seq_len), rope_q, head(seq_len),
                pl.BlockSpec((seq_len, V7X_LANES), lambda b, h: (b, 0)),
                head(seq_len)]
    args = [qn, qr, kn, kr2, vm]
    if has_cache:
        knc, krc, vmc, idx = cache
        past = krc.shape[2]
        in_specs += [head(past),
                     pl.BlockSpec((None, None, past, V7X_LANES), lambda b, h: (b, idx, 0, 0)),
                     head(past)]
        args += [knc, krc, vmc]
    return pl.pallas_call(
        functools.partial(_mla_kernel, scale=scale, has_cache=has_cache),
        grid=(n_seq, n_heads // heads_per_step),
        in_specs=in_specs,
        out_specs=head(seq_len),
        out_shape=jax.ShapeDtypeStruct(vm.shape, BF16),
        compiler_params=_params(("parallel", "parallel"), 48),
        name="mla",
    )(*args)


def kernel(x_prompt, x_sample, state_lru_fwd, state_lru_bwd, cache_na_k, cache_na_v, cache_gqa_k, cache_gqa_v, cache_mla_ckv, cache_mla_krope, c, c_ctx, w_mod, b_mod, norm_gain, w_ffn_in, w_ffn_out, w_in_ab, conv_w, conv_b, lru_wa, lru_ba, lru_wx, lru_bx, lru_lambda, na_bias, w_out_ab, w_in_cd, gqa_q_gain, gqa_k_gain, mla_q_gain, mla_kv_gain, mla_w_uq, mla_w_uk, mla_w_uv, w_out_cd, final_gain):
    batch, seq, d = x_prompt.shape
    dec_batch, dec_seq, _ = x_sample.shape
    depth = w_mod.shape[0]
    n_even, n_odd = w_in_ab.shape[0], w_in_cd.shape[0]
    n_p = batch * seq
    n_s = dec_batch * dec_seq
    past = cache_na_k.shape[2]
    lru_w = state_lru_fwd.shape[-1]
    na_heads, na_dh = cache_na_k.shape[3], cache_na_k.shape[4]
    na_w = na_heads * na_dh
    gqa_kv, gqa_dh = cache_gqa_k.shape[3], cache_gqa_k.shape[4]
    kv_w = gqa_kv * gqa_dh
    rank = cache_mla_ckv.shape[-1]
    rope_w = cache_mla_krope.shape[-1]
    q_w = w_in_cd.shape[2] - 2 * kv_w - 2 * rank - rope_w
    mla_heads = mla_w_uk.shape[2] // MLA_NOPE
    n_nope = mla_heads * MLA_NOPE
    n_rope = mla_heads * rope_w
    n_v = mla_heads * MLA_V
    mla_qk = MLA_NOPE + rope_w
    assert mla_q_gain.shape[-1] == rank and 2 * rope_w == V7X_LANES
    assert n_p % dec_seq == 0 and dec_seq % PROJ_TILE == 0 and seq % Q_CHUNK == 0
    assert dec_seq // GRID_W == 2 * NA_KR and na_dh == GRID_W and gqa_dh == GRID_W
    assert w_in_ab.shape[2] == 2 * lru_w + 3 * na_w and lru_w == na_w
    assert (q_w // gqa_kv) % V7X_LANES == 0 and q_w % kv_w == 0

    x = jnp.concatenate([x_prompt.reshape(n_p, d), x_sample.reshape(n_s, d)], axis=0)
    cond = jnp.concatenate([c_ctx[None, :], c, jnp.zeros((MOD_ROWS - 1 - dec_batch, d), F32)], axis=0)
    mods = _modulation(cond, w_mod, b_mod)
    gains = norm_gain.reshape(depth * 3, 1, d)

    ffn_w = (w_ffn_in[0, 0].astype(BF16), w_ffn_out[0, 0].astype(BF16))

    def ffn_step(x, ffn_w, layer, which):
        last = layer == depth - 1 and which == 1
        nxt = None if last else (w_ffn_in, w_ffn_out, layer + which, 1 - which)
        out = _ffn(x, mods, gains, ffn_w[0], ffn_w[1], layer, which, nxt, n_p, dec_seq)
        return out[0], tuple(out[1:])
    w_in_ab_bf = w_in_ab.astype(BF16)
    w_out_ab_bf = w_out_ab.astype(BF16)
    w_in_cd_bf = jnp.concatenate([w_in_cd, w_in_cd[:, :, -rope_w:]], axis=2).astype(BF16)
    w_out_cd_bf = w_out_cd.astype(BF16)

    lru_prm = {
        "conv_w": conv_w, "conv_b": conv_b.reshape(n_even, 1, lru_w),
        "wa": _block_diag_tiles(lru_wa, LRU_COLS).astype(BF16),
        "wx": _block_diag_tiles(lru_wx, LRU_COLS).astype(BF16),
        "ba": lru_ba.reshape(n_even, 2, 1, lru_w), "bx": lru_bx.reshape(n_even, 2, 1, lru_w),
        "lam": lru_lambda.reshape(n_even, 2, 1, lru_w),
    }
    zeros_state = jnp.zeros((batch, lru_w), F32)
    na_k_ctx = cache_na_k.reshape(dec_batch, n_even, past, na_w)
    na_v_ctx = cache_na_v.reshape(dec_batch, n_even, past, na_w)
    na_tiles = _na_bias_tiles(na_bias)

    w_uq = mla_w_uq.reshape(n_odd, rank, mla_heads, mla_qk)
    w_uq = jnp.concatenate([w_uq[..., :MLA_NOPE].reshape(n_odd, rank, n_nope),
                            w_uq[..., MLA_NOPE:].reshape(n_odd, rank, n_rope)], axis=2).astype(BF16)
    gqa_scale = gqa_dh ** -0.5
    q_scale = gqa_scale if _is_pow2(gqa_scale) else 1.0
    cd_prm = {
        "ones_bd": jnp.asarray(np.kron(np.eye(q_w // gqa_dh, dtype=np.float32),
                                       np.ones((gqa_dh, gqa_dh), np.float32)), BF16),
        "q_gain": jnp.tile(gqa_q_gain, (1, q_w // gqa_dh)).reshape(n_odd, 1, q_w),
        "k_gain": jnp.tile(gqa_k_gain, (1, gqa_kv)).reshape(n_odd, 1, kv_w),
        "mla_q_gain": mla_q_gain.reshape(n_odd, 1, rank),
        "mla_kv_gain": mla_kv_gain.reshape(n_odd, 1, rank),
        "w_uq": w_uq,
        "w_ukv": jnp.concatenate([mla_w_uk, mla_w_uv], axis=2).astype(BF16),
    }
    rope_tabs = _rope_tables(dec_seq, gqa_dh)
    gqa_k_ctx = cache_gqa_k.reshape(dec_batch, n_odd, past, kv_w)
    gqa_v_ctx = cache_gqa_v.reshape(dec_batch, n_odd, past, kv_w)
    krope_ctx = jnp.concatenate([cache_mla_krope, cache_mla_krope], axis=-1)

    st_f, st_b, na_k, na_v, gq_k, gq_v, ml_c, ml_r = [], [], [], [], [], [], [], []
    for layer in range(depth):
        jdx = layer // 2
        x, ffn_w = ffn_step(x, ffn_w, layer, 0)
        if layer % 2 == 0:
            p_ab = _inproj(x, mods, gains, w_in_ab_bf, layer, jdx, 1024, n_p, dec_seq)
            prm = dict(lru_prm, idx=jdx)
            ya_p, lf, lb = _lru(p_ab, zeros_state, zeros_state, prm, seq, batch, 0)
            ya_s, _, _ = _lru(p_ab, state_lru_fwd[:, jdx], state_lru_bwd[:, jdx], prm, dec_seq, dec_batch, n_p)
            scale = na_dh ** -0.5
            yb_p = _na_ctx(p_ab, batch, seq, 2 * lru_w // na_w, na_w, scale)
            yb_s = _na_lat(p_ab, na_k_ctx, na_v_ctx, na_tiles, jdx, dec_batch, dec_seq, n_p,
                           2 * lru_w // V7X_LANES, na_w // V7X_LANES, scale)
            x = _outproj(x, mods, layer, ya_p, ya_s, yb_p, yb_s, w_out_ab_bf, jdx, n_p, dec_seq)
            st_f.append(lf.reshape(batch, lru_w))
            st_b.append(lb.reshape(batch, lru_w))
            na_k.append(p_ab[:n_p, 2 * lru_w + na_w:2 * lru_w + 2 * na_w].reshape(batch, seq, na_heads, na_dh))
            na_v.append(p_ab[:n_p, 2 * lru_w + 2 * na_w:].reshape(batch, seq, na_heads, na_dh))
        else:
            p_cd = _inproj(x, mods, gains, w_in_cd_bf, layer, jdx, w_in_cd_bf.shape[2] // 3, n_p, dec_seq)
            prm = dict(cd_prm, idx=jdx)
            dims = (q_w, kv_w, rank, gqa_dh, n_nope, n_rope, n_v)
            qp, kp, qnp_, qrp, ckvp, knp_, vmp, krp = _cd_post(p_cd, prm, 0, n_p, None, dims, q_scale)
            qs, ks_, qns, qrs, _, kns, vms, krs = _cd_post(p_cd, prm, n_p, n_s, rope_tabs, dims, q_scale)
            v_col = (q_w + kv_w) // V7X_LANES
            att_scale = None if q_scale != 1.0 else gqa_scale
            yc_p = _gqa(qp, kp, p_cd, v_col, None, None, jdx, batch, seq, 0, gqa_kv, gqa_kv, att_scale)
            yc_s = _gqa(qs, ks_, p_cd, v_col, gqa_k_ctx, gqa_v_ctx, jdx, dec_batch, dec_seq, n_p,
                        gqa_kv, 1, att_scale)
            knc, vmc = _ckv_up(cache_mla_ckv, cd_prm["w_ukv"], jdx, n_nope)
            yd_p = _mla(qnp_, qrp, knp_, krp, vmp, None, batch, seq, mla_heads, mla_qk ** -0.5)
            yd_s = _mla(qns, qrs, kns, krs, vms, (knc, krope_ctx, vmc, jdx), dec_batch, dec_seq, 1,
                        mla_qk ** -0.5)
            x = _outproj(x, mods, layer, yc_p, yc_s, yd_p, yd_s, w_out_cd_bf, jdx, n_p, dec_seq)
            gq_k.append(kp.reshape(batch, seq, gqa_kv, gqa_dh))
            gq_v.append(p_cd[:n_p, q_w + kv_w:q_w + 2 * kv_w].reshape(batch, seq, gqa_kv, gqa_dh))
            ml_c.append(ckvp.reshape(batch, seq, rank))
            ml_r.append(krp[:, :rope_w].reshape(batch, seq, rope_w))
        x, ffn_w = ffn_step(x, ffn_w, layer, 1)

    y_prompt = _final_norm(x, final_gain, 0, n_p).reshape(batch, seq, d)
    y_sample = _final_norm(x, final_gain, n_p, n_s).reshape(dec_batch, dec_seq, d)
    stack = lambda xs: jnp.stack(xs, axis=1)
    return (y_prompt, y_sample, stack(st_f), stack(st_b), stack(na_k), stack(na_v), stack(gq_k), stack(gq_v),
            stack(ml_c), stack(ml_r))
```

```python
import functools
import math

import numpy as np

import jax
import jax.numpy as jnp
from jax import lax
from jax.experimental import pallas as pl
from jax.experimental.pallas import tpu as pltpu

F32 = jnp.float32
BF16 = jnp.bfloat16

EPS = 1e-6
N_MOD = 9
GRID_W = 64
NA_KR = 8
NA_KC = 16
ROPE_THETA = 10000.0
LRU_C = 8.0
CONV_W = 4
MLA_NOPE = 128
MLA_V = 128
NEG_BIAS = -1e30

V7X_LANES = 128
V7X_SUBLANES = 8
MOD_ROWS = 16
TOKEN_TILE = 512
PROJ_TILE = 1024
FF_TILE = 512
Q_CHUNK = 256
NA_Q_CHUNK = 256
NA_WIN = 768
LRU_COLS = 256


def _params(sem, vmem_mib):
    return pltpu.CompilerParams(dimension_semantics=sem, vmem_limit_bytes=vmem_mib * 1024 * 1024)


def _silu(x):
    return x * jax.nn.sigmoid(x)


def _rms(x, gain):
    ms = jnp.mean(x * x, axis=-1, keepdims=True)
    return x * lax.rsqrt(ms + EPS) * gain


def _mod_row(i, n_prompt_tiles, tiles_per_seq):
    return jnp.where(i < n_prompt_tiles, 0, 1 + (i - n_prompt_tiles) // tiles_per_seq)


def _adaln(x, mod_ref, gain_ref, row, mod_base):
    d = x.shape[1]
    shift = mod_ref[pl.ds(row, 1), pl.ds(mod_base * d, d)]
    scale = mod_ref[pl.ds(row, 1), pl.ds((mod_base + 1) * d, d)]
    return (_rms(x, gain_ref[...]) * (1 + scale) + shift).astype(BF16)


def _is_pow2(v):
    return math.frexp(v)[0] == 0.5


def _mod_kernel(cond_ref, w_ref, b_ref, o_ref):
    s = _silu(cond_ref[...]).astype(BF16)
    o_ref[...] = jnp.dot(s, w_ref[...].astype(BF16), preferred_element_type=F32) + b_ref[...]


def _modulation(cond, w_mod, b_mod):
    depth, d, n = w_mod.shape
    tn = 1024
    return pl.pallas_call(
        _mod_kernel,
        grid=(depth, n // tn),
        in_specs=[
            pl.BlockSpec((MOD_ROWS, d), lambda l, j: (0, 0)),
            pl.BlockSpec((None, d, tn), lambda l, j: (l, 0, j)),
            pl.BlockSpec((None, 1, tn), lambda l, j: (l, 0, j)),
        ],
        out_specs=pl.BlockSpec((None, MOD_ROWS, tn), lambda l, j: (l, 0, j)),
        out_shape=jax.ShapeDtypeStruct((depth, MOD_ROWS, n), F32),
        compiler_params=_params(("parallel", "parallel"), 40),
        name="modulation",
    )(cond, w_mod, b_mod.reshape(depth, 1, n))


def _ffn_kernel(*refs, mod_base, n_prompt_tiles, tiles_per_seq, cast_next, split_x):
    i = pl.program_id(0)
    j = pl.program_id(1)
    if split_x:
        xp_ref, xs_ref, *refs = refs
        read_x = lambda: jnp.where(i < n_prompt_tiles, xp_ref[...], xs_ref[...])
    else:
        x_ref, *refs = refs
        read_x = lambda: x_ref[...]
    mod_ref, gain_ref, wg_ref, wu_ref, wo_ref, *rest = refs
    if cast_next:
        nin_ref, nout_ref, o_ref, cin_ref, cout_ref, h_ref = rest
    else:
        o_ref, h_ref = rest
    d = o_ref.shape[1]
    row = _mod_row(i, n_prompt_tiles, tiles_per_seq)

    @pl.when(j == 0)
    def _():
        h_ref[...] = _adaln(read_x(), mod_ref, gain_ref, row, mod_base)
        o_ref[...] = jnp.zeros_like(o_ref)

    h = h_ref[...]
    g = jnp.dot(h, wg_ref[...], preferred_element_type=F32)
    u = jnp.dot(h, wu_ref[...], preferred_element_type=F32)
    if cast_next:
        cin_ref[...] = nin_ref[...].astype(BF16)
        cout_ref[...] = nout_ref[...].astype(BF16)
    act = (_silu(g) * u).astype(BF16)
    o_ref[...] += jnp.dot(act, wo_ref[...], preferred_element_type=F32)

    @pl.when(j == pl.num_programs(1) - 1)
    def _():
        gate = mod_ref[pl.ds(row, 1), pl.ds((mod_base + 2) * d, d)]
        o_ref[...] = read_x() + (0.5 * gate) * o_ref[...]


def _ffn(x, mods, gains, w_in, w_out, layer, which, nxt, n_prompt_rows, seq_rows):
    split_x = isinstance(x, tuple)
    npt = n_prompt_rows // TOKEN_TILE
    if split_x:
        t, d = x[0].shape[0] + x[1].shape[0], x[0].shape[1]
        x_specs = [pl.BlockSpec((TOKEN_TILE, d), lambda i, j: (jnp.minimum(i, npt - 1), 0)),
                   pl.BlockSpec((TOKEN_TILE, d), lambda i, j: (jnp.maximum(i - npt, 0), 0))]
        x_args = list(x)
    else:
        t, d = x.shape
        x_specs = [pl.BlockSpec((TOKEN_TILE, d), lambda i, j: (i, 0))]
        x_args = [x]
    dff = w_out.shape[0]
    nj = dff // FF_TILE
    ni = t // TOKEN_TILE
    mod_base = 6 * which
    kern = functools.partial(_ffn_kernel, mod_base=mod_base, n_prompt_tiles=npt,
                             tiles_per_seq=seq_rows // TOKEN_TILE, cast_next=nxt is not None, split_x=split_x)
    in_specs = x_specs + [
        pl.BlockSpec((None, MOD_ROWS, N_MOD * d), lambda i, j: (layer, 0, 0)),
        pl.BlockSpec((None, 1, d), lambda i, j: (3 * layer + 2 * which, 0, 0)),
        pl.BlockSpec((d, FF_TILE), lambda i, j: (0, j)),
        pl.BlockSpec((d, FF_TILE), lambda i, j: (0, j + nj)),
        pl.BlockSpec((FF_TILE, d), lambda i, j: (j, 0)),
    ]
    args = x_args + [mods, gains, w_in, w_in, w_out]
    out_specs = [pl.BlockSpec((TOKEN_TILE, d), lambda i, j: (i, 0))]
    out_shape = [jax.ShapeDtypeStruct((t, d), F32)]
    if nxt is not None:
        f_in, f_out, nl, nw = nxt
        cast_rows = min(r for r in (16, 32, 64, 128, 256, 512)
                        if d % r == 0 and dff % r == 0 and d // r + dff // r <= ni * nj)
        n_in, n_out = d // cast_rows, dff // cast_rows
        in_blk = lambda i, j: jnp.minimum(i * nj + j, n_in - 1)
        out_blk = lambda i, j: jnp.clip(i * nj + j - n_in, 0, n_out - 1)
        in_specs += [pl.BlockSpec((None, None, cast_rows, 2 * dff), lambda i, j: (nl, nw, in_blk(i, j), 0)),
                     pl.BlockSpec((None, None, cast_rows, d), lambda i, j: (nl, nw, out_blk(i, j), 0))]
        args += [f_in, f_out]
        out_specs += [pl.BlockSpec((cast_rows, 2 * dff), lambda i, j: (in_blk(i, j), 0)),
                      pl.BlockSpec((cast_rows, d), lambda i, j: (out_blk(i, j), 0))]
        out_shape += [jax.ShapeDtypeStruct((d, 2 * dff), BF16), jax.ShapeDtypeStruct((dff, d), BF16)]
    return pl.pallas_call(
        kern,
        grid=(ni, nj),
        in_specs=in_specs,
        out_specs=out_specs,
        out_shape=out_shape,
        scratch_shapes=[pltpu.VMEM((TOKEN_TILE, d), BF16)],
        compiler_params=_params(("arbitrary", "arbitrary"), 56),
        name="ffn",
    )(*args)


def _inproj_kernel(x_ref, mod_ref, gain_ref, w_ref, o_ref, *scratch, n_prompt_tiles, tiles_per_seq):
    row = _mod_row(pl.program_id(0), n_prompt_tiles, tiles_per_seq)
    if not scratch:
        h = _adaln(x_ref[...], mod_ref, gain_ref, row, 3)
        o_ref[...] = jnp.dot(h, w_ref[...], preferred_element_type=F32)
        return
    (h_ref,) = scratch

    @pl.when(pl.program_id(1) == 0)
    def _():
        h_ref[...] = _adaln(x_ref[...], mod_ref, gain_ref, row, 3)

    o_ref[...] = jnp.dot(h_ref[...], w_ref[...], preferred_element_type=F32)


def _inproj(x, mods, gains, w, layer, w_idx, tm, tn, n_prompt_rows, seq_rows):
    t, d = x.shape
    n = w.shape[2]
    kern = functools.partial(_inproj_kernel, n_prompt_tiles=n_prompt_rows // tm, tiles_per_seq=seq_rows // tm)
    return pl.pallas_call(
        kern,
        grid=(t // tm, n // tn),
        in_specs=[
            pl.BlockSpec((tm, d), lambda i, j: (i, 0)),
            pl.BlockSpec((None, MOD_ROWS, N_MOD * d), lambda i, j: (layer, 0, 0)),
            pl.BlockSpec((None, 1, d), lambda i, j: (3 * layer + 1, 0, 0)),
            pl.BlockSpec((None, d, tn), lambda i, j: (w_idx, 0, j)),
        ],
        out_specs=pl.BlockSpec((tm, tn), lambda i, j: (i, j)),
        out_shape=jax.ShapeDtypeStruct((t, n), F32),
        scratch_shapes=[pltpu.VMEM((tm, d), BF16)] if tn < n else [],
        compiler_params=_params(("parallel", "arbitrary"), 52),
        name="inproj",
    )(x, mods, gains, w)


def _outproj_kernel(x_ref, mod_ref, ap_ref, as_ref, bp_ref, bs_ref, w_ref, o_ref, *,
                    n_prompt_tiles, tiles_per_seq):
    i = pl.program_id(0)
    d = x_ref.shape[1]
    half = ap_ref.shape[1]
    is_prompt = i < n_prompt_tiles
    ya = jnp.where(is_prompt, ap_ref[...], as_ref[...])
    yb = jnp.where(is_prompt, bp_ref[...], bs_ref[...])
    y = jnp.dot(ya, w_ref[pl.ds(0, half), :], preferred_element_type=F32)
    y = y + jnp.dot(yb, w_ref[pl.ds(half, half), :], preferred_element_type=F32)
    row = _mod_row(i, n_prompt_tiles, tiles_per_seq)
    gate = mod_ref[pl.ds(row, 1), pl.ds(5 * d, d)]
    o_ref[...] = x_ref[...] + gate * y


def _outproj(x, mods, layer, ya_p, ya_s, yb_p, yb_s, w, w_idx, n_prompt_rows, seq_rows):
    t, d = x.shape
    half = ya_p.shape[1]
    npt = n_prompt_rows // TOKEN_TILE
    kern = functools.partial(_outproj_kernel, n_prompt_tiles=npt, tiles_per_seq=seq_rows // TOKEN_TILE)
    p_map = lambda i: (jnp.minimum(i, npt - 1), 0)
    s_map = lambda i: (jnp.maximum(i - npt, 0), 0)
    return pl.pallas_call(
        kern,
        grid=(t // TOKEN_TILE,),
        in_specs=[
            pl.BlockSpec((TOKEN_TILE, d), lambda i: (i, 0)),
            pl.BlockSpec((None, MOD_ROWS, N_MOD * d), lambda i: (layer, 0, 0)),
            pl.BlockSpec((TOKEN_TILE, half), p_map),
            pl.BlockSpec((TOKEN_TILE, half), s_map),
            pl.BlockSpec((TOKEN_TILE, half), p_map),
            pl.BlockSpec((TOKEN_TILE, half), s_map),
            pl.BlockSpec((None, 2 * half, d), lambda i: (w_idx, 0, 0)),
        ],
        out_specs=pl.BlockSpec((TOKEN_TILE, d), lambda i: (i, 0)),
        out_shape=jax.ShapeDtypeStruct((t, d), F32),
        compiler_params=_params(("parallel",), 48),
        name="outproj",
    )(x, mods, ya_p, ya_s, yb_p, yb_s, w)


def _final_norm_kernel(x_ref, g_ref, o_ref):
    o_ref[...] = _rms(x_ref[...], g_ref[...])


def _final_norm(x, gain, row_start, rows):
    d = x.shape[1]
    off = row_start // TOKEN_TILE
    return pl.pallas_call(
        _final_norm_kernel,
        grid=(rows // TOKEN_TILE,),
        in_specs=[pl.BlockSpec((TOKEN_TILE, d), lambda i: (i + off, 0)),
                  pl.BlockSpec((1, d), lambda i: (0, 0))],
        out_specs=pl.BlockSpec((TOKEN_TILE, d), lambda i: (i, 0)),
        out_shape=jax.ShapeDtypeStruct((rows, d), F32),
        compiler_params=_params(("parallel",), 32),
        name="final_norm",
    )(x, gain.reshape(1, d))


def _group_roll(x, step):
    rows, w = x.shape
    x3 = x.reshape(rows // V7X_SUBLANES, V7X_SUBLANES, w)
    return pltpu.roll(x3, step, axis=1).reshape(rows, w)


def _lru_kernel(xa_ref, ga_ref, h0f_ref, h0b_ref, cw_ref, cb_ref, wa_ref, ba_ref, wx_ref, bx_ref,
                lam_ref, y_ref, lf_ref, lb_ref, a_scr, u_scr, hf_scr, hb_scr):
    seq, w = xa_ref.shape
    groups = seq // V7X_SUBLANES
    xa = xa_ref[...]
    row = lax.broadcasted_iota(jnp.int32, (seq, w), 0)
    sub = row & (V7X_SUBLANES - 1)

    def tap(offset):
        if offset == 0:
            return xa
        shifted = pltpu.roll(xa, (-offset) % seq, axis=0)
        valid = (row + offset >= 0) & (row + offset < seq)
        return jnp.where(valid, shifted, 0.0)

    xc = cb_ref[...]
    for j in range(CONV_W):
        xc = xc + tap(j - CONV_W // 2) * cw_ref[pl.ds(j, 1), :]
    xcb = xc.astype(BF16)

    for direction, (h0_ref, h_scr, last_ref) in enumerate(((h0f_ref, hf_scr, lf_ref),
                                                            (h0b_ref, hb_scr, lb_ref))):
        reverse = direction == 1
        r = jax.nn.sigmoid(jnp.dot(xcb, wa_ref[direction], preferred_element_type=F32) + ba_ref[direction])
        gi = jax.nn.sigmoid(jnp.dot(xcb, wx_ref[direction], preferred_element_type=F32) + bx_ref[direction])
        neg_lam = -lam_ref[direction]
        softplus = jnp.maximum(neg_lam, 0.0) + jnp.log1p(jnp.exp(-jnp.abs(neg_lam)))
        log_a = -LRU_C * r * softplus
        a = jnp.exp(log_a)
        m2 = -jnp.tanh(log_a) * (a * a + 1.0)
        u = jnp.where(m2 > 0.0, m2 * lax.rsqrt(m2), 0.0) * (gi * xc)
        for step in (1, 2, 4):
            if reverse:
                a_nb = _group_roll(a, V7X_SUBLANES - step)
                u_nb = _group_roll(u, V7X_SUBLANES - step)
                valid = sub < V7X_SUBLANES - step
            else:
                a_nb = _group_roll(a, step)
                u_nb = _group_roll(u, step)
                valid = sub >= step
            u = a * jnp.where(valid, u_nb, 0.0) + u
            a = a * jnp.where(valid, a_nb, 1.0)
        a_scr[...] = a
        u_scr[...] = u

        def carry_step(g, carry, reverse=reverse, h_scr=h_scr):
            gg = groups - 1 - g if reverse else g
            off = pl.multiple_of(gg * V7X_SUBLANES, V7X_SUBLANES)
            h = a_scr[pl.ds(off, V7X_SUBLANES), :] * carry + u_scr[pl.ds(off, V7X_SUBLANES), :]
            h_scr[pl.ds(off, V7X_SUBLANES), :] = h
            return h[0:1] if reverse else h[V7X_SUBLANES - 1:V7X_SUBLANES]

        last_ref[...] = lax.fori_loop(0, groups, carry_step, h0_ref[...], unroll=4)

    y_ref[...] = ((hf_scr[...] + hb_scr[...]) * jax.nn.gelu(ga_ref[...])).astype(BF16)


def _lru(p_ab, h0f, h0b, prm, seq_len, n_seq, row_start):
    lru_w = h0f.shape[-1]
    cw = LRU_COLS
    ncb = lru_w // cw
    sb = row_start // seq_len
    vec = lambda: pl.BlockSpec((None, 1, cw), lambda s, c: (s, 0, c))
    par2 = lambda: pl.BlockSpec((None, 2, 1, cw), lambda s, c: (prm["idx"], 0, 0, c))
    gate = lambda: pl.BlockSpec((None, 2, None, cw, cw), lambda s, c: (prm["idx"], 0, c, 0, 0))
    return pl.pallas_call(
        _lru_kernel,
        grid=(n_seq, ncb),
        in_specs=[
            pl.BlockSpec((seq_len, cw), lambda s, c: (s + sb, c)),
            pl.BlockSpec((seq_len, cw), lambda s, c: (s + sb, c + ncb)),
            vec(), vec(),
            pl.BlockSpec((None, CONV_W, cw), lambda s, c: (prm["idx"], 0, c)),
            pl.BlockSpec((None, 1, cw), lambda s, c: (prm["idx"], 0, c)),
            gate(), par2(), gate(), par2(), par2(),
        ],
        out_specs=[
            pl.BlockSpec((seq_len, cw), lambda s, c: (s, c)),
            vec(), vec(),
        ],
        out_shape=[
            jax.ShapeDtypeStruct((n_seq * seq_len, lru_w), BF16),
            jax.ShapeDtypeStruct((n_seq, 1, lru_w), F32),
            jax.ShapeDtypeStruct((n_seq, 1, lru_w), F32),
        ],
        scratch_shapes=[pltpu.VMEM((seq_len, cw), F32)] * 4,
        compiler_params=_params(("parallel", "parallel"), 40),
        name="lru",
    )(p_ab, p_ab, h0f.reshape(n_seq, 1, lru_w), h0b.reshape(n_seq, 1, lru_w), prm["conv_w"],
      prm["conv_b"], prm["wa"], prm["ba"], prm["wx"], prm["bx"], prm["lam"])


def _block_diag_tiles(w, tile):
    n, two, nb, bw, _ = w.shape
    per = tile // bw
    w = w.reshape(n, two, nb // per, per, bw, bw)
    eye = jnp.eye(per, dtype=w.dtype)
    out = jnp.einsum("ndgpij,pq->ndgpiqj", w, eye)
    return out.reshape(n, two, nb // per, tile, tile)


def _softmax_attend(q, ks, vs, biases, scale):
    scores = []
    for k, b in zip(ks, biases):
        s = lax.dot_general(q, k, (((1,), (1,)), ((), ())), preferred_element_type=F32)
        if scale is not None:
            s = s * scale
        scores.append(s if b is None else s + b)
    m = scores[0].max(axis=-1, keepdims=True)
    for s in scores[1:]:
        m = jnp.maximum(m, s.max(axis=-1, keepdims=True))
    denom = None
    out = None
    for s, v in zip(scores, vs):
        e = jnp.exp(s - m)
        part = e.sum(axis=-1, keepdims=True)
        denom = part if denom is None else denom + part
        o = jnp.dot(e.astype(BF16), v, preferred_element_type=F32)
        out = o if out is None else out + o
    return out / denom


def _upper_half(shape):
    lane = lax.broadcasted_iota(jnp.int32, shape, len(shape) - 1)
    return lane % V7X_LANES >= V7X_LANES // 2


def _split_pair(q):
    up = _upper_half(q.shape)
    zero = jnp.zeros_like(q)
    return jnp.concatenate([jnp.where(up, zero, q), jnp.where(up, q, zero)], axis=0)


def _merge_pair(o):
    m = o.shape[0] // 2
    return jnp.where(_upper_half((m, o.shape[1])), o[m:], o[:m])


def _dup_head(block, use_high):
    swapped = pltpu.roll(block, V7X_LANES // 2, axis=1)
    keep = _upper_half(block.shape) == use_high
    return jnp.where(keep, block, swapped)


def _fold_scale(q, scale):
    return (q * scale, None) if _is_pow2(scale) else (q, scale)


def _na_ctx_kernel(q_ref, k_ref, v_ref, o_ref, *, scale):
    for pair in range(q_ref.shape[1] // V7X_LANES):
        sl = pl.ds(pair * V7X_LANES, V7X_LANES)
        q, sc = _fold_scale(q_ref[:, sl], scale)
        o = _softmax_attend(_split_pair(q.astype(BF16)), [k_ref[:, sl].astype(BF16)],
                            [v_ref[:, sl].astype(BF16)], [None], sc)
        o_ref[:, sl] = _merge_pair(o).astype(o_ref.dtype)


def _na_ctx(p_ab, n_seq, seq_len, q_blk, na_w, scale):
    blk = lambda col: pl.BlockSpec((seq_len, na_w), lambda b: (b, col))
    return pl.pallas_call(
        functools.partial(_na_ctx_kernel, scale=scale),
        grid=(n_seq,),
        in_specs=[blk(q_blk), blk(q_blk + 1), blk(q_blk + 2)],
        out_specs=pl.BlockSpec((seq_len, na_w), lambda b: (b, 0)),
        out_shape=jax.ShapeDtypeStruct((n_seq * seq_len, na_w), BF16),
        compiler_params=_params(("parallel",), 40),
        name="na_ctx",
    )(p_ab, p_ab, p_ab)


def _na_tile_index(seq_len):
    rows_n = seq_len // GRID_W
    kr = min(NA_KR, rows_n)
    rows_per_chunk = NA_Q_CHUNK // GRID_W
    n_chunks = rows_n // rows_per_chunk
    win_rows = NA_WIN // GRID_W
    table = []
    for r in range(rows_n):
        r_start = min(max(r - kr // 2, 0), rows_n - kr)
        base = 0 if r // rows_per_chunk < n_chunks // 2 else rows_n - win_rows
        table.append([k - r + NA_KR - 1 if r_start <= k < r_start + kr else 2 * NA_KR - 1
                      for k in range(base, base + win_rows)])
    return table


def _na_lat_kernel(q_ref, k_ref, v_ref, kc_ref, vc_ref, tiles_ref, o_ref, bias_ref, *, scale):
    lq = q_ref.shape[0]
    n_chunks = lq // NA_Q_CHUNK
    rows_per_chunk = NA_Q_CHUNK // GRID_W

    @pl.when(pl.program_id(1) == 0)
    def _():
        low = lax.broadcasted_iota(jnp.int32, (GRID_W, V7X_LANES), 1) < GRID_W
        for half in (0, 1):
            for r, slots in enumerate(_na_tile_index(lq)):
                c, rr = divmod(r, rows_per_chunk)
                rows = pl.ds(half * NA_Q_CHUNK + rr * GRID_W, GRID_W)
                for s in range(0, len(slots), 2):
                    blk = jnp.where(low, tiles_ref[half, slots[s]], tiles_ref[half, slots[s + 1]])
                    bias_ref[c, rows, pl.ds(s * GRID_W, V7X_LANES)] = blk

    kcb = kc_ref[...].astype(BF16)
    vcb = vc_ref[...].astype(BF16)
    for c in range(n_chunks):
        win = 0 if c < n_chunks // 2 else lq - NA_WIN
        kb = k_ref[pl.ds(win, NA_WIN), :].astype(BF16)
        vb = v_ref[pl.ds(win, NA_WIN), :].astype(BF16)
        rows = pl.ds(c * NA_Q_CHUNK, NA_Q_CHUNK)
        q, sc = _fold_scale(q_ref[rows, :], scale)
        o = _softmax_attend(_split_pair(q.astype(BF16)), [kb, kcb], [vb, vcb], [bias_ref[c], None], sc)
        o_ref[rows, :] = _merge_pair(o).astype(o_ref.dtype)


def _na_lat(p_ab, cache_k, cache_v, tiles, idx, n_seq, seq_len, row_start, q_col, n_pairs, scale):
    sb = row_start // seq_len
    past = cache_k.shape[2]
    blk = lambda col: pl.BlockSpec((seq_len, V7X_LANES), lambda h, b: (b + sb, col + h))
    cblk = lambda: pl.BlockSpec((None, None, past, V7X_LANES), lambda h, b: (b, idx, 0, h))
    n_chunks = seq_len // NA_Q_CHUNK
    return pl.pallas_call(
        functools.partial(_na_lat_kernel, scale=scale),
        grid=(n_pairs, n_seq),
        in_specs=[blk(q_col), blk(q_col + n_pairs), blk(q_col + 2 * n_pairs), cblk(), cblk(),
                  pl.BlockSpec((None, 2, 2 * NA_KR, GRID_W, V7X_LANES), lambda h, b: (idx, h, 0, 0, 0))],
        out_specs=pl.BlockSpec((seq_len, V7X_LANES), lambda h, b: (b, h)),
        out_shape=jax.ShapeDtypeStruct((n_seq * seq_len, n_pairs * V7X_LANES), BF16),
        scratch_shapes=[pltpu.VMEM((n_chunks, 2 * NA_Q_CHUNK, NA_WIN), F32)],
        compiler_params=_params(("parallel", "arbitrary"), 56),
        name="na_lat",
    )(p_ab, p_ab, p_ab, cache_k, cache_v, tiles)


def _na_bias_tiles(tab):
    lead = tab.shape[:-1]
    edge = GRID_W - NA_KC
    vec = jnp.pad(tab, [(0, 0)] * len(lead) + [(edge, edge + 1)], mode="edge")
    skew = jnp.broadcast_to(vec[..., None, :], lead + (GRID_W, 2 * GRID_W))
    skew = skew.reshape(lead + (2 * GRID_W * GRID_W,))[..., :GRID_W * (2 * GRID_W - 1)]
    toep = skew.reshape(lead + (GRID_W, 2 * GRID_W - 1))[..., GRID_W - 1:]
    cols = np.arange(GRID_W)
    c_start = np.clip(cols - NA_KC // 2, 0, GRID_W - NA_KC)
    col_ok = (cols[None, :] >= c_start[:, None]) & (cols[None, :] < c_start[:, None] + NA_KC)
    tiles = jnp.where(col_ok, toep, NEG_BIAS)
    masked = jnp.full(lead[:-1] + (1, GRID_W, GRID_W), NEG_BIAS, F32)
    tiles = jnp.concatenate([tiles, masked], axis=-3)
    return jnp.concatenate([tiles, tiles], axis=-1)


def _split_dot(v, m):
    hi = v.astype(BF16)
    lo = (v - hi.astype(F32)).astype(BF16)
    return jnp.dot(hi, m, preferred_element_type=F32) + jnp.dot(lo, m, preferred_element_type=F32)


def _head_rms(x, gather, spread, gain, head_dim):
    inv = lax.rsqrt(_split_dot(x * x, gather) * (1.0 / head_dim) + EPS)
    return x * _split_dot(inv, spread) * gain


def _rope(x, cos, sin_lo, sin_hi):
    w = x.shape[1]
    reps = w // V7X_LANES
    tile = lambda t: t if reps == 1 else jnp.concatenate([t] * reps, axis=1)
    quarter = 16
    return (x * tile(cos) + pltpu.roll(x, w - quarter, axis=1) * tile(sin_lo)
            + pltpu.roll(x, quarter, axis=1) * tile(sin_hi))


def _cd_post_kernel(p_ref, gat_ref, spr_ref, qg_ref, kg_ref, mqg_ref, mkg_ref, wuq_ref, wukv_ref, *rest,
                    rope, head_dim, q_w, kv_w, rank, q_scale):
    if rope:
        cos_ref, slo_ref, shi_ref = rest[:3]
        rest = rest[3:]
        tabs = (cos_ref[...], slo_ref[...], shi_ref[...])
    q_ref, k_ref, qn_ref, qr_ref, ckv_ref, kn_ref, vm_ref, kr_ref = rest
    c0 = 0
    qc = p_ref[:, pl.ds(c0, q_w)]
    c0 += q_w
    kc = p_ref[:, pl.ds(c0, kv_w)]
    c0 += 2 * kv_w
    qa = p_ref[:, pl.ds(c0, rank)]
    c0 += rank
    ckv = p_ref[:, pl.ds(c0, rank)]
    c0 += rank
    kr = p_ref[:, pl.ds(c0, V7X_LANES)]

    qc = _head_rms(qc, gat_ref[...], spr_ref[...], qg_ref[...], head_dim)
    kc = _head_rms(kc, gat_ref[pl.ds(0, kv_w), :], spr_ref[:, pl.ds(0, kv_w)], kg_ref[...], head_dim)
    qd = jnp.dot(_rms(qa, mqg_ref[...]).astype(BF16), wuq_ref[...], preferred_element_type=F32)
    n_nope = qn_ref.shape[1]
    qn = qd[:, :n_nope]
    qr = qd[:, n_nope:]
    ckv = _rms(ckv, mkg_ref[...])
    if rope:
        qc = _rope(qc, *tabs)
        kc = _rope(kc, *tabs)
        qr = _rope(qr, *tabs)
        kr = _rope(kr, *tabs)
    q_ref[...] = (qc * q_scale).astype(BF16)
    k_ref[...] = kc
    qn_ref[...] = qn.astype(BF16)
    qr_ref[...] = qr.astype(BF16)
    ckv_ref[...] = ckv
    kv = jnp.dot(ckv.astype(BF16), wukv_ref[...], preferred_element_type=F32)
    kn_ref[...] = kv[:, :n_nope].astype(BF16)
    vm_ref[...] = kv[:, n_nope:].astype(BF16)
    kr_ref[...] = kr


def _cd_post(p_cd, prm, row_start, rows, rope_tabs, dims, q_scale):
    q_w, kv_w, rank, head_dim, n_nope, n_rope, n_v = dims
    tm = 256
    off = row_start // tm
    n_in = p_cd.shape[1]
    idx = prm["idx"]
    rope = rope_tabs is not None
    const = lambda shape: pl.BlockSpec(shape, lambda i: (0,) * len(shape))
    layer = lambda shape: pl.BlockSpec((None,) + shape, lambda i: (idx,) + (0,) * len(shape))
    in_specs = [
        pl.BlockSpec((tm, n_in), lambda i: (i + off, 0)),
        const((q_w, V7X_LANES)), const((V7X_LANES, q_w)),
        layer((1, q_w)), layer((1, kv_w)), layer((1, rank)), layer((1, rank)),
        layer((rank, n_nope + n_rope)), layer((rank, n_nope + n_v)),
    ]
    args = [p_cd, prm["gather"], prm["spread"], prm["q_gain"], prm["k_gain"], prm["mla_q_gain"],
            prm["mla_kv_gain"], prm["w_uq"], prm["w_ukv"]]
    if rope:
        seq_tiles = rope_tabs[0].shape[0] // tm
        in_specs += [pl.BlockSpec((tm, V7X_LANES), lambda i: (i % seq_tiles, 0))] * 3
        args += list(rope_tabs)
    widths = [(q_w, BF16), (kv_w, F32), (n_nope, BF16), (n_rope, BF16), (rank, F32), (n_nope, BF16),
              (n_v, BF16), (V7X_LANES, F32)]
    kern = functools.partial(_cd_post_kernel, rope=rope, head_dim=head_dim, q_w=q_w, kv_w=kv_w, rank=rank,
                             q_scale=q_scale)
    return pl.pallas_call(
        kern,
        grid=(rows // tm,),
        in_specs=in_specs,
        out_specs=[pl.BlockSpec((tm, w), lambda i: (i, 0)) for w, _ in widths],
        out_shape=[jax.ShapeDtypeStruct((rows, w), dt) for w, dt in widths],
        compiler_params=_params(("parallel",), 48),
        name="cd_post",
    )(*args)


def _rope_tables(seq_len, head_dim):
    half = head_dim // 2
    nf = half // 2
    t = np.arange(seq_len)
    inv_freq = (1.0 / (ROPE_THETA ** (np.arange(nf, dtype=np.float32) / nf))).astype(np.float32)
    zeros = np.zeros((seq_len, nf), np.float32)
    cos, slo, shi = [], [], []
    for pos in (t // GRID_W, t % GRID_W):
        ang = pos.astype(np.float32)[:, None] * inv_freq[None, :]
        c, s = np.cos(ang).astype(np.float32), np.sin(ang).astype(np.float32)
        cos += [c, c]
        slo += [-s, zeros]
        shi += [zeros, s]
    reps = V7X_LANES // head_dim
    return tuple(jnp.asarray(np.tile(np.concatenate(x, axis=1), (1, reps))) for x in (cos, slo, shi))


def _ckv_up_kernel(c_ref, w_ref, kn_ref, vm_ref):
    kv = jnp.dot(c_ref[...].astype(BF16), w_ref[...], preferred_element_type=F32)
    n = kn_ref.shape[1]
    kn_ref[...] = kv[:, :n].astype(BF16)
    vm_ref[...] = kv[:, n:].astype(BF16)


def _ckv_up(ckv, w_ukv, idx, n_nope):
    n_seq, _, tm, rank = ckv.shape
    rows = n_seq * tm
    n = w_ukv.shape[2]
    return pl.pallas_call(
        _ckv_up_kernel,
        grid=(rows // tm,),
        in_specs=[pl.BlockSpec((None, None, tm, rank), lambda i: (i, idx, 0, 0)),
                  pl.BlockSpec((None, rank, n), lambda i: (idx, 0, 0))],
        out_specs=[pl.BlockSpec((tm, n_nope), lambda i: (i, 0)), pl.BlockSpec((tm, n - n_nope), lambda i: (i, 0))],
        out_shape=[jax.ShapeDtypeStruct((rows, n_nope), BF16), jax.ShapeDtypeStruct((rows, n - n_nope), BF16)],
        compiler_params=_params(("parallel",), 32),
        name="ckv_up",
    )(ckv, w_ukv)


def _gqa_kernel(q_ref, k_ref, v_ref, *rest, scale, has_cache, group_w):
    if has_cache:
        kc_ref, vc_ref, o_ref = rest
    else:
        (o_ref,) = rest
    lq = q_ref.shape[0]
    n_local = q_ref.shape[1] // group_w
    n_pairs = group_w // V7X_LANES
    for hl in range(n_local):
        if n_local == 1:
            high = (pl.program_id(1) % 2) == 1
            pick = lambda ref: ref[...]
        else:
            high = hl % 2 == 1
            pick = lambda ref, hl=hl: ref[:, pl.ds((hl // 2) * V7X_LANES, V7X_LANES)]
        ks = [_dup_head(pick(k_ref), high).astype(BF16)]
        vs = [_dup_head(pick(v_ref), high).astype(BF16)]
        if has_cache:
            ks.append(_dup_head(pick(kc_ref), high).astype(BF16))
            vs.append(_dup_head(pick(vc_ref), high).astype(BF16))
        for c in range(lq // Q_CHUNK):
            rows = pl.ds(c * Q_CHUNK, Q_CHUNK)
            cols = [pl.ds(hl * group_w + p * V7X_LANES, V7X_LANES) for p in range(n_pairs)]
            q = jnp.concatenate([_split_pair(q_ref[rows, cl]) for cl in cols], axis=0)
            o = _softmax_attend(q, ks, vs, [None] * len(ks), scale)
            for p, cl in enumerate(cols):
                o_ref[rows, cl] = _merge_pair(o[2 * p * Q_CHUNK:2 * (p + 1) * Q_CHUNK]).astype(o_ref.dtype)


def _gqa(q, k, p_cd, v_col, cache_k, cache_v, idx, n_seq, seq_len, row_start, n_kv, heads_per_step, scale):
    sb = row_start // seq_len
    group_w = q.shape[1] // n_kv
    has_cache = cache_k is not None
    if heads_per_step == 1:
        kv_w, kv_blk, v_blk = V7X_LANES, (lambda h: h // 2), (lambda h: v_col + h // 2)
    else:
        kv_w = k.shape[1]
        kv_blk, v_blk = (lambda h: 0), (lambda h: v_col * V7X_LANES // kv_w)
    in_specs = [
        pl.BlockSpec((seq_len, group_w * heads_per_step), lambda b, h: (b, h)),
        pl.BlockSpec((seq_len, kv_w), lambda b, h: (b, kv_blk(h))),
        pl.BlockSpec((seq_len, kv_w), lambda b, h: (b + sb, v_blk(h))),
    ]
    args = [q, k, p_cd]
    if has_cache:
        past = cache_k.shape[2]
        in_specs += [pl.BlockSpec((None, None, past, kv_w), lambda b, h: (b, idx, 0, kv_blk(h)))] * 2
        args += [cache_k, cache_v]
    return pl.pallas_call(
        functools.partial(_gqa_kernel, scale=scale, has_cache=has_cache, group_w=group_w),
        grid=(n_seq, n_kv // heads_per_step),
        in_specs=in_specs,
        out_specs=pl.BlockSpec((seq_len, group_w * heads_per_step), lambda b, h: (b, h)),
        out_shape=jax.ShapeDtypeStruct(q.shape, BF16),
        compiler_params=_params(("parallel", "parallel"), 56),
        name="gqa",
    )(*args)


def _mla_kernel(qn_ref, qr_ref, kn_ref, kr_ref, v_ref, *rest, scale, has_cache):
    if has_cache:
        knc_ref, krc_ref, vc_ref, o_ref = rest
        krc = krc_ref[...].astype(BF16)
    else:
        (o_ref,) = rest
    lq = qn_ref.shape[0]
    qc = min(2 * Q_CHUNK, lq)
    n_local = qn_ref.shape[1] // V7X_LANES
    kr = kr_ref[...].astype(BF16)
    for hl in range(n_local):
        sl = pl.ds(hl * V7X_LANES, V7X_LANES)
        if n_local == 1:
            high = (pl.program_id(1) % 2) == 1
            pair = pl.ds(0, V7X_LANES)
        else:
            high = hl % 2 == 1
            pair = pl.ds((hl // 2) * V7X_LANES, V7X_LANES)
        ks = [jnp.concatenate([kn_ref[:, sl], kr], axis=1)]
        vs = [v_ref[:, sl]]
        if has_cache:
            ks.append(jnp.concatenate([knc_ref[:, sl], krc], axis=1))
            vs.append(vc_ref[:, sl])
        for c in range(lq // qc):
            rows = pl.ds(c * qc, qc)
            qr = qr_ref[rows, pair]
            qr = jnp.where(_upper_half(qr.shape) == high, qr, jnp.zeros_like(qr))
            q = jnp.concatenate([qn_ref[rows, sl], qr], axis=1)
            o_ref[rows, sl] = _softmax_attend(q, ks, vs, [None] * len(ks), scale).astype(o_ref.dtype)


def _mla(qn, qr, kn, kr2, vm, cache, n_seq, seq_len, heads_per_step, scale):
    has_cache = cache is not None
    n_heads = vm.shape[1] // V7X_LANES
    hw = heads_per_step * V7X_LANES
    head = lambda rows: pl.BlockSpec((rows, hw), lambda b, h: (b, h))
    if heads_per_step == 1:
        rope_q = pl.BlockSpec((seq_len, V7X_LANES), lambda b, h: (b, h // 2))
    else:
        rope_q = pl.BlockSpec((seq_len, qr.shape[1]), lambda b, h: (b, 0))
    in_specs = [head(seq_len), rope_q, head(seq_len),
                pl.BlockSpec((seq_len, V7X_LANES), lambda b, h: (b, 0)),
                head(seq_len)]
    args = [qn, qr, kn, kr2, vm]
    if has_cache:
        knc, krc, vmc, idx = cache
        past = krc.shape[2]
        in_specs += [head(past),
                     pl.BlockSpec((None, None, past, V7X_LANES), lambda b, h: (b, idx, 0, 0)),
                     head(past)]
        args += [knc, krc, vmc]
    return pl.pallas_call(
        functools.partial(_mla_kernel, scale=scale, has_cache=has_cache),
        grid=(n_seq, n_heads // heads_per_step),
        in_specs=in_specs,
        out_specs=head(seq_len),
        out_shape=jax.ShapeDtypeStruct(vm.shape, BF16),
        compiler_params=_params(("parallel", "parallel"), 48),
        name="mla",
    )(*args)


def kernel(x_prompt, x_sample, state_lru_fwd, state_lru_bwd, cache_na_k, cache_na_v, cache_gqa_k, cache_gqa_v, cache_mla_ckv, cache_mla_krope, c, c_ctx, w_mod, b_mod, norm_gain, w_ffn_in, w_ffn_out, w_in_ab, conv_w, conv_b, lru_wa, lru_ba, lru_wx, lru_bx, lru_lambda, na_bias, w_out_ab, w_in_cd, gqa_q_gain, gqa_k_gain, mla_q_gain, mla_kv_gain, mla_w_uq, mla_w_uk, mla_w_uv, w_out_cd, final_gain):
    batch, seq, d = x_prompt.shape
    dec_batch, dec_seq, _ = x_sample.shape
    depth = w_mod.shape[0]
    n_even, n_odd = w_in_ab.shape[0], w_in_cd.shape[0]
    n_p = batch * seq
    n_s = dec_batch * dec_seq
    past = cache_na_k.shape[2]
    lru_w = state_lru_fwd.shape[-1]
    na_heads, na_dh = cache_na_k.shape[3], cache_na_k.shape[4]
    na_w = na_heads * na_dh
    gqa_kv, gqa_dh = cache_gqa_k.shape[3], cache_gqa_k.shape[4]
    kv_w = gqa_kv * gqa_dh
    rank = cache_mla_ckv.shape[-1]
    rope_w = cache_mla_krope.shape[-1]
    q_w = w_in_cd.shape[2] - 2 * kv_w - 2 * rank - rope_w
    mla_heads = mla_w_uk.shape[2] // MLA_NOPE
    n_nope = mla_heads * MLA_NOPE
    n_rope = mla_heads * rope_w
    n_v = mla_heads * MLA_V
    mla_qk = MLA_NOPE + rope_w
    assert mla_q_gain.shape[-1] == rank and 2 * rope_w == V7X_LANES
    assert n_p % dec_seq == 0 and dec_seq % PROJ_TILE == 0 and seq % Q_CHUNK == 0
    assert dec_seq // GRID_W == 2 * NA_KR and na_dh == GRID_W and gqa_dh == GRID_W
    assert w_in_ab.shape[2] == 2 * lru_w + 3 * na_w and lru_w == na_w
    assert (q_w // gqa_kv) % V7X_LANES == 0 and q_w % kv_w == 0

    x = (x_prompt.reshape(n_p, d), x_sample.reshape(n_s, d))
    cond = jnp.concatenate([c_ctx[None, :], c, jnp.zeros((MOD_ROWS - 1 - dec_batch, d), F32)], axis=0)
    mods = _modulation(cond, w_mod, b_mod)
    gains = norm_gain.reshape(depth * 3, 1, d)

    ffn_w = (w_ffn_in[0, 0].astype(BF16), w_ffn_out[0, 0].astype(BF16))

    def ffn_step(x, ffn_w, layer, which):
        last = layer == depth - 1 and which == 1
        nxt = None if last else (w_ffn_in, w_ffn_out, layer + which, 1 - which)
        out = _ffn(x, mods, gains, ffn_w[0], ffn_w[1], layer, which, nxt, n_p, dec_seq)
        return out[0], tuple(out[1:])
    w_in_ab_bf = w_in_ab.astype(BF16)
    w_out_ab_bf = w_out_ab.astype(BF16)
    w_in_cd_bf = jnp.concatenate([w_in_cd, w_in_cd[:, :, -rope_w:]], axis=2).astype(BF16)
    w_out_cd_bf = w_out_cd.astype(BF16)

    lru_prm = {
        "conv_w": conv_w, "conv_b": conv_b.reshape(n_even, 1, lru_w),
        "wa": _block_diag_tiles(lru_wa, LRU_COLS).astype(BF16),
        "wx": _block_diag_tiles(lru_wx, LRU_COLS).astype(BF16),
        "ba": lru_ba.reshape(n_even, 2, 1, lru_w), "bx": lru_bx.reshape(n_even, 2, 1, lru_w),
        "lam": lru_lambda.reshape(n_even, 2, 1, lru_w),
    }
    zeros_state = jnp.zeros((batch, lru_w), F32)
    na_k_ctx = cache_na_k.reshape(dec_batch, n_even, past, na_w)
    na_v_ctx = cache_na_v.reshape(dec_batch, n_even, past, na_w)
    na_tiles = _na_bias_tiles(na_bias)

    w_uq = mla_w_uq.reshape(n_odd, rank, mla_heads, mla_qk)
    w_uq = jnp.concatenate([w_uq[..., :MLA_NOPE].reshape(n_odd, rank, n_nope),
                            w_uq[..., MLA_NOPE:].reshape(n_odd, rank, n_rope)], axis=2).astype(BF16)
    gqa_scale = gqa_dh ** -0.5
    q_scale = gqa_scale if _is_pow2(gqa_scale) else 1.0
    head_of_lane = np.arange(q_w) // gqa_dh
    gather = (head_of_lane[:, None] == np.arange(V7X_LANES)[None, :]).astype(np.float32)
    assert q_w // gqa_dh <= V7X_LANES
    cd_prm = {
        "gather": jnp.asarray(gather, BF16), "spread": jnp.asarray(gather.T, BF16),
        "q_gain": jnp.tile(gqa_q_gain, (1, q_w // gqa_dh)).reshape(n_odd, 1, q_w),
        "k_gain": jnp.tile(gqa_k_gain, (1, gqa_kv)).reshape(n_odd, 1, kv_w),
        "mla_q_gain": mla_q_gain.reshape(n_odd, 1, rank),
        "mla_kv_gain": mla_kv_gain.reshape(n_odd, 1, rank),
        "w_uq": w_uq,
        "w_ukv": jnp.concatenate([mla_w_uk, mla_w_uv], axis=2).astype(BF16),
    }
    rope_tabs = _rope_tables(dec_seq, gqa_dh)
    gqa_k_ctx = cache_gqa_k.reshape(dec_batch, n_odd, past, kv_w)
    gqa_v_ctx = cache_gqa_v.reshape(dec_batch, n_odd, past, kv_w)
    krope_ctx = jnp.concatenate([cache_mla_krope, cache_mla_krope], axis=-1)

    st_f, st_b, na_k, na_v, gq_k, gq_v, ml_c, ml_r = [], [], [], [], [], [], [], []
    for layer in range(depth):
        jdx = layer // 2
        x, ffn_w = ffn_step(x, ffn_w, layer, 0)
        if layer % 2 == 0:
            p_ab = _inproj(x, mods, gains, w_in_ab_bf, layer, jdx, PROJ_TILE, 1024, n_p, dec_seq)
            prm = dict(lru_prm, idx=jdx)
            ya_p, lf, lb = _lru(p_ab, zeros_state, zeros_state, prm, seq, batch, 0)
            ya_s, _, _ = _lru(p_ab, state_lru_fwd[:, jdx], state_lru_bwd[:, jdx], prm, dec_seq, dec_batch, n_p)
            scale = na_dh ** -0.5
            yb_p = _na_ctx(p_ab, batch, seq, 2 * lru_w // na_w, na_w, scale)
            yb_s = _na_lat(p_ab, na_k_ctx, na_v_ctx, na_tiles, jdx, dec_batch, dec_seq, n_p,
                           2 * lru_w // V7X_LANES, na_w // V7X_LANES, scale)
            x = _outproj(x, mods, layer, ya_p, ya_s, yb_p, yb_s, w_out_ab_bf, jdx, n_p, dec_seq)
            st_f.append(lf.reshape(batch, lru_w))
            st_b.append(lb.reshape(batch, lru_w))
            na_k.append(p_ab[:n_p, 2 * lru_w + na_w:2 * lru_w + 2 * na_w].reshape(batch, seq, na_heads, na_dh))
            na_v.append(p_ab[:n_p, 2 * lru_w + 2 * na_w:].reshape(batch, seq, na_heads, na_dh))
        else:
            p_cd = _inproj(x, mods, gains, w_in_cd_bf, layer, jdx, TOKEN_TILE, w_in_cd_bf.shape[2], n_p, dec_seq)
            prm = dict(cd_prm, idx=jdx)
            dims = (q_w, kv_w, rank, gqa_dh, n_nope, n_rope, n_v)
            qp, kp, qnp_, qrp, ckvp, knp_, vmp, krp = _cd_post(p_cd, prm, 0, n_p, None, dims, q_scale)
            qs, ks_, qns, qrs, _, kns, vms, krs = _cd_post(p_cd, prm, n_p, n_s, rope_tabs, dims, q_scale)
            v_col = (q_w + kv_w) // V7X_LANES
            att_scale = None if q_scale != 1.0 else gqa_scale
            yc_p = _gqa(qp, kp, p_cd, v_col, None, None, jdx, batch, seq, 0, gqa_kv, gqa_kv, att_scale)
            yc_s = _gqa(qs, ks_, p_cd, v_col, gqa_k_ctx, gqa_v_ctx, jdx, dec_batch, dec_seq, n_p,
                        gqa_kv, 1, att_scale)
            knc, vmc = _ckv_up(cache_mla_ckv, cd_prm["w_ukv"], jdx, n_nope)
            yd_p = _mla(qnp_, qrp, knp_, krp, vmp, None, batch, seq, mla_heads, mla_qk ** -0.5)
            yd_s = _mla(qns, qrs, kns, krs, vms, (knc, krope_ctx, vmc, jdx), dec_batch, dec_seq, 1,
                        mla_qk ** -0.5)
            x = _outproj(x, mods, layer, yc_p, yc_s, yd_p, yd_s, w_out_cd_bf, jdx, n_p, dec_seq)
            gq_k.append(kp.reshape(batch, seq, gqa_kv, gqa_dh))
            gq_v.append(p_cd[:n_p, q_w + kv_w:q_w + 2 * kv_w].reshape(batch, seq, gqa_kv, gqa_dh))
            ml_c.append(ckvp.reshape(batch, seq, rank))
            ml_r.append(krp[:, :rope_w].reshape(batch, seq, rope_w))
        x, ffn_w = ffn_step(x, ffn_w, layer, 1)

    y_prompt = _final_norm(x, final_gain, 0, n_p).reshape(batch, seq, d)
    y_sample = _final_norm(x, final_gain, n_p, n_s).reshape(dec_batch, dec_seq, d)
    stack = lambda xs: jnp.stack(xs, axis=1)
    return (y_prompt, y_sample, stack(st_f), stack(st_b), stack(na_k), stack(na_v), stack(gq_k), stack(gq_v),
            stack(ml_c), stack(ml_r))
```

```python
import functools
import math

import numpy as np

import jax
import jax.numpy as jnp
from jax import lax
from jax.experimental import pallas as pl
from jax.experimental.pallas import tpu as pltpu

F32 = jnp.float32
BF16 = jnp.bfloat16

EPS = 1e-6
N_MOD = 9
GRID_W = 64
NA_KR = 8
NA_KC = 16
ROPE_THETA = 10000.0
LRU_C = 8.0
CONV_W = 4
MLA_NOPE = 128
MLA_V = 128
NEG_BIAS = -1e30

V7X_LANES = 128
V7X_SUBLANES = 8
MOD_ROWS = 16
TOKEN_TILE = 512
FF_TILE = 512
Q_CHUNK = 128
NA_Q_CHUNK = 256
NA_WIN = 768
LRU_COLS = 256


def _params(sem, vmem_mib):
    return pltpu.CompilerParams(dimension_semantics=sem, vmem_limit_bytes=vmem_mib * 1024 * 1024)


def _silu(x):
    return x * jax.nn.sigmoid(x)


def _rms(x, gain):
    ms = jnp.mean(x * x, axis=-1, keepdims=True)
    return x * lax.rsqrt(ms + EPS) * gain


def _mod_row(i, n_prompt_tiles, tiles_per_seq):
    return jnp.where(i < n_prompt_tiles, 0, 1 + (i - n_prompt_tiles) // tiles_per_seq)


def _adaln(x, mod_ref, gain_ref, row, mod_base):
    d = x.shape[1]
    shift = mod_ref[pl.ds(row, 1), pl.ds(mod_base * d, d)]
    scale = mod_ref[pl.ds(row, 1), pl.ds((mod_base + 1) * d, d)]
    return (_rms(x, gain_ref[...]) * (1 + scale) + shift).astype(BF16)


def _is_pow2(v):
    return math.frexp(v)[0] == 0.5


def _mod_kernel(cond_ref, w_ref, b_ref, o_ref):
    s = _silu(cond_ref[...]).astype(BF16)
    o_ref[...] = jnp.dot(s, w_ref[...].astype(BF16), preferred_element_type=F32) + b_ref[...]


def _modulation(cond, w_mod, b_mod):
    depth, d, n = w_mod.shape
    tn = 1024
    return pl.pallas_call(
        _mod_kernel,
        grid=(depth, n // tn),
        in_specs=[
            pl.BlockSpec((MOD_ROWS, d), lambda l, j: (0, 0)),
            pl.BlockSpec((None, d, tn), lambda l, j: (l, 0, j)),
            pl.BlockSpec((None, 1, tn), lambda l, j: (l, 0, j)),
        ],
        out_specs=pl.BlockSpec((None, MOD_ROWS, tn), lambda l, j: (l, 0, j)),
        out_shape=jax.ShapeDtypeStruct((depth, MOD_ROWS, n), F32),
        compiler_params=_params(("parallel", "parallel"), 40),
        name="modulation",
    )(cond, w_mod, b_mod.reshape(depth, 1, n))


def _ffn_kernel(*refs, mod_base, n_prompt_tiles, tiles_per_seq, cast_next, split_x):
    i = pl.program_id(0)
    j = pl.program_id(1)
    if split_x:
        xp_ref, xs_ref, *refs = refs
        read_x = lambda: jnp.where(i < n_prompt_tiles, xp_ref[...], xs_ref[...])
    else:
        x_ref, *refs = refs
        read_x = lambda: x_ref[...]
    mod_ref, gain_ref, wg_ref, wu_ref, wo_ref, *rest = refs
    if cast_next:
        nin_ref, nout_ref, o_ref, cin_ref, cout_ref, h_ref = rest
    else:
        o_ref, h_ref = rest
    d = o_ref.shape[1]
    row = _mod_row(i, n_prompt_tiles, tiles_per_seq)

    @pl.when(j == 0)
    def _():
        h_ref[...] = _adaln(read_x(), mod_ref, gain_ref, row, mod_base)
        o_ref[...] = jnp.zeros_like(o_ref)

    h = h_ref[...]
    g = jnp.dot(h, wg_ref[...], preferred_element_type=F32)
    u = jnp.dot(h, wu_ref[...], preferred_element_type=F32)
    if cast_next:
        cin_ref[...] = nin_ref[...].astype(BF16)
        cout_ref[...] = nout_ref[...].astype(BF16)
    act = (_silu(g) * u).astype(BF16)
    o_ref[...] += jnp.dot(act, wo_ref[...], preferred_element_type=F32)

    @pl.when(j == pl.num_programs(1) - 1)
    def _():
        gate = mod_ref[pl.ds(row, 1), pl.ds((mod_base + 2) * d, d)]
        o_ref[...] = read_x() + (0.5 * gate) * o_ref[...]


def _ffn(x, mods, gains, w_in, w_out, layer, which, nxt, n_prompt_rows, seq_rows):
    split_x = isinstance(x, tuple)
    npt = n_prompt_rows // TOKEN_TILE
    if split_x:
        t, d = x[0].shape[0] + x[1].shape[0], x[0].shape[1]
        x_specs = [pl.BlockSpec((TOKEN_TILE, d), lambda i, j: (jnp.minimum(i, npt - 1), 0)),
                   pl.BlockSpec((TOKEN_TILE, d), lambda i, j: (jnp.maximum(i - npt, 0), 0))]
        x_args = list(x)
    else:
        t, d = x.shape
        x_specs = [pl.BlockSpec((TOKEN_TILE, d), lambda i, j: (i, 0))]
        x_args = [x]
    dff = w_out.shape[0]
    nj = dff // FF_TILE
    ni = t // TOKEN_TILE
    mod_base = 6 * which
    kern = functools.partial(_ffn_kernel, mod_base=mod_base, n_prompt_tiles=npt,
                             tiles_per_seq=seq_rows // TOKEN_TILE, cast_next=nxt is not None, split_x=split_x)
    in_specs = x_specs + [
        pl.BlockSpec((None, MOD_ROWS, N_MOD * d), lambda i, j: (layer, 0, 0)),
        pl.BlockSpec((None, 1, d), lambda i, j: (3 * layer + 2 * which, 0, 0)),
        pl.BlockSpec((d, FF_TILE), lambda i, j: (0, j)),
        pl.BlockSpec((d, FF_TILE), lambda i, j: (0, j + nj)),
        pl.BlockSpec((FF_TILE, d), lambda i, j: (j, 0)),
    ]
    args = x_args + [mods, gains, w_in, w_in, w_out]
    out_specs = [pl.BlockSpec((TOKEN_TILE, d), lambda i, j: (i, 0))]
    out_shape = [jax.ShapeDtypeStruct((t, d), F32)]
    if nxt is not None:
        f_in, f_out, nl, nw = nxt
        cast_rows = min(r for r in (16, 32, 64, 128, 256, 512)
                        if d % r == 0 and dff % r == 0 and d // r + dff // r <= ni * nj)
        n_in, n_out = d // cast_rows, dff // cast_rows
        in_blk = lambda i, j: jnp.minimum(i * nj + j, n_in - 1)
        out_blk = lambda i, j: jnp.clip(i * nj + j - n_in, 0, n_out - 1)
        in_specs += [pl.BlockSpec((None, None, cast_rows, 2 * dff), lambda i, j: (nl, nw, in_blk(i, j), 0)),
                     pl.BlockSpec((None, None, cast_rows, d), lambda i, j: (nl, nw, out_blk(i, j), 0))]
        args += [f_in, f_out]
        out_specs += [pl.BlockSpec((cast_rows, 2 * dff), lambda i, j: (in_blk(i, j), 0)),
                      pl.BlockSpec((cast_rows, d), lambda i, j: (out_blk(i, j), 0))]
        out_shape += [jax.ShapeDtypeStruct((d, 2 * dff), BF16), jax.ShapeDtypeStruct((dff, d), BF16)]
    return pl.pallas_call(
        kern,
        grid=(ni, nj),
        in_specs=in_specs,
        out_specs=out_specs,
        out_shape=out_shape,
        scratch_shapes=[pltpu.VMEM((TOKEN_TILE, d), BF16)],
        compiler_params=_params(("arbitrary", "arbitrary"), 56),
        name="ffn",
    )(*args)


def _inproj_kernel(x_ref, mod_ref, gain_ref, w_ref, o_ref, *, n_prompt_tiles, tiles_per_seq):
    row = _mod_row(pl.program_id(0), n_prompt_tiles, tiles_per_seq)
    h = _adaln(x_ref[...], mod_ref, gain_ref, row, 3)
    o_ref[...] = jnp.dot(h, w_ref[...], preferred_element_type=F32)


def _inproj(x, mods, gains, w, layer, w_idx, tn, n_prompt_rows, seq_rows):
    t, d = x.shape
    n = w.shape[2]
    tm = TOKEN_TILE
    kern = functools.partial(_inproj_kernel, n_prompt_tiles=n_prompt_rows // tm, tiles_per_seq=seq_rows // tm)
    return pl.pallas_call(
        kern,
        grid=(t // tm, n // tn),
        in_specs=[
            pl.BlockSpec((tm, d), lambda i, j: (i, 0)),
            pl.BlockSpec((None, MOD_ROWS, N_MOD * d), lambda i, j: (layer, 0, 0)),
            pl.BlockSpec((None, 1, d), lambda i, j: (3 * layer + 1, 0, 0)),
            pl.BlockSpec((None, d, tn), lambda i, j: (w_idx, 0, j)),
        ],
        out_specs=pl.BlockSpec((tm, tn), lambda i, j: (i, j)),
        out_shape=jax.ShapeDtypeStruct((t, n), F32),
        compiler_params=_params(("parallel", "parallel"), 52),
        name="inproj",
    )(x, mods, gains, w)


def _outproj_kernel(x_ref, mod_ref, ap_ref, as_ref, bp_ref, bs_ref, w_ref, o_ref, *,
                    n_prompt_tiles, tiles_per_seq):
    i = pl.program_id(0)
    d = x_ref.shape[1]
    half = ap_ref.shape[1]
    is_prompt = i < n_prompt_tiles
    ya = jnp.where(is_prompt, ap_ref[...], as_ref[...])
    yb = jnp.where(is_prompt, bp_ref[...], bs_ref[...])
    y = jnp.dot(ya, w_ref[pl.ds(0, half), :], preferred_element_type=F32)
    y = y + jnp.dot(yb, w_ref[pl.ds(half, half), :], preferred_element_type=F32)
    row = _mod_row(i, n_prompt_tiles, tiles_per_seq)
    gate = mod_ref[pl.ds(row, 1), pl.ds(5 * d, d)]
    o_ref[...] = x_ref[...] + gate * y


def _outproj(x, mods, layer, ya_p, ya_s, yb_p, yb_s, w, w_idx, n_prompt_rows, seq_rows):
    t, d = x.shape
    half = ya_p.shape[1]
    npt = n_prompt_rows // TOKEN_TILE
    kern = functools.partial(_outproj_kernel, n_prompt_tiles=npt, tiles_per_seq=seq_rows // TOKEN_TILE)
    p_map = lambda i: (jnp.minimum(i, npt - 1), 0)
    s_map = lambda i: (jnp.maximum(i - npt, 0), 0)
    return pl.pallas_call(
        kern,
        grid=(t // TOKEN_TILE,),
        in_specs=[
            pl.BlockSpec((TOKEN_TILE, d), lambda i: (i, 0)),
            pl.BlockSpec((None, MOD_ROWS, N_MOD * d), lambda i: (layer, 0, 0)),
            pl.BlockSpec((TOKEN_TILE, half), p_map),
            pl.BlockSpec((TOKEN_TILE, half), s_map),
            pl.BlockSpec((TOKEN_TILE, half), p_map),
            pl.BlockSpec((TOKEN_TILE, half), s_map),
            pl.BlockSpec((None, 2 * half, d), lambda i: (w_idx, 0, 0)),
        ],
        out_specs=pl.BlockSpec((TOKEN_TILE, d), lambda i: (i, 0)),
        out_shape=jax.ShapeDtypeStruct((t, d), F32),
        compiler_params=_params(("parallel",), 48),
        name="outproj",
    )(x, mods, ya_p, ya_s, yb_p, yb_s, w)


def _final_norm_kernel(x_ref, g_ref, o_ref):
    o_ref[...] = _rms(x_ref[...], g_ref[...])


def _final_norm(x, gain, row_start, rows):
    d = x.shape[1]
    off = row_start // TOKEN_TILE
    return pl.pallas_call(
        _final_norm_kernel,
        grid=(rows // TOKEN_TILE,),
        in_specs=[pl.BlockSpec((TOKEN_TILE, d), lambda i: (i + off, 0)),
                  pl.BlockSpec((1, d), lambda i: (0, 0))],
        out_specs=pl.BlockSpec((TOKEN_TILE, d), lambda i: (i, 0)),
        out_shape=jax.ShapeDtypeStruct((rows, d), F32),
        compiler_params=_params(("parallel",), 32),
        name="final_norm",
    )(x, gain.reshape(1, d))


def _group_roll(x, step):
    rows, w = x.shape
    x3 = x.reshape(rows // V7X_SUBLANES, V7X_SUBLANES, w)
    return pltpu.roll(x3, step, axis=1).reshape(rows, w)


def _lru_kernel(xa_ref, ga_ref, h0f_ref, h0b_ref, cw_ref, cb_ref, wa_ref, ba_ref, wx_ref, bx_ref,
                lam_ref, y_ref, lf_ref, lb_ref, a_scr, u_scr, hf_scr, hb_scr):
    seq, w = xa_ref.shape
    groups = seq // V7X_SUBLANES
    xa = xa_ref[...]
    row = lax.broadcasted_iota(jnp.int32, (seq, w), 0)
    sub = row & (V7X_SUBLANES - 1)

    def tap(offset):
        if offset == 0:
            return xa
        shifted = pltpu.roll(xa, (-offset) % seq, axis=0)
        valid = (row + offset >= 0) & (row + offset < seq)
        return jnp.where(valid, shifted, 0.0)

    xc = cb_ref[...]
    for j in range(CONV_W):
        xc = xc + tap(j - CONV_W // 2) * cw_ref[pl.ds(j, 1), :]
    xcb = xc.astype(BF16)

    for direction, (h0_ref, h_scr, last_ref) in enumerate(((h0f_ref, hf_scr, lf_ref),
                                                            (h0b_ref, hb_scr, lb_ref))):
        reverse = direction == 1
        r = jax.nn.sigmoid(jnp.dot(xcb, wa_ref[direction], preferred_element_type=F32) + ba_ref[direction])
        gi = jax.nn.sigmoid(jnp.dot(xcb, wx_ref[direction], preferred_element_type=F32) + bx_ref[direction])
        neg_lam = -lam_ref[direction]
        softplus = jnp.maximum(neg_lam, 0.0) + jnp.log1p(jnp.exp(-jnp.abs(neg_lam)))
        log_a = -LRU_C * r * softplus
        a = jnp.exp(log_a)
        m2 = -jnp.tanh(log_a) * (a * a + 1.0)
        u = jnp.where(m2 > 0.0, m2 * lax.rsqrt(m2), 0.0) * (gi * xc)
        for step in (1, 2, 4):
            if reverse:
                a_nb = _group_roll(a, V7X_SUBLANES - step)
                u_nb = _group_roll(u, V7X_SUBLANES - step)
                valid = sub < V7X_SUBLANES - step
            else:
                a_nb = _group_roll(a, step)
                u_nb = _group_roll(u, step)
                valid = sub >= step
            u = a * jnp.where(valid, u_nb, 0.0) + u
            a = a * jnp.where(valid, a_nb, 1.0)
        a_scr[...] = a
        u_scr[...] = u

        def carry_step(g, carry, reverse=reverse, h_scr=h_scr):
            gg = groups - 1 - g if reverse else g
            off = pl.multiple_of(gg * V7X_SUBLANES, V7X_SUBLANES)
            h = a_scr[pl.ds(off, V7X_SUBLANES), :] * carry + u_scr[pl.ds(off, V7X_SUBLANES), :]
            h_scr[pl.ds(off, V7X_SUBLANES), :] = h
            return h[0:1] if reverse else h[V7X_SUBLANES - 1:V7X_SUBLANES]

        last_ref[...] = lax.fori_loop(0, groups, carry_step, h0_ref[...], unroll=4)

    y_ref[...] = ((hf_scr[...] + hb_scr[...]) * jax.nn.gelu(ga_ref[...])).astype(BF16)


def _lru(p_ab, h0f, h0b, prm, seq_len, n_seq, row_start):
    lru_w = h0f.shape[-1]
    cw = LRU_COLS
    ncb = lru_w // cw
    sb = row_start // seq_len
    vec = lambda: pl.BlockSpec((None, 1, cw), lambda s, c: (s, 0, c))
    par2 = lambda: pl.BlockSpec((None, 2, 1, cw), lambda s, c: (prm["idx"], 0, 0, c))
    gate = lambda: pl.BlockSpec((None, 2, None, cw, cw), lambda s, c: (prm["idx"], 0, c, 0, 0))
    return pl.pallas_call(
        _lru_kernel,
        grid=(n_seq, ncb),
        in_specs=[
            pl.BlockSpec((seq_len, cw), lambda s, c: (s + sb, c)),
            pl.BlockSpec((seq_len, cw), lambda s, c: (s + sb, c + ncb)),
            vec(), vec(),
            pl.BlockSpec((None, CONV_W, cw), lambda s, c: (prm["idx"], 0, c)),
            pl.BlockSpec((None, 1, cw), lambda s, c: (prm["idx"], 0, c)),
            gate(), par2(), gate(), par2(), par2(),
        ],
        out_specs=[
            pl.BlockSpec((seq_len, cw), lambda s, c: (s, c)),
            vec(), vec(),
        ],
        out_shape=[
            jax.ShapeDtypeStruct((n_seq * seq_len, lru_w), BF16),
            jax.ShapeDtypeStruct((n_seq, 1, lru_w), F32),
            jax.ShapeDtypeStruct((n_seq, 1, lru_w), F32),
        ],
        scratch_shapes=[pltpu.VMEM((seq_len, cw), F32)] * 4,
        compiler_params=_params(("parallel", "parallel"), 40),
        name="lru",
    )(p_ab, p_ab, h0f.reshape(n_seq, 1, lru_w), h0b.reshape(n_seq, 1, lru_w), prm["conv_w"],
      prm["conv_b"], prm["wa"], prm["ba"], prm["wx"], prm["bx"], prm["lam"])


def _block_diag_tiles(w, tile):
    n, two, nb, bw, _ = w.shape
    per = tile // bw
    w = w.reshape(n, two, nb // per, per, bw, bw)
    eye = jnp.eye(per, dtype=w.dtype)
    out = jnp.einsum("ndgpij,pq->ndgpiqj", w, eye)
    return out.reshape(n, two, nb // per, tile, tile)


def _softmax_attend(q, ks, vs, biases, scale):
    scores = []
    for k, b in zip(ks, biases):
        s = lax.dot_general(q, k, (((1,), (1,)), ((), ())), preferred_element_type=F32)
        if scale is not None:
            s = s * scale
        scores.append(s if b is None else s + b)
    m = scores[0].max(axis=-1, keepdims=True)
    for s in scores[1:]:
        m = jnp.maximum(m, s.max(axis=-1, keepdims=True))
    denom = None
    out = None
    for s, v in zip(scores, vs):
        e = jnp.exp(s - m)
        part = e.sum(axis=-1, keepdims=True)
        denom = part if denom is None else denom + part
        o = jnp.dot(e.astype(BF16), v, preferred_element_type=F32)
        out = o if out is None else out + o
    return out / denom


def _upper_half(shape):
    lane = lax.broadcasted_iota(jnp.int32, shape, len(shape) - 1)
    return lane % V7X_LANES >= V7X_LANES // 2


def _split_pair(q):
    up = _upper_half(q.shape)
    zero = jnp.zeros_like(q)
    return jnp.concatenate([jnp.where(up, zero, q), jnp.where(up, q, zero)], axis=0)


def _merge_pair(o):
    m = o.shape[0] // 2
    return jnp.where(_upper_half((m, o.shape[1])), o[m:], o[:m])


def _dup_head(block, use_high):
    swapped = pltpu.roll(block, V7X_LANES // 2, axis=1)
    keep = _upper_half(block.shape) == use_high
    return jnp.where(keep, block, swapped)


def _fold_scale(q, scale):
    return (q * scale, None) if _is_pow2(scale) else (q, scale)


def _na_ctx_kernel(q_ref, k_ref, v_ref, o_ref, *, scale):
    for pair in range(q_ref.shape[1] // V7X_LANES):
        sl = pl.ds(pair * V7X_LANES, V7X_LANES)
        q, sc = _fold_scale(q_ref[:, sl], scale)
        o = _softmax_attend(_split_pair(q.astype(BF16)), [k_ref[:, sl].astype(BF16)],
                            [v_ref[:, sl].astype(BF16)], [None], sc)
        o_ref[:, sl] = _merge_pair(o).astype(o_ref.dtype)


def _na_ctx(p_ab, n_seq, seq_len, q_blk, na_w, scale):
    blk = lambda col: pl.BlockSpec((seq_len, na_w), lambda b: (b, col))
    return pl.pallas_call(
        functools.partial(_na_ctx_kernel, scale=scale),
        grid=(n_seq,),
        in_specs=[blk(q_blk), blk(q_blk + 1), blk(q_blk + 2)],
        out_specs=pl.BlockSpec((seq_len, na_w), lambda b: (b, 0)),
        out_shape=jax.ShapeDtypeStruct((n_seq * seq_len, na_w), BF16),
        compiler_params=_params(("parallel",), 40),
        name="na_ctx",
    )(p_ab, p_ab, p_ab)


def _na_tile_index(seq_len):
    rows_n = seq_len // GRID_W
    kr = min(NA_KR, rows_n)
    rows_per_chunk = NA_Q_CHUNK // GRID_W
    n_chunks = rows_n // rows_per_chunk
    win_rows = NA_WIN // GRID_W
    table = []
    for r in range(rows_n):
        r_start = min(max(r - kr // 2, 0), rows_n - kr)
        base = 0 if r // rows_per_chunk < n_chunks // 2 else rows_n - win_rows
        table.append([k - r + NA_KR - 1 if r_start <= k < r_start + kr else 2 * NA_KR - 1
                      for k in range(base, base + win_rows)])
    return table


def _na_lat_kernel(q_ref, k_ref, v_ref, kc_ref, vc_ref, tiles_ref, o_ref, bias_ref, *, scale):
    lq = q_ref.shape[0]
    n_chunks = lq // NA_Q_CHUNK
    rows_per_chunk = NA_Q_CHUNK // GRID_W

    @pl.when(pl.program_id(1) == 0)
    def _():
        low = lax.broadcasted_iota(jnp.int32, (GRID_W, V7X_LANES), 1) < GRID_W
        for half in (0, 1):
            for r, slots in enumerate(_na_tile_index(lq)):
                c, rr = divmod(r, rows_per_chunk)
                rows = pl.ds(half * NA_Q_CHUNK + rr * GRID_W, GRID_W)
                for s in range(0, len(slots), 2):
                    blk = jnp.where(low, tiles_ref[half, slots[s]], tiles_ref[half, slots[s + 1]])
                    bias_ref[c, rows, pl.ds(s * GRID_W, V7X_LANES)] = blk

    kcb = kc_ref[...].astype(BF16)
    vcb = vc_ref[...].astype(BF16)
    for c in range(n_chunks):
        win = 0 if c < n_chunks // 2 else lq - NA_WIN
        kb = k_ref[pl.ds(win, NA_WIN), :].astype(BF16)
        vb = v_ref[pl.ds(win, NA_WIN), :].astype(BF16)
        rows = pl.ds(c * NA_Q_CHUNK, NA_Q_CHUNK)
        q, sc = _fold_scale(q_ref[rows, :], scale)
        o = _softmax_attend(_split_pair(q.astype(BF16)), [kb, kcb], [vb, vcb], [bias_ref[c], None], sc)
        o_ref[rows, :] = _merge_pair(o).astype(o_ref.dtype)


def _na_lat(p_ab, cache_k, cache_v, tiles, idx, n_seq, seq_len, row_start, q_col, n_pairs, scale):
    sb = row_start // seq_len
    past = cache_k.shape[2]
    blk = lambda col: pl.BlockSpec((seq_len, V7X_LANES), lambda h, b: (b + sb, col + h))
    cblk = lambda: pl.BlockSpec((None, None, past, V7X_LANES), lambda h, b: (b, idx, 0, h))
    n_chunks = seq_len // NA_Q_CHUNK
    return pl.pallas_call(
        functools.partial(_na_lat_kernel, scale=scale),
        grid=(n_pairs, n_seq),
        in_specs=[blk(q_col), blk(q_col + n_pairs), blk(q_col + 2 * n_pairs), cblk(), cblk(),
                  pl.BlockSpec((None, 2, 2 * NA_KR, GRID_W, V7X_LANES), lambda h, b: (idx, h, 0, 0, 0))],
        out_specs=pl.BlockSpec((seq_len, V7X_LANES), lambda h, b: (b, h)),
        out_shape=jax.ShapeDtypeStruct((n_seq * seq_len, n_pairs * V7X_LANES), BF16),
        scratch_shapes=[pltpu.VMEM((n_chunks, 2 * NA_Q_CHUNK, NA_WIN), F32)],
        compiler_params=_params(("parallel", "arbitrary"), 56),
        name="na_lat",
    )(p_ab, p_ab, p_ab, cache_k, cache_v, tiles)


def _na_bias_tiles(tab):
    lead = tab.shape[:-1]
    edge = GRID_W - NA_KC
    vec = jnp.pad(tab, [(0, 0)] * len(lead) + [(edge, edge + 1)], mode="edge")
    skew = jnp.broadcast_to(vec[..., None, :], lead + (GRID_W, 2 * GRID_W))
    skew = skew.reshape(lead + (2 * GRID_W * GRID_W,))[..., :GRID_W * (2 * GRID_W - 1)]
    toep = skew.reshape(lead + (GRID_W, 2 * GRID_W - 1))[..., GRID_W - 1:]
    cols = np.arange(GRID_W)
    c_start = np.clip(cols - NA_KC // 2, 0, GRID_W - NA_KC)
    col_ok = (cols[None, :] >= c_start[:, None]) & (cols[None, :] < c_start[:, None] + NA_KC)
    tiles = jnp.where(col_ok, toep, NEG_BIAS)
    masked = jnp.full(lead[:-1] + (1, GRID_W, GRID_W), NEG_BIAS, F32)
    tiles = jnp.concatenate([tiles, masked], axis=-3)
    return jnp.concatenate([tiles, tiles], axis=-1)


def _split_dot(v, m):
    hi = v.astype(BF16)
    lo = (v - hi.astype(F32)).astype(BF16)
    return jnp.dot(hi, m, preferred_element_type=F32) + jnp.dot(lo, m, preferred_element_type=F32)


def _head_rms(x, gather, spread, gain, head_dim):
    inv = lax.rsqrt(_split_dot(x * x, gather) * (1.0 / head_dim) + EPS)
    return x * _split_dot(inv, spread) * gain


def _rope(x, cos, sin_lo, sin_hi):
    w = x.shape[1]
    reps = w // V7X_LANES
    tile = lambda t: t if reps == 1 else jnp.concatenate([t] * reps, axis=1)
    quarter = 16
    return (x * tile(cos) + pltpu.roll(x, w - quarter, axis=1) * tile(sin_lo)
            + pltpu.roll(x, quarter, axis=1) * tile(sin_hi))


def _cd_post_kernel(p_ref, gat_ref, spr_ref, qg_ref, kg_ref, mqg_ref, mkg_ref, wuq_ref, wukv_ref, *rest,
                    rope, head_dim, q_w, kv_w, rank, q_scale):
    if rope:
        cos_ref, slo_ref, shi_ref = rest[:3]
        rest = rest[3:]
        tabs = (cos_ref[...], slo_ref[...], shi_ref[...])
    q_ref, k_ref, qn_ref, qr_ref, ckv_ref, kn_ref, vm_ref, kr_ref = rest
    c0 = 0
    qc = p_ref[:, pl.ds(c0, q_w)]
    c0 += q_w
    kc = p_ref[:, pl.ds(c0, kv_w)]
    c0 += 2 * kv_w
    qa = p_ref[:, pl.ds(c0, rank)]
    c0 += rank
    ckv = p_ref[:, pl.ds(c0, rank)]
    c0 += rank
    kr = p_ref[:, pl.ds(c0, V7X_LANES)]

    qc = _head_rms(qc, gat_ref[...], spr_ref[...], qg_ref[...], head_dim)
    kc = _head_rms(kc, gat_ref[pl.ds(0, kv_w), :], spr_ref[:, pl.ds(0, kv_w)], kg_ref[...], head_dim)
    qd = jnp.dot(_rms(qa, mqg_ref[...]).astype(BF16), wuq_ref[...], preferred_element_type=F32)
    n_nope = qn_ref.shape[1]
    qn = qd[:, :n_nope]
    qr = qd[:, n_nope:]
    ckv = _rms(ckv, mkg_ref[...])
    if rope:
        qc = _rope(qc, *tabs)
        kc = _rope(kc, *tabs)
        qr = _rope(qr, *tabs)
        kr = _rope(kr, *tabs)
    q_ref[...] = (qc * q_scale).astype(BF16)
    k_ref[...] = kc
    qn_ref[...] = qn.astype(BF16)
    qr_ref[...] = qr.astype(BF16)
    ckv_ref[...] = ckv
    kv = jnp.dot(ckv.astype(BF16), wukv_ref[...], preferred_element_type=F32)
    kn_ref[...] = kv[:, :n_nope].astype(BF16)
    vm_ref[...] = kv[:, n_nope:].astype(BF16)
    kr_ref[...] = kr


def _cd_post(p_cd, prm, row_start, rows, rope_tabs, dims, q_scale):
    q_w, kv_w, rank, head_dim, n_nope, n_rope, n_v = dims
    tm = 256
    off = row_start // tm
    n_in = p_cd.shape[1]
    idx = prm["idx"]
    rope = rope_tabs is not None
    const = lambda shape: pl.BlockSpec(shape, lambda i: (0,) * len(shape))
    layer = lambda shape: pl.BlockSpec((None,) + shape, lambda i: (idx,) + (0,) * len(shape))
    in_specs = [
        pl.BlockSpec((tm, n_in), lambda i: (i + off, 0)),
        const((q_w, V7X_LANES)), const((V7X_LANES, q_w)),
        layer((1, q_w)), layer((1, kv_w)), layer((1, rank)), layer((1, rank)),
        layer((rank, n_nope + n_rope)), layer((rank, n_nope + n_v)),
    ]
    args = [p_cd, prm["gather"], prm["spread"], prm["q_gain"], prm["k_gain"], prm["mla_q_gain"],
            prm["mla_kv_gain"], prm["w_uq"], prm["w_ukv"]]
    if rope:
        seq_tiles = rope_tabs[0].shape[0] // tm
        in_specs += [pl.BlockSpec((tm, V7X_LANES), lambda i: (i % seq_tiles, 0))] * 3
        args += list(rope_tabs)
    widths = [(q_w, BF16), (kv_w, F32), (n_nope, BF16), (n_rope, BF16), (rank, F32), (n_nope, BF16),
              (n_v, BF16), (V7X_LANES, F32)]
    kern = functools.partial(_cd_post_kernel, rope=rope, head_dim=head_dim, q_w=q_w, kv_w=kv_w, rank=rank,
                             q_scale=q_scale)
    return pl.pallas_call(
        kern,
        grid=(rows // tm,),
        in_specs=in_specs,
        out_specs=[pl.BlockSpec((tm, w), lambda i: (i, 0)) for w, _ in widths],
        out_shape=[jax.ShapeDtypeStruct((rows, w), dt) for w, dt in widths],
        compiler_params=_params(("parallel",), 48),
        name="cd_post",
    )(*args)


def _rope_tables(seq_len, head_dim):
    half = head_dim // 2
    nf = half // 2
    t = np.arange(seq_len)
    inv_freq = (1.0 / (ROPE_THETA ** (np.arange(nf, dtype=np.float32) / nf))).astype(np.float32)
    zeros = np.zeros((seq_len, nf), np.float32)
    cos, slo, shi = [], [], []
    for pos in (t // GRID_W, t % GRID_W):
        ang = pos.astype(np.float32)[:, None] * inv_freq[None, :]
        c, s = np.cos(ang).astype(np.float32), np.sin(ang).astype(np.float32)
        cos += [c, c]
        slo += [-s, zeros]
        shi += [zeros, s]
    reps = V7X_LANES // head_dim
    return tuple(jnp.asarray(np.tile(np.concatenate(x, axis=1), (1, reps))) for x in (cos, slo, shi))


def _ckv_up_kernel(c_ref, w_ref, kn_ref, vm_ref):
    kv = jnp.dot(c_ref[...].astype(BF16), w_ref[...], preferred_element_type=F32)
    n = kn_ref.shape[1]
    kn_ref[...] = kv[:, :n].astype(BF16)
    vm_ref[...] = kv[:, n:].astype(BF16)


def _ckv_up(ckv, w_ukv, idx, n_nope):
    n_seq, _, tm, rank = ckv.shape
    rows = n_seq * tm
    n = w_ukv.shape[2]
    return pl.pallas_call(
        _ckv_up_kernel,
        grid=(rows // tm,),
        in_specs=[pl.BlockSpec((None, None, tm, rank), lambda i: (i, idx, 0, 0)),
                  pl.BlockSpec((None, rank, n), lambda i: (idx, 0, 0))],
        out_specs=[pl.BlockSpec((tm, n_nope), lambda i: (i, 0)), pl.BlockSpec((tm, n - n_nope), lambda i: (i, 0))],
        out_shape=[jax.ShapeDtypeStruct((rows, n_nope), BF16), jax.ShapeDtypeStruct((rows, n - n_nope), BF16)],
        compiler_params=_params(("parallel",), 32),
        name="ckv_up",
    )(ckv, w_ukv)


def _gqa_kernel(q_ref, k_ref, v_ref, *rest, scale, has_cache, group_w):
    if has_cache:
        kc_ref, vc_ref, o_ref = rest
    else:
        (o_ref,) = rest
    lq = q_ref.shape[0]
    n_local = q_ref.shape[1] // group_w
    n_pairs = group_w // V7X_LANES
    for hl in range(n_local):
        if n_local == 1:
            high = (pl.program_id(1) % 2) == 1
            pick = lambda ref: ref[...]
        else:
            high = hl % 2 == 1
            pick = lambda ref, hl=hl: ref[:, pl.ds((hl // 2) * V7X_LANES, V7X_LANES)]
        ks = [_dup_head(pick(k_ref), high).astype(BF16)]
        vs = [_dup_head(pick(v_ref), high).astype(BF16)]
        if has_cache:
            ks.append(_dup_head(pick(kc_ref), high).astype(BF16))
            vs.append(_dup_head(pick(vc_ref), high).astype(BF16))
        for c in range(lq // Q_CHUNK):
            rows = pl.ds(c * Q_CHUNK, Q_CHUNK)
            cols = [pl.ds(hl * group_w + p * V7X_LANES, V7X_LANES) for p in range(n_pairs)]
            q = jnp.concatenate([_split_pair(q_ref[rows, cl]) for cl in cols], axis=0)
            o = _softmax_attend(q, ks, vs, [None] * len(ks), scale)
            for p, cl in enumerate(cols):
                o_ref[rows, cl] = _merge_pair(o[2 * p * Q_CHUNK:2 * (p + 1) * Q_CHUNK]).astype(o_ref.dtype)


def _gqa(q, k, p_cd, v_col, cache_k, cache_v, idx, n_seq, seq_len, row_start, n_kv, heads_per_step, scale):
    sb = row_start // seq_len
    group_w = q.shape[1] // n_kv
    has_cache = cache_k is not None
    if heads_per_step == 1:
        kv_w, kv_blk, v_blk = V7X_LANES, (lambda h: h // 2), (lambda h: v_col + h // 2)
    else:
        kv_w = k.shape[1]
        kv_blk, v_blk = (lambda h: 0), (lambda h: v_col * V7X_LANES // kv_w)
    in_specs = [
        pl.BlockSpec((seq_len, group_w * heads_per_step), lambda b, h: (b, h)),
        pl.BlockSpec((seq_len, kv_w), lambda b, h: (b, kv_blk(h))),
        pl.BlockSpec((seq_len, kv_w), lambda b, h: (b + sb, v_blk(h))),
    ]
    args = [q, k, p_cd]
    if has_cache:
        past = cache_k.shape[2]
        in_specs += [pl.BlockSpec((None, None, past, kv_w), lambda b, h: (b, idx, 0, kv_blk(h)))] * 2
        args += [cache_k, cache_v]
    return pl.pallas_call(
        functools.partial(_gqa_kernel, scale=scale, has_cache=has_cache, group_w=group_w),
        grid=(n_seq, n_kv // heads_per_step),
        in_specs=in_specs,
        out_specs=pl.BlockSpec((seq_len, group_w * heads_per_step), lambda b, h: (b, h)),
        out_shape=jax.ShapeDtypeStruct(q.shape, BF16),
        compiler_params=_params(("parallel", "parallel"), 56),
        name="gqa",
    )(*args)


def _mla_kernel(qn_ref, qr_ref, kn_ref, kr_ref, v_ref, *rest, scale, has_cache):
    if has_cache:
        knc_ref, krc_ref, vc_ref, o_ref = rest
        krc = krc_ref[...].astype(BF16)
    else:
        (o_ref,) = rest
    lq = qn_ref.shape[0]
    qc = min(2 * Q_CHUNK, lq)
    n_local = qn_ref.shape[1] // V7X_LANES
    kr = kr_ref[...].astype(BF16)
    for hl in range(n_local):
        sl = pl.ds(hl * V7X_LANES, V7X_LANES)
        if n_local == 1:
            high = (pl.program_id(1) % 2) == 1
            pair = pl.ds(0, V7X_LANES)
        else:
            high = hl % 2 == 1
            pair = pl.ds((hl // 2) * V7X_LANES, V7X_LANES)
        ks = [jnp.concatenate([kn_ref[:, sl], kr], axis=1)]
        vs = [v_ref[:, sl]]
        if has_cache:
            ks.append(jnp.concatenate([knc_ref[:, sl], krc], axis=1))
            vs.append(vc_ref[:, sl])
        for c in range(lq // qc):
            rows = pl.ds(c * qc, qc)
            qr = qr_ref[rows, pair]
            qr = jnp.where(_upper_half(qr.shape) == high, qr, jnp.zeros_like(qr))
            q = jnp.concatenate([qn_ref[rows, sl], qr], axis=1)
            o_ref[rows, sl] = _softmax_attend(q, ks, vs, [None] * len(ks), scale).astype(o_ref.dtype)


def _mla(qn, qr, kn, kr2, vm, cache, n_seq, seq_len, heads_per_step, scale):
    has_cache = cache is not None
    n_heads = vm.shape[1] // V7X_LANES
    hw = heads_per_step * V7X_LANES
    head = lambda rows: pl.BlockSpec((rows, hw), lambda b, h: (b, h))
    if heads_per_step == 1:
        rope_q = pl.BlockSpec((seq_len, V7X_LANES), lambda b, h: (b, h // 2))
    else:
        rope_q = pl.BlockSpec((seq_len, qr.shape[1]), lambda b, h: (b, 0))
    in_specs = [head(seq_len), rope_q, head(seq_len),
                pl.BlockSpec((seq_len, V7X_LANES), lambda b, h: (b, 0)),
                head(seq_len)]
    args = [qn, qr, kn, kr2, vm]
    if has_cache:
        knc, krc, vmc, idx = cache
        past = krc.shape[2]
        in_specs += [head(past),
                     pl.BlockSpec((None, None, past, V7X_LANES), lambda b, h: (b, idx, 0, 0)),
                     head(past)]
        args += [knc, krc, vmc]
    return pl.pallas_call(
        functools.partial(_mla_kernel, scale=scale, has_cache=has_cache),
        grid=(n_seq, n_heads // heads_per_step),
        in_specs=in_specs,
        out_specs=head(seq_len),
        out_shape=jax.ShapeDtypeStruct(vm.shape, BF16),
        compiler_params=_params(("parallel", "parallel"), 48),
        name="mla",
    )(*args)


def kernel(x_prompt, x_sample, state_lru_fwd, state_lru_bwd, cache_na_k, cache_na_v, cache_gqa_k, cache_gqa_v, cache_mla_ckv, cache_mla_krope, c, c_ctx, w_mod, b_mod, norm_gain, w_ffn_in, w_ffn_out, w_in_ab, conv_w, conv_b, lru_wa, lru_ba, lru_wx, lru_bx, lru_lambda, na_bias, w_out_ab, w_in_cd, gqa_q_gain, gqa_k_gain, mla_q_gain, mla_kv_gain, mla_w_uq, mla_w_uk, mla_w_uv, w_out_cd, final_gain):
    batch, seq, d = x_prompt.shape
    dec_batch, dec_seq, _ = x_sample.shape
    depth = w_mod.shape[0]
    n_even, n_odd = w_in_ab.shape[0], w_in_cd.shape[0]
    n_p = batch * seq
    n_s = dec_batch * dec_seq
    past = cache_na_k.shape[2]
    lru_w = state_lru_fwd.shape[-1]
    na_heads, na_dh = cache_na_k.shape[3], cache_na_k.shape[4]
    na_w = na_heads * na_dh
    gqa_kv, gqa_dh = cache_gqa_k.shape[3], cache_gqa_k.shape[4]
    kv_w = gqa_kv * gqa_dh
    rank = cache_mla_ckv.shape[-1]
    rope_w = cache_mla_krope.shape[-1]
    q_w = w_in_cd.shape[2] - 2 * kv_w - 2 * rank - rope_w
    mla_heads = mla_w_uk.shape[2] // MLA_NOPE
    n_nope = mla_heads * MLA_NOPE
    n_rope = mla_heads * rope_w
    n_v = mla_heads * MLA_V
    mla_qk = MLA_NOPE + rope_w
    assert mla_q_gain.shape[-1] == rank and 2 * rope_w == V7X_LANES
    assert n_p % dec_seq == 0 and dec_seq % TOKEN_TILE == 0 and seq % Q_CHUNK == 0
    assert dec_seq // GRID_W == 2 * NA_KR and na_dh == GRID_W and gqa_dh == GRID_W
    assert w_in_ab.shape[2] == 2 * lru_w + 3 * na_w and lru_w == na_w
    assert (q_w // gqa_kv) % V7X_LANES == 0 and q_w % kv_w == 0

    x = (x_prompt.reshape(n_p, d), x_sample.reshape(n_s, d))
    cond = jnp.concatenate([c_ctx[None, :], c, jnp.zeros((MOD_ROWS - 1 - dec_batch, d), F32)], axis=0)
    mods = _modulation(cond, w_mod, b_mod)
    gains = norm_gain.reshape(depth * 3, 1, d)

    ffn_w = (w_ffn_in[0, 0].astype(BF16), w_ffn_out[0, 0].astype(BF16))

    def ffn_step(x, ffn_w, layer, which):
        last = layer == depth - 1 and which == 1
        nxt = None if last else (w_ffn_in, w_ffn_out, layer + which, 1 - which)
        out = _ffn(x, mods, gains, ffn_w[0], ffn_w[1], layer, which, nxt, n_p, dec_seq)
        return out[0], tuple(out[1:])
    w_in_ab_bf = w_in_ab.astype(BF16)
    w_out_ab_bf = w_out_ab.astype(BF16)
    w_in_cd_bf = jnp.concatenate([w_in_cd, w_in_cd[:, :, -rope_w:]], axis=2).astype(BF16)
    w_out_cd_bf = w_out_cd.astype(BF16)

    lru_prm = {
        "conv_w": conv_w, "conv_b": conv_b.reshape(n_even, 1, lru_w),
        "wa": _block_diag_tiles(lru_wa, LRU_COLS).astype(BF16),
        "wx": _block_diag_tiles(lru_wx, LRU_COLS).astype(BF16),
        "ba": lru_ba.reshape(n_even, 2, 1, lru_w), "bx": lru_bx.reshape(n_even, 2, 1, lru_w),
        "lam": lru_lambda.reshape(n_even, 2, 1, lru_w),
    }
    zeros_state = jnp.zeros((batch, lru_w), F32)
    na_k_ctx = cache_na_k.reshape(dec_batch, n_even, past, na_w)
    na_v_ctx = cache_na_v.reshape(dec_batch, n_even, past, na_w)
    na_tiles = _na_bias_tiles(na_bias)

    w_uq = mla_w_uq.reshape(n_odd, rank, mla_heads, mla_qk)
    w_uq = jnp.concatenate([w_uq[..., :MLA_NOPE].reshape(n_odd, rank, n_nope),
                            w_uq[..., MLA_NOPE:].reshape(n_odd, rank, n_rope)], axis=2).astype(BF16)
    gqa_scale = gqa_dh ** -0.5
    q_scale = gqa_scale if _is_pow2(gqa_scale) else 1.0
    head_of_lane = np.arange(q_w) // gqa_dh
    gather = (head_of_lane[:, None] == np.arange(V7X_LANES)[None, :]).astype(np.float32)
    assert q_w // gqa_dh <= V7X_LANES
    cd_prm = {
        "gather": jnp.asarray(gather, BF16), "spread": jnp.asarray(gather.T, BF16),
        "q_gain": jnp.tile(gqa_q_gain, (1, q_w // gqa_dh)).reshape(n_odd, 1, q_w),
        "k_gain": jnp.tile(gqa_k_gain, (1, gqa_kv)).reshape(n_odd, 1, kv_w),
        "mla_q_gain": mla_q_gain.reshape(n_odd, 1, rank),
        "mla_kv_gain": mla_kv_gain.reshape(n_odd, 1, rank),
        "w_uq": w_uq,
        "w_ukv": jnp.concatenate([mla_w_uk, mla_w_uv], axis=2).astype(BF16),
    }
    rope_tabs = _rope_tables(dec_seq, gqa_dh)
    gqa_k_ctx = cache_gqa_k.reshape(dec_batch, n_odd, past, kv_w)
    gqa_v_ctx = cache_gqa_v.reshape(dec_batch, n_odd, past, kv_w)
    krope_ctx = jnp.concatenate([cache_mla_krope, cache_mla_krope], axis=-1)

    st_f, st_b, na_k, na_v, gq_k, gq_v, ml_c, ml_r = [], [], [], [], [], [], [], []
    for layer in range(depth):
        jdx = layer // 2
        x, ffn_w = ffn_step(x, ffn_w, layer, 0)
        if layer % 2 == 0:
            p_ab = _inproj(x, mods, gains, w_in_ab_bf, layer, jdx, w_in_ab_bf.shape[2] // 2, n_p, dec_seq)
            prm = dict(lru_prm, idx=jdx)
            ya_p, lf, lb = _lru(p_ab, zeros_state, zeros_state, prm, seq, batch, 0)
            ya_s, _, _ = _lru(p_ab, state_lru_fwd[:, jdx], state_lru_bwd[:, jdx], prm, dec_seq, dec_batch, n_p)
            scale = na_dh ** -0.5
            yb_p = _na_ctx(p_ab, batch, seq, 2 * lru_w // na_w, na_w, scale)
            yb_s = _na_lat(p_ab, na_k_ctx, na_v_ctx, na_tiles, jdx, dec_batch, dec_seq, n_p,
                           2 * lru_w // V7X_LANES, na_w // V7X_LANES, scale)
            x = _outproj(x, mods, layer, ya_p, ya_s, yb_p, yb_s, w_out_ab_bf, jdx, n_p, dec_seq)
            st_f.append(lf.reshape(batch, lru_w))
            st_b.append(lb.reshape(batch, lru_w))
            na_k.append(p_ab[:n_p, 2 * lru_w + na_w:2 * lru_w + 2 * na_w].reshape(batch, seq, na_heads, na_dh))
            na_v.append(p_ab[:n_p, 2 * lru_w + 2 * na_w:].reshape(batch, seq, na_heads, na_dh))
        else:
            p_cd = _inproj(x, mods, gains, w_in_cd_bf, layer, jdx, w_in_cd_bf.shape[2], n_p, dec_seq)
            prm = dict(cd_prm, idx=jdx)
            dims = (q_w, kv_w, rank, gqa_dh, n_nope, n_rope, n_v)
            qp, kp, qnp_, qrp, ckvp, knp_, vmp, krp = _cd_post(p_cd, prm, 0, n_p, None, dims, q_scale)
            qs, ks_, qns, qrs, _, kns, vms, krs = _cd_post(p_cd, prm, n_p, n_s, rope_tabs, dims, q_scale)
            v_col = (q_w + kv_w) // V7X_LANES
            att_scale = None if q_scale != 1.0 else gqa_scale
            yc_p = _gqa(qp, kp, p_cd, v_col, None, None, jdx, batch, seq, 0, gqa_kv, gqa_kv, att_scale)
            yc_s = _gqa(qs, ks_, p_cd, v_col, gqa_k_ctx, gqa_v_ctx, jdx, dec_batch, dec_seq, n_p,
                        gqa_kv, 1, att_scale)
            knc, vmc = _ckv_up(cache_mla_ckv, cd_prm["w_ukv"], jdx, n_nope)
            yd_p = _mla(qnp_, qrp, knp_, krp, vmp, None, batch, seq, mla_heads, mla_qk ** -0.5)
            yd_s = _mla(qns, qrs, kns, krs, vms, (knc, krope_ctx, vmc, jdx), dec_batch, dec_seq, 1,
                        mla_qk ** -0.5)
            x = _outproj(x, mods, layer, yc_p, yc_s, yd_p, yd_s, w_out_cd_bf, jdx, n_p, dec_seq)
            gq_k.append(kp.reshape(batch, seq, gqa_kv, gqa_dh))
            gq_v.append(p_cd[:n_p, q_w + kv_w:q_w + 2 * kv_w].reshape(batch, seq, gqa_kv, gqa_dh))
            ml_c.append(ckvp.reshape(batch, seq, rank))
            ml_r.append(krp[:, :rope_w].reshape(batch, seq, rope_w))
        x, ffn_w = ffn_step(x, ffn_w, layer, 1)

    y_prompt = _final_norm(x, final_gain, 0, n_p).reshape(batch, seq, d)
    y_sample = _final_norm(x, final_gain, n_p, n_s).reshape(dec_batch, dec_seq, d)
    stack = lambda xs: jnp.stack(xs, axis=1)
    return (y_prompt, y_sample, stack(st_f), stack(st_b), stack(na_k), stack(na_v), stack(gq_k), stack(gq_v),
            stack(ml_c), stack(ml_r))
```

```python
import functools
import math

import numpy as np

import jax
import jax.numpy as jnp
from jax import lax
from jax.experimental import pallas as pl
from jax.experimental.pallas import tpu as pltpu

F32 = jnp.float32
BF16 = jnp.bfloat16

EPS = 1e-6
N_MOD = 9
GRID_W = 64
NA_KR = 8
NA_KC = 16
ROPE_THETA = 10000.0
LRU_C = 8.0
CONV_W = 4
MLA_NOPE = 128
MLA_V = 128
NEG_BIAS = -1e30

V7X_LANES = 128
V7X_SUBLANES = 8
MOD_ROWS = 16
TOKEN_TILE = 512
FF_TILE = 512
Q_CHUNK = 128
NA_Q_CHUNK = 256
NA_WIN = 768
LRU_COLS = 256


def _params(sem, vmem_mib):
    return pltpu.CompilerParams(dimension_semantics=sem, vmem_limit_bytes=vmem_mib * 1024 * 1024)


def _silu(x):
    return x * jax.nn.sigmoid(x)


def _rms(x, gain):
    ms = jnp.mean(x * x, axis=-1, keepdims=True)
    return x * lax.rsqrt(ms + EPS) * gain


def _mod_row(i, n_prompt_tiles, tiles_per_seq):
    return jnp.where(i < n_prompt_tiles, 0, 1 + (i - n_prompt_tiles) // tiles_per_seq)


def _adaln(x, mod_ref, gain_ref, row, mod_base):
    d = x.shape[1]
    shift = mod_ref[pl.ds(row, 1), pl.ds(mod_base * d, d)]
    scale = mod_ref[pl.ds(row, 1), pl.ds((mod_base + 1) * d, d)]
    return (_rms(x, gain_ref[...]) * (1 + scale) + shift).astype(BF16)


def _is_pow2(v):
    return math.frexp(v)[0] == 0.5


def _mod_kernel(cond_ref, w_ref, b_ref, o_ref):
    s = _silu(cond_ref[...]).astype(BF16)
    o_ref[...] = jnp.dot(s, w_ref[...].astype(BF16), preferred_element_type=F32) + b_ref[...]


def _modulation(cond, w_mod, b_mod):
    depth, d, n = w_mod.shape
    tn = 1024
    return pl.pallas_call(
        _mod_kernel,
        grid=(depth, n // tn),
        in_specs=[
            pl.BlockSpec((MOD_ROWS, d), lambda l, j: (0, 0)),
            pl.BlockSpec((None, d, tn), lambda l, j: (l, 0, j)),
            pl.BlockSpec((None, 1, tn), lambda l, j: (l, 0, j)),
        ],
        out_specs=pl.BlockSpec((None, MOD_ROWS, tn), lambda l, j: (l, 0, j)),
        out_shape=jax.ShapeDtypeStruct((depth, MOD_ROWS, n), F32),
        compiler_params=_params(("parallel", "parallel"), 40),
        name="modulation",
    )(cond, w_mod, b_mod.reshape(depth, 1, n))


def _ffn_kernel(*refs, mod_base, n_prompt_tiles, tiles_per_seq, cast_next, split_x):
    i = pl.program_id(0)
    j = pl.program_id(1)
    if split_x:
        xp_ref, xs_ref, *refs = refs
        read_x = lambda: jnp.where(i < n_prompt_tiles, xp_ref[...], xs_ref[...])
    else:
        x_ref, *refs = refs
        read_x = lambda: x_ref[...]
    mod_ref, gain_ref, wg_ref, wu_ref, wo_ref, *rest = refs
    if cast_next:
        nin_ref, nout_ref, o_ref, cin_ref, cout_ref, h_ref = rest
    else:
        o_ref, h_ref = rest
    d = o_ref.shape[1]
    nj = pl.num_programs(1)
    row = _mod_row(i, n_prompt_tiles, tiles_per_seq)

    def step(first, last):
        if first:
            h = _adaln(read_x(), mod_ref, gain_ref, row, mod_base)
            h_ref[...] = h
        else:
            h = h_ref[...]
        g = jnp.dot(h, wg_ref[...], preferred_element_type=F32)
        u = jnp.dot(h, wu_ref[...], preferred_element_type=F32)
        if cast_next:
            cin_ref[...] = nin_ref[...].astype(BF16)
            cout_ref[...] = nout_ref[...].astype(BF16)
        act = (_silu(g) * u).astype(BF16)
        acc = jnp.dot(act, wo_ref[...], preferred_element_type=F32)
        if not first:
            acc = o_ref[...] + acc
        if last:
            gate = mod_ref[pl.ds(row, 1), pl.ds((mod_base + 2) * d, d)]
            acc = read_x() + (0.5 * gate) * acc
        o_ref[...] = acc

    pl.when(j == 0)(lambda: step(True, False))
    pl.when((j > 0) & (j < nj - 1))(lambda: step(False, False))
    pl.when(j == nj - 1)(lambda: step(False, True))


def _ffn(x, mods, gains, w_in, w_out, layer, which, nxt, n_prompt_rows, seq_rows):
    split_x = isinstance(x, tuple)
    npt = n_prompt_rows // TOKEN_TILE
    if split_x:
        t, d = x[0].shape[0] + x[1].shape[0], x[0].shape[1]
        x_specs = [pl.BlockSpec((TOKEN_TILE, d), lambda i, j: (jnp.minimum(i, npt - 1), 0)),
                   pl.BlockSpec((TOKEN_TILE, d), lambda i, j: (jnp.maximum(i - npt, 0), 0))]
        x_args = list(x)
    else:
        t, d = x.shape
        x_specs = [pl.BlockSpec((TOKEN_TILE, d), lambda i, j: (i, 0))]
        x_args = [x]
    dff = w_out.shape[0]
    nj = dff // FF_TILE
    ni = t // TOKEN_TILE
    assert nj >= 2
    mod_base = 6 * which
    kern = functools.partial(_ffn_kernel, mod_base=mod_base, n_prompt_tiles=npt,
                             tiles_per_seq=seq_rows // TOKEN_TILE, cast_next=nxt is not None, split_x=split_x)
    in_specs = x_specs + [
        pl.BlockSpec((None, MOD_ROWS, N_MOD * d), lambda i, j: (layer, 0, 0)),
        pl.BlockSpec((None, 1, d), lambda i, j: (3 * layer + 2 * which, 0, 0)),
        pl.BlockSpec((d, FF_TILE), lambda i, j: (0, j)),
        pl.BlockSpec((d, FF_TILE), lambda i, j: (0, j + nj)),
        pl.BlockSpec((FF_TILE, d), lambda i, j: (j, 0)),
    ]
    args = x_args + [mods, gains, w_in, w_in, w_out]
    out_specs = [pl.BlockSpec((TOKEN_TILE, d), lambda i, j: (i, 0))]
    out_shape = [jax.ShapeDtypeStruct((t, d), F32)]
    if nxt is not None:
        f_in, f_out, nl, nw = nxt
        cast_rows = min(r for r in (16, 32, 64, 128, 256, 512)
                        if d % r == 0 and dff % r == 0 and d // r + dff // r <= ni * nj)
        n_in, n_out = d // cast_rows, dff // cast_rows
        in_blk = lambda i, j: jnp.minimum(i * nj + j, n_in - 1)
        out_blk = lambda i, j: jnp.clip(i * nj + j - n_in, 0, n_out - 1)
        in_specs += [pl.BlockSpec((None, None, cast_rows, 2 * dff), lambda i, j: (nl, nw, in_blk(i, j), 0)),
                     pl.BlockSpec((None, None, cast_rows, d), lambda i, j: (nl, nw, out_blk(i, j), 0))]
        args += [f_in, f_out]
        out_specs += [pl.BlockSpec((cast_rows, 2 * dff), lambda i, j: (in_blk(i, j), 0)),
                      pl.BlockSpec((cast_rows, d), lambda i, j: (out_blk(i, j), 0))]
        out_shape += [jax.ShapeDtypeStruct((d, 2 * dff), BF16), jax.ShapeDtypeStruct((dff, d), BF16)]
    return pl.pallas_call(
        kern,
        grid=(ni, nj),
        in_specs=in_specs,
        out_specs=out_specs,
        out_shape=out_shape,
        scratch_shapes=[pltpu.VMEM((TOKEN_TILE, d), BF16)],
        compiler_params=_params(("arbitrary", "arbitrary"), 56),
        name="ffn",
    )(*args)


def _inproj_kernel(x_ref, mod_ref, gain_ref, w_ref, o_ref, *, n_prompt_tiles, tiles_per_seq):
    row = _mod_row(pl.program_id(0), n_prompt_tiles, tiles_per_seq)
    h = _adaln(x_ref[...], mod_ref, gain_ref, row, 3)
    o_ref[...] = jnp.dot(h, w_ref[...], preferred_element_type=F32)


def _inproj(x, mods, gains, w, layer, w_idx, tm, tn, n_prompt_rows, seq_rows):
    t, d = x.shape
    n = w.shape[2]
    kern = functools.partial(_inproj_kernel, n_prompt_tiles=n_prompt_rows // tm, tiles_per_seq=seq_rows // tm)
    return pl.pallas_call(
        kern,
        grid=(t // tm, n // tn),
        in_specs=[
            pl.BlockSpec((tm, d), lambda i, j: (i, 0)),
            pl.BlockSpec((None, MOD_ROWS, N_MOD * d), lambda i, j: (layer, 0, 0)),
            pl.BlockSpec((None, 1, d), lambda i, j: (3 * layer + 1, 0, 0)),
            pl.BlockSpec((None, d, tn), lambda i, j: (w_idx, 0, j)),
        ],
        out_specs=pl.BlockSpec((tm, tn), lambda i, j: (i, j)),
        out_shape=jax.ShapeDtypeStruct((t, n), F32),
        compiler_params=_params(("parallel", "parallel"), 52),
        name="inproj",
    )(x, mods, gains, w)


def _outproj_kernel(x_ref, mod_ref, ap_ref, as_ref, bp_ref, bs_ref, w_ref, o_ref, *,
                    n_prompt_tiles, tiles_per_seq):
    i = pl.program_id(0)
    d = x_ref.shape[1]
    half = ap_ref.shape[1]
    is_prompt = i < n_prompt_tiles
    ya = jnp.where(is_prompt, ap_ref[...], as_ref[...])
    yb = jnp.where(is_prompt, bp_ref[...], bs_ref[...])
    y = jnp.dot(ya, w_ref[pl.ds(0, half), :], preferred_element_type=F32)
    y = y + jnp.dot(yb, w_ref[pl.ds(half, half), :], preferred_element_type=F32)
    row = _mod_row(i, n_prompt_tiles, tiles_per_seq)
    gate = mod_ref[pl.ds(row, 1), pl.ds(5 * d, d)]
    o_ref[...] = x_ref[...] + gate * y


def _outproj(x, mods, layer, ya_p, ya_s, yb_p, yb_s, w, w_idx, n_prompt_rows, seq_rows):
    t, d = x.shape
    half = ya_p.shape[1]
    npt = n_prompt_rows // TOKEN_TILE
    kern = functools.partial(_outproj_kernel, n_prompt_tiles=npt, tiles_per_seq=seq_rows // TOKEN_TILE)
    p_map = lambda i: (jnp.minimum(i, npt - 1), 0)
    s_map = lambda i: (jnp.maximum(i - npt, 0), 0)
    return pl.pallas_call(
        kern,
        grid=(t // TOKEN_TILE,),
        in_specs=[
            pl.BlockSpec((TOKEN_TILE, d), lambda i: (i, 0)),
            pl.BlockSpec((None, MOD_ROWS, N_MOD * d), lambda i: (layer, 0, 0)),
            pl.BlockSpec((TOKEN_TILE, half), p_map),
            pl.BlockSpec((TOKEN_TILE, half), s_map),
            pl.BlockSpec((TOKEN_TILE, half), p_map),
            pl.BlockSpec((TOKEN_TILE, half), s_map),
            pl.BlockSpec((None, 2 * half, d), lambda i: (w_idx, 0, 0)),
        ],
        out_specs=pl.BlockSpec((TOKEN_TILE, d), lambda i: (i, 0)),
        out_shape=jax.ShapeDtypeStruct((t, d), F32),
        compiler_params=_params(("parallel",), 48),
        name="outproj",
    )(x, mods, ya_p, ya_s, yb_p, yb_s, w)


def _final_norm_kernel(x_ref, g_ref, o_ref):
    o_ref[...] = _rms(x_ref[...], g_ref[...])


def _final_norm(x, gain, row_start, rows):
    d = x.shape[1]
    off = row_start // TOKEN_TILE
    return pl.pallas_call(
        _final_norm_kernel,
        grid=(rows // TOKEN_TILE,),
        in_specs=[pl.BlockSpec((TOKEN_TILE, d), lambda i: (i + off, 0)),
                  pl.BlockSpec((1, d), lambda i: (0, 0))],
        out_specs=pl.BlockSpec((TOKEN_TILE, d), lambda i: (i, 0)),
        out_shape=jax.ShapeDtypeStruct((rows, d), F32),
        compiler_params=_params(("parallel",), 32),
        name="final_norm",
    )(x, gain.reshape(1, d))


def _group_roll(x, step):
    rows, w = x.shape
    x3 = x.reshape(rows // V7X_SUBLANES, V7X_SUBLANES, w)
    return pltpu.roll(x3, step, axis=1).reshape(rows, w)


def _lru_kernel(xa_ref, ga_ref, h0f_ref, h0b_ref, cw_ref, cb_ref, wa_ref, ba_ref, wx_ref, bx_ref,
                lam_ref, y_ref, lf_ref, lb_ref, a_scr, u_scr, hf_scr, hb_scr):
    seq, w = xa_ref.shape
    groups = seq // V7X_SUBLANES
    xa = xa_ref[...]
    row = lax.broadcasted_iota(jnp.int32, (seq, w), 0)
    sub = row & (V7X_SUBLANES - 1)

    def tap(offset):
        if offset == 0:
            return xa
        shifted = pltpu.roll(xa, (-offset) % seq, axis=0)
        valid = row >= -offset if offset < 0 else row < seq - offset
        return jnp.where(valid, shifted, 0.0)

    xc = cb_ref[...]
    for j in range(CONV_W):
        xc = xc + tap(j - CONV_W // 2) * cw_ref[pl.ds(j, 1), :]
    xcb = xc.astype(BF16)

    for direction, (h0_ref, h_scr, last_ref) in enumerate(((h0f_ref, hf_scr, lf_ref),
                                                            (h0b_ref, hb_scr, lb_ref))):
        reverse = direction == 1
        r = jax.nn.sigmoid(jnp.dot(xcb, wa_ref[direction], preferred_element_type=F32) + ba_ref[direction])
        gi = jax.nn.sigmoid(jnp.dot(xcb, wx_ref[direction], preferred_element_type=F32) + bx_ref[direction])
        neg_lam = -lam_ref[direction]
        softplus = jnp.maximum(neg_lam, 0.0) + jnp.log1p(jnp.exp(-jnp.abs(neg_lam)))
        log_a = r * (-LRU_C * softplus)
        a = jnp.exp(log_a)
        m2 = (1.0 - a) * (1.0 + a)
        u = jnp.where(m2 > 0.0, m2 * lax.rsqrt(m2), 0.0) * (gi * xc)
        for step in (1, 2, 4):
            if reverse:
                a_nb = _group_roll(a, V7X_SUBLANES - step)
                u_nb = _group_roll(u, V7X_SUBLANES - step)
                valid = sub < V7X_SUBLANES - step
            else:
                a_nb = _group_roll(a, step)
                u_nb = _group_roll(u, step)
                valid = sub >= step
            u = a * jnp.where(valid, u_nb, 0.0) + u
            a = a * jnp.where(valid, a_nb, 1.0)
        a_scr[...] = a
        u_scr[...] = u

        def carry_step(g, carry, reverse=reverse, h_scr=h_scr):
            gg = groups - 1 - g if reverse else g
            off = pl.multiple_of(gg * V7X_SUBLANES, V7X_SUBLANES)
            h = a_scr[pl.ds(off, V7X_SUBLANES), :] * carry + u_scr[pl.ds(off, V7X_SUBLANES), :]
            h_scr[pl.ds(off, V7X_SUBLANES), :] = h
            return h[0:1] if reverse else h[V7X_SUBLANES - 1:V7X_SUBLANES]

        last_ref[...] = lax.fori_loop(0, groups, carry_step, h0_ref[...], unroll=4)

    y_ref[...] = ((hf_scr[...] + hb_scr[...]) * jax.nn.gelu(ga_ref[...])).astype(BF16)


def _lru(p_ab, h0f, h0b, prm, seq_len, n_seq, row_start):
    lru_w = h0f.shape[-1]
    cw = LRU_COLS
    ncb = lru_w // cw
    sb = row_start // seq_len
    vec = lambda: pl.BlockSpec((None, 1, cw), lambda s, c: (s, 0, c))
    par2 = lambda: pl.BlockSpec((None, 2, 1, cw), lambda s, c: (prm["idx"], 0, 0, c))
    gate = lambda: pl.BlockSpec((None, 2, None, cw, cw), lambda s, c: (prm["idx"], 0, c, 0, 0))
    return pl.pallas_call(
        _lru_kernel,
        grid=(n_seq, ncb),
        in_specs=[
            pl.BlockSpec((seq_len, cw), lambda s, c: (s + sb, c)),
            pl.BlockSpec((seq_len, cw), lambda s, c: (s + sb, c + ncb)),
            vec(), vec(),
            pl.BlockSpec((None, CONV_W, cw), lambda s, c: (prm["idx"], 0, c)),
            pl.BlockSpec((None, 1, cw), lambda s, c: (prm["idx"], 0, c)),
            gate(), par2(), gate(), par2(), par2(),
        ],
        out_specs=[
            pl.BlockSpec((seq_len, cw), lambda s, c: (s, c)),
            vec(), vec(),
        ],
        out_shape=[
            jax.ShapeDtypeStruct((n_seq * seq_len, lru_w), BF16),
            jax.ShapeDtypeStruct((n_seq, 1, lru_w), F32),
            jax.ShapeDtypeStruct((n_seq, 1, lru_w), F32),
        ],
        scratch_shapes=[pltpu.VMEM((seq_len, cw), F32)] * 4,
        compiler_params=_params(("parallel", "parallel"), 40),
        name="lru",
    )(p_ab, p_ab, h0f.reshape(n_seq, 1, lru_w), h0b.reshape(n_seq, 1, lru_w), prm["conv_w"],
      prm["conv_b"], prm["wa"], prm["ba"], prm["wx"], prm["bx"], prm["lam"])


def _block_diag_tiles(w, tile):
    n, two, nb, bw, _ = w.shape
    per = tile // bw
    w = w.reshape(n, two, nb // per, per, bw, bw)
    eye = jnp.eye(per, dtype=w.dtype)
    out = jnp.einsum("ndgpij,pq->ndgpiqj", w, eye)
    return out.reshape(n, two, nb // per, tile, tile)


def _softmax_attend(q, ks, vs, biases, scale):
    scores = []
    for k, b in zip(ks, biases):
        s = lax.dot_general(q, k, (((1,), (1,)), ((), ())), preferred_element_type=F32)
        if scale is not None:
            s = s * scale
        scores.append(s if b is None else s + b)
    m = scores[0].max(axis=-1, keepdims=True)
    for s in scores[1:]:
        m = jnp.maximum(m, s.max(axis=-1, keepdims=True))
    denom = None
    out = None
    for s, v in zip(scores, vs):
        e = jnp.exp(s - m)
        part = e.sum(axis=-1, keepdims=True)
        denom = part if denom is None else denom + part
        o = jnp.dot(e.astype(BF16), v, preferred_element_type=F32)
        out = o if out is None else out + o
    return out / denom


def _upper_half(shape):
    lane = lax.broadcasted_iota(jnp.int32, shape, len(shape) - 1)
    return lane % V7X_LANES >= V7X_LANES // 2


def _split_pair(q):
    up = _upper_half(q.shape)
    zero = jnp.zeros_like(q)
    return jnp.concatenate([jnp.where(up, zero, q), jnp.where(up, q, zero)], axis=0)


def _merge_pair(o):
    m = o.shape[0] // 2
    return jnp.where(_upper_half((m, o.shape[1])), o[m:], o[:m])


def _dup_head(block, use_high):
    swapped = pltpu.roll(block, V7X_LANES // 2, axis=1)
    keep = _upper_half(block.shape) == use_high
    return jnp.where(keep, block, swapped)


def _fold_scale(q, scale):
    return (q * scale, None) if _is_pow2(scale) else (q, scale)


def _na_ctx_kernel(q_ref, k_ref, v_ref, o_ref, *, scale):
    for pair in range(q_ref.shape[1] // V7X_LANES):
        sl = pl.ds(pair * V7X_LANES, V7X_LANES)
        q, sc = _fold_scale(q_ref[:, sl], scale)
        o = _softmax_attend(_split_pair(q.astype(BF16)), [k_ref[:, sl].astype(BF16)],
                            [v_ref[:, sl].astype(BF16)], [None], sc)
        o_ref[:, sl] = _merge_pair(o).astype(o_ref.dtype)


def _na_ctx(p_ab, n_seq, seq_len, q_blk, na_w, scale):
    blk = lambda col: pl.BlockSpec((seq_len, na_w), lambda b: (b, col))
    return pl.pallas_call(
        functools.partial(_na_ctx_kernel, scale=scale),
        grid=(n_seq,),
        in_specs=[blk(q_blk), blk(q_blk + 1), blk(q_blk + 2)],
        out_specs=pl.BlockSpec((seq_len, na_w), lambda b: (b, 0)),
        out_shape=jax.ShapeDtypeStruct((n_seq * seq_len, na_w), BF16),
        compiler_params=_params(("parallel",), 40),
        name="na_ctx",
    )(p_ab, p_ab, p_ab)


def _na_tile_index(seq_len):
    rows_n = seq_len // GRID_W
    kr = min(NA_KR, rows_n)
    rows_per_chunk = NA_Q_CHUNK // GRID_W
    n_chunks = rows_n // rows_per_chunk
    win_rows = NA_WIN // GRID_W
    table = []
    for r in range(rows_n):
        r_start = min(max(r - kr // 2, 0), rows_n - kr)
        base = 0 if r // rows_per_chunk < n_chunks // 2 else rows_n - win_rows
        table.append([k - r + NA_KR - 1 if r_start <= k < r_start + kr else 2 * NA_KR - 1
                      for k in range(base, base + win_rows)])
    return table


def _na_lat_kernel(q_ref, k_ref, v_ref, kc_ref, vc_ref, tiles_ref, o_ref, bias_ref, *, scale):
    lq = q_ref.shape[0]
    n_chunks = lq // NA_Q_CHUNK
    rows_per_chunk = NA_Q_CHUNK // GRID_W

    @pl.when(pl.program_id(1) == 0)
    def _():
        low = lax.broadcasted_iota(jnp.int32, (GRID_W, V7X_LANES), 1) < GRID_W
        for half in (0, 1):
            for r, slots in enumerate(_na_tile_index(lq)):
                c, rr = divmod(r, rows_per_chunk)
                rows = pl.ds(half * NA_Q_CHUNK + rr * GRID_W, GRID_W)
                for s in range(0, len(slots), 2):
                    blk = jnp.where(low, tiles_ref[half, slots[s]], tiles_ref[half, slots[s + 1]])
                    bias_ref[c, rows, pl.ds(s * GRID_W, V7X_LANES)] = blk

    kcb = kc_ref[...].astype(BF16)
    vcb = vc_ref[...].astype(BF16)
    for c in range(n_chunks):
        win = 0 if c < n_chunks // 2 else lq - NA_WIN
        kb = k_ref[pl.ds(win, NA_WIN), :].astype(BF16)
        vb = v_ref[pl.ds(win, NA_WIN), :].astype(BF16)
        rows = pl.ds(c * NA_Q_CHUNK, NA_Q_CHUNK)
        q, sc = _fold_scale(q_ref[rows, :], scale)
        o = _softmax_attend(_split_pair(q.astype(BF16)), [kb, kcb], [vb, vcb], [bias_ref[c], None], sc)
        o_ref[rows, :] = _merge_pair(o).astype(o_ref.dtype)


def _na_lat(p_ab, cache_k, cache_v, tiles, idx, n_seq, seq_len, row_start, q_col, n_pairs, scale):
    sb = row_start // seq_len
    past = cache_k.shape[2]
    blk = lambda col: pl.BlockSpec((seq_len, V7X_LANES), lambda h, b: (b + sb, col + h))
    cblk = lambda: pl.BlockSpec((None, None, past, V7X_LANES), lambda h, b: (b, idx, 0, h))
    n_chunks = seq_len // NA_Q_CHUNK
    return pl.pallas_call(
        functools.partial(_na_lat_kernel, scale=scale),
        grid=(n_pairs, n_seq),
        in_specs=[blk(q_col), blk(q_col + n_pairs), blk(q_col + 2 * n_pairs), cblk(), cblk(),
                  pl.BlockSpec((None, 2, 2 * NA_KR, GRID_W, V7X_LANES), lambda h, b: (idx, h, 0, 0, 0))],
        out_specs=pl.BlockSpec((seq_len, V7X_LANES), lambda h, b: (b, h)),
        out_shape=jax.ShapeDtypeStruct((n_seq * seq_len, n_pairs * V7X_LANES), BF16),
        scratch_shapes=[pltpu.VMEM((n_chunks, 2 * NA_Q_CHUNK, NA_WIN), F32)],
        compiler_params=_params(("parallel", "arbitrary"), 56),
        name="na_lat",
    )(p_ab, p_ab, p_ab, cache_k, cache_v, tiles)


def _na_bias_tiles(tab):
    lead = tab.shape[:-1]
    edge = GRID_W - NA_KC
    vec = jnp.pad(tab, [(0, 0)] * len(lead) + [(edge, edge + 1)], mode="edge")
    skew = jnp.broadcast_to(vec[..., None, :], lead + (GRID_W, 2 * GRID_W))
    skew = skew.reshape(lead + (2 * GRID_W * GRID_W,))[..., :GRID_W * (2 * GRID_W - 1)]
    toep = skew.reshape(lead + (GRID_W, 2 * GRID_W - 1))[..., GRID_W - 1:]
    cols = np.arange(GRID_W)
    c_start = np.clip(cols - NA_KC // 2, 0, GRID_W - NA_KC)
    col_ok = (cols[None, :] >= c_start[:, None]) & (cols[None, :] < c_start[:, None] + NA_KC)
    tiles = jnp.where(col_ok, toep, NEG_BIAS)
    masked = jnp.full(lead[:-1] + (1, GRID_W, GRID_W), NEG_BIAS, F32)
    tiles = jnp.concatenate([tiles, masked], axis=-3)
    return jnp.concatenate([tiles, tiles], axis=-1)


def _split_dot(v, m):
    hi = v.astype(BF16)
    lo = (v - hi.astype(F32)).astype(BF16)
    return jnp.dot(hi, m, preferred_element_type=F32) + jnp.dot(lo, m, preferred_element_type=F32)


def _head_rms(x, gather, spread, gain, head_dim):
    inv = lax.rsqrt(_split_dot(x * x, gather) * (1.0 / head_dim) + EPS)
    return x * _split_dot(inv, spread) * gain


def _rope(x, cos, sin_lo, sin_hi):
    w = x.shape[1]
    reps = w // V7X_LANES
    tile = lambda t: t if reps == 1 else jnp.concatenate([t] * reps, axis=1)
    quarter = 16
    return (x * tile(cos) + pltpu.roll(x, w - quarter, axis=1) * tile(sin_lo)
            + pltpu.roll(x, quarter, axis=1) * tile(sin_hi))


def _cd_post_kernel(p_ref, gat_ref, spr_ref, qg_ref, kg_ref, mqg_ref, mkg_ref, wuq_ref, wukv_ref, *rest,
                    rope, head_dim, q_w, kv_w, rank, q_scale):
    if rope:
        cos_ref, slo_ref, shi_ref = rest[:3]
        rest = rest[3:]
        tabs = (cos_ref[...], slo_ref[...], shi_ref[...])
    q_ref, k_ref, qn_ref, qr_ref, ckv_ref, kn_ref, vm_ref, kr_ref = rest
    c0 = 0
    qc = p_ref[:, pl.ds(c0, q_w)]
    c0 += q_w
    kc = p_ref[:, pl.ds(c0, kv_w)]
    c0 += 2 * kv_w
    qa = p_ref[:, pl.ds(c0, rank)]
    c0 += rank
    ckv = p_ref[:, pl.ds(c0, rank)]
    c0 += rank
    kr = p_ref[:, pl.ds(c0, V7X_LANES)]

    qc = _head_rms(qc, gat_ref[...], spr_ref[...], qg_ref[...], head_dim)
    kc = _head_rms(kc, gat_ref[pl.ds(0, kv_w), :], spr_ref[:, pl.ds(0, kv_w)], kg_ref[...], head_dim)
    qd = jnp.dot(_rms(qa, mqg_ref[...]).astype(BF16), wuq_ref[...], preferred_element_type=F32)
    n_nope = qn_ref.shape[1]
    qn = qd[:, :n_nope]
    qr = qd[:, n_nope:]
    ckv = _rms(ckv, mkg_ref[...])
    if rope:
        qc = _rope(qc, *tabs)
        kc = _rope(kc, *tabs)
        qr = _rope(qr, *tabs)
        kr = _rope(kr, *tabs)
    q_ref[...] = (qc * q_scale).astype(BF16)
    k_ref[...] = kc
    qn_ref[...] = qn.astype(BF16)
    qr_ref[...] = qr.astype(BF16)
    ckv_ref[...] = ckv
    kv = jnp.dot(ckv.astype(BF16), wukv_ref[...], preferred_element_type=F32)
    kn_ref[...] = kv[:, :n_nope].astype(BF16)
    vm_ref[...] = kv[:, n_nope:].astype(BF16)
    kr_ref[...] = kr


def _cd_post(p_cd, prm, row_start, rows, rope_tabs, dims, q_scale):
    q_w, kv_w, rank, head_dim, n_nope, n_rope, n_v = dims
    tm = 256
    off = row_start // tm
    n_in = p_cd.shape[1]
    idx = prm["idx"]
    rope = rope_tabs is not None
    const = lambda shape: pl.BlockSpec(shape, lambda i: (0,) * len(shape))
    layer = lambda shape: pl.BlockSpec((None,) + shape, lambda i: (idx,) + (0,) * len(shape))
    in_specs = [
        pl.BlockSpec((tm, n_in), lambda i: (i + off, 0)),
        const((q_w, V7X_LANES)), const((V7X_LANES, q_w)),
        layer((1, q_w)), layer((1, kv_w)), layer((1, rank)), layer((1, rank)),
        layer((rank, n_nope + n_rope)), layer((rank, n_nope + n_v)),
    ]
    args = [p_cd, prm["gather"], prm["spread"], prm["q_gain"], prm["k_gain"], prm["mla_q_gain"],
            prm["mla_kv_gain"], prm["w_uq"], prm["w_ukv"]]
    if rope:
        seq_tiles = rope_tabs[0].shape[0] // tm
        in_specs += [pl.BlockSpec((tm, V7X_LANES), lambda i: (i % seq_tiles, 0))] * 3
        args += list(rope_tabs)
    widths = [(q_w, BF16), (kv_w, F32), (n_nope, BF16), (n_rope, BF16), (rank, F32), (n_nope, BF16),
              (n_v, BF16), (V7X_LANES, F32)]
    kern = functools.partial(_cd_post_kernel, rope=rope, head_dim=head_dim, q_w=q_w, kv_w=kv_w, rank=rank,
                             q_scale=q_scale)
    return pl.pallas_call(
        kern,
        grid=(rows // tm,),
        in_specs=in_specs,
        out_specs=[pl.BlockSpec((tm, w), lambda i: (i, 0)) for w, _ in widths],
        out_shape=[jax.ShapeDtypeStruct((rows, w), dt) for w, dt in widths],
        compiler_params=_params(("parallel",), 48),
        name="cd_post",
    )(*args)


def _rope_tables(seq_len, head_dim):
    half = head_dim // 2
    nf = half // 2
    t = np.arange(seq_len)
    inv_freq = (1.0 / (ROPE_THETA ** (np.arange(nf, dtype=np.float32) / nf))).astype(np.float32)
    zeros = np.zeros((seq_len, nf), np.float32)
    cos, slo, shi = [], [], []
    for pos in (t // GRID_W, t % GRID_W):
        ang = pos.astype(np.float32)[:, None] * inv_freq[None, :]
        c, s = np.cos(ang).astype(np.float32), np.sin(ang).astype(np.float32)
        cos += [c, c]
        slo += [-s, zeros]
        shi += [zeros, s]
    reps = V7X_LANES // head_dim
    return tuple(jnp.asarray(np.tile(np.concatenate(x, axis=1), (1, reps))) for x in (cos, slo, shi))


def _ckv_up_kernel(c_ref, w_ref, kn_ref, vm_ref):
    kv = jnp.dot(c_ref[...].astype(BF16), w_ref[...], preferred_element_type=F32)
    n = kn_ref.shape[1]
    kn_ref[...] = kv[:, :n].astype(BF16)
    vm_ref[...] = kv[:, n:].astype(BF16)


def _ckv_up(ckv, w_ukv, idx, n_nope):
    n_seq, _, tm, rank = ckv.shape
    rows = n_seq * tm
    n = w_ukv.shape[2]
    return pl.pallas_call(
        _ckv_up_kernel,
        grid=(rows // tm,),
        in_specs=[pl.BlockSpec((None, None, tm, rank), lambda i: (i, idx, 0, 0)),
                  pl.BlockSpec((None, rank, n), lambda i: (idx, 0, 0))],
        out_specs=[pl.BlockSpec((tm, n_nope), lambda i: (i, 0)), pl.BlockSpec((tm, n - n_nope), lambda i: (i, 0))],
        out_shape=[jax.ShapeDtypeStruct((rows, n_nope), BF16), jax.ShapeDtypeStruct((rows, n - n_nope), BF16)],
        compiler_params=_params(("parallel",), 32),
        name="ckv_up",
    )(ckv, w_ukv)


def _gqa_kernel(q_ref, k_ref, v_ref, *rest, scale, has_cache, group_w):
    if has_cache:
        kc_ref, vc_ref, o_ref = rest
    else:
        (o_ref,) = rest
    lq = q_ref.shape[0]
    n_local = q_ref.shape[1] // group_w
    n_pairs = group_w // V7X_LANES
    for hl in range(n_local):
        if n_local == 1:
            high = (pl.program_id(1) % 2) == 1
            pick = lambda ref: ref[...]
        else:
            high = hl % 2 == 1
            pick = lambda ref, hl=hl: ref[:, pl.ds((hl // 2) * V7X_LANES, V7X_LANES)]
        ks = [_dup_head(pick(k_ref), high).astype(BF16)]
        vs = [_dup_head(pick(v_ref), high).astype(BF16)]
        if has_cache:
            ks.append(_dup_head(pick(kc_ref), high).astype(BF16))
            vs.append(_dup_head(pick(vc_ref), high).astype(BF16))
        for c in range(lq // Q_CHUNK):
            rows = pl.ds(c * Q_CHUNK, Q_CHUNK)
            cols = [pl.ds(hl * group_w + p * V7X_LANES, V7X_LANES) for p in range(n_pairs)]
            q = jnp.concatenate([_split_pair(q_ref[rows, cl]) for cl in cols], axis=0)
            o = _softmax_attend(q, ks, vs, [None] * len(ks), scale)
            for p, cl in enumerate(cols):
                o_ref[rows, cl] = _merge_pair(o[2 * p * Q_CHUNK:2 * (p + 1) * Q_CHUNK]).astype(o_ref.dtype)


def _gqa(q, k, p_cd, v_col, cache_k, cache_v, idx, n_seq, seq_len, row_start, n_kv, heads_per_step, scale):
    sb = row_start // seq_len
    group_w = q.shape[1] // n_kv
    has_cache = cache_k is not None
    if heads_per_step == 1:
        kv_w, kv_blk, v_blk = V7X_LANES, (lambda h: h // 2), (lambda h: v_col + h // 2)
    else:
        kv_w = k.shape[1]
        kv_blk, v_blk = (lambda h: 0), (lambda h: v_col * V7X_LANES // kv_w)
    in_specs = [
        pl.BlockSpec((seq_len, group_w * heads_per_step), lambda b, h: (b, h)),
        pl.BlockSpec((seq_len, kv_w), lambda b, h: (b, kv_blk(h))),
        pl.BlockSpec((seq_len, kv_w), lambda b, h: (b + sb, v_blk(h))),
    ]
    args = [q, k, p_cd]
    if has_cache:
        past = cache_k.shape[2]
        in_specs += [pl.BlockSpec((None, None, past, kv_w), lambda b, h: (b, idx, 0, kv_blk(h)))] * 2
        args += [cache_k, cache_v]
    return pl.pallas_call(
        functools.partial(_gqa_kernel, scale=scale, has_cache=has_cache, group_w=group_w),
        grid=(n_seq, n_kv // heads_per_step),
        in_specs=in_specs,
        out_specs=pl.BlockSpec((seq_len, group_w * heads_per_step), lambda b, h: (b, h)),
        out_shape=jax.ShapeDtypeStruct(q.shape, BF16),
        compiler_params=_params(("parallel", "parallel"), 56),
        name="gqa",
    )(*args)


def _mla_kernel(qn_ref, qr_ref, kn_ref, kr_ref, v_ref, *rest, scale, has_cache):
    if has_cache:
        knc_ref, krc_ref, vc_ref, o_ref = rest
        krc = krc_ref[...].astype(BF16)
    else:
        (o_ref,) = rest
    lq = qn_ref.shape[0]
    qc = min(2 * Q_CHUNK, lq)
    n_local = qn_ref.shape[1] // V7X_LANES
    kr = kr_ref[...].astype(BF16)
    for hl in range(n_local):
        sl = pl.ds(hl * V7X_LANES, V7X_LANES)
        if n_local == 1:
            high = (pl.program_id(1) % 2) == 1
            pair = pl.ds(0, V7X_LANES)
        else:
            high = hl % 2 == 1
            pair = pl.ds((hl // 2) * V7X_LANES, V7X_LANES)
        ks = [jnp.concatenate([kn_ref[:, sl], kr], axis=1)]
        vs = [v_ref[:, sl]]
        if has_cache:
            ks.append(jnp.concatenate([knc_ref[:, sl], krc], axis=1))
            vs.append(vc_ref[:, sl])
        for c in range(lq // qc):
            rows = pl.ds(c * qc, qc)
            qr = qr_ref[rows, pair]
            qr = jnp.where(_upper_half(qr.shape) == high, qr, jnp.zeros_like(qr))
            q = jnp.concatenate([qn_ref[rows, sl], qr], axis=1)
            o_ref[rows, sl] = _softmax_attend(q, ks, vs, [None] * len(ks), scale).astype(o_ref.dtype)


def _mla(qn, qr, kn, kr2, vm, cache, n_seq, seq_len, heads_per_step, scale):
    has_cache = cache is not None
    n_heads = vm.shape[1] // V7X_LANES
    hw = heads_per_step * V7X_LANES
    head = lambda rows: pl.BlockSpec((rows, hw), lambda b, h: (b, h))
    if heads_per_step == 1:
        rope_q = pl.BlockSpec((seq_len, V7X_LANES), lambda b, h: (b, h // 2))
    else:
        rope_q = pl.BlockSpec((seq_len, qr.shape[1]), lambda b, h: (b, 0))
    in_specs = [head(seq_len), rope_q, head(seq_len),
                pl.BlockSpec((seq_len, V7X_LANES), lambda b, h: (b, 0)),
                head(seq_len)]
    args = [qn, qr, kn, kr2, vm]
    if has_cache:
        knc, krc, vmc, idx = cache
        past = krc.shape[2]
        in_specs += [head(past),
                     pl.BlockSpec((None, None, past, V7X_LANES), lambda b, h: (b, idx, 0, 0)),
                     head(past)]
        args += [knc, krc, vmc]
    return pl.pallas_call(
        functools.partial(_mla_kernel, scale=scale, has_cache=has_cache),
        grid=(n_seq, n_heads // heads_per_step),
        in_specs=in_specs,
        out_specs=head(seq_len),
        out_shape=jax.ShapeDtypeStruct(vm.shape, BF16),
        compiler_params=_params(("parallel", "parallel"), 48),
        name="mla",
    )(*args)


def kernel(x_prompt, x_sample, state_lru_fwd, state_lru_bwd, cache_na_k, cache_na_v, cache_gqa_k, cache_gqa_v, cache_mla_ckv, cache_mla_krope, c, c_ctx, w_mod, b_mod, norm_gain, w_ffn_in, w_ffn_out, w_in_ab, conv_w, conv_b, lru_wa, lru_ba, lru_wx, lru_bx, lru_lambda, na_bias, w_out_ab, w_in_cd, gqa_q_gain, gqa_k_gain, mla_q_gain, mla_kv_gain, mla_w_uq, mla_w_uk, mla_w_uv, w_out_cd, final_gain):
    batch, seq, d = x_prompt.shape
    dec_batch, dec_seq, _ = x_sample.shape
    depth = w_mod.shape[0]
    n_even, n_odd = w_in_ab.shape[0], w_in_cd.shape[0]
    n_p = batch * seq
    n_s = dec_batch * dec_seq
    past = cache_na_k.shape[2]
    lru_w = state_lru_fwd.shape[-1]
    na_heads, na_dh = cache_na_k.shape[3], cache_na_k.shape[4]
    na_w = na_heads * na_dh
    gqa_kv, gqa_dh = cache_gqa_k.shape[3], cache_gqa_k.shape[4]
    kv_w = gqa_kv * gqa_dh
    rank = cache_mla_ckv.shape[-1]
    rope_w = cache_mla_krope.shape[-1]
    q_w = w_in_cd.shape[2] - 2 * kv_w - 2 * rank - rope_w
    mla_heads = mla_w_uk.shape[2] // MLA_NOPE
    n_nope = mla_heads * MLA_NOPE
    n_rope = mla_heads * rope_w
    n_v = mla_heads * MLA_V
    mla_qk = MLA_NOPE + rope_w
    assert mla_q_gain.shape[-1] == rank and 2 * rope_w == V7X_LANES
    assert n_p % dec_seq == 0 and dec_seq % TOKEN_TILE == 0 and seq % Q_CHUNK == 0
    assert dec_seq // GRID_W == 2 * NA_KR and na_dh == GRID_W and gqa_dh == GRID_W
    assert w_in_ab.shape[2] == 2 * lru_w + 3 * na_w and lru_w == na_w
    assert (q_w // gqa_kv) % V7X_LANES == 0 and q_w % kv_w == 0

    x = (x_prompt.reshape(n_p, d), x_sample.reshape(n_s, d))
    cond = jnp.concatenate([c_ctx[None, :], c, jnp.zeros((MOD_ROWS - 1 - dec_batch, d), F32)], axis=0)
    mods = _modulation(cond, w_mod, b_mod)
    gains = norm_gain.reshape(depth * 3, 1, d)

    ffn_w = (w_ffn_in[0, 0].astype(BF16), w_ffn_out[0, 0].astype(BF16))

    def ffn_step(x, ffn_w, layer, which):
        last = layer == depth - 1 and which == 1
        nxt = None if last else (w_ffn_in, w_ffn_out, layer + which, 1 - which)
        out = _ffn(x, mods, gains, ffn_w[0], ffn_w[1], layer, which, nxt, n_p, dec_seq)
        return out[0], tuple(out[1:])
    w_in_ab_bf = w_in_ab.astype(BF16)
    w_out_ab_bf = w_out_ab.astype(BF16)
    w_in_cd_bf = jnp.concatenate([w_in_cd, w_in_cd[:, :, -rope_w:]], axis=2).astype(BF16)
    w_out_cd_bf = w_out_cd.astype(BF16)

    lru_prm = {
        "conv_w": conv_w, "conv_b": conv_b.reshape(n_even, 1, lru_w),
        "wa": _block_diag_tiles(lru_wa, LRU_COLS).astype(BF16),
        "wx": _block_diag_tiles(lru_wx, LRU_COLS).astype(BF16),
        "ba": lru_ba.reshape(n_even, 2, 1, lru_w), "bx": lru_bx.reshape(n_even, 2, 1, lru_w),
        "lam": lru_lambda.reshape(n_even, 2, 1, lru_w),
    }
    zeros_state = jnp.zeros((batch, lru_w), F32)
    na_k_ctx = cache_na_k.reshape(dec_batch, n_even, past, na_w)
    na_v_ctx = cache_na_v.reshape(dec_batch, n_even, past, na_w)
    na_tiles = _na_bias_tiles(na_bias)

    w_uq = mla_w_uq.reshape(n_odd, rank, mla_heads, mla_qk)
    w_uq = jnp.concatenate([w_uq[..., :MLA_NOPE].reshape(n_odd, rank, n_nope),
                            w_uq[..., MLA_NOPE:].reshape(n_odd, rank, n_rope)], axis=2).astype(BF16)
    gqa_scale = gqa_dh ** -0.5
    q_scale = gqa_scale if _is_pow2(gqa_scale) else 1.0
    head_of_lane = np.arange(q_w) // gqa_dh
    gather = (head_of_lane[:, None] == np.arange(V7X_LANES)[None, :]).astype(np.float32)
    assert q_w // gqa_dh <= V7X_LANES
    cd_prm = {
        "gather": jnp.asarray(gather, BF16), "spread": jnp.asarray(gather.T, BF16),
        "q_gain": jnp.tile(gqa_q_gain, (1, q_w // gqa_dh)).reshape(n_odd, 1, q_w),
        "k_gain": jnp.tile(gqa_k_gain, (1, gqa_kv)).reshape(n_odd, 1, kv_w),
        "mla_q_gain": mla_q_gain.reshape(n_odd, 1, rank),
        "mla_kv_gain": mla_kv_gain.reshape(n_odd, 1, rank),
        "w_uq": w_uq,
        "w_ukv": jnp.concatenate([mla_w_uk, mla_w_uv], axis=2).astype(BF16),
    }
    rope_tabs = _rope_tables(dec_seq, gqa_dh)
    gqa_k_ctx = cache_gqa_k.reshape(dec_batch, n_odd, past, kv_w)
    gqa_v_ctx = cache_gqa_v.reshape(dec_batch, n_odd, past, kv_w)
    krope_ctx = jnp.concatenate([cache_mla_krope, cache_mla_krope], axis=-1)

    st_f, st_b, na_k, na_v, gq_k, gq_v, ml_c, ml_r = [], [], [], [], [], [], [], []
    for layer in range(depth):
        jdx = layer // 2
        x, ffn_w = ffn_step(x, ffn_w, layer, 0)
        if layer % 2 == 0:
            p_ab = _inproj(x, mods, gains, w_in_ab_bf, layer, jdx, TOKEN_TILE, w_in_ab_bf.shape[2] // 2,
                           n_p, dec_seq)
            prm = dict(lru_prm, idx=jdx)
            ya_p, lf, lb = _lru(p_ab, zeros_state, zeros_state, prm, seq, batch, 0)
            ya_s, _, _ = _lru(p_ab, state_lru_fwd[:, jdx], state_lru_bwd[:, jdx], prm, dec_seq, dec_batch, n_p)
            scale = na_dh ** -0.5
            yb_p = _na_ctx(p_ab, batch, seq, 2 * lru_w // na_w, na_w, scale)
            yb_s = _na_lat(p_ab, na_k_ctx, na_v_ctx, na_tiles, jdx, dec_batch, dec_seq, n_p,
                           2 * lru_w // V7X_LANES, na_w // V7X_LANES, scale)
            x = _outproj(x, mods, layer, ya_p, ya_s, yb_p, yb_s, w_out_ab_bf, jdx, n_p, dec_seq)
            st_f.append(lf.reshape(batch, lru_w))
            st_b.append(lb.reshape(batch, lru_w))
            na_k.append(p_ab[:n_p, 2 * lru_w + na_w:2 * lru_w + 2 * na_w].reshape(batch, seq, na_heads, na_dh))
            na_v.append(p_ab[:n_p, 2 * lru_w + 2 * na_w:].reshape(batch, seq, na_heads, na_dh))
        else:
            p_cd = _inproj(x, mods, gains, w_in_cd_bf, layer, jdx, TOKEN_TILE, w_in_cd_bf.shape[2], n_p, dec_seq)
            prm = dict(cd_prm, idx=jdx)
            dims = (q_w, kv_w, rank, gqa_dh, n_nope, n_rope, n_v)
            qp, kp, qnp_, qrp, ckvp, knp_, vmp, krp = _cd_post(p_cd, prm, 0, n_p, None, dims, q_scale)
            qs, ks_, qns, qrs, _, kns, vms, krs = _cd_post(p_cd, prm, n_p, n_s, rope_tabs, dims, q_scale)
            v_col = (q_w + kv_w) // V7X_LANES
            att_scale = None if q_scale != 1.0 else gqa_scale
            yc_p = _gqa(qp, kp, p_cd, v_col, None, None, jdx, batch, seq, 0, gqa_kv, gqa_kv, att_scale)
            yc_s = _gqa(qs, ks_, p_cd, v_col, gqa_k_ctx, gqa_v_ctx, jdx, dec_batch, dec_seq, n_p,
                        gqa_kv, 1, att_scale)
            knc, vmc = _ckv_up(cache_mla_ckv, cd_prm["w_ukv"], jdx, n_nope)
            yd_p = _mla(qnp_, qrp, knp_, krp, vmp, None, batch, seq, mla_heads, mla_qk ** -0.5)
            yd_s = _mla(qns, qrs, kns, krs, vms, (knc, krope_ctx, vmc, jdx), dec_batch, dec_seq, 1,
                        mla_qk ** -0.5)
            x = _outproj(x, mods, layer, yc_p, yc_s, yd_p, yd_s, w_out_cd_bf, jdx, n_p, dec_seq)
            gq_k.append(kp.reshape(batch, seq, gqa_kv, gqa_dh))
            gq_v.append(p_cd[:n_p, q_w + kv_w:q_w + 2 * kv_w].reshape(batch, seq, gqa_kv, gqa_dh))
            ml_c.append(ckvp.reshape(batch, seq, rank))
            ml_r.append(krp[:, :rope_w].reshape(batch, seq, rope_w))
        x, ffn_w = ffn_step(x, ffn_w, layer, 1)

    y_prompt = _final_norm(x, final_gain, 0, n_p).reshape(batch, seq, d)
    y_sample = _final_norm(x, final_gain, n_p, n_s).reshape(dec_batch, dec_seq, d)
    stack = lambda xs: jnp.stack(xs, axis=1)
    return (y_prompt, y_sample, stack(st_f), stack(st_b), stack(na_k), stack(na_v), stack(gq_k), stack(gq_v),
            stack(ml_c), stack(ml_r))
```

```python
import functools
import math

import numpy as np

import jax
import jax.numpy as jnp
from jax import lax
from jax.experimental import pallas as pl
from jax.experimental.pallas import tpu as pltpu

F32 = jnp.float32
BF16 = jnp.bfloat16

EPS = 1e-6
N_MOD = 9
GRID_W = 64
NA_KR = 8
NA_KC = 16
ROPE_THETA = 10000.0
LRU_C = 8.0
CONV_W = 4
MLA_NOPE = 128
MLA_V = 128
NEG_BIAS = -1e30

V7X_LANES = 128
V7X_SUBLANES = 8
MOD_ROWS = 16
TOKEN_TILE = 512
FF_TILE = 512
Q_CHUNK = 128
NA_Q_CHUNK = 256
NA_WIN = 768
LRU_COLS = 256


def _params(sem, vmem_mib):
    return pltpu.CompilerParams(dimension_semantics=sem, vmem_limit_bytes=vmem_mib * 1024 * 1024)


def _silu(x):
    return x * jax.nn.sigmoid(x)


def _rms(x, gain):
    ms = jnp.mean(x * x, axis=-1, keepdims=True)
    return x * lax.rsqrt(ms + EPS) * gain


def _mod_row(i, n_prompt_tiles, tiles_per_seq):
    return jnp.where(i < n_prompt_tiles, 0, 1 + (i - n_prompt_tiles) // tiles_per_seq)


def _adaln(x, mod_ref, gain_ref, row, mod_base):
    d = x.shape[1]
    shift = mod_ref[pl.ds(row, 1), pl.ds(mod_base * d, d)]
    scale = mod_ref[pl.ds(row, 1), pl.ds((mod_base + 1) * d, d)]
    return (_rms(x, gain_ref[...]) * (1 + scale) + shift).astype(BF16)


def _is_pow2(v):
    return math.frexp(v)[0] == 0.5


def _mod_kernel(cond_ref, w_ref, b_ref, o_ref):
    s = _silu(cond_ref[...]).astype(BF16)
    o_ref[...] = jnp.dot(s, w_ref[...].astype(BF16), preferred_element_type=F32) + b_ref[...]


def _modulation(cond, w_mod, b_mod):
    depth, d, n = w_mod.shape
    tn = 1024
    return pl.pallas_call(
        _mod_kernel,
        grid=(depth, n // tn),
        in_specs=[
            pl.BlockSpec((MOD_ROWS, d), lambda l, j: (0, 0)),
            pl.BlockSpec((None, d, tn), lambda l, j: (l, 0, j)),
            pl.BlockSpec((None, 1, tn), lambda l, j: (l, 0, j)),
        ],
        out_specs=pl.BlockSpec((None, MOD_ROWS, tn), lambda l, j: (l, 0, j)),
        out_shape=jax.ShapeDtypeStruct((depth, MOD_ROWS, n), F32),
        compiler_params=_params(("parallel", "parallel"), 40),
        name="modulation",
    )(cond, w_mod, b_mod.reshape(depth, 1, n))


def _ffn_kernel(*refs, mod_base, n_prompt_tiles, tiles_per_seq, cast_next, split_x):
    i = pl.program_id(0)
    j = pl.program_id(1)
    if split_x:
        xp_ref, xs_ref, *refs = refs
        read_x = lambda: jnp.where(i < n_prompt_tiles, xp_ref[...], xs_ref[...])
    else:
        x_ref, *refs = refs
        read_x = lambda: x_ref[...]
    mod_ref, gain_ref, wg_ref, wu_ref, wo_ref, *rest = refs
    if cast_next:
        nin_ref, nout_ref, o_ref, cin_ref, cout_ref, h_ref = rest
    else:
        o_ref, h_ref = rest
    d = o_ref.shape[1]
    nj = pl.num_programs(1)
    row = _mod_row(i, n_prompt_tiles, tiles_per_seq)

    def step(first, last):
        if first:
            h = _adaln(read_x(), mod_ref, gain_ref, row, mod_base)
            h_ref[...] = h
        else:
            h = h_ref[...]
        g = jnp.dot(h, wg_ref[...], preferred_element_type=F32)
        u = jnp.dot(h, wu_ref[...], preferred_element_type=F32)
        if cast_next:
            cin_ref[...] = nin_ref[...].astype(BF16)
            cout_ref[...] = nout_ref[...].astype(BF16)
        act = (_silu(g) * u).astype(BF16)
        acc = jnp.dot(act, wo_ref[...], preferred_element_type=F32)
        if not first:
            acc = o_ref[...] + acc
        o_ref[...] = acc

    pl.when(j == 0)(lambda: step(True, False))
    pl.when(j > 0)(lambda: step(False, False))

    @pl.when(j == nj - 1)
    def _():
        gate = mod_ref[pl.ds(row, 1), pl.ds((mod_base + 2) * d, d)]
        o_ref[...] = read_x() + (0.5 * gate) * o_ref[...]


def _ffn(x, mods, gains, w_in, w_out, layer, which, nxt, n_prompt_rows, seq_rows):
    split_x = isinstance(x, tuple)
    npt = n_prompt_rows // TOKEN_TILE
    if split_x:
        t, d = x[0].shape[0] + x[1].shape[0], x[0].shape[1]
        x_specs = [pl.BlockSpec((TOKEN_TILE, d), lambda i, j: (jnp.minimum(i, npt - 1), 0)),
                   pl.BlockSpec((TOKEN_TILE, d), lambda i, j: (jnp.maximum(i - npt, 0), 0))]
        x_args = list(x)
    else:
        t, d = x.shape
        x_specs = [pl.BlockSpec((TOKEN_TILE, d), lambda i, j: (i, 0))]
        x_args = [x]
    dff = w_out.shape[0]
    nj = dff // FF_TILE
    ni = t // TOKEN_TILE
    assert nj >= 2
    mod_base = 6 * which
    kern = functools.partial(_ffn_kernel, mod_base=mod_base, n_prompt_tiles=npt,
                             tiles_per_seq=seq_rows // TOKEN_TILE, cast_next=nxt is not None, split_x=split_x)
    in_specs = x_specs + [
        pl.BlockSpec((None, MOD_ROWS, N_MOD * d), lambda i, j: (layer, 0, 0)),
        pl.BlockSpec((None, 1, d), lambda i, j: (3 * layer + 2 * which, 0, 0)),
        pl.BlockSpec((d, FF_TILE), lambda i, j: (0, j)),
        pl.BlockSpec((d, FF_TILE), lambda i, j: (0, j + nj)),
        pl.BlockSpec((FF_TILE, d), lambda i, j: (j, 0)),
    ]
    args = x_args + [mods, gains, w_in, w_in, w_out]
    out_specs = [pl.BlockSpec((TOKEN_TILE, d), lambda i, j: (i, 0))]
    out_shape = [jax.ShapeDtypeStruct((t, d), F32)]
    if nxt is not None:
        f_in, f_out, nl, nw = nxt
        cast_rows = min(r for r in (16, 32, 64, 128, 256, 512)
                        if d % r == 0 and dff % r == 0 and d // r + dff // r <= ni * nj)
        n_in, n_out = d // cast_rows, dff // cast_rows
        in_blk = lambda i, j: jnp.minimum(i * nj + j, n_in - 1)
        out_blk = lambda i, j: jnp.clip(i * nj + j - n_in, 0, n_out - 1)
        in_specs += [pl.BlockSpec((None, None, cast_rows, 2 * dff), lambda i, j: (nl, nw, in_blk(i, j), 0)),
                     pl.BlockSpec((None, None, cast_rows, d), lambda i, j: (nl, nw, out_blk(i, j), 0))]
        args += [f_in, f_out]
        out_specs += [pl.BlockSpec((cast_rows, 2 * dff), lambda i, j: (in_blk(i, j), 0)),
                      pl.BlockSpec((cast_rows, d), lambda i, j: (out_blk(i, j), 0))]
        out_shape += [jax.ShapeDtypeStruct((d, 2 * dff), BF16), jax.ShapeDtypeStruct((dff, d), BF16)]
    return pl.pallas_call(
        kern,
        grid=(ni, nj),
        in_specs=in_specs,
        out_specs=out_specs,
        out_shape=out_shape,
        scratch_shapes=[pltpu.VMEM((TOKEN_TILE, d), BF16)],
        compiler_params=_params(("arbitrary", "arbitrary"), 56),
        name="ffn",
    )(*args)


def _inproj_kernel(x_ref, mod_ref, gain_ref, w_ref, o_ref, *, n_prompt_tiles, tiles_per_seq):
    row = _mod_row(pl.program_id(0), n_prompt_tiles, tiles_per_seq)
    h = _adaln(x_ref[...], mod_ref, gain_ref, row, 3)
    o_ref[...] = jnp.dot(h, w_ref[...], preferred_element_type=F32)


def _inproj(x, mods, gains, w, layer, w_idx, tm, tn, n_prompt_rows, seq_rows):
    t, d = x.shape
    n = w.shape[2]
    kern = functools.partial(_inproj_kernel, n_prompt_tiles=n_prompt_rows // tm, tiles_per_seq=seq_rows // tm)
    return pl.pallas_call(
        kern,
        grid=(t // tm, n // tn),
        in_specs=[
            pl.BlockSpec((tm, d), lambda i, j: (i, 0)),
            pl.BlockSpec((None, MOD_ROWS, N_MOD * d), lambda i, j: (layer, 0, 0)),
            pl.BlockSpec((None, 1, d), lambda i, j: (3 * layer + 1, 0, 0)),
            pl.BlockSpec((None, d, tn), lambda i, j: (w_idx, 0, j)),
        ],
        out_specs=pl.BlockSpec((tm, tn), lambda i, j: (i, j)),
        out_shape=jax.ShapeDtypeStruct((t, n), F32),
        compiler_params=_params(("parallel", "parallel"), 52),
        name="inproj",
    )(x, mods, gains, w)


def _outproj_kernel(x_ref, mod_ref, ap_ref, as_ref, bp_ref, bs_ref, w_ref, o_ref, *,
                    n_prompt_tiles, tiles_per_seq):
    i = pl.program_id(0)
    d = x_ref.shape[1]
    half = ap_ref.shape[1]
    is_prompt = i < n_prompt_tiles
    ya = jnp.where(is_prompt, ap_ref[...], as_ref[...])
    yb = jnp.where(is_prompt, bp_ref[...], bs_ref[...])
    y = jnp.dot(ya, w_ref[pl.ds(0, half), :], preferred_element_type=F32)
    y = y + jnp.dot(yb, w_ref[pl.ds(half, half), :], preferred_element_type=F32)
    row = _mod_row(i, n_prompt_tiles, tiles_per_seq)
    gate = mod_ref[pl.ds(row, 1), pl.ds(5 * d, d)]
    o_ref[...] = x_ref[...] + gate * y


def _outproj(x, mods, layer, ya_p, ya_s, yb_p, yb_s, w, w_idx, n_prompt_rows, seq_rows):
    t, d = x.shape
    half = ya_p.shape[1]
    npt = n_prompt_rows // TOKEN_TILE
    kern = functools.partial(_outproj_kernel, n_prompt_tiles=npt, tiles_per_seq=seq_rows // TOKEN_TILE)
    p_map = lambda i: (jnp.minimum(i, npt - 1), 0)
    s_map = lambda i: (jnp.maximum(i - npt, 0), 0)
    return pl.pallas_call(
        kern,
        grid=(t // TOKEN_TILE,),
        in_specs=[
            pl.BlockSpec((TOKEN_TILE, d), lambda i: (i, 0)),
            pl.BlockSpec((None, MOD_ROWS, N_MOD * d), lambda i: (layer, 0, 0)),
            pl.BlockSpec((TOKEN_TILE, half), p_map),
            pl.BlockSpec((TOKEN_TILE, half), s_map),
            pl.BlockSpec((TOKEN_TILE, half), p_map),
            pl.BlockSpec((TOKEN_TILE, half), s_map),
            pl.BlockSpec((None, 2 * half, d), lambda i: (w_idx, 0, 0)),
        ],
        out_specs=pl.BlockSpec((TOKEN_TILE, d), lambda i: (i, 0)),
        out_shape=jax.ShapeDtypeStruct((t, d), F32),
        compiler_params=_params(("parallel",), 48),
        name="outproj",
    )(x, mods, ya_p, ya_s, yb_p, yb_s, w)


def _final_norm_kernel(x_ref, g_ref, o_ref):
    o_ref[...] = _rms(x_ref[...], g_ref[...])


def _final_norm(x, gain, row_start, rows):
    d = x.shape[1]
    off = row_start // TOKEN_TILE
    return pl.pallas_call(
        _final_norm_kernel,
        grid=(rows // TOKEN_TILE,),
        in_specs=[pl.BlockSpec((TOKEN_TILE, d), lambda i: (i + off, 0)),
                  pl.BlockSpec((1, d), lambda i: (0, 0))],
        out_specs=pl.BlockSpec((TOKEN_TILE, d), lambda i: (i, 0)),
        out_shape=jax.ShapeDtypeStruct((rows, d), F32),
        compiler_params=_params(("parallel",), 32),
        name="final_norm",
    )(x, gain.reshape(1, d))


def _group_roll(x, step):
    rows, w = x.shape
    x3 = x.reshape(rows // V7X_SUBLANES, V7X_SUBLANES, w)
    return pltpu.roll(x3, step, axis=1).reshape(rows, w)


def _lru_kernel(xa_ref, ga_ref, h0f_ref, h0b_ref, cw_ref, cb_ref, wa_ref, ba_ref, wx_ref, bx_ref,
                lam_ref, y_ref, lf_ref, lb_ref, a_scr, u_scr, hf_scr, hb_scr):
    seq, w = xa_ref.shape
    groups = seq // V7X_SUBLANES
    xa = xa_ref[...]
    row = lax.broadcasted_iota(jnp.int32, (seq, w), 0)
    sub = row & (V7X_SUBLANES - 1)

    def tap(offset):
        if offset == 0:
            return xa
        shifted = pltpu.roll(xa, (-offset) % seq, axis=0)
        valid = row >= -offset if offset < 0 else row < seq - offset
        return jnp.where(valid, shifted, 0.0)

    xc = cb_ref[...]
    for j in range(CONV_W):
        xc = xc + tap(j - CONV_W // 2) * cw_ref[pl.ds(j, 1), :]
    xcb = xc.astype(BF16)

    for direction, (h0_ref, h_scr, last_ref) in enumerate(((h0f_ref, hf_scr, lf_ref),
                                                            (h0b_ref, hb_scr, lb_ref))):
        reverse = direction == 1
        r = jax.nn.sigmoid(jnp.dot(xcb, wa_ref[direction], preferred_element_type=F32) + ba_ref[direction])
        gi = jax.nn.sigmoid(jnp.dot(xcb, wx_ref[direction], preferred_element_type=F32) + bx_ref[direction])
        neg_lam = -lam_ref[direction]
        softplus = jnp.maximum(neg_lam, 0.0) + jnp.log1p(jnp.exp(-jnp.abs(neg_lam)))
        log_a = r * (-LRU_C * softplus)
        a = jnp.exp(log_a)
        m2 = (1.0 - a) * (1.0 + a)
        u = jnp.where(m2 > 0.0, m2 * lax.rsqrt(m2), 0.0) * (gi * xc)
        for step in (1, 2, 4):
            if reverse:
                a_nb = _group_roll(a, V7X_SUBLANES - step)
                u_nb = _group_roll(u, V7X_SUBLANES - step)
                valid = sub < V7X_SUBLANES - step
            else:
                a_nb = _group_roll(a, step)
                u_nb = _group_roll(u, step)
                valid = sub >= step
            u = a * jnp.where(valid, u_nb, 0.0) + u
            a = a * jnp.where(valid, a_nb, 1.0)
        a_scr[...] = a
        u_scr[...] = u

        def carry_step(g, carry, reverse=reverse, h_scr=h_scr):
            gg = groups - 1 - g if reverse else g
            off = pl.multiple_of(gg * V7X_SUBLANES, V7X_SUBLANES)
            h = a_scr[pl.ds(off, V7X_SUBLANES), :] * carry + u_scr[pl.ds(off, V7X_SUBLANES), :]
            h_scr[pl.ds(off, V7X_SUBLANES), :] = h
            return h[0:1] if reverse else h[V7X_SUBLANES - 1:V7X_SUBLANES]

        last_ref[...] = lax.fori_loop(0, groups, carry_step, h0_ref[...], unroll=4)

    y_ref[...] = ((hf_scr[...] + hb_scr[...]) * jax.nn.gelu(ga_ref[...])).astype(BF16)


def _lru(p_ab, h0f, h0b, prm, seq_len, n_seq, row_start):
    lru_w = h0f.shape[-1]
    cw = LRU_COLS
    ncb = lru_w // cw
    sb = row_start // seq_len
    vec = lambda: pl.BlockSpec((None, 1, cw), lambda s, c: (s, 0, c))
    par2 = lambda: pl.BlockSpec((None, 2, 1, cw), lambda s, c: (prm["idx"], 0, 0, c))
    gate = lambda: pl.BlockSpec((None, 2, None, cw, cw), lambda s, c: (prm["idx"], 0, c, 0, 0))
    return pl.pallas_call(
        _lru_kernel,
        grid=(n_seq, ncb),
        in_specs=[
            pl.BlockSpec((seq_len, cw), lambda s, c: (s + sb, c)),
            pl.BlockSpec((seq_len, cw), lambda s, c: (s + sb, c + ncb)),
            vec(), vec(),
            pl.BlockSpec((None, CONV_W, cw), lambda s, c: (prm["idx"], 0, c)),
            pl.BlockSpec((None, 1, cw), lambda s, c: (prm["idx"], 0, c)),
            gate(), par2(), gate(), par2(), par2(),
        ],
        out_specs=[
            pl.BlockSpec((seq_len, cw), lambda s, c: (s, c)),
            vec(), vec(),
        ],
        out_shape=[
            jax.ShapeDtypeStruct((n_seq * seq_len, lru_w), BF16),
            jax.ShapeDtypeStruct((n_seq, 1, lru_w), F32),
            jax.ShapeDtypeStruct((n_seq, 1, lru_w), F32),
        ],
        scratch_shapes=[pltpu.VMEM((seq_len, cw), F32)] * 4,
        compiler_params=_params(("parallel", "parallel"), 40),
        name="lru",
    )(p_ab, p_ab, h0f.reshape(n_seq, 1, lru_w), h0b.reshape(n_seq, 1, lru_w), prm["conv_w"],
      prm["conv_b"], prm["wa"], prm["ba"], prm["wx"], prm["bx"], prm["lam"])


def _block_diag_tiles(w, tile):
    n, two, nb, bw, _ = w.shape
    per = tile // bw
    w = w.reshape(n, two, nb // per, per, bw, bw)
    eye = jnp.eye(per, dtype=w.dtype)
    out = jnp.einsum("ndgpij,pq->ndgpiqj", w, eye)
    return out.reshape(n, two, nb // per, tile, tile)


def _softmax_attend(q, ks, vs, biases, scale):
    scores = []
    for k, b in zip(ks, biases):
        s = lax.dot_general(q, k, (((1,), (1,)), ((), ())), preferred_element_type=F32)
        if scale is not None:
            s = s * scale
        scores.append(s if b is None else s + b)
    m = scores[0].max(axis=-1, keepdims=True)
    for s in scores[1:]:
        m = jnp.maximum(m, s.max(axis=-1, keepdims=True))
    denom = None
    out = None
    for s, v in zip(scores, vs):
        e = jnp.exp(s - m)
        part = e.sum(axis=-1, keepdims=True)
        denom = part if denom is None else denom + part
        o = jnp.dot(e.astype(BF16), v, preferred_element_type=F32)
        out = o if out is None else out + o
    return out / denom


def _upper_half(shape):
    lane = lax.broadcasted_iota(jnp.int32, shape, len(shape) - 1)
    return lane % V7X_LANES >= V7X_LANES // 2


def _split_pair(q):
    up = _upper_half(q.shape)
    zero = jnp.zeros_like(q)
    return jnp.concatenate([jnp.where(up, zero, q), jnp.where(up, q, zero)], axis=0)


def _merge_pair(o):
    m = o.shape[0] // 2
    return jnp.where(_upper_half((m, o.shape[1])), o[m:], o[:m])


def _dup_head(block, use_high):
    swapped = pltpu.roll(block, V7X_LANES // 2, axis=1)
    keep = _upper_half(block.shape) == use_high
    return jnp.where(keep, block, swapped)


def _fold_scale(q, scale):
    return (q * scale, None) if _is_pow2(scale) else (q, scale)


def _na_ctx_kernel(q_ref, k_ref, v_ref, o_ref, *, scale):
    for pair in range(q_ref.shape[1] // V7X_LANES):
        sl = pl.ds(pair * V7X_LANES, V7X_LANES)
        q, sc = _fold_scale(q_ref[:, sl], scale)
        o = _softmax_attend(_split_pair(q.astype(BF16)), [k_ref[:, sl].astype(BF16)],
                            [v_ref[:, sl].astype(BF16)], [None], sc)
        o_ref[:, sl] = _merge_pair(o).astype(o_ref.dtype)


def _na_ctx(p_ab, n_seq, seq_len, q_blk, na_w, scale):
    blk = lambda col: pl.BlockSpec((seq_len, na_w), lambda b: (b, col))
    return pl.pallas_call(
        functools.partial(_na_ctx_kernel, scale=scale),
        grid=(n_seq,),
        in_specs=[blk(q_blk), blk(q_blk + 1), blk(q_blk + 2)],
        out_specs=pl.BlockSpec((seq_len, na_w), lambda b: (b, 0)),
        out_shape=jax.ShapeDtypeStruct((n_seq * seq_len, na_w), BF16),
        compiler_params=_params(("parallel",), 40),
        name="na_ctx",
    )(p_ab, p_ab, p_ab)


def _na_tile_index(seq_len):
    rows_n = seq_len // GRID_W
    kr = min(NA_KR, rows_n)
    rows_per_chunk = NA_Q_CHUNK // GRID_W
    n_chunks = rows_n // rows_per_chunk
    win_rows = NA_WIN // GRID_W
    table = []
    for r in range(rows_n):
        r_start = min(max(r - kr // 2, 0), rows_n - kr)
        base = 0 if r // rows_per_chunk < n_chunks // 2 else rows_n - win_rows
        table.append([k - r + NA_KR - 1 if r_start <= k < r_start + kr else 2 * NA_KR - 1
                      for k in range(base, base + win_rows)])
    return table


def _na_lat_kernel(q_ref, k_ref, v_ref, kc_ref, vc_ref, tiles_ref, o_ref, bias_ref, *, scale):
    lq = q_ref.shape[0]
    n_chunks = lq // NA_Q_CHUNK
    rows_per_chunk = NA_Q_CHUNK // GRID_W

    @pl.when(pl.program_id(1) == 0)
    def _():
        low = lax.broadcasted_iota(jnp.int32, (GRID_W, V7X_LANES), 1) < GRID_W
        for half in (0, 1):
            for r, slots in enumerate(_na_tile_index(lq)):
                c, rr = divmod(r, rows_per_chunk)
                rows = pl.ds(half * NA_Q_CHUNK + rr * GRID_W, GRID_W)
                for s in range(0, len(slots), 2):
                    blk = jnp.where(low, tiles_ref[half, slots[s]], tiles_ref[half, slots[s + 1]])
                    bias_ref[c, rows, pl.ds(s * GRID_W, V7X_LANES)] = blk

    kcb = kc_ref[...].astype(BF16)
    vcb = vc_ref[...].astype(BF16)
    for c in range(n_chunks):
        win = 0 if c < n_chunks // 2 else lq - NA_WIN
        kb = k_ref[pl.ds(win, NA_WIN), :].astype(BF16)
        vb = v_ref[pl.ds(win, NA_WIN), :].astype(BF16)
        rows = pl.ds(c * NA_Q_CHUNK, NA_Q_CHUNK)
        q, sc = _fold_scale(q_ref[rows, :], scale)
        o = _softmax_attend(_split_pair(q.astype(BF16)), [kb, kcb], [vb, vcb], [bias_ref[c], None], sc)
        o_ref[rows, :] = _merge_pair(o).astype(o_ref.dtype)


def _na_lat(p_ab, cache_k, cache_v, tiles, idx, n_seq, seq_len, row_start, q_col, n_pairs, scale):
    sb = row_start // seq_len
    past = cache_k.shape[2]
    blk = lambda col: pl.BlockSpec((seq_len, V7X_LANES), lambda h, b: (b + sb, col + h))
    cblk = lambda: pl.BlockSpec((None, None, past, V7X_LANES), lambda h, b: (b, idx, 0, h))
    n_chunks = seq_len // NA_Q_CHUNK
    return pl.pallas_call(
        functools.partial(_na_lat_kernel, scale=scale),
        grid=(n_pairs, n_seq),
        in_specs=[blk(q_col), blk(q_col + n_pairs), blk(q_col + 2 * n_pairs), cblk(), cblk(),
                  pl.BlockSpec((None, 2, 2 * NA_KR, GRID_W, V7X_LANES), lambda h, b: (idx, h, 0, 0, 0))],
        out_specs=pl.BlockSpec((seq_len, V7X_LANES), lambda h, b: (b, h)),
        out_shape=jax.ShapeDtypeStruct((n_seq * seq_len, n_pairs * V7X_LANES), BF16),
        scratch_shapes=[pltpu.VMEM((n_chunks, 2 * NA_Q_CHUNK, NA_WIN), F32)],
        compiler_params=_params(("parallel", "arbitrary"), 56),
        name="na_lat",
    )(p_ab, p_ab, p_ab, cache_k, cache_v, tiles)


def _na_bias_tiles(tab):
    lead = tab.shape[:-1]
    edge = GRID_W - NA_KC
    vec = jnp.pad(tab, [(0, 0)] * len(lead) + [(edge, edge + 1)], mode="edge")
    skew = jnp.broadcast_to(vec[..., None, :], lead + (GRID_W, 2 * GRID_W))
    skew = skew.reshape(lead + (2 * GRID_W * GRID_W,))[..., :GRID_W * (2 * GRID_W - 1)]
    toep = skew.reshape(lead + (GRID_W, 2 * GRID_W - 1))[..., GRID_W - 1:]
    cols = np.arange(GRID_W)
    c_start = np.clip(cols - NA_KC // 2, 0, GRID_W - NA_KC)
    col_ok = (cols[None, :] >= c_start[:, None]) & (cols[None, :] < c_start[:, None] + NA_KC)
    tiles = jnp.where(col_ok, toep, NEG_BIAS)
    masked = jnp.full(lead[:-1] + (1, GRID_W, GRID_W), NEG_BIAS, F32)
    tiles = jnp.concatenate([tiles, masked], axis=-3)
    return jnp.concatenate([tiles, tiles], axis=-1)


def _split_dot(v, m):
    hi = v.astype(BF16)
    lo = (v - hi.astype(F32)).astype(BF16)
    return jnp.dot(hi, m, preferred_element_type=F32) + jnp.dot(lo, m, preferred_element_type=F32)


def _head_rms(x, gather, spread, gain, head_dim):
    inv = lax.rsqrt(_split_dot(x * x, gather) * (1.0 / head_dim) + EPS)
    return x * _split_dot(inv, spread) * gain


def _rope(x, cos, sin_lo, sin_hi):
    w = x.shape[1]
    reps = w // V7X_LANES
    tile = lambda t: t if reps == 1 else jnp.concatenate([t] * reps, axis=1)
    quarter = 16
    return (x * tile(cos) + pltpu.roll(x, w - quarter, axis=1) * tile(sin_lo)
            + pltpu.roll(x, quarter, axis=1) * tile(sin_hi))


def _cd_post_kernel(p_ref, gat_ref, spr_ref, qg_ref, kg_ref, mqg_ref, mkg_ref, wuq_ref, wukv_ref, *rest,
                    rope, head_dim, q_w, kv_w, rank, q_scale):
    if rope:
        cos_ref, slo_ref, shi_ref = rest[:3]
        rest = rest[3:]
        tabs = (cos_ref[...], slo_ref[...], shi_ref[...])
    q_ref, k_ref, qn_ref, qr_ref, ckv_ref, kn_ref, vm_ref, kr_ref = rest
    c0 = 0
    qc = p_ref[:, pl.ds(c0, q_w)]
    c0 += q_w
    kc = p_ref[:, pl.ds(c0, kv_w)]
    c0 += 2 * kv_w
    qa = p_ref[:, pl.ds(c0, rank)]
    c0 += rank
    ckv = p_ref[:, pl.ds(c0, rank)]
    c0 += rank
    kr = p_ref[:, pl.ds(c0, V7X_LANES)]

    qc = _head_rms(qc, gat_ref[...], spr_ref[...], qg_ref[...], head_dim)
    kc = _head_rms(kc, gat_ref[pl.ds(0, kv_w), :], spr_ref[:, pl.ds(0, kv_w)], kg_ref[...], head_dim)
    qd = jnp.dot(_rms(qa, mqg_ref[...]).astype(BF16), wuq_ref[...], preferred_element_type=F32)
    n_nope = qn_ref.shape[1]
    qn = qd[:, :n_nope]
    qr = qd[:, n_nope:]
    ckv = _rms(ckv, mkg_ref[...])
    if rope:
        qc = _rope(qc, *tabs)
        kc = _rope(kc, *tabs)
        qr = _rope(qr, *tabs)
        kr = _rope(kr, *tabs)
    q_ref[...] = (qc * q_scale).astype(BF16)
    k_ref[...] = kc
    qn_ref[...] = qn.astype(BF16)
    qr_ref[...] = qr.astype(BF16)
    ckv_ref[...] = ckv
    kv = jnp.dot(ckv.astype(BF16), wukv_ref[...], preferred_element_type=F32)
    kn_ref[...] = kv[:, :n_nope].astype(BF16)
    vm_ref[...] = kv[:, n_nope:].astype(BF16)
    kr_ref[...] = kr


def _cd_post(p_cd, prm, row_start, rows, rope_tabs, dims, q_scale):
    q_w, kv_w, rank, head_dim, n_nope, n_rope, n_v = dims
    tm = 256
    off = row_start // tm
    n_in = p_cd.shape[1]
    idx = prm["idx"]
    rope = rope_tabs is not None
    const = lambda shape: pl.BlockSpec(shape, lambda i: (0,) * len(shape))
    layer = lambda shape: pl.BlockSpec((None,) + shape, lambda i: (idx,) + (0,) * len(shape))
    in_specs = [
        pl.BlockSpec((tm, n_in), lambda i: (i + off, 0)),
        const((q_w, V7X_LANES)), const((V7X_LANES, q_w)),
        layer((1, q_w)), layer((1, kv_w)), layer((1, rank)), layer((1, rank)),
        layer((rank, n_nope + n_rope)), layer((rank, n_nope + n_v)),
    ]
    args = [p_cd, prm["gather"], prm["spread"], prm["q_gain"], prm["k_gain"], prm["mla_q_gain"],
            prm["mla_kv_gain"], prm["w_uq"], prm["w_ukv"]]
    if rope:
        seq_tiles = rope_tabs[0].shape[0] // tm
        in_specs += [pl.BlockSpec((tm, V7X_LANES), lambda i: (i % seq_tiles, 0))] * 3
        args += list(rope_tabs)
    widths = [(q_w, BF16), (kv_w, F32), (n_nope, BF16), (n_rope, BF16), (rank, F32), (n_nope, BF16),
              (n_v, BF16), (V7X_LANES, F32)]
    kern = functools.partial(_cd_post_kernel, rope=rope, head_dim=head_dim, q_w=q_w, kv_w=kv_w, rank=rank,
                             q_scale=q_scale)
    return pl.pallas_call(
        kern,
        grid=(rows // tm,),
        in_specs=in_specs,
        out_specs=[pl.BlockSpec((tm, w), lambda i: (i, 0)) for w, _ in widths],
        out_shape=[jax.ShapeDtypeStruct((rows, w), dt) for w, dt in widths],
        compiler_params=_params(("parallel",), 48),
        name="cd_post",
    )(*args)


def _rope_tables(seq_len, head_dim):
    half = head_dim // 2
    nf = half // 2
    t = np.arange(seq_len)
    inv_freq = (1.0 / (ROPE_THETA ** (np.arange(nf, dtype=np.float32) / nf))).astype(np.float32)
    zeros = np.zeros((seq_len, nf), np.float32)
    cos, slo, shi = [], [], []
    for pos in (t // GRID_W, t % GRID_W):
        ang = pos.astype(np.float32)[:, None] * inv_freq[None, :]
        c, s = np.cos(ang).astype(np.float32), np.sin(ang).astype(np.float32)
        cos += [c, c]
        slo += [-s, zeros]
        shi += [zeros, s]
    reps = V7X_LANES // head_dim
    return tuple(jnp.asarray(np.tile(np.concatenate(x, axis=1), (1, reps))) for x in (cos, slo, shi))


def _ckv_up_kernel(c_ref, w_ref, kn_ref, vm_ref):
    kv = jnp.dot(c_ref[...].astype(BF16), w_ref[...], preferred_element_type=F32)
    n = kn_ref.shape[1]
    kn_ref[...] = kv[:, :n].astype(BF16)
    vm_ref[...] = kv[:, n:].astype(BF16)


def _ckv_up(ckv, w_ukv, idx, n_nope):
    n_seq, _, tm, rank = ckv.shape
    rows = n_seq * tm
    n = w_ukv.shape[2]
    return pl.pallas_call(
        _ckv_up_kernel,
        grid=(rows // tm,),
        in_specs=[pl.BlockSpec((None, None, tm, rank), lambda i: (i, idx, 0, 0)),
                  pl.BlockSpec((None, rank, n), lambda i: (idx, 0, 0))],
        out_specs=[pl.BlockSpec((tm, n_nope), lambda i: (i, 0)), pl.BlockSpec((tm, n - n_nope), lambda i: (i, 0))],
        out_shape=[jax.ShapeDtypeStruct((rows, n_nope), BF16), jax.ShapeDtypeStruct((rows, n - n_nope), BF16)],
        compiler_params=_params(("parallel",), 32),
        name="ckv_up",
    )(ckv, w_ukv)


def _gqa_kernel(q_ref, k_ref, v_ref, *rest, scale, has_cache, group_w):
    if has_cache:
        kc_ref, vc_ref, o_ref = rest
    else:
        (o_ref,) = rest
    lq = q_ref.shape[0]
    n_local = q_ref.shape[1] // group_w
    n_pairs = group_w // V7X_LANES
    for hl in range(n_local):
        if n_local == 1:
            high = (pl.program_id(1) % 2) == 1
            pick = lambda ref: ref[...]
        else:
            high = hl % 2 == 1
            pick = lambda ref, hl=hl: ref[:, pl.ds((hl // 2) * V7X_LANES, V7X_LANES)]
        ks = [_dup_head(pick(k_ref), high).astype(BF16)]
        vs = [_dup_head(pick(v_ref), high).astype(BF16)]
        if has_cache:
            ks.append(_dup_head(pick(kc_ref), high).astype(BF16))
            vs.append(_dup_head(pick(vc_ref), high).astype(BF16))
        for c in range(lq // Q_CHUNK):
            rows = pl.ds(c * Q_CHUNK, Q_CHUNK)
            cols = [pl.ds(hl * group_w + p * V7X_LANES, V7X_LANES) for p in range(n_pairs)]
            q = jnp.concatenate([_split_pair(q_ref[rows, cl]) for cl in cols], axis=0)
            o = _softmax_attend(q, ks, vs, [None] * len(ks), scale)
            for p, cl in enumerate(cols):
                o_ref[rows, cl] = _merge_pair(o[2 * p * Q_CHUNK:2 * (p + 1) * Q_CHUNK]).astype(o_ref.dtype)


def _gqa(q, k, p_cd, v_col, cache_k, cache_v, idx, n_seq, seq_len, row_start, n_kv, heads_per_step, scale):
    sb = row_start // seq_len
    group_w = q.shape[1] // n_kv
    has_cache = cache_k is not None
    if heads_per_step == 1:
        kv_w, kv_blk, v_blk = V7X_LANES, (lambda h: h // 2), (lambda h: v_col + h // 2)
    else:
        kv_w = k.shape[1]
        kv_blk, v_blk = (lambda h: 0), (lambda h: v_col * V7X_LANES // kv_w)
    in_specs = [
        pl.BlockSpec((seq_len, group_w * heads_per_step), lambda b, h: (b, h)),
        pl.BlockSpec((seq_len, kv_w), lambda b, h: (b, kv_blk(h))),
        pl.BlockSpec((seq_len, kv_w), lambda b, h: (b + sb, v_blk(h))),
    ]
    args = [q, k, p_cd]
    if has_cache:
        past = cache_k.shape[2]
        in_specs += [pl.BlockSpec((None, None, past, kv_w), lambda b, h: (b, idx, 0, kv_blk(h)))] * 2
        args += [cache_k, cache_v]
    return pl.pallas_call(
        functools.partial(_gqa_kernel, scale=scale, has_cache=has_cache, group_w=group_w),
        grid=(n_seq, n_kv // heads_per_step),
        in_specs=in_specs,
        out_specs=pl.BlockSpec((seq_len, group_w * heads_per_step), lambda b, h: (b, h)),
        out_shape=jax.ShapeDtypeStruct(q.shape, BF16),
        compiler_params=_params(("parallel", "parallel"), 56),
        name="gqa",
    )(*args)


def _mla_kernel(qn_ref, qr_ref, kn_ref, kr_ref, v_ref, *rest, scale, has_cache):
    if has_cache:
        knc_ref, krc_ref, vc_ref, o_ref = rest
        krc = krc_ref[...].astype(BF16)
    else:
        (o_ref,) = rest
    lq = qn_ref.shape[0]
    qc = min(2 * Q_CHUNK, lq)
    n_local = qn_ref.shape[1] // V7X_LANES
    kr = kr_ref[...].astype(BF16)
    for hl in range(n_local):
        sl = pl.ds(hl * V7X_LANES, V7X_LANES)
        if n_local == 1:
            high = (pl.program_id(1) % 2) == 1
            pair = pl.ds(0, V7X_LANES)
        else:
            high = hl % 2 == 1
            pair = pl.ds((hl // 2) * V7X_LANES, V7X_LANES)
        ks = [jnp.concatenate([kn_ref[:, sl], kr], axis=1)]
        vs = [v_ref[:, sl]]
        if has_cache:
            ks.append(jnp.concatenate([knc_ref[:, sl], krc], axis=1))
            vs.append(vc_ref[:, sl])
        for c in range(lq // qc):
            rows = pl.ds(c * qc, qc)
            qr = qr_ref[rows, pair]
            qr = jnp.where(_upper_half(qr.shape) == high, qr, jnp.zeros_like(qr))
            q = jnp.concatenate([qn_ref[rows, sl], qr], axis=1)
            o_ref[rows, sl] = _softmax_attend(q, ks, vs, [None] * len(ks), scale).astype(o_ref.dtype)


def _mla(qn, qr, kn, kr2, vm, cache, n_seq, seq_len, heads_per_step, scale):
    has_cache = cache is not None
    n_heads = vm.shape[1] // V7X_LANES
    hw = heads_per_step * V7X_LANES
    head = lambda rows: pl.BlockSpec((rows, hw), lambda b, h: (b, h))
    if heads_per_step == 1:
        rope_q = pl.BlockSpec((seq_len, V7X_LANES), lambda b, h: (b, h // 2))
    else:
        rope_q = pl.BlockSpec((seq_len, qr.shape[1]), lambda b, h: (b, 0))
    in_specs = [head(seq_len), rope_q, head(seq_len),
                pl.BlockSpec((seq_len, V7X_LANES), lambda b, h: (b, 0)),
                head(seq_len)]
    args = [qn, qr, kn, kr2, vm]
    if has_cache:
        knc, krc, vmc, idx = cache
        past = krc.shape[2]
        in_specs += [head(past),
                     pl.BlockSpec((None, None, past, V7X_LANES), lambda b, h: (b, idx, 0, 0)),
                     head(past)]
        args += [knc, krc, vmc]
    return pl.pallas_call(
        functools.partial(_mla_kernel, scale=scale, has_cache=has_cache),
        grid=(n_seq, n_heads // heads_per_step),
        in_specs=in_specs,
        out_specs=head(seq_len),
        out_shape=jax.ShapeDtypeStruct(vm.shape, BF16),
        compiler_params=_params(("parallel", "parallel"), 48),
        name="mla",
    )(*args)


def kernel(x_prompt, x_sample, state_lru_fwd, state_lru_bwd, cache_na_k, cache_na_v, cache_gqa_k, cache_gqa_v, cache_mla_ckv, cache_mla_krope, c, c_ctx, w_mod, b_mod, norm_gain, w_ffn_in, w_ffn_out, w_in_ab, conv_w, conv_b, lru_wa, lru_ba, lru_wx, lru_bx, lru_lambda, na_bias, w_out_ab, w_in_cd, gqa_q_gain, gqa_k_gain, mla_q_gain, mla_kv_gain, mla_w_uq, mla_w_uk, mla_w_uv, w_out_cd, final_gain):
    batch, seq, d = x_prompt.shape
    dec_batch, dec_seq, _ = x_sample.shape
    depth = w_mod.shape[0]
    n_even, n_odd = w_in_ab.shape[0], w_in_cd.shape[0]
    n_p = batch * seq
    n_s = dec_batch * dec_seq
    past = cache_na_k.shape[2]
    lru_w = state_lru_fwd.shape[-1]
    na_heads, na_dh = cache_na_k.shape[3], cache_na_k.shape[4]
    na_w = na_heads * na_dh
    gqa_kv, gqa_dh = cache_gqa_k.shape[3], cache_gqa_k.shape[4]
    kv_w = gqa_kv * gqa_dh
    rank = cache_mla_ckv.shape[-1]
    rope_w = cache_mla_krope.shape[-1]
    q_w = w_in_cd.shape[2] - 2 * kv_w - 2 * rank - rope_w
    mla_heads = mla_w_uk.shape[2] // MLA_NOPE
    n_nope = mla_heads * MLA_NOPE
    n_rope = mla_heads * rope_w
    n_v = mla_heads * MLA_V
    mla_qk = MLA_NOPE + rope_w
    assert mla_q_gain.shape[-1] == rank and 2 * rope_w == V7X_LANES
    assert n_p % dec_seq == 0 and dec_seq % TOKEN_TILE == 0 and seq % Q_CHUNK == 0
    assert dec_seq // GRID_W == 2 * NA_KR and na_dh == GRID_W and gqa_dh == GRID_W
    assert w_in_ab.shape[2] == 2 * lru_w + 3 * na_w and lru_w == na_w
    assert (q_w // gqa_kv) % V7X_LANES == 0 and q_w % kv_w == 0

    x = (x_prompt.reshape(n_p, d), x_sample.reshape(n_s, d))
    cond = jnp.concatenate([c_ctx[None, :], c, jnp.zeros((MOD_ROWS - 1 - dec_batch, d), F32)], axis=0)
    mods = _modulation(cond, w_mod, b_mod)
    gains = norm_gain.reshape(depth * 3, 1, d)

    ffn_w = (w_ffn_in[0, 0].astype(BF16), w_ffn_out[0, 0].astype(BF16))

    def ffn_step(x, ffn_w, layer, which):
        last = layer == depth - 1 and which == 1
        nxt = None if last else (w_ffn_in, w_ffn_out, layer + which, 1 - which)
        out = _ffn(x, mods, gains, ffn_w[0], ffn_w[1], layer, which, nxt, n_p, dec_seq)
        return out[0], tuple(out[1:])
    w_in_ab_bf = w_in_ab.astype(BF16)
    w_out_ab_bf = w_out_ab.astype(BF16)
    w_in_cd_bf = jnp.concatenate([w_in_cd, w_in_cd[:, :, -rope_w:]], axis=2).astype(BF16)
    w_out_cd_bf = w_out_cd.astype(BF16)

    lru_prm = {
        "conv_w": conv_w, "conv_b": conv_b.reshape(n_even, 1, lru_w),
        "wa": _block_diag_tiles(lru_wa, LRU_COLS).astype(BF16),
        "wx": _block_diag_tiles(lru_wx, LRU_COLS).astype(BF16),
        "ba": lru_ba.reshape(n_even, 2, 1, lru_w), "bx": lru_bx.reshape(n_even, 2, 1, lru_w),
        "lam": lru_lambda.reshape(n_even, 2, 1, lru_w),
    }
    zeros_state = jnp.zeros((batch, lru_w), F32)
    na_k_ctx = cache_na_k.reshape(dec_batch, n_even, past, na_w)
    na_v_ctx = cache_na_v.reshape(dec_batch, n_even, past, na_w)
    na_tiles = _na_bias_tiles(na_bias)

    w_uq = mla_w_uq.reshape(n_odd, rank, mla_heads, mla_qk)
    w_uq = jnp.concatenate([w_uq[..., :MLA_NOPE].reshape(n_odd, rank, n_nope),
                            w_uq[..., MLA_NOPE:].reshape(n_odd, rank, n_rope)], axis=2).astype(BF16)
    gqa_scale = gqa_dh ** -0.5
    q_scale = gqa_scale if _is_pow2(gqa_scale) else 1.0
    head_of_lane = np.arange(q_w) // gqa_dh
    gather = (head_of_lane[:, None] == np.arange(V7X_LANES)[None, :]).astype(np.float32)
    assert q_w // gqa_dh <= V7X_LANES
    cd_prm = {
        "gather": jnp.asarray(gather, BF16), "spread": jnp.asarray(gather.T, BF16),
        "q_gain": jnp.tile(gqa_q_gain, (1, q_w // gqa_dh)).reshape(n_odd, 1, q_w),
        "k_gain": jnp.tile(gqa_k_gain, (1, gqa_kv)).reshape(n_odd, 1, kv_w),
        "mla_q_gain": mla_q_gain.reshape(n_odd, 1, rank),
        "mla_kv_gain": mla_kv_gain.reshape(n_odd, 1, rank),
        "w_uq": w_uq,
        "w_ukv": jnp.concatenate([mla_w_uk, mla_w_uv], axis=2).astype(BF16),
    }
    rope_tabs = _rope_tables(dec_seq, gqa_dh)
    gqa_k_ctx = cache_gqa_k.reshape(dec_batch, n_odd, past, kv_w)
    gqa_v_ctx = cache_gqa_v.reshape(dec_batch, n_odd, past, kv_w)
    krope_ctx = jnp.concatenate([cache_mla_krope, cache_mla_krope], axis=-1)

    st_f, st_b, na_k, na_v, gq_k, gq_v, ml_c, ml_r = [], [], [], [], [], [], [], []
    for layer in range(depth):
        jdx = layer // 2
        x, ffn_w = ffn_step(x, ffn_w, layer, 0)
        if layer % 2 == 0:
            p_ab = _inproj(x, mods, gains, w_in_ab_bf, layer, jdx, TOKEN_TILE, w_in_ab_bf.shape[2] // 2,
                           n_p, dec_seq)
            prm = dict(lru_prm, idx=jdx)
            ya_p, lf, lb = _lru(p_ab, zeros_state, zeros_state, prm, seq, batch, 0)
            ya_s, _, _ = _lru(p_ab, state_lru_fwd[:, jdx], state_lru_bwd[:, jdx], prm, dec_seq, dec_batch, n_p)
            scale = na_dh ** -0.5
            yb_p = _na_ctx(p_ab, batch, seq, 2 * lru_w // na_w, na_w, scale)
            yb_s = _na_lat(p_ab, na_k_ctx, na_v_ctx, na_tiles, jdx, dec_batch, dec_seq, n_p,
                           2 * lru_w // V7X_LANES, na_w // V7X_LANES, scale)
            x = _outproj(x, mods, layer, ya_p, ya_s, yb_p, yb_s, w_out_ab_bf, jdx, n_p, dec_seq)
            st_f.append(lf.reshape(batch, lru_w))
            st_b.append(lb.reshape(batch, lru_w))
            na_k.append(p_ab[:n_p, 2 * lru_w + na_w:2 * lru_w + 2 * na_w].reshape(batch, seq, na_heads, na_dh))
            na_v.append(p_ab[:n_p, 2 * lru_w + 2 * na_w:].reshape(batch, seq, na_heads, na_dh))
        else:
            p_cd = _inproj(x, mods, gains, w_in_cd_bf, layer, jdx, TOKEN_TILE, w_in_cd_bf.shape[2], n_p, dec_seq)
            prm = dict(cd_prm, idx=jdx)
            dims = (q_w, kv_w, rank, gqa_dh, n_nope, n_rope, n_v)
            qp, kp, qnp_, qrp, ckvp, knp_, vmp, krp = _cd_post(p_cd, prm, 0, n_p, None, dims, q_scale)
            qs, ks_, qns, qrs, _, kns, vms, krs = _cd_post(p_cd, prm, n_p, n_s, rope_tabs, dims, q_scale)
            v_col = (q_w + kv_w) // V7X_LANES
            att_scale = None if q_scale != 1.0 else gqa_scale
            yc_p = _gqa(qp, kp, p_cd, v_col, None, None, jdx, batch, seq, 0, gqa_kv, gqa_kv, att_scale)
            yc_s = _gqa(qs, ks_, p_cd, v_col, gqa_k_ctx, gqa_v_ctx, jdx, dec_batch, dec_seq, n_p,
                        gqa_kv, 1, att_scale)
            knc, vmc = _ckv_up(cache_mla_ckv, cd_prm["w_ukv"], jdx, n_nope)
            yd_p = _mla(qnp_, qrp, knp_, krp, vmp, None, batch, seq, mla_heads, mla_qk ** -0.5)
            yd_s = _mla(qns, qrs, kns, krs, vms, (knc, krope_ctx, vmc, jdx), dec_batch, dec_seq, 1,
                        mla_qk ** -0.5)
            x = _outproj(x, mods, layer, yc_p, yc_s, yd_p, yd_s, w_out_cd_bf, jdx, n_p, dec_seq)
            gq_k.append(kp.reshape(batch, seq, gqa_kv, gqa_dh))
            gq_v.append(p_cd[:n_p, q_w + kv_w:q_w + 2 * kv_w].reshape(batch, seq, gqa_kv, gqa_dh))
            ml_c.append(ckvp.reshape(batch, seq, rank))
            ml_r.append(krp[:, :rope_w].reshape(batch, seq, rope_w))
        x, ffn_w = ffn_step(x, ffn_w, layer, 1)

    y_prompt = _final_norm(x, final_gain, 0, n_p).reshape(batch, seq, d)
    y_sample = _final_norm(x, final_gain, n_p, n_s).reshape(dec_batch, dec_seq, d)
    stack = lambda xs: jnp.stack(xs, axis=1)
    return (y_prompt, y_sample, stack(st_f), stack(st_b), stack(na_k), stack(na_v), stack(gq_k), stack(gq_v),
            stack(ml_c), stack(ml_r))
```

```python
import functools
import math

import numpy as np

import jax
import jax.numpy as jnp
from jax import lax
from jax.experimental import pallas as pl
from jax.experimental.pallas import tpu as pltpu

F32 = jnp.float32
BF16 = jnp.bfloat16

EPS = 1e-6
N_MOD = 9
GRID_W = 64
NA_KR = 8
NA_KC = 16
ROPE_THETA = 10000.0
LRU_C = 8.0
CONV_W = 4
MLA_NOPE = 128
MLA_V = 128
NEG_BIAS = -1e30

V7X_LANES = 128
V7X_SUBLANES = 8
MOD_ROWS = 16
TOKEN_TILE = 512
FF_TILE = 512
Q_CHUNK = 128
NA_Q_CHUNK = 256
NA_WIN = 768
LRU_COLS = 256


def _params(sem, vmem_mib):
    return pltpu.CompilerParams(dimension_semantics=sem, vmem_limit_bytes=vmem_mib * 1024 * 1024)


def _silu(x):
    return x * jax.nn.sigmoid(x)


def _rms(x, gain):
    ms = jnp.mean(x * x, axis=-1, keepdims=True)
    return x * lax.rsqrt(ms + EPS) * gain


def _mod_row(i, n_prompt_tiles, tiles_per_seq):
    return jnp.where(i < n_prompt_tiles, 0, 1 + (i - n_prompt_tiles) // tiles_per_seq)


def _adaln(x, mod_ref, gain_ref, row, mod_base):
    d = x.shape[1]
    shift = mod_ref[pl.ds(row, 1), pl.ds(mod_base * d, d)]
    scale = mod_ref[pl.ds(row, 1), pl.ds((mod_base + 1) * d, d)]
    return (_rms(x, gain_ref[...]) * (1 + scale) + shift).astype(BF16)


def _is_pow2(v):
    return math.frexp(v)[0] == 0.5


def _mod_kernel(cond_ref, w_ref, b_ref, o_ref):
    s = _silu(cond_ref[...]).astype(BF16)
    o_ref[...] = jnp.dot(s, w_ref[...].astype(BF16), preferred_element_type=F32) + b_ref[...]


def _modulation(cond, w_mod, b_mod):
    depth, d, n = w_mod.shape
    tn = 1024
    return pl.pallas_call(
        _mod_kernel,
        grid=(depth, n // tn),
        in_specs=[
            pl.BlockSpec((MOD_ROWS, d), lambda l, j: (0, 0)),
            pl.BlockSpec((None, d, tn), lambda l, j: (l, 0, j)),
            pl.BlockSpec((None, 1, tn), lambda l, j: (l, 0, j)),
        ],
        out_specs=pl.BlockSpec((None, MOD_ROWS, tn), lambda l, j: (l, 0, j)),
        out_shape=jax.ShapeDtypeStruct((depth, MOD_ROWS, n), F32),
        compiler_params=_params(("parallel", "parallel"), 40),
        name="modulation",
    )(cond, w_mod, b_mod.reshape(depth, 1, n))


def _ffn_kernel(*refs, mod_base, n_prompt_tiles, tiles_per_seq, cast_next, split_x):
    i = pl.program_id(0)
    j = pl.program_id(1)
    if split_x:
        xp_ref, xs_ref, *refs = refs
        read_x = lambda: jnp.where(i < n_prompt_tiles, xp_ref[...], xs_ref[...])
    else:
        x_ref, *refs = refs
        read_x = lambda: x_ref[...]
    mod_ref, gain_ref, wg_ref, wu_ref, wo_ref, *rest = refs
    if cast_next:
        nin_ref, nout_ref, o_ref, cin_ref, cout_ref, h_ref = rest
    else:
        o_ref, h_ref = rest
    d = o_ref.shape[1]
    nj = pl.num_programs(1)
    row = _mod_row(i, n_prompt_tiles, tiles_per_seq)

    def step(first, last):
        if first:
            h = _adaln(read_x(), mod_ref, gain_ref, row, mod_base)
            h_ref[...] = h
        else:
            h = h_ref[...]
        g = jnp.dot(h, wg_ref[...], preferred_element_type=F32)
        u = jnp.dot(h, wu_ref[...], preferred_element_type=F32)
        if cast_next:
            cin_ref[...] = nin_ref[...].astype(BF16)
            cout_ref[...] = nout_ref[...].astype(BF16)
        act = (_silu(g) * u).astype(BF16)
        acc = jnp.dot(act, wo_ref[...], preferred_element_type=F32)
        if not first:
            acc = o_ref[...] + acc
        if last:
            gate = mod_ref[pl.ds(row, 1), pl.ds((mod_base + 2) * d, d)]
            acc = read_x() + (0.5 * gate) * acc
        o_ref[...] = acc

    pl.when(j == 0)(lambda: step(True, False))
    pl.when((j > 0) & (j < nj - 1))(lambda: step(False, False))
    pl.when(j == nj - 1)(lambda: step(False, True))


def _ffn(x, mods, gains, w_in, w_out, layer, which, nxt, n_prompt_rows, seq_rows):
    split_x = isinstance(x, tuple)
    npt = n_prompt_rows // TOKEN_TILE
    if split_x:
        t, d = x[0].shape[0] + x[1].shape[0], x[0].shape[1]
        x_specs = [pl.BlockSpec((TOKEN_TILE, d), lambda i, j: (jnp.minimum(i, npt - 1), 0)),
                   pl.BlockSpec((TOKEN_TILE, d), lambda i, j: (jnp.maximum(i - npt, 0), 0))]
        x_args = list(x)
    else:
        t, d = x.shape
        x_specs = [pl.BlockSpec((TOKEN_TILE, d), lambda i, j: (i, 0))]
        x_args = [x]
    dff = w_out.shape[0]
    nj = dff // FF_TILE
    ni = t // TOKEN_TILE
    assert nj >= 2
    mod_base = 6 * which
    kern = functools.partial(_ffn_kernel, mod_base=mod_base, n_prompt_tiles=npt,
                             tiles_per_seq=seq_rows // TOKEN_TILE, cast_next=nxt is not None, split_x=split_x)
    in_specs = x_specs + [
        pl.BlockSpec((None, MOD_ROWS, N_MOD * d), lambda i, j: (layer, 0, 0)),
        pl.BlockSpec((None, 1, d), lambda i, j: (3 * layer + 2 * which, 0, 0)),
        pl.BlockSpec((d, FF_TILE), lambda i, j: (0, j)),
        pl.BlockSpec((d, FF_TILE), lambda i, j: (0, j + nj)),
        pl.BlockSpec((FF_TILE, d), lambda i, j: (j, 0)),
    ]
    args = x_args + [mods, gains, w_in, w_in, w_out]
    out_specs = [pl.BlockSpec((TOKEN_TILE, d), lambda i, j: (i, 0))]
    out_shape = [jax.ShapeDtypeStruct((t, d), F32)]
    if nxt is not None:
        f_in, f_out, nl, nw = nxt
        cast_rows = min(r for r in (16, 32, 64, 128, 256, 512)
                        if d % r == 0 and dff % r == 0 and d // r + dff // r <= ni * nj)
        n_in, n_out = d // cast_rows, dff // cast_rows
        in_blk = lambda i, j: jnp.minimum(i * nj + j, n_in - 1)
        out_blk = lambda i, j: jnp.clip(i * nj + j - n_in, 0, n_out - 1)
        in_specs += [pl.BlockSpec((None, None, cast_rows, 2 * dff), lambda i, j: (nl, nw, in_blk(i, j), 0)),
                     pl.BlockSpec((None, None, cast_rows, d), lambda i, j: (nl, nw, out_blk(i, j), 0))]
        args += [f_in, f_out]
        out_specs += [pl.BlockSpec((cast_rows, 2 * dff), lambda i, j: (in_blk(i, j), 0)),
                      pl.BlockSpec((cast_rows, d), lambda i, j: (out_blk(i, j), 0))]
        out_shape += [jax.ShapeDtypeStruct((d, 2 * dff), BF16), jax.ShapeDtypeStruct((dff, d), BF16)]
    return pl.pallas_call(
        kern,
        grid=(ni, nj),
        in_specs=in_specs,
        out_specs=out_specs,
        out_shape=out_shape,
        scratch_shapes=[pltpu.VMEM((TOKEN_TILE, d), BF16)],
        compiler_params=_params(("arbitrary", "arbitrary"), 56),
        name="ffn",
    )(*args)


def _inproj_kernel(x_ref, mod_ref, gain_ref, w_ref, o_ref, *, n_prompt_tiles, tiles_per_seq):
    row = _mod_row(pl.program_id(0), n_prompt_tiles, tiles_per_seq)
    h = _adaln(x_ref[...], mod_ref, gain_ref, row, 3)
    o_ref[...] = jnp.dot(h, w_ref[pl.program_id(1)], preferred_element_type=F32)


def _inproj(x, mods, gains, w, layer, w_idx, n_prompt_rows, seq_rows):
    t, d = x.shape
    ncb, tn = w.shape[1], w.shape[3]
    tm = TOKEN_TILE
    kern = functools.partial(_inproj_kernel, n_prompt_tiles=n_prompt_rows // tm, tiles_per_seq=seq_rows // tm)
    return pl.pallas_call(
        kern,
        grid=(t // tm, ncb),
        in_specs=[
            pl.BlockSpec((tm, d), lambda i, j: (i, 0)),
            pl.BlockSpec((None, MOD_ROWS, N_MOD * d), lambda i, j: (layer, 0, 0)),
            pl.BlockSpec((None, 1, d), lambda i, j: (3 * layer + 1, 0, 0)),
            pl.BlockSpec((None, ncb, d, tn), lambda i, j: (w_idx, 0, 0, 0), pipeline_mode=pl.Buffered(1)),
        ],
        out_specs=pl.BlockSpec((tm, tn), lambda i, j: (i, j)),
        out_shape=jax.ShapeDtypeStruct((t, ncb * tn), F32),
        compiler_params=_params(("parallel", "parallel"), 52),
        name="inproj",
    )(x, mods, gains, w)


def _outproj_kernel(x_ref, mod_ref, ap_ref, as_ref, bp_ref, bs_ref, w_ref, o_ref, *,
                    n_prompt_tiles, tiles_per_seq):
    i = pl.program_id(0)
    d = x_ref.shape[1]
    half = ap_ref.shape[1]
    is_prompt = i < n_prompt_tiles
    ya = jnp.where(is_prompt, ap_ref[...], as_ref[...])
    yb = jnp.where(is_prompt, bp_ref[...], bs_ref[...])
    y = jnp.dot(ya, w_ref[pl.ds(0, half), :], preferred_element_type=F32)
    y = y + jnp.dot(yb, w_ref[pl.ds(half, half), :], preferred_element_type=F32)
    row = _mod_row(i, n_prompt_tiles, tiles_per_seq)
    gate = mod_ref[pl.ds(row, 1), pl.ds(5 * d, d)]
    o_ref[...] = x_ref[...] + gate * y


def _outproj(x, mods, layer, ya_p, ya_s, yb_p, yb_s, w, w_idx, n_prompt_rows, seq_rows):
    t, d = x.shape
    half = ya_p.shape[1]
    npt = n_prompt_rows // TOKEN_TILE
    kern = functools.partial(_outproj_kernel, n_prompt_tiles=npt, tiles_per_seq=seq_rows // TOKEN_TILE)
    p_map = lambda i: (jnp.minimum(i, npt - 1), 0)
    s_map = lambda i: (jnp.maximum(i - npt, 0), 0)
    return pl.pallas_call(
        kern,
        grid=(t // TOKEN_TILE,),
        in_specs=[
            pl.BlockSpec((TOKEN_TILE, d), lambda i: (i, 0)),
            pl.BlockSpec((None, MOD_ROWS, N_MOD * d), lambda i: (layer, 0, 0)),
            pl.BlockSpec((TOKEN_TILE, half), p_map),
            pl.BlockSpec((TOKEN_TILE, half), s_map),
            pl.BlockSpec((TOKEN_TILE, half), p_map),
            pl.BlockSpec((TOKEN_TILE, half), s_map),
            pl.BlockSpec((None, 2 * half, d), lambda i: (w_idx, 0, 0)),
        ],
        out_specs=pl.BlockSpec((TOKEN_TILE, d), lambda i: (i, 0)),
        out_shape=jax.ShapeDtypeStruct((t, d), F32),
        compiler_params=_params(("parallel",), 48),
        name="outproj",
    )(x, mods, ya_p, ya_s, yb_p, yb_s, w)


def _final_norm_kernel(x_ref, g_ref, o_ref):
    o_ref[...] = _rms(x_ref[...], g_ref[...])


def _final_norm(x, gain, row_start, rows):
    d = x.shape[1]
    off = row_start // TOKEN_TILE
    return pl.pallas_call(
        _final_norm_kernel,
        grid=(rows // TOKEN_TILE,),
        in_specs=[pl.BlockSpec((TOKEN_TILE, d), lambda i: (i + off, 0)),
                  pl.BlockSpec((1, d), lambda i: (0, 0))],
        out_specs=pl.BlockSpec((TOKEN_TILE, d), lambda i: (i, 0)),
        out_shape=jax.ShapeDtypeStruct((rows, d), F32),
        compiler_params=_params(("parallel",), 32),
        name="final_norm",
    )(x, gain.reshape(1, d))


def _group_roll(x, step):
    rows, w = x.shape
    x3 = x.reshape(rows // V7X_SUBLANES, V7X_SUBLANES, w)
    return pltpu.roll(x3, step, axis=1).reshape(rows, w)


def _lru_kernel(xa_ref, ga_ref, h0f_ref, h0b_ref, cw_ref, cb_ref, wa_ref, ba_ref, wx_ref, bx_ref,
                lam_ref, y_ref, lf_ref, lb_ref, a_scr, u_scr, hf_scr, hb_scr):
    seq, w = xa_ref.shape
    groups = seq // V7X_SUBLANES
    xa = xa_ref[...]
    row = lax.broadcasted_iota(jnp.int32, (seq, w), 0)
    sub = row & (V7X_SUBLANES - 1)

    def tap(offset):
        if offset == 0:
            return xa
        shifted = pltpu.roll(xa, (-offset) % seq, axis=0)
        valid = row >= -offset if offset < 0 else row < seq - offset
        return jnp.where(valid, shifted, 0.0)

    xc = cb_ref[...]
    for j in range(CONV_W):
        xc = xc + tap(j - CONV_W // 2) * cw_ref[pl.ds(j, 1), :]
    xcb = xc.astype(BF16)

    for direction, (h0_ref, h_scr, last_ref) in enumerate(((h0f_ref, hf_scr, lf_ref),
                                                            (h0b_ref, hb_scr, lb_ref))):
        reverse = direction == 1
        r = jax.nn.sigmoid(jnp.dot(xcb, wa_ref[direction], preferred_element_type=F32) + ba_ref[direction])
        gi = jax.nn.sigmoid(jnp.dot(xcb, wx_ref[direction], preferred_element_type=F32) + bx_ref[direction])
        neg_lam = -lam_ref[direction]
        softplus = jnp.maximum(neg_lam, 0.0) + jnp.log1p(jnp.exp(-jnp.abs(neg_lam)))
        log_a = r * (-LRU_C * softplus)
        a = jnp.exp(log_a)
        m2 = (1.0 - a) * (1.0 + a)
        u = jnp.where(m2 > 0.0, m2 * lax.rsqrt(m2), 0.0) * (gi * xc)
        for step in (1, 2, 4):
            if reverse:
                a_nb = _group_roll(a, V7X_SUBLANES - step)
                u_nb = _group_roll(u, V7X_SUBLANES - step)
                valid = sub < V7X_SUBLANES - step
            else:
                a_nb = _group_roll(a, step)
                u_nb = _group_roll(u, step)
                valid = sub >= step
            u = a * jnp.where(valid, u_nb, 0.0) + u
            a = a * jnp.where(valid, a_nb, 1.0)
        a_scr[...] = a
        u_scr[...] = u

        def carry_step(g, carry, reverse=reverse, h_scr=h_scr):
            gg = groups - 1 - g if reverse else g
            off = pl.multiple_of(gg * V7X_SUBLANES, V7X_SUBLANES)
            h = a_scr[pl.ds(off, V7X_SUBLANES), :] * carry + u_scr[pl.ds(off, V7X_SUBLANES), :]
            h_scr[pl.ds(off, V7X_SUBLANES), :] = h
            return h[0:1] if reverse else h[V7X_SUBLANES - 1:V7X_SUBLANES]

        last_ref[...] = lax.fori_loop(0, groups, carry_step, h0_ref[...], unroll=4)

    y_ref[...] = ((hf_scr[...] + hb_scr[...]) * jax.nn.gelu(ga_ref[...])).astype(BF16)


def _lru(p_ab, h0f, h0b, prm, seq_len, n_seq, row_start):
    lru_w = h0f.shape[-1]
    cw = LRU_COLS
    ncb = lru_w // cw
    sb = row_start // seq_len
    vec = lambda: pl.BlockSpec((None, 1, cw), lambda s, c: (s, 0, c))
    par2 = lambda: pl.BlockSpec((None, 2, 1, cw), lambda s, c: (prm["idx"], 0, 0, c))
    gate = lambda: pl.BlockSpec((None, 2, None, cw, cw), lambda s, c: (prm["idx"], 0, c, 0, 0))
    return pl.pallas_call(
        _lru_kernel,
        grid=(n_seq, ncb),
        in_specs=[
            pl.BlockSpec((seq_len, cw), lambda s, c: (s + sb, c)),
            pl.BlockSpec((seq_len, cw), lambda s, c: (s + sb, c + ncb)),
            vec(), vec(),
            pl.BlockSpec((None, CONV_W, cw), lambda s, c: (prm["idx"], 0, c)),
            pl.BlockSpec((None, 1, cw), lambda s, c: (prm["idx"], 0, c)),
            gate(), par2(), gate(), par2(), par2(),
        ],
        out_specs=[
            pl.BlockSpec((seq_len, cw), lambda s, c: (s, c)),
            vec(), vec(),
        ],
        out_shape=[
            jax.ShapeDtypeStruct((n_seq * seq_len, lru_w), BF16),
            jax.ShapeDtypeStruct((n_seq, 1, lru_w), F32),
            jax.ShapeDtypeStruct((n_seq, 1, lru_w), F32),
        ],
        scratch_shapes=[pltpu.VMEM((seq_len, cw), F32)] * 4,
        compiler_params=_params(("parallel", "parallel"), 40),
        name="lru",
    )(p_ab, p_ab, h0f.reshape(n_seq, 1, lru_w), h0b.reshape(n_seq, 1, lru_w), prm["conv_w"],
      prm["conv_b"], prm["wa"], prm["ba"], prm["wx"], prm["bx"], prm["lam"])


def _block_diag_tiles(w, tile):
    n, two, nb, bw, _ = w.shape
    per = tile // bw
    w = w.reshape(n, two, nb // per, per, bw, bw)
    eye = jnp.eye(per, dtype=w.dtype)
    out = jnp.einsum("ndgpij,pq->ndgpiqj", w, eye)
    return out.reshape(n, two, nb // per, tile, tile)


def _softmax_attend(q, ks, vs, biases, scale):
    scores = []
    for k, b in zip(ks, biases):
        s = lax.dot_general(q, k, (((1,), (1,)), ((), ())), preferred_element_type=F32)
        if scale is not None:
            s = s * scale
        scores.append(s if b is None else s + b)
    m = scores[0].max(axis=-1, keepdims=True)
    for s in scores[1:]:
        m = jnp.maximum(m, s.max(axis=-1, keepdims=True))
    denom = None
    out = None
    for s, v in zip(scores, vs):
        e = jnp.exp(s - m)
        part = e.sum(axis=-1, keepdims=True)
        denom = part if denom is None else denom + part
        o = jnp.dot(e.astype(BF16), v, preferred_element_type=F32)
        out = o if out is None else out + o
    return out / denom


def _upper_half(shape):
    lane = lax.broadcasted_iota(jnp.int32, shape, len(shape) - 1)
    return lane % V7X_LANES >= V7X_LANES // 2


def _split_pair(q):
    up = _upper_half(q.shape)
    zero = jnp.zeros_like(q)
    return jnp.concatenate([jnp.where(up, zero, q), jnp.where(up, q, zero)], axis=0)


def _merge_pair(o):
    m = o.shape[0] // 2
    return jnp.where(_upper_half((m, o.shape[1])), o[m:], o[:m])


def _dup_head(block, use_high):
    swapped = pltpu.roll(block, V7X_LANES // 2, axis=1)
    keep = _upper_half(block.shape) == use_high
    return jnp.where(keep, block, swapped)


def _fold_scale(q, scale):
    return (q * scale, None) if _is_pow2(scale) else (q, scale)


def _na_ctx_kernel(q_ref, k_ref, v_ref, o_ref, *, scale):
    for pair in range(q_ref.shape[1] // V7X_LANES):
        sl = pl.ds(pair * V7X_LANES, V7X_LANES)
        q, sc = _fold_scale(q_ref[:, sl], scale)
        o = _softmax_attend(_split_pair(q.astype(BF16)), [k_ref[:, sl].astype(BF16)],
                            [v_ref[:, sl].astype(BF16)], [None], sc)
        o_ref[:, sl] = _merge_pair(o).astype(o_ref.dtype)


def _na_ctx(p_ab, n_seq, seq_len, q_blk, na_w, scale):
    blk = lambda col: pl.BlockSpec((seq_len, na_w), lambda b: (b, col))
    return pl.pallas_call(
        functools.partial(_na_ctx_kernel, scale=scale),
        grid=(n_seq,),
        in_specs=[blk(q_blk), blk(q_blk + 1), blk(q_blk + 2)],
        out_specs=pl.BlockSpec((seq_len, na_w), lambda b: (b, 0)),
        out_shape=jax.ShapeDtypeStruct((n_seq * seq_len, na_w), BF16),
        compiler_params=_params(("parallel",), 40),
        name="na_ctx",
    )(p_ab, p_ab, p_ab)


def _na_tile_index(seq_len):
    rows_n = seq_len // GRID_W
    kr = min(NA_KR, rows_n)
    rows_per_chunk = NA_Q_CHUNK // GRID_W
    n_chunks = rows_n // rows_per_chunk
    win_rows = NA_WIN // GRID_W
    table = []
    for r in range(rows_n):
        r_start = min(max(r - kr // 2, 0), rows_n - kr)
        base = 0 if r // rows_per_chunk < n_chunks // 2 else rows_n - win_rows
        table.append([k - r + NA_KR - 1 if r_start <= k < r_start + kr else 2 * NA_KR - 1
                      for k in range(base, base + win_rows)])
    return table


def _na_lat_kernel(q_ref, k_ref, v_ref, kc_ref, vc_ref, tiles_ref, o_ref, bias_ref, *, scale):
    lq = q_ref.shape[0]
    n_chunks = lq // NA_Q_CHUNK
    rows_per_chunk = NA_Q_CHUNK // GRID_W

    @pl.when(pl.program_id(1) == 0)
    def _():
        low = lax.broadcasted_iota(jnp.int32, (GRID_W, V7X_LANES), 1) < GRID_W
        for half in (0, 1):
            for r, slots in enumerate(_na_tile_index(lq)):
                c, rr = divmod(r, rows_per_chunk)
                rows = pl.ds(half * NA_Q_CHUNK + rr * GRID_W, GRID_W)
                for s in range(0, len(slots), 2):
                    blk = jnp.where(low, tiles_ref[half, slots[s]], tiles_ref[half, slots[s + 1]])
                    bias_ref[c, rows, pl.ds(s * GRID_W, V7X_LANES)] = blk

    kcb = kc_ref[...].astype(BF16)
    vcb = vc_ref[...].astype(BF16)
    for c in range(n_chunks):
        win = 0 if c < n_chunks // 2 else lq - NA_WIN
        kb = k_ref[pl.ds(win, NA_WIN), :].astype(BF16)
        vb = v_ref[pl.ds(win, NA_WIN), :].astype(BF16)
        rows = pl.ds(c * NA_Q_CHUNK, NA_Q_CHUNK)
        q, sc = _fold_scale(q_ref[rows, :], scale)
        o = _softmax_attend(_split_pair(q.astype(BF16)), [kb, kcb], [vb, vcb], [bias_ref[c], None], sc)
        o_ref[rows, :] = _merge_pair(o).astype(o_ref.dtype)


def _na_lat(p_ab, cache_k, cache_v, tiles, idx, n_seq, seq_len, row_start, q_col, n_pairs, scale):
    sb = row_start // seq_len
    past = cache_k.shape[2]
    blk = lambda col: pl.BlockSpec((seq_len, V7X_LANES), lambda h, b: (b + sb, col + h))
    cblk = lambda: pl.BlockSpec((None, None, past, V7X_LANES), lambda h, b: (b, idx, 0, h))
    n_chunks = seq_len // NA_Q_CHUNK
    return pl.pallas_call(
        functools.partial(_na_lat_kernel, scale=scale),
        grid=(n_pairs, n_seq),
        in_specs=[blk(q_col), blk(q_col + n_pairs), blk(q_col + 2 * n_pairs), cblk(), cblk(),
                  pl.BlockSpec((None, 2, 2 * NA_KR, GRID_W, V7X_LANES), lambda h, b: (idx, h, 0, 0, 0))],
        out_specs=pl.BlockSpec((seq_len, V7X_LANES), lambda h, b: (b, h)),
        out_shape=jax.ShapeDtypeStruct((n_seq * seq_len, n_pairs * V7X_LANES), BF16),
        scratch_shapes=[pltpu.VMEM((n_chunks, 2 * NA_Q_CHUNK, NA_WIN), F32)],
        compiler_params=_params(("parallel", "arbitrary"), 56),
        name="na_lat",
    )(p_ab, p_ab, p_ab, cache_k, cache_v, tiles)


def _na_bias_tiles(tab):
    lead = tab.shape[:-1]
    edge = GRID_W - NA_KC
    vec = jnp.pad(tab, [(0, 0)] * len(lead) + [(edge, edge + 1)], mode="edge")
    skew = jnp.broadcast_to(vec[..., None, :], lead + (GRID_W, 2 * GRID_W))
    skew = skew.reshape(lead + (2 * GRID_W * GRID_W,))[..., :GRID_W * (2 * GRID_W - 1)]
    toep = skew.reshape(lead + (GRID_W, 2 * GRID_W - 1))[..., GRID_W - 1:]
    cols = np.arange(GRID_W)
    c_start = np.clip(cols - NA_KC // 2, 0, GRID_W - NA_KC)
    col_ok = (cols[None, :] >= c_start[:, None]) & (cols[None, :] < c_start[:, None] + NA_KC)
    tiles = jnp.where(col_ok, toep, NEG_BIAS)
    masked = jnp.full(lead[:-1] + (1, GRID_W, GRID_W), NEG_BIAS, F32)
    tiles = jnp.concatenate([tiles, masked], axis=-3)
    return jnp.concatenate([tiles, tiles], axis=-1)


def _split_dot(v, m):
    hi = v.astype(BF16)
    lo = (v - hi.astype(F32)).astype(BF16)
    return jnp.dot(hi, m, preferred_element_type=F32) + jnp.dot(lo, m, preferred_element_type=F32)


def _head_rms(x, gather, spread, gain, head_dim):
    inv = lax.rsqrt(_split_dot(x * x, gather) * (1.0 / head_dim) + EPS)
    return x * _split_dot(inv, spread) * gain


def _rope(x, cos, sin_lo, sin_hi):
    w = x.shape[1]
    reps = w // V7X_LANES
    tile = lambda t: t if reps == 1 else jnp.concatenate([t] * reps, axis=1)
    quarter = 16
    return (x * tile(cos) + pltpu.roll(x, w - quarter, axis=1) * tile(sin_lo)
            + pltpu.roll(x, quarter, axis=1) * tile(sin_hi))


def _cd_post_kernel(p_ref, gat_ref, spr_ref, qg_ref, kg_ref, mqg_ref, mkg_ref, wuq_ref, wukv_ref, *rest,
                    rope, head_dim, q_w, kv_w, rank, q_scale):
    if rope:
        cos_ref, slo_ref, shi_ref = rest[:3]
        rest = rest[3:]
        tabs = (cos_ref[...], slo_ref[...], shi_ref[...])
    q_ref, k_ref, qn_ref, qr_ref, ckv_ref, kn_ref, vm_ref, kr_ref = rest
    c0 = 0
    qc = p_ref[:, pl.ds(c0, q_w)]
    c0 += q_w
    kc = p_ref[:, pl.ds(c0, kv_w)]
    c0 += 2 * kv_w
    qa = p_ref[:, pl.ds(c0, rank)]
    c0 += rank
    ckv = p_ref[:, pl.ds(c0, rank)]
    c0 += rank
    kr = p_ref[:, pl.ds(c0, V7X_LANES)]

    qc = _head_rms(qc, gat_ref[...], spr_ref[...], qg_ref[...], head_dim)
    kc = _head_rms(kc, gat_ref[pl.ds(0, kv_w), :], spr_ref[:, pl.ds(0, kv_w)], kg_ref[...], head_dim)
    qd = jnp.dot(_rms(qa, mqg_ref[...]).astype(BF16), wuq_ref[...], preferred_element_type=F32)
    n_nope = qn_ref.shape[1]
    qn = qd[:, :n_nope]
    qr = qd[:, n_nope:]
    ckv = _rms(ckv, mkg_ref[...])
    if rope:
        qc = _rope(qc, *tabs)
        kc = _rope(kc, *tabs)
        qr = _rope(qr, *tabs)
        kr = _rope(kr, *tabs)
    q_ref[...] = (qc * q_scale).astype(BF16)
    k_ref[...] = kc
    qn_ref[...] = qn.astype(BF16)
    qr_ref[...] = qr.astype(BF16)
    ckv_ref[...] = ckv
    kv = jnp.dot(ckv.astype(BF16), wukv_ref[...], preferred_element_type=F32)
    kn_ref[...] = kv[:, :n_nope].astype(BF16)
    vm_ref[...] = kv[:, n_nope:].astype(BF16)
    kr_ref[...] = kr


def _cd_post(p_cd, prm, row_start, rows, rope_tabs, dims, q_scale):
    q_w, kv_w, rank, head_dim, n_nope, n_rope, n_v = dims
    tm = 256
    off = row_start // tm
    n_in = p_cd.shape[1]
    idx = prm["idx"]
    rope = rope_tabs is not None
    const = lambda shape: pl.BlockSpec(shape, lambda i: (0,) * len(shape))
    layer = lambda shape: pl.BlockSpec((None,) + shape, lambda i: (idx,) + (0,) * len(shape))
    in_specs = [
        pl.BlockSpec((tm, n_in), lambda i: (i + off, 0)),
        const((q_w, V7X_LANES)), const((V7X_LANES, q_w)),
        layer((1, q_w)), layer((1, kv_w)), layer((1, rank)), layer((1, rank)),
        layer((rank, n_nope + n_rope)), layer((rank, n_nope + n_v)),
    ]
    args = [p_cd, prm["gather"], prm["spread"], prm["q_gain"], prm["k_gain"], prm["mla_q_gain"],
            prm["mla_kv_gain"], prm["w_uq"], prm["w_ukv"]]
    if rope:
        seq_tiles = rope_tabs[0].shape[0] // tm
        in_specs += [pl.BlockSpec((tm, V7X_LANES), lambda i: (i % seq_tiles, 0))] * 3
        args += list(rope_tabs)
    widths = [(q_w, BF16), (kv_w, F32), (n_nope, BF16), (n_rope, BF16), (rank, F32), (n_nope, BF16),
              (n_v, BF16), (V7X_LANES, F32)]
    kern = functools.partial(_cd_post_kernel, rope=rope, head_dim=head_dim, q_w=q_w, kv_w=kv_w, rank=rank,
                             q_scale=q_scale)
    return pl.pallas_call(
        kern,
        grid=(rows // tm,),
        in_specs=in_specs,
        out_specs=[pl.BlockSpec((tm, w), lambda i: (i, 0)) for w, _ in widths],
        out_shape=[jax.ShapeDtypeStruct((rows, w), dt) for w, dt in widths],
        compiler_params=_params(("parallel",), 48),
        name="cd_post",
    )(*args)


def _rope_tables(seq_len, head_dim):
    half = head_dim // 2
    nf = half // 2
    t = np.arange(seq_len)
    inv_freq = (1.0 / (ROPE_THETA ** (np.arange(nf, dtype=np.float32) / nf))).astype(np.float32)
    zeros = np.zeros((seq_len, nf), np.float32)
    cos, slo, shi = [], [], []
    for pos in (t // GRID_W, t % GRID_W):
        ang = pos.astype(np.float32)[:, None] * inv_freq[None, :]
        c, s = np.cos(ang).astype(np.float32), np.sin(ang).astype(np.float32)
        cos += [c, c]
        slo += [-s, zeros]
        shi += [zeros, s]
    reps = V7X_LANES // head_dim
    return tuple(jnp.asarray(np.tile(np.concatenate(x, axis=1), (1, reps))) for x in (cos, slo, shi))


def _ckv_up_kernel(c_ref, w_ref, kn_ref, vm_ref):
    kv = jnp.dot(c_ref[...].astype(BF16), w_ref[...], preferred_element_type=F32)
    n = kn_ref.shape[1]
    kn_ref[...] = kv[:, :n].astype(BF16)
    vm_ref[...] = kv[:, n:].astype(BF16)


def _ckv_up(ckv, w_ukv, idx, n_nope):
    n_seq, _, tm, rank = ckv.shape
    rows = n_seq * tm
    n = w_ukv.shape[2]
    return pl.pallas_call(
        _ckv_up_kernel,
        grid=(rows // tm,),
        in_specs=[pl.BlockSpec((None, None, tm, rank), lambda i: (i, idx, 0, 0)),
                  pl.BlockSpec((None, rank, n), lambda i: (idx, 0, 0))],
        out_specs=[pl.BlockSpec((tm, n_nope), lambda i: (i, 0)), pl.BlockSpec((tm, n - n_nope), lambda i: (i, 0))],
        out_shape=[jax.ShapeDtypeStruct((rows, n_nope), BF16), jax.ShapeDtypeStruct((rows, n - n_nope), BF16)],
        compiler_params=_params(("parallel",), 32),
        name="ckv_up",
    )(ckv, w_ukv)


def _gqa_kernel(q_ref, k_ref, v_ref, *rest, scale, has_cache, group_w):
    if has_cache:
        kc_ref, vc_ref, o_ref = rest
    else:
        (o_ref,) = rest
    lq = q_ref.shape[0]
    n_local = q_ref.shape[1] // group_w
    n_pairs = group_w // V7X_LANES
    for hl in range(n_local):
        if n_local == 1:
            high = (pl.program_id(1) % 2) == 1
            pick = lambda ref: ref[...]
        else:
            high = hl % 2 == 1
            pick = lambda ref, hl=hl: ref[:, pl.ds((hl // 2) * V7X_LANES, V7X_LANES)]
        ks = [_dup_head(pick(k_ref), high).astype(BF16)]
        vs = [_dup_head(pick(v_ref), high).astype(BF16)]
        if has_cache:
            ks.append(_dup_head(pick(kc_ref), high).astype(BF16))
            vs.append(_dup_head(pick(vc_ref), high).astype(BF16))
        for c in range(lq // Q_CHUNK):
            rows = pl.ds(c * Q_CHUNK, Q_CHUNK)
            cols = [pl.ds(hl * group_w + p * V7X_LANES, V7X_LANES) for p in range(n_pairs)]
            q = jnp.concatenate([_split_pair(q_ref[rows, cl]) for cl in cols], axis=0)
            o = _softmax_attend(q, ks, vs, [None] * len(ks), scale)
            for p, cl in enumerate(cols):
                o_ref[rows, cl] = _merge_pair(o[2 * p * Q_CHUNK:2 * (p + 1) * Q_CHUNK]).astype(o_ref.dtype)


def _gqa(q, k, p_cd, v_col, cache_k, cache_v, idx, n_seq, seq_len, row_start, n_kv, heads_per_step, scale):
    sb = row_start // seq_len
    group_w = q.shape[1] // n_kv
    has_cache = cache_k is not None
    if heads_per_step == 1:
        kv_w, kv_blk, v_blk = V7X_LANES, (lambda h: h // 2), (lambda h: v_col + h // 2)
    else:
        kv_w = k.shape[1]
        kv_blk, v_blk = (lambda h: 0), (lambda h: v_col * V7X_LANES // kv_w)
    in_specs = [
        pl.BlockSpec((seq_len, group_w * heads_per_step), lambda b, h: (b, h)),
        pl.BlockSpec((seq_len, kv_w), lambda b, h: (b, kv_blk(h))),
        pl.BlockSpec((seq_len, kv_w), lambda b, h: (b + sb, v_blk(h))),
    ]
    args = [q, k, p_cd]
    if has_cache:
        past = cache_k.shape[2]
        in_specs += [pl.BlockSpec((None, None, past, kv_w), lambda b, h: (b, idx, 0, kv_blk(h)))] * 2
        args += [cache_k, cache_v]
    return pl.pallas_call(
        functools.partial(_gqa_kernel, scale=scale, has_cache=has_cache, group_w=group_w),
        grid=(n_seq, n_kv // heads_per_step),
        in_specs=in_specs,
        out_specs=pl.BlockSpec((seq_len, group_w * heads_per_step), lambda b, h: (b, h)),
        out_shape=jax.ShapeDtypeStruct(q.shape, BF16),
        compiler_params=_params(("parallel", "parallel"), 56),
        name="gqa",
    )(*args)


def _mla_kernel(qn_ref, qr_ref, kn_ref, kr_ref, v_ref, *rest, scale, has_cache):
    if has_cache:
        knc_ref, krc_ref, vc_ref, o_ref = rest
        krc = krc_ref[...].astype(BF16)
    else:
        (o_ref,) = rest
    lq = qn_ref.shape[0]
    qc = min(2 * Q_CHUNK, lq)
    n_local = qn_ref.shape[1] // V7X_LANES
    kr = kr_ref[...].astype(BF16)
    for hl in range(n_local):
        sl = pl.ds(hl * V7X_LANES, V7X_LANES)
        if n_local == 1:
            high = (pl.program_id(1) % 2) == 1
            pair = pl.ds(0, V7X_LANES)
        else:
            high = hl % 2 == 1
            pair = pl.ds((hl // 2) * V7X_LANES, V7X_LANES)
        ks = [jnp.concatenate([kn_ref[:, sl], kr], axis=1)]
        vs = [v_ref[:, sl]]
        if has_cache:
            ks.append(jnp.concatenate([knc_ref[:, sl], krc], axis=1))
            vs.append(vc_ref[:, sl])
        for c in range(lq // qc):
            rows = pl.ds(c * qc, qc)
            qr = qr_ref[rows, pair]
            qr = jnp.where(_upper_half(qr.shape) == high, qr, jnp.zeros_like(qr))
            q = jnp.concatenate([qn_ref[rows, sl], qr], axis=1)
            o_ref[rows, sl] = _softmax_attend(q, ks, vs, [None] * len(ks), scale).astype(o_ref.dtype)


def _mla(qn, qr, kn, kr2, vm, cache, n_seq, seq_len, heads_per_step, scale):
    has_cache = cache is not None
    n_heads = vm.shape[1] // V7X_LANES
    hw = heads_per_step * V7X_LANES
    head = lambda rows: pl.BlockSpec((rows, hw), lambda b, h: (b, h))
    if heads_per_step == 1:
        rope_q = pl.BlockSpec((seq_len, V7X_LANES), lambda b, h: (b, h // 2))
    else:
        rope_q = pl.BlockSpec((seq_len, qr.shape[1]), lambda b, h: (b, 0))
    in_specs = [head(seq_len), rope_q, head(seq_len),
                pl.BlockSpec((seq_len, V7X_LANES), lambda b, h: (b, 0)),
                head(seq_len)]
    args = [qn, qr, kn, kr2, vm]
    if has_cache:
        knc, krc, vmc, idx = cache
        past = krc.shape[2]
        in_specs += [head(past),
                     pl.BlockSpec((None, None, past, V7X_LANES), lambda b, h: (b, idx, 0, 0)),
                     head(past)]
        args += [knc, krc, vmc]
    return pl.pallas_call(
        functools.partial(_mla_kernel, scale=scale, has_cache=has_cache),
        grid=(n_seq, n_heads // heads_per_step),
        in_specs=in_specs,
        out_specs=head(seq_len),
        out_shape=jax.ShapeDtypeStruct(vm.shape, BF16),
        compiler_params=_params(("parallel", "parallel"), 48),
        name="mla",
    )(*args)


def kernel(x_prompt, x_sample, state_lru_fwd, state_lru_bwd, cache_na_k, cache_na_v, cache_gqa_k, cache_gqa_v, cache_mla_ckv, cache_mla_krope, c, c_ctx, w_mod, b_mod, norm_gain, w_ffn_in, w_ffn_out, w_in_ab, conv_w, conv_b, lru_wa, lru_ba, lru_wx, lru_bx, lru_lambda, na_bias, w_out_ab, w_in_cd, gqa_q_gain, gqa_k_gain, mla_q_gain, mla_kv_gain, mla_w_uq, mla_w_uk, mla_w_uv, w_out_cd, final_gain):
    batch, seq, d = x_prompt.shape
    dec_batch, dec_seq, _ = x_sample.shape
    depth = w_mod.shape[0]
    n_even, n_odd = w_in_ab.shape[0], w_in_cd.shape[0]
    n_p = batch * seq
    n_s = dec_batch * dec_seq
    past = cache_na_k.shape[2]
    lru_w = state_lru_fwd.shape[-1]
    na_heads, na_dh = cache_na_k.shape[3], cache_na_k.shape[4]
    na_w = na_heads * na_dh
    gqa_kv, gqa_dh = cache_gqa_k.shape[3], cache_gqa_k.shape[4]
    kv_w = gqa_kv * gqa_dh
    rank = cache_mla_ckv.shape[-1]
    rope_w = cache_mla_krope.shape[-1]
    q_w = w_in_cd.shape[2] - 2 * kv_w - 2 * rank - rope_w
    mla_heads = mla_w_uk.shape[2] // MLA_NOPE
    n_nope = mla_heads * MLA_NOPE
    n_rope = mla_heads * rope_w
    n_v = mla_heads * MLA_V
    mla_qk = MLA_NOPE + rope_w
    assert mla_q_gain.shape[-1] == rank and 2 * rope_w == V7X_LANES
    assert n_p % dec_seq == 0 and dec_seq % TOKEN_TILE == 0 and seq % Q_CHUNK == 0
    assert dec_seq // GRID_W == 2 * NA_KR and na_dh == GRID_W and gqa_dh == GRID_W
    assert w_in_ab.shape[2] == 2 * lru_w + 3 * na_w and lru_w == na_w
    assert (q_w // gqa_kv) % V7X_LANES == 0 and q_w % kv_w == 0

    x = (x_prompt.reshape(n_p, d), x_sample.reshape(n_s, d))
    cond = jnp.concatenate([c_ctx[None, :], c, jnp.zeros((MOD_ROWS - 1 - dec_batch, d), F32)], axis=0)
    mods = _modulation(cond, w_mod, b_mod)
    gains = norm_gain.reshape(depth * 3, 1, d)

    ffn_w = (w_ffn_in[0, 0].astype(BF16), w_ffn_out[0, 0].astype(BF16))

    def ffn_step(x, ffn_w, layer, which):
        last = layer == depth - 1 and which == 1
        nxt = None if last else (w_ffn_in, w_ffn_out, layer + which, 1 - which)
        out = _ffn(x, mods, gains, ffn_w[0], ffn_w[1], layer, which, nxt, n_p, dec_seq)
        return out[0], tuple(out[1:])

    ab_blocks = 2
    w_in_ab_bf = w_in_ab.astype(BF16).reshape(n_even, d, ab_blocks, -1).transpose(0, 2, 1, 3)
    w_out_ab_bf = w_out_ab.astype(BF16)
    w_in_cd_bf = jnp.concatenate([w_in_cd, w_in_cd[:, :, -rope_w:]], axis=2).astype(BF16)[:, None]
    w_out_cd_bf = w_out_cd.astype(BF16)

    lru_prm = {
        "conv_w": conv_w, "conv_b": conv_b.reshape(n_even, 1, lru_w),
        "wa": _block_diag_tiles(lru_wa, LRU_COLS).astype(BF16),
        "wx": _block_diag_tiles(lru_wx, LRU_COLS).astype(BF16),
        "ba": lru_ba.reshape(n_even, 2, 1, lru_w), "bx": lru_bx.reshape(n_even, 2, 1, lru_w),
        "lam": lru_lambda.reshape(n_even, 2, 1, lru_w),
    }
    zeros_state = jnp.zeros((batch, lru_w), F32)
    na_k_ctx = cache_na_k.reshape(dec_batch, n_even, past, na_w)
    na_v_ctx = cache_na_v.reshape(dec_batch, n_even, past, na_w)
    na_tiles = _na_bias_tiles(na_bias)

    w_uq = mla_w_uq.reshape(n_odd, rank, mla_heads, mla_qk)
    w_uq = jnp.concatenate([w_uq[..., :MLA_NOPE].reshape(n_odd, rank, n_nope),
                            w_uq[..., MLA_NOPE:].reshape(n_odd, rank, n_rope)], axis=2).astype(BF16)
    gqa_scale = gqa_dh ** -0.5
    q_scale = gqa_scale if _is_pow2(gqa_scale) else 1.0
    head_of_lane = np.arange(q_w) // gqa_dh
    gather = (head_of_lane[:, None] == np.arange(V7X_LANES)[None, :]).astype(np.float32)
    assert q_w // gqa_dh <= V7X_LANES
    cd_prm = {
        "gather": jnp.asarray(gather, BF16), "spread": jnp.asarray(gather.T, BF16),
        "q_gain": jnp.tile(gqa_q_gain, (1, q_w // gqa_dh)).reshape(n_odd, 1, q_w),
        "k_gain": jnp.tile(gqa_k_gain, (1, gqa_kv)).reshape(n_odd, 1, kv_w),
        "mla_q_gain": mla_q_gain.reshape(n_odd, 1, rank),
        "mla_kv_gain": mla_kv_gain.reshape(n_odd, 1, rank),
        "w_uq": w_uq,
        "w_ukv": jnp.concatenate([mla_w_uk, mla_w_uv], axis=2).astype(BF16),
    }
    rope_tabs = _rope_tables(dec_seq, gqa_dh)
    gqa_k_ctx = cache_gqa_k.reshape(dec_batch, n_odd, past, kv_w)
    gqa_v_ctx = cache_gqa_v.reshape(dec_batch, n_odd, past, kv_w)
    krope_ctx = jnp.concatenate([cache_mla_krope, cache_mla_krope], axis=-1)

    st_f, st_b, na_k, na_v, gq_k, gq_v, ml_c, ml_r = [], [], [], [], [], [], [], []
    for layer in range(depth):
        jdx = layer // 2
        x, ffn_w = ffn_step(x, ffn_w, layer, 0)
        if layer % 2 == 0:
            p_ab = _inproj(x, mods, gains, w_in_ab_bf, layer, jdx, n_p, dec_seq)
            prm = dict(lru_prm, idx=jdx)
            ya_p, lf, lb = _lru(p_ab, zeros_state, zeros_state, prm, seq, batch, 0)
            ya_s, _, _ = _lru(p_ab, state_lru_fwd[:, jdx], state_lru_bwd[:, jdx], prm, dec_seq, dec_batch, n_p)
            scale = na_dh ** -0.5
            yb_p = _na_ctx(p_ab, batch, seq, 2 * lru_w // na_w, na_w, scale)
            yb_s = _na_lat(p_ab, na_k_ctx, na_v_ctx, na_tiles, jdx, dec_batch, dec_seq, n_p,
                           2 * lru_w // V7X_LANES, na_w // V7X_LANES, scale)
            x = _outproj(x, mods, layer, ya_p, ya_s, yb_p, yb_s, w_out_ab_bf, jdx, n_p, dec_seq)
            st_f.append(lf.reshape(batch, lru_w))
            st_b.append(lb.reshape(batch, lru_w))
            na_k.append(p_ab[:n_p, 2 * lru_w + na_w:2 * lru_w + 2 * na_w].reshape(batch, seq, na_heads, na_dh))
            na_v.append(p_ab[:n_p, 2 * lru_w + 2 * na_w:].reshape(batch, seq, na_heads, na_dh))
        else:
            p_cd = _inproj(x, mods, gains, w_in_cd_bf, layer, jdx, n_p, dec_seq)
            prm = dict(cd_prm, idx=jdx)
            dims = (q_w, kv_w, rank, gqa_dh, n_nope, n_rope, n_v)
            qp, kp, qnp_, qrp, ckvp, knp_, vmp, krp = _cd_post(p_cd, prm, 0, n_p, None, dims, q_scale)
            qs, ks_, qns, qrs, _, kns, vms, krs = _cd_post(p_cd, prm, n_p, n_s, rope_tabs, dims, q_scale)
            v_col = (q_w + kv_w) // V7X_LANES
            att_scale = None if q_scale != 1.0 else gqa_scale
            yc_p = _gqa(qp, kp, p_cd, v_col, None, None, jdx, batch, seq, 0, gqa_kv, gqa_kv, att_scale)
            yc_s = _gqa(qs, ks_, p_cd, v_col, gqa_k_ctx, gqa_v_ctx, jdx, dec_batch, dec_seq, n_p,
                        gqa_kv, 1, att_scale)
            knc, vmc = _ckv_up(cache_mla_ckv, cd_prm["w_ukv"], jdx, n_nope)
            yd_p = _mla(qnp_, qrp, knp_, krp, vmp, None, batch, seq, mla_heads, mla_qk ** -0.5)
            yd_s = _mla(qns, qrs, kns, krs, vms, (knc, krope_ctx, vmc, jdx), dec_batch, dec_seq, 1,
                        mla_qk ** -0.5)
            x = _outproj(x, mods, layer, yc_p, yc_s, yd_p, yd_s, w_out_cd_bf, jdx, n_p, dec_seq)
            gq_k.append(kp.reshape(batch, seq, gqa_kv, gqa_dh))
            gq_v.append(p_cd[:n_p, q_w + kv_w:q_w + 2 * kv_w].reshape(batch, seq, gqa_kv, gqa_dh))
            ml_c.append(ckvp.reshape(batch, seq, rank))
            ml_r.append(krp[:, :rope_w].reshape(batch, seq, rope_w))
        x, ffn_w = ffn_step(x, ffn_w, layer, 1)

    y_prompt = _final_norm(x, final_gain, 0, n_p).reshape(batch, seq, d)
    y_sample = _final_norm(x, final_gain, n_p, n_s).reshape(dec_batch, dec_seq, d)
    stack = lambda xs: jnp.stack(xs, axis=1)
    return (y_prompt, y_sample, stack(st_f), stack(st_b), stack(na_k), stack(na_v), stack(gq_k), stack(gq_v),
            stack(ml_c), stack(ml_r))
```

```python
import functools
import math

import numpy as np

import jax
import jax.numpy as jnp
from jax import lax
from jax.experimental import pallas as pl
from jax.experimental.pallas import tpu as pltpu

F32 = jnp.float32
BF16 = jnp.bfloat16

EPS = 1e-6
N_MOD = 9
GRID_W = 64
NA_KR = 8
NA_KC = 16
ROPE_THETA = 10000.0
LRU_C = 8.0
CONV_W = 4
MLA_NOPE = 128
MLA_V = 128
NEG_BIAS = -1e30

V7X_LANES = 128
V7X_SUBLANES = 8
MOD_ROWS = 16
TOKEN_TILE = 512
FF_TILE = 512
Q_CHUNK = 128
NA_Q_CHUNK = 256
NA_WIN = 768
LRU_COLS = 256


def _params(sem, vmem_mib):
    return pltpu.CompilerParams(dimension_semantics=sem, vmem_limit_bytes=vmem_mib * 1024 * 1024)


def _silu(x):
    return x * jax.nn.sigmoid(x)


def _rms(x, gain):
    ms = jnp.mean(x * x, axis=-1, keepdims=True)
    return x * lax.rsqrt(ms + EPS) * gain


def _mod_row(i, n_prompt_tiles, tiles_per_seq):
    return jnp.where(i < n_prompt_tiles, 0, 1 + (i - n_prompt_tiles) // tiles_per_seq)


def _adaln(x, mod_ref, gain_ref, row, mod_base):
    d = x.shape[1]
    shift = mod_ref[pl.ds(row, 1), pl.ds(mod_base * d, d)]
    scale = mod_ref[pl.ds(row, 1), pl.ds((mod_base + 1) * d, d)]
    return (_rms(x, gain_ref[...]) * (1 + scale) + shift).astype(BF16)


def _is_pow2(v):
    return math.frexp(v)[0] == 0.5


def _mod_kernel(cond_ref, w_ref, b_ref, o_ref):
    s = _silu(cond_ref[...]).astype(BF16)
    o_ref[...] = jnp.dot(s, w_ref[...].astype(BF16), preferred_element_type=F32) + b_ref[...]


def _modulation(cond, w_mod, b_mod):
    depth, d, n = w_mod.shape
    tn = 1024
    return pl.pallas_call(
        _mod_kernel,
        grid=(depth, n // tn),
        in_specs=[
            pl.BlockSpec((MOD_ROWS, d), lambda l, j: (0, 0)),
            pl.BlockSpec((None, d, tn), lambda l, j: (l, 0, j)),
            pl.BlockSpec((None, 1, tn), lambda l, j: (l, 0, j)),
        ],
        out_specs=pl.BlockSpec((None, MOD_ROWS, tn), lambda l, j: (l, 0, j)),
        out_shape=jax.ShapeDtypeStruct((depth, MOD_ROWS, n), F32),
        compiler_params=_params(("parallel", "parallel"), 40),
        name="modulation",
    )(cond, w_mod, b_mod.reshape(depth, 1, n))


def _ffn_kernel(*refs, mod_base, n_prompt_tiles, tiles_per_seq, cast_next, split_x):
    i = pl.program_id(0)
    j = pl.program_id(1)
    if split_x:
        xp_ref, xs_ref, *refs = refs
        read_x = lambda: jnp.where(i < n_prompt_tiles, xp_ref[...], xs_ref[...])
    else:
        x_ref, *refs = refs
        read_x = lambda: x_ref[...]
    mod_ref, gain_ref, wg_ref, wu_ref, wo_ref, *rest = refs
    if cast_next:
        nin_ref, nout_ref, o_ref, cin_ref, cout_ref, h_ref = rest
    else:
        o_ref, h_ref = rest
    d = o_ref.shape[1]
    nj = pl.num_programs(1)
    row = _mod_row(i, n_prompt_tiles, tiles_per_seq)

    def step(first, last):
        if first:
            h = _adaln(read_x(), mod_ref, gain_ref, row, mod_base)
            h_ref[...] = h
        else:
            h = h_ref[...]
        g = jnp.dot(h, wg_ref[...], preferred_element_type=F32)
        u = jnp.dot(h, wu_ref[...], preferred_element_type=F32)
        if cast_next:
            cin_ref[...] = nin_ref[...].astype(BF16)
            cout_ref[...] = nout_ref[...].astype(BF16)
        act = (_silu(g) * u).astype(BF16)
        acc = jnp.dot(act, wo_ref[...], preferred_element_type=F32)
        if not first:
            acc = o_ref[...] + acc
        if last:
            gate = mod_ref[pl.ds(row, 1), pl.ds((mod_base + 2) * d, d)]
            acc = read_x() + (0.5 * gate) * acc
        o_ref[...] = acc

    pl.when(j == 0)(lambda: step(True, False))
    pl.when((j > 0) & (j < nj - 1))(lambda: step(False, False))
    pl.when(j == nj - 1)(lambda: step(False, True))


def _ffn(x, mods, gains, w_in, w_out, layer, which, nxt, n_prompt_rows, seq_rows):
    split_x = isinstance(x, tuple)
    npt = n_prompt_rows // TOKEN_TILE
    if split_x:
        t, d = x[0].shape[0] + x[1].shape[0], x[0].shape[1]
        x_specs = [pl.BlockSpec((TOKEN_TILE, d), lambda i, j: (jnp.minimum(i, npt - 1), 0)),
                   pl.BlockSpec((TOKEN_TILE, d), lambda i, j: (jnp.maximum(i - npt, 0), 0))]
        x_args = list(x)
    else:
        t, d = x.shape
        x_specs = [pl.BlockSpec((TOKEN_TILE, d), lambda i, j: (i, 0))]
        x_args = [x]
    dff = w_out.shape[0]
    nj = dff // FF_TILE
    ni = t // TOKEN_TILE
    assert nj >= 2
    mod_base = 6 * which
    kern = functools.partial(_ffn_kernel, mod_base=mod_base, n_prompt_tiles=npt,
                             tiles_per_seq=seq_rows // TOKEN_TILE, cast_next=nxt is not None, split_x=split_x)
    in_specs = x_specs + [
        pl.BlockSpec((None, MOD_ROWS, N_MOD * d), lambda i, j: (layer, 0, 0)),
        pl.BlockSpec((None, 1, d), lambda i, j: (3 * layer + 2 * which, 0, 0)),
        pl.BlockSpec((d, FF_TILE), lambda i, j: (0, j)),
        pl.BlockSpec((d, FF_TILE), lambda i, j: (0, j + nj)),
        pl.BlockSpec((FF_TILE, d), lambda i, j: (j, 0)),
    ]
    args = x_args + [mods, gains, w_in, w_in, w_out]
    out_specs = [pl.BlockSpec((TOKEN_TILE, d), lambda i, j: (i, 0))]
    out_shape = [jax.ShapeDtypeStruct((t, d), F32)]
    if nxt is not None:
        f_in, f_out, nl, nw = nxt
        cast_rows = min(r for r in (16, 32, 64, 128, 256, 512)
                        if d % r == 0 and dff % r == 0 and d // r + dff // r <= ni * nj)
        n_in, n_out = d // cast_rows, dff // cast_rows
        in_blk = lambda i, j: jnp.minimum(i * nj + j, n_in - 1)
        out_blk = lambda i, j: jnp.clip(i * nj + j - n_in, 0, n_out - 1)
        in_specs += [pl.BlockSpec((None, None, cast_rows, 2 * dff), lambda i, j: (nl, nw, in_blk(i, j), 0)),
                     pl.BlockSpec((None, None, cast_rows, d), lambda i, j: (nl, nw, out_blk(i, j), 0))]
        args += [f_in, f_out]
        out_specs += [pl.BlockSpec((cast_rows, 2 * dff), lambda i, j: (in_blk(i, j), 0)),
                      pl.BlockSpec((cast_rows, d), lambda i, j: (out_blk(i, j), 0))]
        out_shape += [jax.ShapeDtypeStruct((d, 2 * dff), BF16), jax.ShapeDtypeStruct((dff, d), BF16)]
    return pl.pallas_call(
        kern,
        grid=(ni, nj),
        in_specs=in_specs,
        out_specs=out_specs,
        out_shape=out_shape,
        scratch_shapes=[pltpu.VMEM((TOKEN_TILE, d), BF16)],
        compiler_params=_params(("arbitrary", "arbitrary"), 56),
        name="ffn",
    )(*args)


def _inproj_kernel(x_ref, mod_ref, gain_ref, w_ref, o_ref, *, n_prompt_tiles, tiles_per_seq):
    row = _mod_row(pl.program_id(0), n_prompt_tiles, tiles_per_seq)
    tn = o_ref.shape[1]

    def block(c):
        h = _adaln(x_ref[...], mod_ref, gain_ref, row, 3)
        o_ref[...] = jnp.dot(h, w_ref[:, pl.ds(c * tn, tn)], preferred_element_type=F32)

    for c in range(w_ref.shape[1] // tn):
        pl.when(pl.program_id(1) == c)(functools.partial(block, c))


def _inproj(x, mods, gains, w, layer, w_idx, ncb, n_prompt_rows, seq_rows):
    t, d = x.shape
    tn = w.shape[2] // ncb
    tm = TOKEN_TILE
    kern = functools.partial(_inproj_kernel, n_prompt_tiles=n_prompt_rows // tm, tiles_per_seq=seq_rows // tm)
    return pl.pallas_call(
        kern,
        grid=(t // tm, ncb),
        in_specs=[
            pl.BlockSpec((tm, d), lambda i, j: (i, 0)),
            pl.BlockSpec((None, MOD_ROWS, N_MOD * d), lambda i, j: (layer, 0, 0)),
            pl.BlockSpec((None, 1, d), lambda i, j: (3 * layer + 1, 0, 0)),
            pl.BlockSpec((None, d, ncb * tn), lambda i, j: (w_idx, 0, 0), pipeline_mode=pl.Buffered(1)),
        ],
        out_specs=pl.BlockSpec((tm, tn), lambda i, j: (i, j)),
        out_shape=jax.ShapeDtypeStruct((t, ncb * tn), F32),
        compiler_params=_params(("parallel", "parallel"), 52),
        name="inproj",
    )(x, mods, gains, w)


def _outproj_kernel(x_ref, mod_ref, ap_ref, as_ref, bp_ref, bs_ref, w_ref, o_ref, *,
                    n_prompt_tiles, tiles_per_seq):
    i = pl.program_id(0)
    d = x_ref.shape[1]
    half = ap_ref.shape[1]
    is_prompt = i < n_prompt_tiles
    ya = jnp.where(is_prompt, ap_ref[...], as_ref[...])
    yb = jnp.where(is_prompt, bp_ref[...], bs_ref[...])
    y = jnp.dot(ya, w_ref[pl.ds(0, half), :], preferred_element_type=F32)
    y = y + jnp.dot(yb, w_ref[pl.ds(half, half), :], preferred_element_type=F32)
    row = _mod_row(i, n_prompt_tiles, tiles_per_seq)
    gate = mod_ref[pl.ds(row, 1), pl.ds(5 * d, d)]
    o_ref[...] = x_ref[...] + gate * y


def _outproj(x, mods, layer, ya_p, ya_s, yb_p, yb_s, w, w_idx, n_prompt_rows, seq_rows):
    t, d = x.shape
    half = ya_p.shape[1]
    npt = n_prompt_rows // TOKEN_TILE
    kern = functools.partial(_outproj_kernel, n_prompt_tiles=npt, tiles_per_seq=seq_rows // TOKEN_TILE)
    p_map = lambda i: (jnp.minimum(i, npt - 1), 0)
    s_map = lambda i: (jnp.maximum(i - npt, 0), 0)
    return pl.pallas_call(
        kern,
        grid=(t // TOKEN_TILE,),
        in_specs=[
            pl.BlockSpec((TOKEN_TILE, d), lambda i: (i, 0)),
            pl.BlockSpec((None, MOD_ROWS, N_MOD * d), lambda i: (layer, 0, 0)),
            pl.BlockSpec((TOKEN_TILE, half), p_map),
            pl.BlockSpec((TOKEN_TILE, half), s_map),
            pl.BlockSpec((TOKEN_TILE, half), p_map),
            pl.BlockSpec((TOKEN_TILE, half), s_map),
            pl.BlockSpec((None, 2 * half, d), lambda i: (w_idx, 0, 0)),
        ],
        out_specs=pl.BlockSpec((TOKEN_TILE, d), lambda i: (i, 0)),
        out_shape=jax.ShapeDtypeStruct((t, d), F32),
        compiler_params=_params(("parallel",), 48),
        name="outproj",
    )(x, mods, ya_p, ya_s, yb_p, yb_s, w)


def _final_norm_kernel(x_ref, g_ref, o_ref):
    o_ref[...] = _rms(x_ref[...], g_ref[...])


def _final_norm(x, gain, row_start, rows):
    d = x.shape[1]
    off = row_start // TOKEN_TILE
    return pl.pallas_call(
        _final_norm_kernel,
        grid=(rows // TOKEN_TILE,),
        in_specs=[pl.BlockSpec((TOKEN_TILE, d), lambda i: (i + off, 0)),
                  pl.BlockSpec((1, d), lambda i: (0, 0))],
        out_specs=pl.BlockSpec((TOKEN_TILE, d), lambda i: (i, 0)),
        out_shape=jax.ShapeDtypeStruct((rows, d), F32),
        compiler_params=_params(("parallel",), 32),
        name="final_norm",
    )(x, gain.reshape(1, d))


def _group_roll(x, step):
    rows, w = x.shape
    x3 = x.reshape(rows // V7X_SUBLANES, V7X_SUBLANES, w)
    return pltpu.roll(x3, step, axis=1).reshape(rows, w)


def _lru_kernel(xa_ref, ga_ref, h0f_ref, h0b_ref, cw_ref, cb_ref, wa_ref, ba_ref, wx_ref, bx_ref,
                lam_ref, y_ref, lf_ref, lb_ref, a_scr, u_scr, hf_scr, hb_scr):
    seq, w = xa_ref.shape
    groups = seq // V7X_SUBLANES
    xa = xa_ref[...]
    row = lax.broadcasted_iota(jnp.int32, (seq, w), 0)
    sub = row & (V7X_SUBLANES - 1)

    def tap(offset):
        if offset == 0:
            return xa
        shifted = pltpu.roll(xa, (-offset) % seq, axis=0)
        valid = row >= -offset if offset < 0 else row < seq - offset
        return jnp.where(valid, shifted, 0.0)

    xc = cb_ref[...]
    for j in range(CONV_W):
        xc = xc + tap(j - CONV_W // 2) * cw_ref[pl.ds(j, 1), :]
    xcb = xc.astype(BF16)

    for direction, (h0_ref, h_scr, last_ref) in enumerate(((h0f_ref, hf_scr, lf_ref),
                                                            (h0b_ref, hb_scr, lb_ref))):
        reverse = direction == 1
        r = jax.nn.sigmoid(jnp.dot(xcb, wa_ref[direction], preferred_element_type=F32) + ba_ref[direction])
        gi = jax.nn.sigmoid(jnp.dot(xcb, wx_ref[direction], preferred_element_type=F32) + bx_ref[direction])
        neg_lam = -lam_ref[direction]
        softplus = jnp.maximum(neg_lam, 0.0) + jnp.log1p(jnp.exp(-jnp.abs(neg_lam)))
        log_a = r * (-LRU_C * softplus)
        a = jnp.exp(log_a)
        m2 = (1.0 - a) * (1.0 + a)
        u = jnp.where(m2 > 0.0, m2 * lax.rsqrt(m2), 0.0) * (gi * xc)
        for step in (1, 2, 4):
            if reverse:
                a_nb = _group_roll(a, V7X_SUBLANES - step)
                u_nb = _group_roll(u, V7X_SUBLANES - step)
                valid = sub < V7X_SUBLANES - step
            else:
                a_nb = _group_roll(a, step)
                u_nb = _group_roll(u, step)
                valid = sub >= step
            u = a * jnp.where(valid, u_nb, 0.0) + u
            a = a * jnp.where(valid, a_nb, 1.0)
        a_scr[...] = a
        u_scr[...] = u

        def carry_step(g, carry, reverse=reverse, h_scr=h_scr):
            gg = groups - 1 - g if reverse else g
            off = pl.multiple_of(gg * V7X_SUBLANES, V7X_SUBLANES)
            h = a_scr[pl.ds(off, V7X_SUBLANES), :] * carry + u_scr[pl.ds(off, V7X_SUBLANES), :]
            h_scr[pl.ds(off, V7X_SUBLANES), :] = h
            return h[0:1] if reverse else h[V7X_SUBLANES - 1:V7X_SUBLANES]

        last_ref[...] = lax.fori_loop(0, groups, carry_step, h0_ref[...], unroll=4)

    y_ref[...] = ((hf_scr[...] + hb_scr[...]) * jax.nn.gelu(ga_ref[...])).astype(BF16)


def _lru(p_ab, h0f, h0b, prm, seq_len, n_seq, row_start):
    lru_w = h0f.shape[-1]
    cw = LRU_COLS
    ncb = lru_w // cw
    sb = row_start // seq_len
    vec = lambda: pl.BlockSpec((None, 1, cw), lambda s, c: (s, 0, c))
    par2 = lambda: pl.BlockSpec((None, 2, 1, cw), lambda s, c: (prm["idx"], 0, 0, c))
    gate = lambda: pl.BlockSpec((None, 2, None, cw, cw), lambda s, c: (prm["idx"], 0, c, 0, 0))
    return pl.pallas_call(
        _lru_kernel,
        grid=(n_seq, ncb),
        in_specs=[
            pl.BlockSpec((seq_len, cw), lambda s, c: (s + sb, c)),
            pl.BlockSpec((seq_len, cw), lambda s, c: (s + sb, c + ncb)),
            vec(), vec(),
            pl.BlockSpec((None, CONV_W, cw), lambda s, c: (prm["idx"], 0, c)),
            pl.BlockSpec((None, 1, cw), lambda s, c: (prm["idx"], 0, c)),
            gate(), par2(), gate(), par2(), par2(),
        ],
        out_specs=[
            pl.BlockSpec((seq_len, cw), lambda s, c: (s, c)),
            vec(), vec(),
        ],
        out_shape=[
            jax.ShapeDtypeStruct((n_seq * seq_len, lru_w), BF16),
            jax.ShapeDtypeStruct((n_seq, 1, lru_w), F32),
            jax.ShapeDtypeStruct((n_seq, 1, lru_w), F32),
        ],
        scratch_shapes=[pltpu.VMEM((seq_len, cw), F32)] * 4,
        compiler_params=_params(("parallel", "parallel"), 40),
        name="lru",
    )(p_ab, p_ab, h0f.reshape(n_seq, 1, lru_w), h0b.reshape(n_seq, 1, lru_w), prm["conv_w"],
      prm["conv_b"], prm["wa"], prm["ba"], prm["wx"], prm["bx"], prm["lam"])


def _block_diag_tiles(w, tile):
    n, two, nb, bw, _ = w.shape
    per = tile // bw
    w = w.reshape(n, two, nb // per, per, bw, bw)
    eye = jnp.eye(per, dtype=w.dtype)
    out = jnp.einsum("ndgpij,pq->ndgpiqj", w, eye)
    return out.reshape(n, two, nb // per, tile, tile)


def _softmax_attend(q, ks, vs, biases, scale):
    scores = []
    for k, b in zip(ks, biases):
        s = lax.dot_general(q, k, (((1,), (1,)), ((), ())), preferred_element_type=F32)
        if scale is not None:
            s = s * scale
        scores.append(s if b is None else s + b)
    m = scores[0].max(axis=-1, keepdims=True)
    for s in scores[1:]:
        m = jnp.maximum(m, s.max(axis=-1, keepdims=True))
    denom = None
    out = None
    for s, v in zip(scores, vs):
        e = jnp.exp(s - m)
        part = e.sum(axis=-1, keepdims=True)
        denom = part if denom is None else denom + part
        o = jnp.dot(e.astype(BF16), v, preferred_element_type=F32)
        out = o if out is None else out + o
    return out / denom


def _upper_half(shape):
    lane = lax.broadcasted_iota(jnp.int32, shape, len(shape) - 1)
    return lane % V7X_LANES >= V7X_LANES // 2


def _split_pair(q):
    up = _upper_half(q.shape)
    zero = jnp.zeros_like(q)
    return jnp.concatenate([jnp.where(up, zero, q), jnp.where(up, q, zero)], axis=0)


def _merge_pair(o):
    m = o.shape[0] // 2
    return jnp.where(_upper_half((m, o.shape[1])), o[m:], o[:m])


def _dup_head(block, use_high):
    swapped = pltpu.roll(block, V7X_LANES // 2, axis=1)
    keep = _upper_half(block.shape) == use_high
    return jnp.where(keep, block, swapped)


def _fold_scale(q, scale):
    return (q * scale, None) if _is_pow2(scale) else (q, scale)


def _na_ctx_kernel(q_ref, k_ref, v_ref, o_ref, *, scale):
    for pair in range(q_ref.shape[1] // V7X_LANES):
        sl = pl.ds(pair * V7X_LANES, V7X_LANES)
        q, sc = _fold_scale(q_ref[:, sl], scale)
        o = _softmax_attend(_split_pair(q.astype(BF16)), [k_ref[:, sl].astype(BF16)],
                            [v_ref[:, sl].astype(BF16)], [None], sc)
        o_ref[:, sl] = _merge_pair(o).astype(o_ref.dtype)


def _na_ctx(p_ab, n_seq, seq_len, q_blk, na_w, scale):
    blk = lambda col: pl.BlockSpec((seq_len, na_w), lambda b: (b, col))
    return pl.pallas_call(
        functools.partial(_na_ctx_kernel, scale=scale),
        grid=(n_seq,),
        in_specs=[blk(q_blk), blk(q_blk + 1), blk(q_blk + 2)],
        out_specs=pl.BlockSpec((seq_len, na_w), lambda b: (b, 0)),
        out_shape=jax.ShapeDtypeStruct((n_seq * seq_len, na_w), BF16),
        compiler_params=_params(("parallel",), 40),
        name="na_ctx",
    )(p_ab, p_ab, p_ab)


def _na_tile_index(seq_len):
    rows_n = seq_len // GRID_W
    kr = min(NA_KR, rows_n)
    rows_per_chunk = NA_Q_CHUNK // GRID_W
    n_chunks = rows_n // rows_per_chunk
    win_rows = NA_WIN // GRID_W
    table = []
    for r in range(rows_n):
        r_start = min(max(r - kr // 2, 0), rows_n - kr)
        base = 0 if r // rows_per_chunk < n_chunks // 2 else rows_n - win_rows
        table.append([k - r + NA_KR - 1 if r_start <= k < r_start + kr else 2 * NA_KR - 1
                      for k in range(base, base + win_rows)])
    return table


def _na_lat_kernel(q_ref, k_ref, v_ref, kc_ref, vc_ref, tiles_ref, o_ref, bias_ref, *, scale):
    lq = q_ref.shape[0]
    n_chunks = lq // NA_Q_CHUNK
    rows_per_chunk = NA_Q_CHUNK // GRID_W

    @pl.when(pl.program_id(1) == 0)
    def _():
        low = lax.broadcasted_iota(jnp.int32, (GRID_W, V7X_LANES), 1) < GRID_W
        for half in (0, 1):
            for r, slots in enumerate(_na_tile_index(lq)):
                c, rr = divmod(r, rows_per_chunk)
                rows = pl.ds(half * NA_Q_CHUNK + rr * GRID_W, GRID_W)
                for s in range(0, len(slots), 2):
                    blk = jnp.where(low, tiles_ref[half, slots[s]], tiles_ref[half, slots[s + 1]])
                    bias_ref[c, rows, pl.ds(s * GRID_W, V7X_LANES)] = blk

    kcb = kc_ref[...].astype(BF16)
    vcb = vc_ref[...].astype(BF16)
    for c in range(n_chunks):
        win = 0 if c < n_chunks // 2 else lq - NA_WIN
        kb = k_ref[pl.ds(win, NA_WIN), :].astype(BF16)
        vb = v_ref[pl.ds(win, NA_WIN), :].astype(BF16)
        rows = pl.ds(c * NA_Q_CHUNK, NA_Q_CHUNK)
        q, sc = _fold_scale(q_ref[rows, :], scale)
        o = _softmax_attend(_split_pair(q.astype(BF16)), [kb, kcb], [vb, vcb], [bias_ref[c], None], sc)
        o_ref[rows, :] = _merge_pair(o).astype(o_ref.dtype)


def _na_lat(p_ab, cache_k, cache_v, tiles, idx, n_seq, seq_len, row_start, q_col, n_pairs, scale):
    sb = row_start // seq_len
    past = cache_k.shape[2]
    blk = lambda col: pl.BlockSpec((seq_len, V7X_LANES), lambda h, b: (b + sb, col + h))
    cblk = lambda: pl.BlockSpec((None, None, past, V7X_LANES), lambda h, b: (b, idx, 0, h))
    n_chunks = seq_len // NA_Q_CHUNK
    return pl.pallas_call(
        functools.partial(_na_lat_kernel, scale=scale),
        grid=(n_pairs, n_seq),
        in_specs=[blk(q_col), blk(q_col + n_pairs), blk(q_col + 2 * n_pairs), cblk(), cblk(),
                  pl.BlockSpec((None, 2, 2 * NA_KR, GRID_W, V7X_LANES), lambda h, b: (idx, h, 0, 0, 0))],
        out_specs=pl.BlockSpec((seq_len, V7X_LANES), lambda h, b: (b, h)),
        out_shape=jax.ShapeDtypeStruct((n_seq * seq_len, n_pairs * V7X_LANES), BF16),
        scratch_shapes=[pltpu.VMEM((n_chunks, 2 * NA_Q_CHUNK, NA_WIN), F32)],
        compiler_params=_params(("parallel", "arbitrary"), 56),
        name="na_lat",
    )(p_ab, p_ab, p_ab, cache_k, cache_v, tiles)


def _na_bias_tiles(tab):
    lead = tab.shape[:-1]
    edge = GRID_W - NA_KC
    vec = jnp.pad(tab, [(0, 0)] * len(lead) + [(edge, edge + 1)], mode="edge")
    skew = jnp.broadcast_to(vec[..., None, :], lead + (GRID_W, 2 * GRID_W))
    skew = skew.reshape(lead + (2 * GRID_W * GRID_W,))[..., :GRID_W * (2 * GRID_W - 1)]
    toep = skew.reshape(lead + (GRID_W, 2 * GRID_W - 1))[..., GRID_W - 1:]
    cols = np.arange(GRID_W)
    c_start = np.clip(cols - NA_KC // 2, 0, GRID_W - NA_KC)
    col_ok = (cols[None, :] >= c_start[:, None]) & (cols[None, :] < c_start[:, None] + NA_KC)
    tiles = jnp.where(col_ok, toep, NEG_BIAS)
    masked = jnp.full(lead[:-1] + (1, GRID_W, GRID_W), NEG_BIAS, F32)
    tiles = jnp.concatenate([tiles, masked], axis=-3)
    return jnp.concatenate([tiles, tiles], axis=-1)


def _split_dot(v, m):
    hi = v.astype(BF16)
    lo = (v - hi.astype(F32)).astype(BF16)
    return jnp.dot(hi, m, preferred_element_type=F32) + jnp.dot(lo, m, preferred_element_type=F32)


def _head_rms(x, gather, spread, gain, head_dim):
    inv = lax.rsqrt(_split_dot(x * x, gather) * (1.0 / head_dim) + EPS)
    return x * _split_dot(inv, spread) * gain


def _rope(x, cos, sin_lo, sin_hi):
    w = x.shape[1]
    reps = w // V7X_LANES
    tile = lambda t: t if reps == 1 else jnp.concatenate([t] * reps, axis=1)
    quarter = 16
    return (x * tile(cos) + pltpu.roll(x, w - quarter, axis=1) * tile(sin_lo)
            + pltpu.roll(x, quarter, axis=1) * tile(sin_hi))


def _cd_post_kernel(p_ref, gat_ref, spr_ref, qg_ref, kg_ref, mqg_ref, mkg_ref, wuq_ref, wukv_ref, *rest,
                    rope, head_dim, q_w, kv_w, rank, q_scale):
    if rope:
        cos_ref, slo_ref, shi_ref = rest[:3]
        rest = rest[3:]
        tabs = (cos_ref[...], slo_ref[...], shi_ref[...])
    q_ref, k_ref, qn_ref, qr_ref, ckv_ref, kn_ref, vm_ref, kr_ref = rest
    c0 = 0
    qc = p_ref[:, pl.ds(c0, q_w)]
    c0 += q_w
    kc = p_ref[:, pl.ds(c0, kv_w)]
    c0 += 2 * kv_w
    qa = p_ref[:, pl.ds(c0, rank)]
    c0 += rank
    ckv = p_ref[:, pl.ds(c0, rank)]
    c0 += rank
    kr = p_ref[:, pl.ds(c0, V7X_LANES)]

    qc = _head_rms(qc, gat_ref[...], spr_ref[...], qg_ref[...], head_dim)
    kc = _head_rms(kc, gat_ref[pl.ds(0, kv_w), :], spr_ref[:, pl.ds(0, kv_w)], kg_ref[...], head_dim)
    qd = jnp.dot(_rms(qa, mqg_ref[...]).astype(BF16), wuq_ref[...], preferred_element_type=F32)
    n_nope = qn_ref.shape[1]
    qn = qd[:, :n_nope]
    qr = qd[:, n_nope:]
    ckv = _rms(ckv, mkg_ref[...])
    if rope:
        qc = _rope(qc, *tabs)
        kc = _rope(kc, *tabs)
        qr = _rope(qr, *tabs)
        kr = _rope(kr, *tabs)
    q_ref[...] = (qc * q_scale).astype(BF16)
    k_ref[...] = kc
    qn_ref[...] = qn.astype(BF16)
    qr_ref[...] = qr.astype(BF16)
    ckv_ref[...] = ckv
    kv = jnp.dot(ckv.astype(BF16), wukv_ref[...], preferred_element_type=F32)
    kn_ref[...] = kv[:, :n_nope].astype(BF16)
    vm_ref[...] = kv[:, n_nope:].astype(BF16)
    kr_ref[...] = kr


def _cd_post(p_cd, prm, row_start, rows, rope_tabs, dims, q_scale):
    q_w, kv_w, rank, head_dim, n_nope, n_rope, n_v = dims
    tm = 256
    off = row_start // tm
    n_in = p_cd.shape[1]
    idx = prm["idx"]
    rope = rope_tabs is not None
    const = lambda shape: pl.BlockSpec(shape, lambda i: (0,) * len(shape))
    layer = lambda shape: pl.BlockSpec((None,) + shape, lambda i: (idx,) + (0,) * len(shape))
    in_specs = [
        pl.BlockSpec((tm, n_in), lambda i: (i + off, 0)),
        const((q_w, V7X_LANES)), const((V7X_LANES, q_w)),
        layer((1, q_w)), layer((1, kv_w)), layer((1, rank)), layer((1, rank)),
        layer((rank, n_nope + n_rope)), layer((rank, n_nope + n_v)),
    ]
    args = [p_cd, prm["gather"], prm["spread"], prm["q_gain"], prm["k_gain"], prm["mla_q_gain"],
            prm["mla_kv_gain"], prm["w_uq"], prm["w_ukv"]]
    if rope:
        seq_tiles = rope_tabs[0].shape[0] // tm
        in_specs += [pl.BlockSpec((tm, V7X_LANES), lambda i: (i % seq_tiles, 0))] * 3
        args += list(rope_tabs)
    widths = [(q_w, BF16), (kv_w, F32), (n_nope, BF16), (n_rope, BF16), (rank, F32), (n_nope, BF16),
              (n_v, BF16), (V7X_LANES, F32)]
    kern = functools.partial(_cd_post_kernel, rope=rope, head_dim=head_dim, q_w=q_w, kv_w=kv_w, rank=rank,
                             q_scale=q_scale)
    return pl.pallas_call(
        kern,
        grid=(rows // tm,),
        in_specs=in_specs,
        out_specs=[pl.BlockSpec((tm, w), lambda i: (i, 0)) for w, _ in widths],
        out_shape=[jax.ShapeDtypeStruct((rows, w), dt) for w, dt in widths],
        compiler_params=_params(("parallel",), 48),
        name="cd_post",
    )(*args)


def _rope_tables(seq_len, head_dim):
    half = head_dim // 2
    nf = half // 2
    t = np.arange(seq_len)
    inv_freq = (1.0 / (ROPE_THETA ** (np.arange(nf, dtype=np.float32) / nf))).astype(np.float32)
    zeros = np.zeros((seq_len, nf), np.float32)
    cos, slo, shi = [], [], []
    for pos in (t // GRID_W, t % GRID_W):
        ang = pos.astype(np.float32)[:, None] * inv_freq[None, :]
        c, s = np.cos(ang).astype(np.float32), np.sin(ang).astype(np.float32)
        cos += [c, c]
        slo += [-s, zeros]
        shi += [zeros, s]
    reps = V7X_LANES // head_dim
    return tuple(jnp.asarray(np.tile(np.concatenate(x, axis=1), (1, reps))) for x in (cos, slo, shi))


def _ckv_up_kernel(c_ref, w_ref, kn_ref, vm_ref):
    kv = jnp.dot(c_ref[...].astype(BF16), w_ref[...], preferred_element_type=F32)
    n = kn_ref.shape[1]
    kn_ref[...] = kv[:, :n].astype(BF16)
    vm_ref[...] = kv[:, n:].astype(BF16)


def _ckv_up(ckv, w_ukv, idx, n_nope):
    n_seq, _, tm, rank = ckv.shape
    rows = n_seq * tm
    n = w_ukv.shape[2]
    return pl.pallas_call(
        _ckv_up_kernel,
        grid=(rows // tm,),
        in_specs=[pl.BlockSpec((None, None, tm, rank), lambda i: (i, idx, 0, 0)),
                  pl.BlockSpec((None, rank, n), lambda i: (idx, 0, 0))],
        out_specs=[pl.BlockSpec((tm, n_nope), lambda i: (i, 0)), pl.BlockSpec((tm, n - n_nope), lambda i: (i, 0))],
        out_shape=[jax.ShapeDtypeStruct((rows, n_nope), BF16), jax.ShapeDtypeStruct((rows, n - n_nope), BF16)],
        compiler_params=_params(("parallel",), 32),
        name="ckv_up",
    )(ckv, w_ukv)


def _gqa_kernel(q_ref, k_ref, v_ref, *rest, scale, has_cache, group_w):
    if has_cache:
        kc_ref, vc_ref, o_ref = rest
    else:
        (o_ref,) = rest
    lq = q_ref.shape[0]
    n_local = q_ref.shape[1] // group_w
    n_pairs = group_w // V7X_LANES
    for hl in range(n_local):
        if n_local == 1:
            high = (pl.program_id(1) % 2) == 1
            pick = lambda ref: ref[...]
        else:
            high = hl % 2 == 1
            pick = lambda ref, hl=hl: ref[:, pl.ds((hl // 2) * V7X_LANES, V7X_LANES)]
        ks = [_dup_head(pick(k_ref), high).astype(BF16)]
        vs = [_dup_head(pick(v_ref), high).astype(BF16)]
        if has_cache:
            ks.append(_dup_head(pick(kc_ref), high).astype(BF16))
            vs.append(_dup_head(pick(vc_ref), high).astype(BF16))
        for c in range(lq // Q_CHUNK):
            rows = pl.ds(c * Q_CHUNK, Q_CHUNK)
            cols = [pl.ds(hl * group_w + p * V7X_LANES, V7X_LANES) for p in range(n_pairs)]
            q = jnp.concatenate([_split_pair(q_ref[rows, cl]) for cl in cols], axis=0)
            o = _softmax_attend(q, ks, vs, [None] * len(ks), scale)
            for p, cl in enumerate(cols):
                o_ref[rows, cl] = _merge_pair(o[2 * p * Q_CHUNK:2 * (p + 1) * Q_CHUNK]).astype(o_ref.dtype)


def _gqa(q, k, p_cd, v_col, cache_k, cache_v, idx, n_seq, seq_len, row_start, n_kv, heads_per_step, scale):
    sb = row_start // seq_len
    group_w = q.shape[1] // n_kv
    has_cache = cache_k is not None
    if heads_per_step == 1:
        kv_w, kv_blk, v_blk = V7X_LANES, (lambda h: h // 2), (lambda h: v_col + h // 2)
    else:
        kv_w = k.shape[1]
        kv_blk, v_blk = (lambda h: 0), (lambda h: v_col * V7X_LANES // kv_w)
    in_specs = [
        pl.BlockSpec((seq_len, group_w * heads_per_step), lambda b, h: (b, h)),
        pl.BlockSpec((seq_len, kv_w), lambda b, h: (b, kv_blk(h))),
        pl.BlockSpec((seq_len, kv_w), lambda b, h: (b + sb, v_blk(h))),
    ]
    args = [q, k, p_cd]
    if has_cache:
        past = cache_k.shape[2]
        in_specs += [pl.BlockSpec((None, None, past, kv_w), lambda b, h: (b, idx, 0, kv_blk(h)))] * 2
        args += [cache_k, cache_v]
    return pl.pallas_call(
        functools.partial(_gqa_kernel, scale=scale, has_cache=has_cache, group_w=group_w),
        grid=(n_seq, n_kv // heads_per_step),
        in_specs=in_specs,
        out_specs=pl.BlockSpec((seq_len, group_w * heads_per_step), lambda b, h: (b, h)),
        out_shape=jax.ShapeDtypeStruct(q.shape, BF16),
        compiler_params=_params(("parallel", "parallel"), 56),
        name="gqa",
    )(*args)


def _mla_kernel(qn_ref, qr_ref, kn_ref, kr_ref, v_ref, *rest, scale, has_cache):
    if has_cache:
        knc_ref, krc_ref, vc_ref, o_ref = rest
        krc = krc_ref[...].astype(BF16)
    else:
        (o_ref,) = rest
    lq = qn_ref.shape[0]
    qc = min(2 * Q_CHUNK, lq)
    n_local = qn_ref.shape[1] // V7X_LANES
    kr = kr_ref[...].astype(BF16)
    for hl in range(n_local):
        sl = pl.ds(hl * V7X_LANES, V7X_LANES)
        if n_local == 1:
            high = (pl.program_id(1) % 2) == 1
            pair = pl.ds(0, V7X_LANES)
        else:
            high = hl % 2 == 1
            pair = pl.ds((hl // 2) * V7X_LANES, V7X_LANES)
        ks = [jnp.concatenate([kn_ref[:, sl], kr], axis=1)]
        vs = [v_ref[:, sl]]
        if has_cache:
            ks.append(jnp.concatenate([knc_ref[:, sl], krc], axis=1))
            vs.append(vc_ref[:, sl])
        for c in range(lq // qc):
            rows = pl.ds(c * qc, qc)
            qr = qr_ref[rows, pair]
            qr = jnp.where(_upper_half(qr.shape) == high, qr, jnp.zeros_like(qr))
            q = jnp.concatenate([qn_ref[rows, sl], qr], axis=1)
            o_ref[rows, sl] = _softmax_attend(q, ks, vs, [None] * len(ks), scale).astype(o_ref.dtype)


def _mla(qn, qr, kn, kr2, vm, cache, n_seq, seq_len, heads_per_step, scale):
    has_cache = cache is not None
    n_heads = vm.shape[1] // V7X_LANES
    hw = heads_per_step * V7X_LANES
    head = lambda rows: pl.BlockSpec((rows, hw), lambda b, h: (b, h))
    if heads_per_step == 1:
        rope_q = pl.BlockSpec((seq_len, V7X_LANES), lambda b, h: (b, h // 2))
    else:
        rope_q = pl.BlockSpec((seq_len, qr.shape[1]), lambda b, h: (b, 0))
    in_specs = [head(seq_len), rope_q, head(seq_len),
                pl.BlockSpec((seq_len, V7X_LANES), lambda b, h: (b, 0)),
                head(seq_len)]
    args = [qn, qr, kn, kr2, vm]
    if has_cache:
        knc, krc, vmc, idx = cache
        past = krc.shape[2]
        in_specs += [head(past),
                     pl.BlockSpec((None, None, past, V7X_LANES), lambda b, h: (b, idx, 0, 0)),
                     head(past)]
        args += [knc, krc, vmc]
    return pl.pallas_call(
        functools.partial(_mla_kernel, scale=scale, has_cache=has_cache),
        grid=(n_seq, n_heads // heads_per_step),
        in_specs=in_specs,
        out_specs=head(seq_len),
        out_shape=jax.ShapeDtypeStruct(vm.shape, BF16),
        compiler_params=_params(("parallel", "parallel"), 48),
        name="mla",
    )(*args)


def kernel(x_prompt, x_sample, state_lru_fwd, state_lru_bwd, cache_na_k, cache_na_v, cache_gqa_k, cache_gqa_v, cache_mla_ckv, cache_mla_krope, c, c_ctx, w_mod, b_mod, norm_gain, w_ffn_in, w_ffn_out, w_in_ab, conv_w, conv_b, lru_wa, lru_ba, lru_wx, lru_bx, lru_lambda, na_bias, w_out_ab, w_in_cd, gqa_q_gain, gqa_k_gain, mla_q_gain, mla_kv_gain, mla_w_uq, mla_w_uk, mla_w_uv, w_out_cd, final_gain):
    batch, seq, d = x_prompt.shape
    dec_batch, dec_seq, _ = x_sample.shape
    depth = w_mod.shape[0]
    n_even, n_odd = w_in_ab.shape[0], w_in_cd.shape[0]
    n_p = batch * seq
    n_s = dec_batch * dec_seq
    past = cache_na_k.shape[2]
    lru_w = state_lru_fwd.shape[-1]
    na_heads, na_dh = cache_na_k.shape[3], cache_na_k.shape[4]
    na_w = na_heads * na_dh
    gqa_kv, gqa_dh = cache_gqa_k.shape[3], cache_gqa_k.shape[4]
    kv_w = gqa_kv * gqa_dh
    rank = cache_mla_ckv.shape[-1]
    rope_w = cache_mla_krope.shape[-1]
    q_w = w_in_cd.shape[2] - 2 * kv_w - 2 * rank - rope_w
    mla_heads = mla_w_uk.shape[2] // MLA_NOPE
    n_nope = mla_heads * MLA_NOPE
    n_rope = mla_heads * rope_w
    n_v = mla_heads * MLA_V
    mla_qk = MLA_NOPE + rope_w
    assert mla_q_gain.shape[-1] == rank and 2 * rope_w == V7X_LANES
    assert n_p % dec_seq == 0 and dec_seq % TOKEN_TILE == 0 and seq % Q_CHUNK == 0
    assert dec_seq // GRID_W == 2 * NA_KR and na_dh == GRID_W and gqa_dh == GRID_W
    assert w_in_ab.shape[2] == 2 * lru_w + 3 * na_w and lru_w == na_w
    assert (q_w // gqa_kv) % V7X_LANES == 0 and q_w % kv_w == 0

    x = (x_prompt.reshape(n_p, d), x_sample.reshape(n_s, d))
    cond = jnp.concatenate([c_ctx[None, :], c, jnp.zeros((MOD_ROWS - 1 - dec_batch, d), F32)], axis=0)
    mods = _modulation(cond, w_mod, b_mod)
    gains = norm_gain.reshape(depth * 3, 1, d)

    ffn_w = (w_ffn_in[0, 0].astype(BF16), w_ffn_out[0, 0].astype(BF16))

    def ffn_step(x, ffn_w, layer, which):
        last = layer == depth - 1 and which == 1
        nxt = None if last else (w_ffn_in, w_ffn_out, layer + which, 1 - which)
        out = _ffn(x, mods, gains, ffn_w[0], ffn_w[1], layer, which, nxt, n_p, dec_seq)
        return out[0], tuple(out[1:])

    w_in_ab_bf = w_in_ab.astype(BF16)
    w_out_ab_bf = w_out_ab.astype(BF16)
    w_in_cd_bf = jnp.concatenate([w_in_cd, w_in_cd[:, :, -rope_w:]], axis=2).astype(BF16)
    w_out_cd_bf = w_out_cd.astype(BF16)

    lru_prm = {
        "conv_w": conv_w, "conv_b": conv_b.reshape(n_even, 1, lru_w),
        "wa": _block_diag_tiles(lru_wa, LRU_COLS).astype(BF16),
        "wx": _block_diag_tiles(lru_wx, LRU_COLS).astype(BF16),
        "ba": lru_ba.reshape(n_even, 2, 1, lru_w), "bx": lru_bx.reshape(n_even, 2, 1, lru_w),
        "lam": lru_lambda.reshape(n_even, 2, 1, lru_w),
    }
    zeros_state = jnp.zeros((batch, lru_w), F32)
    na_k_ctx = cache_na_k.reshape(dec_batch, n_even, past, na_w)
    na_v_ctx = cache_na_v.reshape(dec_batch, n_even, past, na_w)
    na_tiles = _na_bias_tiles(na_bias)

    w_uq = mla_w_uq.reshape(n_odd, rank, mla_heads, mla_qk)
    w_uq = jnp.concatenate([w_uq[..., :MLA_NOPE].reshape(n_odd, rank, n_nope),
                            w_uq[..., MLA_NOPE:].reshape(n_odd, rank, n_rope)], axis=2).astype(BF16)
    gqa_scale = gqa_dh ** -0.5
    q_scale = gqa_scale if _is_pow2(gqa_scale) else 1.0
    head_of_lane = np.arange(q_w) // gqa_dh
    gather = (head_of_lane[:, None] == np.arange(V7X_LANES)[None, :]).astype(np.float32)
    assert q_w // gqa_dh <= V7X_LANES
    cd_prm = {
        "gather": jnp.asarray(gather, BF16), "spread": jnp.asarray(gather.T, BF16),
        "q_gain": jnp.tile(gqa_q_gain, (1, q_w // gqa_dh)).reshape(n_odd, 1, q_w),
        "k_gain": jnp.tile(gqa_k_gain, (1, gqa_kv)).reshape(n_odd, 1, kv_w),
        "mla_q_gain": mla_q_gain.reshape(n_odd, 1, rank),
        "mla_kv_gain": mla_kv_gain.reshape(n_odd, 1, rank),
        "w_uq": w_uq,
        "w_ukv": jnp.concatenate([mla_w_uk, mla_w_uv], axis=2).astype(BF16),
    }
    rope_tabs = _rope_tables(dec_seq, gqa_dh)
    gqa_k_ctx = cache_gqa_k.reshape(dec_batch, n_odd, past, kv_w)
    gqa_v_ctx = cache_gqa_v.reshape(dec_batch, n_odd, past, kv_w)
    krope_ctx = jnp.concatenate([cache_mla_krope, cache_mla_krope], axis=-1)

    st_f, st_b, na_k, na_v, gq_k, gq_v, ml_c, ml_r = [], [], [], [], [], [], [], []
    for layer in range(depth):
        jdx = layer // 2
        x, ffn_w = ffn_step(x, ffn_w, layer, 0)
        if layer % 2 == 0:
            p_ab = _inproj(x, mods, gains, w_in_ab_bf, layer, jdx, 2, n_p, dec_seq)
            prm = dict(lru_prm, idx=jdx)
            ya_p, lf, lb = _lru(p_ab, zeros_state, zeros_state, prm, seq, batch, 0)
            ya_s, _, _ = _lru(p_ab, state_lru_fwd[:, jdx], state_lru_bwd[:, jdx], prm, dec_seq, dec_batch, n_p)
            scale = na_dh ** -0.5
            yb_p = _na_ctx(p_ab, batch, seq, 2 * lru_w // na_w, na_w, scale)
            yb_s = _na_lat(p_ab, na_k_ctx, na_v_ctx, na_tiles, jdx, dec_batch, dec_seq, n_p,
                           2 * lru_w // V7X_LANES, na_w // V7X_LANES, scale)
            x = _outproj(x, mods, layer, ya_p, ya_s, yb_p, yb_s, w_out_ab_bf, jdx, n_p, dec_seq)
            st_f.append(lf.reshape(batch, lru_w))
            st_b.append(lb.reshape(batch, lru_w))
            na_k.append(p_ab[:n_p, 2 * lru_w + na_w:2 * lru_w + 2 * na_w].reshape(batch, seq, na_heads, na_dh))
            na_v.append(p_ab[:n_p, 2 * lru_w + 2 * na_w:].reshape(batch, seq, na_heads, na_dh))
        else:
            p_cd = _inproj(x, mods, gains, w_in_cd_bf, layer, jdx, 1, n_p, dec_seq)
            prm = dict(cd_prm, idx=jdx)
            dims = (q_w, kv_w, rank, gqa_dh, n_nope, n_rope, n_v)
            qp, kp, qnp_, qrp, ckvp, knp_, vmp, krp = _cd_post(p_cd, prm, 0, n_p, None, dims, q_scale)
            qs, ks_, qns, qrs, _, kns, vms, krs = _cd_post(p_cd, prm, n_p, n_s, rope_tabs, dims, q_scale)
            v_col = (q_w + kv_w) // V7X_LANES
            att_scale = None if q_scale != 1.0 else gqa_scale
            yc_p = _gqa(qp, kp, p_cd, v_col, None, None, jdx, batch, seq, 0, gqa_kv, gqa_kv, att_scale)
            yc_s = _gqa(qs, ks_, p_cd, v_col, gqa_k_ctx, gqa_v_ctx, jdx, dec_batch, dec_seq, n_p,
                        gqa_kv, 1, att_scale)
            knc, vmc = _ckv_up(cache_mla_ckv, cd_prm["w_ukv"], jdx, n_nope)
            yd_p = _mla(qnp_, qrp, knp_, krp, vmp, None, batch, seq, mla_heads, mla_qk ** -0.5)
            yd_s = _mla(qns, qrs, kns, krs, vms, (knc, krope_ctx, vmc, jdx), dec_batch, dec_seq, 1,
                        mla_qk ** -0.5)
            x = _outproj(x, mods, layer, yc_p, yc_s, yd_p, yd_s, w_out_cd_bf, jdx, n_p, dec_seq)
            gq_k.append(kp.reshape(batch, seq, gqa_kv, gqa_dh))
            gq_v.append(p_cd[:n_p, q_w + kv_w:q_w + 2 * kv_w].reshape(batch, seq, gqa_kv, gqa_dh))
            ml_c.append(ckvp.reshape(batch, seq, rank))
            ml_r.append(krp[:, :rope_w].reshape(batch, seq, rope_w))
        x, ffn_w = ffn_step(x, ffn_w, layer, 1)

    y_prompt = _final_norm(x, final_gain, 0, n_p).reshape(batch, seq, d)
    y_sample = _final_norm(x, final_gain, n_p, n_s).reshape(dec_batch, dec_seq, d)
    stack = lambda xs: jnp.stack(xs, axis=1)
    return (y_prompt, y_sample, stack(st_f), stack(st_b), stack(na_k), stack(na_v), stack(gq_k), stack(gq_v),
            stack(ml_c), stack(ml_r))
```

```python
import functools
import math

import numpy as np

import jax
import jax.numpy as jnp
from jax import lax
from jax.experimental import pallas as pl
from jax.experimental.pallas import tpu as pltpu

F32 = jnp.float32
BF16 = jnp.bfloat16

EPS = 1e-6
N_MOD = 9
GRID_W = 64
NA_KR = 8
NA_KC = 16
ROPE_THETA = 10000.0
LRU_C = 8.0
CONV_W = 4
MLA_NOPE = 128
MLA_V = 128
NEG_BIAS = -1e30

V7X_LANES = 128
V7X_SUBLANES = 8
MOD_ROWS = 16
TOKEN_TILE = 512
FF_TILE = 512
Q_CHUNK = 128
NA_Q_CHUNK = 256
NA_WIN = 768
LRU_COLS = 256


def _params(sem, vmem_mib):
    return pltpu.CompilerParams(dimension_semantics=sem, vmem_limit_bytes=vmem_mib * 1024 * 1024)


def _silu(x):
    return x * jax.nn.sigmoid(x)


def _rms(x, gain):
    ms = jnp.mean(x * x, axis=-1, keepdims=True)
    return x * lax.rsqrt(ms + EPS) * gain


def _mod_row(i, n_prompt_tiles, tiles_per_seq):
    return jnp.where(i < n_prompt_tiles, 0, 1 + (i - n_prompt_tiles) // tiles_per_seq)


def _adaln(x, mod_ref, gain_ref, row, mod_base):
    d = x.shape[1]
    shift = mod_ref[pl.ds(row, 1), pl.ds(mod_base * d, d)]
    scale = mod_ref[pl.ds(row, 1), pl.ds((mod_base + 1) * d, d)]
    return (_rms(x, gain_ref[...]) * (1 + scale) + shift).astype(BF16)


def _is_pow2(v):
    return math.frexp(v)[0] == 0.5


def _mod_kernel(cond_ref, w_ref, b_ref, o_ref):
    s = _silu(cond_ref[...]).astype(BF16)
    o_ref[...] = jnp.dot(s, w_ref[...].astype(BF16), preferred_element_type=F32) + b_ref[...]


def _modulation(cond, w_mod, b_mod):
    depth, d, n = w_mod.shape
    tn = 1024
    return pl.pallas_call(
        _mod_kernel,
        grid=(depth, n // tn),
        in_specs=[
            pl.BlockSpec((MOD_ROWS, d), lambda l, j: (0, 0)),
            pl.BlockSpec((None, d, tn), lambda l, j: (l, 0, j)),
            pl.BlockSpec((None, 1, tn), lambda l, j: (l, 0, j)),
        ],
        out_specs=pl.BlockSpec((None, MOD_ROWS, tn), lambda l, j: (l, 0, j)),
        out_shape=jax.ShapeDtypeStruct((depth, MOD_ROWS, n), F32),
        compiler_params=_params(("parallel", "parallel"), 40),
        name="modulation",
    )(cond, w_mod, b_mod.reshape(depth, 1, n))


def _ffn_kernel(*refs, mod_base, n_prompt_tiles, tiles_per_seq, cast_next, split_x):
    i = pl.program_id(0)
    j = pl.program_id(1)
    if split_x:
        xp_ref, xs_ref, *refs = refs
        read_x = lambda: jnp.where(i < n_prompt_tiles, xp_ref[...], xs_ref[...])
    else:
        x_ref, *refs = refs
        read_x = lambda: x_ref[...]
    mod_ref, gain_ref, wg_ref, wu_ref, wo_ref, *rest = refs
    if cast_next:
        nin_ref, nout_ref, o_ref, cin_ref, cout_ref, h_ref = rest
    else:
        o_ref, h_ref = rest
    d = o_ref.shape[1]
    nj = pl.num_programs(1)
    row = _mod_row(i, n_prompt_tiles, tiles_per_seq)

    def step(first, last):
        if first:
            h = _adaln(read_x(), mod_ref, gain_ref, row, mod_base)
            h_ref[...] = h
        else:
            h = h_ref[...]
        g = jnp.dot(h, wg_ref[...], preferred_element_type=F32)
        u = jnp.dot(h, wu_ref[...], preferred_element_type=F32)
        if cast_next:
            cin_ref[...] = nin_ref[...].astype(BF16)
            cout_ref[...] = nout_ref[...].astype(BF16)
        act = (_silu(g) * u).astype(BF16)
        acc = jnp.dot(act, wo_ref[...], preferred_element_type=F32)
        if not first:
            acc = o_ref[...] + acc
        if last:
            gate = mod_ref[pl.ds(row, 1), pl.ds((mod_base + 2) * d, d)]
            acc = read_x() + (0.5 * gate) * acc
        o_ref[...] = acc

    pl.when(j == 0)(lambda: step(True, False))
    pl.when((j > 0) & (j < nj - 1))(lambda: step(False, False))
    pl.when(j == nj - 1)(lambda: step(False, True))


def _ffn(x, mods, gains, w_in, w_out, layer, which, nxt, n_prompt_rows, seq_rows):
    split_x = isinstance(x, tuple)
    npt = n_prompt_rows // TOKEN_TILE
    if split_x:
        t, d = x[0].shape[0] + x[1].shape[0], x[0].shape[1]
        x_specs = [pl.BlockSpec((TOKEN_TILE, d), lambda i, j: (jnp.minimum(i, npt - 1), 0)),
                   pl.BlockSpec((TOKEN_TILE, d), lambda i, j: (jnp.maximum(i - npt, 0), 0))]
        x_args = list(x)
    else:
        t, d = x.shape
        x_specs = [pl.BlockSpec((TOKEN_TILE, d), lambda i, j: (i, 0))]
        x_args = [x]
    dff = w_out.shape[0]
    nj = dff // FF_TILE
    ni = t // TOKEN_TILE
    assert nj >= 2
    mod_base = 6 * which
    kern = functools.partial(_ffn_kernel, mod_base=mod_base, n_prompt_tiles=npt,
                             tiles_per_seq=seq_rows // TOKEN_TILE, cast_next=nxt is not None, split_x=split_x)
    in_specs = x_specs + [
        pl.BlockSpec((None, MOD_ROWS, N_MOD * d), lambda i, j: (layer, 0, 0)),
        pl.BlockSpec((None, 1, d), lambda i, j: (3 * layer + 2 * which, 0, 0)),
        pl.BlockSpec((d, FF_TILE), lambda i, j: (0, j)),
        pl.BlockSpec((d, FF_TILE), lambda i, j: (0, j + nj)),
        pl.BlockSpec((FF_TILE, d), lambda i, j: (j, 0)),
    ]
    args = x_args + [mods, gains, w_in, w_in, w_out]
    out_specs = [pl.BlockSpec((TOKEN_TILE, d), lambda i, j: (i, 0))]
    out_shape = [jax.ShapeDtypeStruct((t, d), F32)]
    if nxt is not None:
        f_in, f_out, nl, nw = nxt
        cast_rows = min(r for r in (16, 32, 64, 128, 256, 512)
                        if d % r == 0 and dff % r == 0 and d // r + dff // r <= ni * nj)
        n_in, n_out = d // cast_rows, dff // cast_rows
        in_blk = lambda i, j: jnp.minimum(i * nj + j, n_in - 1)
        out_blk = lambda i, j: jnp.clip(i * nj + j - n_in, 0, n_out - 1)
        in_specs += [pl.BlockSpec((None, None, cast_rows, 2 * dff), lambda i, j: (nl, nw, in_blk(i, j), 0)),
                     pl.BlockSpec((None, None, cast_rows, d), lambda i, j: (nl, nw, out_blk(i, j), 0))]
        args += [f_in, f_out]
        out_specs += [pl.BlockSpec((cast_rows, 2 * dff), lambda i, j: (in_blk(i, j), 0)),
                      pl.BlockSpec((cast_rows, d), lambda i, j: (out_blk(i, j), 0))]
        out_shape += [jax.ShapeDtypeStruct((d, 2 * dff), BF16), jax.ShapeDtypeStruct((dff, d), BF16)]
    return pl.pallas_call(
        kern,
        grid=(ni, nj),
        in_specs=in_specs,
        out_specs=out_specs,
        out_shape=out_shape,
        scratch_shapes=[pltpu.VMEM((TOKEN_TILE, d), BF16)],
        compiler_params=_params(("arbitrary", "arbitrary"), 56),
        name="ffn",
    )(*args)


def _inproj_kernel(x_ref, mod_ref, gain_ref, w_ref, o_ref, *, n_prompt_tiles, tiles_per_seq):
    row = _mod_row(pl.program_id(0), n_prompt_tiles, tiles_per_seq)
    tn = o_ref.shape[1]

    def block(c):
        h = _adaln(x_ref[...], mod_ref, gain_ref, row, 3)
        o_ref[...] = jnp.dot(h, w_ref[:, pl.ds(c * tn, tn)], preferred_element_type=F32)

    for c in range(w_ref.shape[1] // tn):
        pl.when(pl.program_id(1) == c)(functools.partial(block, c))


def _inproj(x, mods, gains, w, layer, w_idx, ncb, n_prompt_rows, seq_rows):
    t, d = x.shape
    tn = w.shape[2] // ncb
    tm = TOKEN_TILE
    kern = functools.partial(_inproj_kernel, n_prompt_tiles=n_prompt_rows // tm, tiles_per_seq=seq_rows // tm)
    return pl.pallas_call(
        kern,
        grid=(t // tm, ncb),
        in_specs=[
            pl.BlockSpec((tm, d), lambda i, j: (i, 0)),
            pl.BlockSpec((None, MOD_ROWS, N_MOD * d), lambda i, j: (layer, 0, 0)),
            pl.BlockSpec((None, 1, d), lambda i, j: (3 * layer + 1, 0, 0)),
            pl.BlockSpec((None, d, ncb * tn), lambda i, j: (w_idx, 0, 0), pipeline_mode=pl.Buffered(1)),
        ],
        out_specs=pl.BlockSpec((tm, tn), lambda i, j: (i, j)),
        out_shape=jax.ShapeDtypeStruct((t, ncb * tn), F32),
        compiler_params=_params(("parallel", "parallel"), 52),
        name="inproj",
    )(x, mods, gains, w)


def _outproj_kernel(x_ref, mod_ref, ap_ref, as_ref, bp_ref, bs_ref, w_ref, o_ref, *,
                    n_prompt_tiles, tiles_per_seq):
    i = pl.program_id(0)
    d = x_ref.shape[1]
    half = ap_ref.shape[1]
    is_prompt = i < n_prompt_tiles
    ya = jnp.where(is_prompt, ap_ref[...], as_ref[...])
    yb = jnp.where(is_prompt, bp_ref[...], bs_ref[...])
    y = jnp.dot(ya, w_ref[pl.ds(0, half), :], preferred_element_type=F32)
    y = y + jnp.dot(yb, w_ref[pl.ds(half, half), :], preferred_element_type=F32)
    row = _mod_row(i, n_prompt_tiles, tiles_per_seq)
    gate = mod_ref[pl.ds(row, 1), pl.ds(5 * d, d)]
    o_ref[...] = x_ref[...] + gate * y


def _outproj(x, mods, layer, ya_p, ya_s, yb_p, yb_s, w, w_idx, n_prompt_rows, seq_rows):
    t, d = x.shape
    half = ya_p.shape[1]
    npt = n_prompt_rows // TOKEN_TILE
    kern = functools.partial(_outproj_kernel, n_prompt_tiles=npt, tiles_per_seq=seq_rows // TOKEN_TILE)
    p_map = lambda i: (jnp.minimum(i, npt - 1), 0)
    s_map = lambda i: (jnp.maximum(i - npt, 0), 0)
    return pl.pallas_call(
        kern,
        grid=(t // TOKEN_TILE,),
        in_specs=[
            pl.BlockSpec((TOKEN_TILE, d), lambda i: (i, 0)),
            pl.BlockSpec((None, MOD_ROWS, N_MOD * d), lambda i: (layer, 0, 0)),
            pl.BlockSpec((TOKEN_TILE, half), p_map),
            pl.BlockSpec((TOKEN_TILE, half), s_map),
            pl.BlockSpec((TOKEN_TILE, half), p_map),
            pl.BlockSpec((TOKEN_TILE, half), s_map),
            pl.BlockSpec((None, 2 * half, d), lambda i: (w_idx, 0, 0)),
        ],
        out_specs=pl.BlockSpec((TOKEN_TILE, d), lambda i: (i, 0)),
        out_shape=jax.ShapeDtypeStruct((t, d), F32),
        compiler_params=_params(("parallel",), 48),
        name="outproj",
    )(x, mods, ya_p, ya_s, yb_p, yb_s, w)


def _final_norm_kernel(x_ref, g_ref, o_ref):
    o_ref[...] = _rms(x_ref[...], g_ref[...])


def _final_norm(x, gain, row_start, rows):
    d = x.shape[1]
    off = row_start // TOKEN_TILE
    return pl.pallas_call(
        _final_norm_kernel,
        grid=(rows // TOKEN_TILE,),
        in_specs=[pl.BlockSpec((TOKEN_TILE, d), lambda i: (i + off, 0)),
                  pl.BlockSpec((1, d), lambda i: (0, 0))],
        out_specs=pl.BlockSpec((TOKEN_TILE, d), lambda i: (i, 0)),
        out_shape=jax.ShapeDtypeStruct((rows, d), F32),
        compiler_params=_params(("parallel",), 32),
        name="final_norm",
    )(x, gain.reshape(1, d))


def _group_roll(x, step):
    rows, w = x.shape
    x3 = x.reshape(rows // V7X_SUBLANES, V7X_SUBLANES, w)
    return pltpu.roll(x3, step, axis=1).reshape(rows, w)


def _lru_kernel(xa_ref, ga_ref, h0f_ref, h0b_ref, cw_ref, cb_ref, wa_ref, ba_ref, wx_ref, bx_ref,
                lam_ref, y_ref, lf_ref, lb_ref, af_scr, uf_scr, ab_scr, ub_scr):
    seq, w = xa_ref.shape
    groups = seq // V7X_SUBLANES
    xa = xa_ref[...]
    row = lax.broadcasted_iota(jnp.int32, (seq, w), 0)
    sub = row & (V7X_SUBLANES - 1)

    def tap(offset):
        if offset == 0:
            return xa
        shifted = pltpu.roll(xa, (-offset) % seq, axis=0)
        valid = row >= -offset if offset < 0 else row < seq - offset
        return jnp.where(valid, shifted, 0.0)

    xc = cb_ref[...]
    for j in range(CONV_W):
        xc = xc + tap(j - CONV_W // 2) * cw_ref[pl.ds(j, 1), :]
    xcb = xc.astype(BF16)

    for direction, (a_scr, u_scr) in enumerate(((af_scr, uf_scr), (ab_scr, ub_scr))):
        reverse = direction == 1
        r = jax.nn.sigmoid(jnp.dot(xcb, wa_ref[direction], preferred_element_type=F32) + ba_ref[direction])
        gi = jax.nn.sigmoid(jnp.dot(xcb, wx_ref[direction], preferred_element_type=F32) + bx_ref[direction])
        neg_lam = -lam_ref[direction]
        softplus = jnp.maximum(neg_lam, 0.0) + jnp.log1p(jnp.exp(-jnp.abs(neg_lam)))
        log_a = r * (-LRU_C * softplus)
        a = jnp.exp(log_a)
        m2 = (1.0 - a) * (1.0 + a)
        u = jnp.where(m2 > 0.0, m2 * lax.rsqrt(m2), 0.0) * (gi * xc)
        for step in (1, 2, 4):
            if reverse:
                a_nb = _group_roll(a, V7X_SUBLANES - step)
                u_nb = _group_roll(u, V7X_SUBLANES - step)
                valid = sub < V7X_SUBLANES - step
            else:
                a_nb = _group_roll(a, step)
                u_nb = _group_roll(u, step)
                valid = sub >= step
            u = a * jnp.where(valid, u_nb, 0.0) + u
            a = a * jnp.where(valid, a_nb, 1.0)
        a_scr[...] = a
        u_scr[...] = u

    def carry_step(g, carry):
        cf, cb = carry
        off_f = pl.multiple_of(g * V7X_SUBLANES, V7X_SUBLANES)
        off_b = pl.multiple_of((groups - 1 - g) * V7X_SUBLANES, V7X_SUBLANES)
        hf = af_scr[pl.ds(off_f, V7X_SUBLANES), :] * cf + uf_scr[pl.ds(off_f, V7X_SUBLANES), :]
        hb = ab_scr[pl.ds(off_b, V7X_SUBLANES), :] * cb + ub_scr[pl.ds(off_b, V7X_SUBLANES), :]
        uf_scr[pl.ds(off_f, V7X_SUBLANES), :] = hf
        ub_scr[pl.ds(off_b, V7X_SUBLANES), :] = hb
        return hf[V7X_SUBLANES - 1:V7X_SUBLANES], hb[0:1]

    lf_ref[...], lb_ref[...] = lax.fori_loop(0, groups, carry_step, (h0f_ref[...], h0b_ref[...]), unroll=4)
    y_ref[...] = ((uf_scr[...] + ub_scr[...]) * jax.nn.gelu(ga_ref[...])).astype(BF16)


def _lru(p_ab, h0f, h0b, prm, seq_len, n_seq, row_start):
    lru_w = h0f.shape[-1]
    cw = LRU_COLS
    ncb = lru_w // cw
    sb = row_start // seq_len
    vec = lambda: pl.BlockSpec((None, 1, cw), lambda s, c: (s, 0, c))
    par2 = lambda: pl.BlockSpec((None, 2, 1, cw), lambda s, c: (prm["idx"], 0, 0, c))
    gate = lambda: pl.BlockSpec((None, 2, None, cw, cw), lambda s, c: (prm["idx"], 0, c, 0, 0))
    return pl.pallas_call(
        _lru_kernel,
        grid=(n_seq, ncb),
        in_specs=[
            pl.BlockSpec((seq_len, cw), lambda s, c: (s + sb, c)),
            pl.BlockSpec((seq_len, cw), lambda s, c: (s + sb, c + ncb)),
            vec(), vec(),
            pl.BlockSpec((None, CONV_W, cw), lambda s, c: (prm["idx"], 0, c)),
            pl.BlockSpec((None, 1, cw), lambda s, c: (prm["idx"], 0, c)),
            gate(), par2(), gate(), par2(), par2(),
        ],
        out_specs=[
            pl.BlockSpec((seq_len, cw), lambda s, c: (s, c)),
            vec(), vec(),
        ],
        out_shape=[
            jax.ShapeDtypeStruct((n_seq * seq_len, lru_w), BF16),
            jax.ShapeDtypeStruct((n_seq, 1, lru_w), F32),
            jax.ShapeDtypeStruct((n_seq, 1, lru_w), F32),
        ],
        scratch_shapes=[pltpu.VMEM((seq_len, cw), F32)] * 4,
        compiler_params=_params(("parallel", "parallel"), 40),
        name="lru",
    )(p_ab, p_ab, h0f.reshape(n_seq, 1, lru_w), h0b.reshape(n_seq, 1, lru_w), prm["conv_w"],
      prm["conv_b"], prm["wa"], prm["ba"], prm["wx"], prm["bx"], prm["lam"])


def _block_diag_tiles(w, tile):
    n, two, nb, bw, _ = w.shape
    per = tile // bw
    w = w.reshape(n, two, nb // per, per, bw, bw)
    eye = jnp.eye(per, dtype=w.dtype)
    out = jnp.einsum("ndgpij,pq->ndgpiqj", w, eye)
    return out.reshape(n, two, nb // per, tile, tile)


def _softmax_attend(q, ks, vs, biases, scale):
    scores = []
    for k, b in zip(ks, biases):
        s = lax.dot_general(q, k, (((1,), (1,)), ((), ())), preferred_element_type=F32)
        if scale is not None:
            s = s * scale
        scores.append(s if b is None else s + b)
    m = scores[0].max(axis=-1, keepdims=True)
    for s in scores[1:]:
        m = jnp.maximum(m, s.max(axis=-1, keepdims=True))
    denom = None
    out = None
    for s, v in zip(scores, vs):
        e = jnp.exp(s - m)
        part = e.sum(axis=-1, keepdims=True)
        denom = part if denom is None else denom + part
        o = jnp.dot(e.astype(BF16), v, preferred_element_type=F32)
        out = o if out is None else out + o
    return out / denom


def _upper_half(shape):
    lane = lax.broadcasted_iota(jnp.int32, shape, len(shape) - 1)
    return lane % V7X_LANES >= V7X_LANES // 2


def _split_pair(q):
    up = _upper_half(q.shape)
    zero = jnp.zeros_like(q)
    return jnp.concatenate([jnp.where(up, zero, q), jnp.where(up, q, zero)], axis=0)


def _merge_pair(o):
    m = o.shape[0] // 2
    return jnp.where(_upper_half((m, o.shape[1])), o[m:], o[:m])


def _dup_head(block, use_high):
    swapped = pltpu.roll(block, V7X_LANES // 2, axis=1)
    keep = _upper_half(block.shape) == use_high
    return jnp.where(keep, block, swapped)


def _fold_scale(q, scale):
    return (q * scale, None) if _is_pow2(scale) else (q, scale)


def _na_ctx_kernel(q_ref, k_ref, v_ref, o_ref, *, scale):
    for pair in range(q_ref.shape[1] // V7X_LANES):
        sl = pl.ds(pair * V7X_LANES, V7X_LANES)
        q, sc = _fold_scale(q_ref[:, sl], scale)
        o = _softmax_attend(_split_pair(q.astype(BF16)), [k_ref[:, sl].astype(BF16)],
                            [v_ref[:, sl].astype(BF16)], [None], sc)
        o_ref[:, sl] = _merge_pair(o).astype(o_ref.dtype)


def _na_ctx(p_ab, n_seq, seq_len, q_blk, na_w, scale):
    blk = lambda col: pl.BlockSpec((seq_len, na_w), lambda b: (b, col))
    return pl.pallas_call(
        functools.partial(_na_ctx_kernel, scale=scale),
        grid=(n_seq,),
        in_specs=[blk(q_blk), blk(q_blk + 1), blk(q_blk + 2)],
        out_specs=pl.BlockSpec((seq_len, na_w), lambda b: (b, 0)),
        out_shape=jax.ShapeDtypeStruct((n_seq * seq_len, na_w), BF16),
        compiler_params=_params(("parallel",), 40),
        name="na_ctx",
    )(p_ab, p_ab, p_ab)


def _na_tile_index(seq_len):
    rows_n = seq_len // GRID_W
    kr = min(NA_KR, rows_n)
    rows_per_chunk = NA_Q_CHUNK // GRID_W
    n_chunks = rows_n // rows_per_chunk
    win_rows = NA_WIN // GRID_W
    table = []
    for r in range(rows_n):
        r_start = min(max(r - kr // 2, 0), rows_n - kr)
        base = 0 if r // rows_per_chunk < n_chunks // 2 else rows_n - win_rows
        table.append([k - r + NA_KR - 1 if r_start <= k < r_start + kr else 2 * NA_KR - 1
                      for k in range(base, base + win_rows)])
    return table


def _na_lat_kernel(q_ref, k_ref, v_ref, kc_ref, vc_ref, tiles_ref, o_ref, bias_ref, *, scale):
    lq = q_ref.shape[0]
    n_chunks = lq // NA_Q_CHUNK
    rows_per_chunk = NA_Q_CHUNK // GRID_W

    @pl.when(pl.program_id(1) == 0)
    def _():
        low = lax.broadcasted_iota(jnp.int32, (GRID_W, V7X_LANES), 1) < GRID_W
        for half in (0, 1):
            for r, slots in enumerate(_na_tile_index(lq)):
                c, rr = divmod(r, rows_per_chunk)
                rows = pl.ds(half * NA_Q_CHUNK + rr * GRID_W, GRID_W)
                for s in range(0, len(slots), 2):
                    blk = jnp.where(low, tiles_ref[half, slots[s]], tiles_ref[half, slots[s + 1]])
                    bias_ref[c, rows, pl.ds(s * GRID_W, V7X_LANES)] = blk

    kcb = kc_ref[...].astype(BF16)
    vcb = vc_ref[...].astype(BF16)
    for c in range(n_chunks):
        win = 0 if c < n_chunks // 2 else lq - NA_WIN
        kb = k_ref[pl.ds(win, NA_WIN), :].astype(BF16)
        vb = v_ref[pl.ds(win, NA_WIN), :].astype(BF16)
        rows = pl.ds(c * NA_Q_CHUNK, NA_Q_CHUNK)
        q, sc = _fold_scale(q_ref[rows, :], scale)
        o = _softmax_attend(_split_pair(q.astype(BF16)), [kb, kcb], [vb, vcb], [bias_ref[c], None], sc)
        o_ref[rows, :] = _merge_pair(o).astype(o_ref.dtype)


def _na_lat(p_ab, cache_k, cache_v, tiles, idx, n_seq, seq_len, row_start, q_col, n_pairs, scale):
    sb = row_start // seq_len
    past = cache_k.shape[2]
    blk = lambda col: pl.BlockSpec((seq_len, V7X_LANES), lambda h, b: (b + sb, col + h))
    cblk = lambda: pl.BlockSpec((None, None, past, V7X_LANES), lambda h, b: (b, idx, 0, h))
    n_chunks = seq_len // NA_Q_CHUNK
    return pl.pallas_call(
        functools.partial(_na_lat_kernel, scale=scale),
        grid=(n_pairs, n_seq),
        in_specs=[blk(q_col), blk(q_col + n_pairs), blk(q_col + 2 * n_pairs), cblk(), cblk(),
                  pl.BlockSpec((None, 2, 2 * NA_KR, GRID_W, V7X_LANES), lambda h, b: (idx, h, 0, 0, 0))],
        out_specs=pl.BlockSpec((seq_len, V7X_LANES), lambda h, b: (b, h)),
        out_shape=jax.ShapeDtypeStruct((n_seq * seq_len, n_pairs * V7X_LANES), BF16),
        scratch_shapes=[pltpu.VMEM((n_chunks, 2 * NA_Q_CHUNK, NA_WIN), F32)],
        compiler_params=_params(("parallel", "arbitrary"), 56),
        name="na_lat",
    )(p_ab, p_ab, p_ab, cache_k, cache_v, tiles)


def _na_bias_tiles(tab):
    lead = tab.shape[:-1]
    edge = GRID_W - NA_KC
    vec = jnp.pad(tab, [(0, 0)] * len(lead) + [(edge, edge + 1)], mode="edge")
    skew = jnp.broadcast_to(vec[..., None, :], lead + (GRID_W, 2 * GRID_W))
    skew = skew.reshape(lead + (2 * GRID_W * GRID_W,))[..., :GRID_W * (2 * GRID_W - 1)]
    toep = skew.reshape(lead + (GRID_W, 2 * GRID_W - 1))[..., GRID_W - 1:]
    cols = np.arange(GRID_W)
    c_start = np.clip(cols - NA_KC // 2, 0, GRID_W - NA_KC)
    col_ok = (cols[None, :] >= c_start[:, None]) & (cols[None, :] < c_start[:, None] + NA_KC)
    tiles = jnp.where(col_ok, toep, NEG_BIAS)
    masked = jnp.full(lead[:-1] + (1, GRID_W, GRID_W), NEG_BIAS, F32)
    tiles = jnp.concatenate([tiles, masked], axis=-3)
    return jnp.concatenate([tiles, tiles], axis=-1)


def _split_dot(v, m):
    hi = v.astype(BF16)
    lo = (v - hi.astype(F32)).astype(BF16)
    return jnp.dot(hi, m, preferred_element_type=F32) + jnp.dot(lo, m, preferred_element_type=F32)


def _head_rms(x, gather, spread, gain, head_dim):
    inv = lax.rsqrt(_split_dot(x * x, gather) * (1.0 / head_dim) + EPS)
    return x * _split_dot(inv, spread) * gain


def _rope(x, cos, sin_lo, sin_hi):
    w = x.shape[1]
    reps = w // V7X_LANES
    tile = lambda t: t if reps == 1 else jnp.concatenate([t] * reps, axis=1)
    quarter = 16
    return (x * tile(cos) + pltpu.roll(x, w - quarter, axis=1) * tile(sin_lo)
            + pltpu.roll(x, quarter, axis=1) * tile(sin_hi))


def _cd_post_kernel(p_ref, gat_ref, spr_ref, qg_ref, kg_ref, mqg_ref, mkg_ref, wuq_ref, wukv_ref, *rest,
                    rope, head_dim, q_w, kv_w, rank, q_scale):
    if rope:
        cos_ref, slo_ref, shi_ref = rest[:3]
        rest = rest[3:]
        tabs = (cos_ref[...], slo_ref[...], shi_ref[...])
    q_ref, k_ref, qn_ref, qr_ref, ckv_ref, kn_ref, vm_ref, kr_ref = rest
    c0 = 0
    qc = p_ref[:, pl.ds(c0, q_w)]
    c0 += q_w
    kc = p_ref[:, pl.ds(c0, kv_w)]
    c0 += 2 * kv_w
    qa = p_ref[:, pl.ds(c0, rank)]
    c0 += rank
    ckv = p_ref[:, pl.ds(c0, rank)]
    c0 += rank
    kr = p_ref[:, pl.ds(c0, V7X_LANES)]

    qc = _head_rms(qc, gat_ref[...], spr_ref[...], qg_ref[...], head_dim)
    kc = _head_rms(kc, gat_ref[pl.ds(0, kv_w), :], spr_ref[:, pl.ds(0, kv_w)], kg_ref[...], head_dim)
    qd = jnp.dot(_rms(qa, mqg_ref[...]).astype(BF16), wuq_ref[...], preferred_element_type=F32)
    n_nope = qn_ref.shape[1]
    qn = qd[:, :n_nope]
    qr = qd[:, n_nope:]
    ckv = _rms(ckv, mkg_ref[...])
    if rope:
        qc = _rope(qc, *tabs)
        kc = _rope(kc, *tabs)
        qr = _rope(qr, *tabs)
        kr = _rope(kr, *tabs)
    q_ref[...] = (qc * q_scale).astype(BF16)
    k_ref[...] = kc
    qn_ref[...] = qn.astype(BF16)
    qr_ref[...] = qr.astype(BF16)
    ckv_ref[...] = ckv
    kv = jnp.dot(ckv.astype(BF16), wukv_ref[...], preferred_element_type=F32)
    kn_ref[...] = kv[:, :n_nope].astype(BF16)
    vm_ref[...] = kv[:, n_nope:].astype(BF16)
    kr_ref[...] = kr


def _cd_post(p_cd, prm, row_start, rows, rope_tabs, dims, q_scale):
    q_w, kv_w, rank, head_dim, n_nope, n_rope, n_v = dims
    tm = 256
    off = row_start // tm
    n_in = p_cd.shape[1]
    idx = prm["idx"]
    rope = rope_tabs is not None
    const = lambda shape: pl.BlockSpec(shape, lambda i: (0,) * len(shape))
    layer = lambda shape: pl.BlockSpec((None,) + shape, lambda i: (idx,) + (0,) * len(shape))
    in_specs = [
        pl.BlockSpec((tm, n_in), lambda i: (i + off, 0)),
        const((q_w, V7X_LANES)), const((V7X_LANES, q_w)),
        layer((1, q_w)), layer((1, kv_w)), layer((1, rank)), layer((1, rank)),
        layer((rank, n_nope + n_rope)), layer((rank, n_nope + n_v)),
    ]
    args = [p_cd, prm["gather"], prm["spread"], prm["q_gain"], prm["k_gain"], prm["mla_q_gain"],
            prm["mla_kv_gain"], prm["w_uq"], prm["w_ukv"]]
    if rope:
        seq_tiles = rope_tabs[0].shape[0] // tm
        in_specs += [pl.BlockSpec((tm, V7X_LANES), lambda i: (i % seq_tiles, 0))] * 3
        args += list(rope_tabs)
    widths = [(q_w, BF16), (kv_w, F32), (n_nope, BF16), (n_rope, BF16), (rank, F32), (n_nope, BF16),
              (n_v, BF16), (V7X_LANES, F32)]
    kern = functools.partial(_cd_post_kernel, rope=rope, head_dim=head_dim, q_w=q_w, kv_w=kv_w, rank=rank,
                             q_scale=q_scale)
    return pl.pallas_call(
        kern,
        grid=(rows // tm,),
        in_specs=in_specs,
        out_specs=[pl.BlockSpec((tm, w), lambda i: (i, 0)) for w, _ in widths],
        out_shape=[jax.ShapeDtypeStruct((rows, w), dt) for w, dt in widths],
        compiler_params=_params(("parallel",), 48),
        name="cd_post",
    )(*args)


def _rope_tables(seq_len, head_dim):
    half = head_dim // 2
    nf = half // 2
    t = np.arange(seq_len)
    inv_freq = (1.0 / (ROPE_THETA ** (np.arange(nf, dtype=np.float32) / nf))).astype(np.float32)
    zeros = np.zeros((seq_len, nf), np.float32)
    cos, slo, shi = [], [], []
    for pos in (t // GRID_W, t % GRID_W):
        ang = pos.astype(np.float32)[:, None] * inv_freq[None, :]
        c, s = np.cos(ang).astype(np.float32), np.sin(ang).astype(np.float32)
        cos += [c, c]
        slo += [-s, zeros]
        shi += [zeros, s]
    reps = V7X_LANES // head_dim
    return tuple(jnp.asarray(np.tile(np.concatenate(x, axis=1), (1, reps))) for x in (cos, slo, shi))


def _ckv_up_kernel(c_ref, w_ref, kn_ref, vm_ref):
    kv = jnp.dot(c_ref[...].astype(BF16), w_ref[...], preferred_element_type=F32)
    n = kn_ref.shape[1]
    kn_ref[...] = kv[:, :n].astype(BF16)
    vm_ref[...] = kv[:, n:].astype(BF16)


def _ckv_up(ckv, w_ukv, idx, n_nope):
    n_seq, _, tm, rank = ckv.shape
    rows = n_seq * tm
    n = w_ukv.shape[2]
    return pl.pallas_call(
        _ckv_up_kernel,
        grid=(rows // tm,),
        in_specs=[pl.BlockSpec((None, None, tm, rank), lambda i: (i, idx, 0, 0)),
                  pl.BlockSpec((None, rank, n), lambda i: (idx, 0, 0))],
        out_specs=[pl.BlockSpec((tm, n_nope), lambda i: (i, 0)), pl.BlockSpec((tm, n - n_nope), lambda i: (i, 0))],
        out_shape=[jax.ShapeDtypeStruct((rows, n_nope), BF16), jax.ShapeDtypeStruct((rows, n - n_nope), BF16)],
        compiler_params=_params(("parallel",), 32),
        name="ckv_up",
    )(ckv, w_ukv)


def _gqa_kernel(q_ref, k_ref, v_ref, *rest, scale, has_cache, group_w):
    if has_cache:
        kc_ref, vc_ref, o_ref = rest
    else:
        (o_ref,) = rest
    lq = q_ref.shape[0]
    n_local = q_ref.shape[1] // group_w
    n_pairs = group_w // V7X_LANES
    for hl in range(n_local):
        if n_local == 1:
            high = (pl.program_id(1) % 2) == 1
            pick = lambda ref: ref[...]
        else:
            high = hl % 2 == 1
            pick = lambda ref, hl=hl: ref[:, pl.ds((hl // 2) * V7X_LANES, V7X_LANES)]
        ks = [_dup_head(pick(k_ref), high).astype(BF16)]
        vs = [_dup_head(pick(v_ref), high).astype(BF16)]
        if has_cache:
            ks.append(_dup_head(pick(kc_ref), high).astype(BF16))
            vs.append(_dup_head(pick(vc_ref), high).astype(BF16))
        for c in range(lq // Q_CHUNK):
            rows = pl.ds(c * Q_CHUNK, Q_CHUNK)
            cols = [pl.ds(hl * group_w + p * V7X_LANES, V7X_LANES) for p in range(n_pairs)]
            q = jnp.concatenate([_split_pair(q_ref[rows, cl]) for cl in cols], axis=0)
            o = _softmax_attend(q, ks, vs, [None] * len(ks), scale)
            for p, cl in enumerate(cols):
                o_ref[rows, cl] = _merge_pair(o[2 * p * Q_CHUNK:2 * (p + 1) * Q_CHUNK]).astype(o_ref.dtype)


def _gqa(q, k, p_cd, v_col, cache_k, cache_v, idx, n_seq, seq_len, row_start, n_kv, heads_per_step, scale):
    sb = row_start // seq_len
    group_w = q.shape[1] // n_kv
    has_cache = cache_k is not None
    if heads_per_step == 1:
        kv_w, kv_blk, v_blk = V7X_LANES, (lambda h: h // 2), (lambda h: v_col + h // 2)
    else:
        kv_w = k.shape[1]
        kv_blk, v_blk = (lambda h: 0), (lambda h: v_col * V7X_LANES // kv_w)
    in_specs = [
        pl.BlockSpec((seq_len, group_w * heads_per_step), lambda b, h: (b, h)),
        pl.BlockSpec((seq_len, kv_w), lambda b, h: (b, kv_blk(h))),
        pl.BlockSpec((seq_len, kv_w), lambda b, h: (b + sb, v_blk(h))),
    ]
    args = [q, k, p_cd]
    if has_cache:
        past = cache_k.shape[2]
        in_specs += [pl.BlockSpec((None, None, past, kv_w), lambda b, h: (b, idx, 0, kv_blk(h)))] * 2
        args += [cache_k, cache_v]
    return pl.pallas_call(
        functools.partial(_gqa_kernel, scale=scale, has_cache=has_cache, group_w=group_w),
        grid=(n_seq, n_kv // heads_per_step),
        in_specs=in_specs,
        out_specs=pl.BlockSpec((seq_len, group_w * heads_per_step), lambda b, h: (b, h)),
        out_shape=jax.ShapeDtypeStruct(q.shape, BF16),
        compiler_params=_params(("parallel", "parallel"), 56),
        name="gqa",
    )(*args)


def _mla_kernel(qn_ref, qr_ref, kn_ref, kr_ref, v_ref, *rest, scale, has_cache):
    if has_cache:
        knc_ref, krc_ref, vc_ref, o_ref = rest
        krc = krc_ref[...].astype(BF16)
    else:
        (o_ref,) = rest
    lq = qn_ref.shape[0]
    qc = min(2 * Q_CHUNK, lq)
    n_local = qn_ref.shape[1] // V7X_LANES
    kr = kr_ref[...].astype(BF16)
    for hl in range(n_local):
        sl = pl.ds(hl * V7X_LANES, V7X_LANES)
        if n_local == 1:
            high = (pl.program_id(1) % 2) == 1
            pair = pl.ds(0, V7X_LANES)
        else:
            high = hl % 2 == 1
            pair = pl.ds((hl // 2) * V7X_LANES, V7X_LANES)
        ks = [jnp.concatenate([kn_ref[:, sl], kr], axis=1)]
        vs = [v_ref[:, sl]]
        if has_cache:
            ks.append(jnp.concatenate([knc_ref[:, sl], krc], axis=1))
            vs.append(vc_ref[:, sl])
        for c in range(lq // qc):
            rows = pl.ds(c * qc, qc)
            qr = qr_ref[rows, pair]
            qr = jnp.where(_upper_half(qr.shape) == high, qr, jnp.zeros_like(qr))
            q = jnp.concatenate([qn_ref[rows, sl], qr], axis=1)
            o_ref[rows, sl] = _softmax_attend(q, ks, vs, [None] * len(ks), scale).astype(o_ref.dtype)


def _mla(qn, qr, kn, kr2, vm, cache, n_seq, seq_len, heads_per_step, scale):
    has_cache = cache is not None
    n_heads = vm.shape[1] // V7X_LANES
    hw = heads_per_step * V7X_LANES
    head = lambda rows: pl.BlockSpec((rows, hw), lambda b, h: (b, h))
    if heads_per_step == 1:
        rope_q = pl.BlockSpec((seq_len, V7X_LANES), lambda b, h: (b, h // 2))
    else:
        rope_q = pl.BlockSpec((seq_len, hw // 2), lambda b, h: (b, h))
    in_specs = [head(seq_len), rope_q, head(seq_len),
                pl.BlockSpec((seq_len, V7X_LANES), lambda b, h: (b, 0)),
                head(seq_len)]
    args = [qn, qr, kn, kr2, vm]
    if has_cache:
        knc, krc, vmc, idx = cache
        past = krc.shape[2]
        in_specs += [head(past),
                     pl.BlockSpec((None, None, past, V7X_LANES), lambda b, h: (b, idx, 0, 0)),
                     head(past)]
        args += [knc, krc, vmc]
    return pl.pallas_call(
        functools.partial(_mla_kernel, scale=scale, has_cache=has_cache),
        grid=(n_seq, n_heads // heads_per_step),
        in_specs=in_specs,
        out_specs=head(seq_len),
        out_shape=jax.ShapeDtypeStruct(vm.shape, BF16),
        compiler_params=_params(("parallel", "parallel"), 48),
        name="mla",
    )(*args)


def kernel(x_prompt, x_sample, state_lru_fwd, state_lru_bwd, cache_na_k, cache_na_v, cache_gqa_k, cache_gqa_v, cache_mla_ckv, cache_mla_krope, c, c_ctx, w_mod, b_mod, norm_gain, w_ffn_in, w_ffn_out, w_in_ab, conv_w, conv_b, lru_wa, lru_ba, lru_wx, lru_bx, lru_lambda, na_bias, w_out_ab, w_in_cd, gqa_q_gain, gqa_k_gain, mla_q_gain, mla_kv_gain, mla_w_uq, mla_w_uk, mla_w_uv, w_out_cd, final_gain):
    batch, seq, d = x_prompt.shape
    dec_batch, dec_seq, _ = x_sample.shape
    depth = w_mod.shape[0]
    n_even, n_odd = w_in_ab.shape[0], w_in_cd.shape[0]
    n_p = batch * seq
    n_s = dec_batch * dec_seq
    past = cache_na_k.shape[2]
    lru_w = state_lru_fwd.shape[-1]
    na_heads, na_dh = cache_na_k.shape[3], cache_na_k.shape[4]
    na_w = na_heads * na_dh
    gqa_kv, gqa_dh = cache_gqa_k.shape[3], cache_gqa_k.shape[4]
    kv_w = gqa_kv * gqa_dh
    rank = cache_mla_ckv.shape[-1]
    rope_w = cache_mla_krope.shape[-1]
    q_w = w_in_cd.shape[2] - 2 * kv_w - 2 * rank - rope_w
    mla_heads = mla_w_uk.shape[2] // MLA_NOPE
    n_nope = mla_heads * MLA_NOPE
    n_rope = mla_heads * rope_w
    n_v = mla_heads * MLA_V
    mla_qk = MLA_NOPE + rope_w
    assert mla_q_gain.shape[-1] == rank and 2 * rope_w == V7X_LANES
    assert n_p % dec_seq == 0 and dec_seq % TOKEN_TILE == 0 and seq % Q_CHUNK == 0
    assert dec_seq // GRID_W == 2 * NA_KR and na_dh == GRID_W and gqa_dh == GRID_W
    assert w_in_ab.shape[2] == 2 * lru_w + 3 * na_w and lru_w == na_w
    assert (q_w // gqa_kv) % V7X_LANES == 0 and q_w % kv_w == 0

    x = (x_prompt.reshape(n_p, d), x_sample.reshape(n_s, d))
    cond = jnp.concatenate([c_ctx[None, :], c, jnp.zeros((MOD_ROWS - 1 - dec_batch, d), F32)], axis=0)
    mods = _modulation(cond, w_mod, b_mod)
    gains = norm_gain.reshape(depth * 3, 1, d)

    ffn_w = (w_ffn_in[0, 0].astype(BF16), w_ffn_out[0, 0].astype(BF16))

    def ffn_step(x, ffn_w, layer, which):
        last = layer == depth - 1 and which == 1
        nxt = None if last else (w_ffn_in, w_ffn_out, layer + which, 1 - which)
        out = _ffn(x, mods, gains, ffn_w[0], ffn_w[1], layer, which, nxt, n_p, dec_seq)
        return out[0], tuple(out[1:])

    w_in_ab_bf = w_in_ab.astype(BF16)
    w_out_ab_bf = w_out_ab.astype(BF16)
    w_in_cd_bf = jnp.concatenate([w_in_cd, w_in_cd[:, :, -rope_w:]], axis=2).astype(BF16)
    w_out_cd_bf = w_out_cd.astype(BF16)

    lru_prm = {
        "conv_w": conv_w, "conv_b": conv_b.reshape(n_even, 1, lru_w),
        "wa": _block_diag_tiles(lru_wa, LRU_COLS).astype(BF16),
        "wx": _block_diag_tiles(lru_wx, LRU_COLS).astype(BF16),
        "ba": lru_ba.reshape(n_even, 2, 1, lru_w), "bx": lru_bx.reshape(n_even, 2, 1, lru_w),
        "lam": lru_lambda.reshape(n_even, 2, 1, lru_w),
    }
    zeros_state = jnp.zeros((batch, lru_w), F32)
    na_k_ctx = cache_na_k.reshape(dec_batch, n_even, past, na_w)
    na_v_ctx = cache_na_v.reshape(dec_batch, n_even, past, na_w)
    na_tiles = _na_bias_tiles(na_bias)

    w_uq = mla_w_uq.reshape(n_odd, rank, mla_heads, mla_qk)
    w_uq = jnp.concatenate([w_uq[..., :MLA_NOPE].reshape(n_odd, rank, n_nope),
                            w_uq[..., MLA_NOPE:].reshape(n_odd, rank, n_rope)], axis=2).astype(BF16)
    gqa_scale = gqa_dh ** -0.5
    q_scale = gqa_scale if _is_pow2(gqa_scale) else 1.0
    head_of_lane = np.arange(q_w) // gqa_dh
    gather = (head_of_lane[:, None] == np.arange(V7X_LANES)[None, :]).astype(np.float32)
    assert q_w // gqa_dh <= V7X_LANES
    cd_prm = {
        "gather": jnp.asarray(gather, BF16), "spread": jnp.asarray(gather.T, BF16),
        "q_gain": jnp.tile(gqa_q_gain, (1, q_w // gqa_dh)).reshape(n_odd, 1, q_w),
        "k_gain": jnp.tile(gqa_k_gain, (1, gqa_kv)).reshape(n_odd, 1, kv_w),
        "mla_q_gain": mla_q_gain.reshape(n_odd, 1, rank),
        "mla_kv_gain": mla_kv_gain.reshape(n_odd, 1, rank),
        "w_uq": w_uq,
        "w_ukv": jnp.concatenate([mla_w_uk, mla_w_uv], axis=2).astype(BF16),
    }
    rope_tabs = _rope_tables(dec_seq, gqa_dh)
    gqa_k_ctx = cache_gqa_k.reshape(dec_batch, n_odd, past, kv_w)
    gqa_v_ctx = cache_gqa_v.reshape(dec_batch, n_odd, past, kv_w)
    krope_ctx = jnp.concatenate([cache_mla_krope, cache_mla_krope], axis=-1)

    st_f, st_b, na_k, na_v, gq_k, gq_v, ml_c, ml_r = [], [], [], [], [], [], [], []
    for layer in range(depth):
        jdx = layer // 2
        x, ffn_w = ffn_step(x, ffn_w, layer, 0)
        if layer % 2 == 0:
            p_ab = _inproj(x, mods, gains, w_in_ab_bf, layer, jdx, 2, n_p, dec_seq)
            prm = dict(lru_prm, idx=jdx)
            ya_p, lf, lb = _lru(p_ab, zeros_state, zeros_state, prm, seq, batch, 0)
            ya_s, _, _ = _lru(p_ab, state_lru_fwd[:, jdx], state_lru_bwd[:, jdx], prm, dec_seq, dec_batch, n_p)
            scale = na_dh ** -0.5
            yb_p = _na_ctx(p_ab, batch, seq, 2 * lru_w // na_w, na_w, scale)
            yb_s = _na_lat(p_ab, na_k_ctx, na_v_ctx, na_tiles, jdx, dec_batch, dec_seq, n_p,
                           2 * lru_w // V7X_LANES, na_w // V7X_LANES, scale)
            x = _outproj(x, mods, layer, ya_p, ya_s, yb_p, yb_s, w_out_ab_bf, jdx, n_p, dec_seq)
            st_f.append(lf.reshape(batch, lru_w))
            st_b.append(lb.reshape(batch, lru_w))
            na_k.append(p_ab[:n_p, 2 * lru_w + na_w:2 * lru_w + 2 * na_w].reshape(batch, seq, na_heads, na_dh))
            na_v.append(p_ab[:n_p, 2 * lru_w + 2 * na_w:].reshape(batch, seq, na_heads, na_dh))
        else:
            p_cd = _inproj(x, mods, gains, w_in_cd_bf, layer, jdx, 1, n_p, dec_seq)
            prm = dict(cd_prm, idx=jdx)
            dims = (q_w, kv_w, rank, gqa_dh, n_nope, n_rope, n_v)
            qp, kp, qnp_, qrp, ckvp, knp_, vmp, krp = _cd_post(p_cd, prm, 0, n_p, None, dims, q_scale)
            qs, ks_, qns, qrs, _, kns, vms, krs = _cd_post(p_cd, prm, n_p, n_s, rope_tabs, dims, q_scale)
            v_col = (q_w + kv_w) // V7X_LANES
            att_scale = None if q_scale != 1.0 else gqa_scale
            yc_p = _gqa(qp, kp, p_cd, v_col, None, None, jdx, batch, seq, 0, gqa_kv, gqa_kv, att_scale)
            yc_s = _gqa(qs, ks_, p_cd, v_col, gqa_k_ctx, gqa_v_ctx, jdx, dec_batch, dec_seq, n_p,
                        gqa_kv, 1, att_scale)
            knc, vmc = _ckv_up(cache_mla_ckv, cd_prm["w_ukv"], jdx, n_nope)
            yd_p = _mla(qnp_, qrp, knp_, krp, vmp, None, batch, seq, mla_heads, mla_qk ** -0.5)
            yd_s = _mla(qns, qrs, kns, krs, vms, (knc, krope_ctx, vmc, jdx), dec_batch, dec_seq, 2,
                        mla_qk ** -0.5)
            x = _outproj(x, mods, layer, yc_p, yc_s, yd_p, yd_s, w_out_cd_bf, jdx, n_p, dec_seq)
            gq_k.append(kp.reshape(batch, seq, gqa_kv, gqa_dh))
            gq_v.append(p_cd[:n_p, q_w + kv_w:q_w + 2 * kv_w].reshape(batch, seq, gqa_kv, gqa_dh))
            ml_c.append(ckvp.reshape(batch, seq, rank))
            ml_r.append(krp[:, :rope_w].reshape(batch, seq, rope_w))
        x, ffn_w = ffn_step(x, ffn_w, layer, 1)

    y_prompt = _final_norm(x, final_gain, 0, n_p).reshape(batch, seq, d)
    y_sample = _final_norm(x, final_gain, n_p, n_s).reshape(dec_batch, dec_seq, d)
    stack = lambda xs: jnp.stack(xs, axis=1)
    return (y_prompt, y_sample, stack(st_f), stack(st_b), stack(na_k), stack(na_v), stack(gq_k), stack(gq_v),
            stack(ml_c), stack(ml_r))
```

```python
import functools
import math

import numpy as np

import jax
import jax.numpy as jnp
from jax import lax
from jax.experimental import pallas as pl
from jax.experimental.pallas import tpu as pltpu

F32 = jnp.float32
BF16 = jnp.bfloat16

EPS = 1e-6
N_MOD = 9
GRID_W = 64
NA_KR = 8
NA_KC = 16
ROPE_THETA = 10000.0
LRU_C = 8.0
CONV_W = 4
MLA_NOPE = 128
MLA_V = 128
NEG_BIAS = -1e30

V7X_LANES = 128
V7X_SUBLANES = 8
MOD_ROWS = 16
TOKEN_TILE = 512
FF_TILE = 512
Q_CHUNK = 128
NA_Q_CHUNK = 256
NA_WIN = 768
LRU_COLS = 512


def _params(sem, vmem_mib):
    return pltpu.CompilerParams(dimension_semantics=sem, vmem_limit_bytes=vmem_mib * 1024 * 1024)


def _silu(x):
    return x * jax.nn.sigmoid(x)


def _rms(x, gain):
    ms = jnp.mean(x * x, axis=-1, keepdims=True)
    return x * lax.rsqrt(ms + EPS) * gain


def _mod_row(i, n_prompt_tiles, tiles_per_seq):
    return jnp.where(i < n_prompt_tiles, 0, 1 + (i - n_prompt_tiles) // tiles_per_seq)


def _adaln(x, mod_ref, gain_ref, row, mod_base):
    d = x.shape[1]
    shift = mod_ref[pl.ds(row, 1), pl.ds(mod_base * d, d)]
    scale = mod_ref[pl.ds(row, 1), pl.ds((mod_base + 1) * d, d)]
    return (_rms(x, gain_ref[...]) * (1 + scale) + shift).astype(BF16)


def _is_pow2(v):
    return math.frexp(v)[0] == 0.5


def _mod_kernel(cond_ref, w_ref, b_ref, o_ref):
    s = _silu(cond_ref[...]).astype(BF16)
    o_ref[...] = jnp.dot(s, w_ref[...].astype(BF16), preferred_element_type=F32) + b_ref[...]


def _modulation(cond, w_mod, b_mod):
    depth, d, n = w_mod.shape
    tn = 1024
    return pl.pallas_call(
        _mod_kernel,
        grid=(depth, n // tn),
        in_specs=[
            pl.BlockSpec((MOD_ROWS, d), lambda l, j: (0, 0)),
            pl.BlockSpec((None, d, tn), lambda l, j: (l, 0, j)),
            pl.BlockSpec((None, 1, tn), lambda l, j: (l, 0, j)),
        ],
        out_specs=pl.BlockSpec((None, MOD_ROWS, tn), lambda l, j: (l, 0, j)),
        out_shape=jax.ShapeDtypeStruct((depth, MOD_ROWS, n), F32),
        compiler_params=_params(("parallel", "parallel"), 40),
        name="modulation",
    )(cond, w_mod, b_mod.reshape(depth, 1, n))


def _ffn_kernel(*refs, mod_base, n_prompt_tiles, tiles_per_seq, cast_next, split_x):
    i = pl.program_id(0)
    j = pl.program_id(1)
    if split_x:
        xp_ref, xs_ref, *refs = refs
        read_x = lambda: jnp.where(i < n_prompt_tiles, xp_ref[...], xs_ref[...])
    else:
        x_ref, *refs = refs
        read_x = lambda: x_ref[...]
    mod_ref, gain_ref, wg_ref, wu_ref, wo_ref, *rest = refs
    if cast_next:
        nin_ref, nout_ref, o_ref, cin_ref, cout_ref, h_ref = rest
    else:
        fg_ref, yp_ref, ys_ref, h_ref, o_ref = rest
    d = o_ref.shape[1]
    nj = pl.num_programs(1)
    row = _mod_row(i, n_prompt_tiles, tiles_per_seq)

    def step(first, last):
        if first:
            h = _adaln(read_x(), mod_ref, gain_ref, row, mod_base)
            h_ref[...] = h
        else:
            h = h_ref[...]
        g = jnp.dot(h, wg_ref[...], preferred_element_type=F32)
        u = jnp.dot(h, wu_ref[...], preferred_element_type=F32)
        if cast_next:
            cin_ref[...] = nin_ref[...].astype(BF16)
            cout_ref[...] = nout_ref[...].astype(BF16)
        act = (_silu(g) * u).astype(BF16)
        acc = jnp.dot(act, wo_ref[...], preferred_element_type=F32)
        if not first:
            acc = o_ref[...] + acc
        if last:
            gate = mod_ref[pl.ds(row, 1), pl.ds((mod_base + 2) * d, d)]
            acc = read_x() + (0.5 * gate) * acc
        if last and not cast_next:
            y = _rms(acc, fg_ref[...])

            @pl.when(i < n_prompt_tiles)
            def _():
                yp_ref[...] = y

            @pl.when(i >= n_prompt_tiles)
            def _():
                ys_ref[...] = y
        else:
            o_ref[...] = acc

    pl.when(j == 0)(lambda: step(True, False))
    pl.when((j > 0) & (j < nj - 1))(lambda: step(False, False))
    pl.when(j == nj - 1)(lambda: step(False, True))


def _ffn(x, mods, gains, w_in, w_out, layer, which, nxt, final_gain, n_prompt_rows, seq_rows):
    split_x = isinstance(x, tuple)
    npt = n_prompt_rows // TOKEN_TILE
    if split_x:
        t, d = x[0].shape[0] + x[1].shape[0], x[0].shape[1]
        x_specs = [pl.BlockSpec((TOKEN_TILE, d), lambda i, j: (jnp.minimum(i, npt - 1), 0)),
                   pl.BlockSpec((TOKEN_TILE, d), lambda i, j: (jnp.maximum(i - npt, 0), 0))]
        x_args = list(x)
    else:
        t, d = x.shape
        x_specs = [pl.BlockSpec((TOKEN_TILE, d), lambda i, j: (i, 0))]
        x_args = [x]
    dff = w_out.shape[0]
    nj = dff // FF_TILE
    ni = t // TOKEN_TILE
    assert nj >= 2
    mod_base = 6 * which
    kern = functools.partial(_ffn_kernel, mod_base=mod_base, n_prompt_tiles=npt,
                             tiles_per_seq=seq_rows // TOKEN_TILE, cast_next=nxt is not None, split_x=split_x)
    in_specs = x_specs + [
        pl.BlockSpec((None, MOD_ROWS, N_MOD * d), lambda i, j: (layer, 0, 0)),
        pl.BlockSpec((None, 1, d), lambda i, j: (3 * layer + 2 * which, 0, 0)),
        pl.BlockSpec((d, FF_TILE), lambda i, j: (0, j)),
        pl.BlockSpec((d, FF_TILE), lambda i, j: (0, j + nj)),
        pl.BlockSpec((FF_TILE, d), lambda i, j: (j, 0)),
    ]
    args = x_args + [mods, gains, w_in, w_in, w_out]
    out_specs = [pl.BlockSpec((TOKEN_TILE, d), lambda i, j: (i, 0))]
    out_shape = [jax.ShapeDtypeStruct((t, d), F32)]
    scratch = [pltpu.VMEM((TOKEN_TILE, d), BF16)]
    if nxt is None:
        in_specs.append(pl.BlockSpec((1, d), lambda i, j: (0, 0)))
        args.append(final_gain.reshape(1, d))
        out_specs = [pl.BlockSpec((TOKEN_TILE, d), lambda i, j: (jnp.minimum(i, npt - 1), 0)),
                     pl.BlockSpec((TOKEN_TILE, d), lambda i, j: (jnp.maximum(i - npt, 0), 0))]
        out_shape = [jax.ShapeDtypeStruct((n_prompt_rows, d), F32),
                     jax.ShapeDtypeStruct((t - n_prompt_rows, d), F32)]
        scratch.append(pltpu.VMEM((TOKEN_TILE, d), F32))
    else:
        f_in, f_out, nl, nw = nxt
        cast_rows = min(r for r in (16, 32, 64, 128, 256, 512)
                        if d % r == 0 and dff % r == 0 and d // r + dff // r <= ni * nj)
        n_in, n_out = d // cast_rows, dff // cast_rows
        in_blk = lambda i, j: jnp.minimum(i * nj + j, n_in - 1)
        out_blk = lambda i, j: jnp.clip(i * nj + j - n_in, 0, n_out - 1)
        in_specs += [pl.BlockSpec((None, None, cast_rows, 2 * dff), lambda i, j: (nl, nw, in_blk(i, j), 0)),
                     pl.BlockSpec((None, None, cast_rows, d), lambda i, j: (nl, nw, out_blk(i, j), 0))]
        args += [f_in, f_out]
        out_specs += [pl.BlockSpec((cast_rows, 2 * dff), lambda i, j: (in_blk(i, j), 0)),
                      pl.BlockSpec((cast_rows, d), lambda i, j: (out_blk(i, j), 0))]
        out_shape += [jax.ShapeDtypeStruct((d, 2 * dff), BF16), jax.ShapeDtypeStruct((dff, d), BF16)]
    return pl.pallas_call(
        kern,
        grid=(ni, nj),
        in_specs=in_specs,
        out_specs=out_specs,
        out_shape=out_shape,
        scratch_shapes=scratch,
        compiler_params=_params(("arbitrary", "arbitrary"), 56),
        name="ffn",
    )(*args)


def _inproj_kernel(x_ref, mod_ref, gain_ref, w_ref, o_ref, *, n_prompt_tiles, tiles_per_seq):
    row = _mod_row(pl.program_id(0), n_prompt_tiles, tiles_per_seq)
    tn = o_ref.shape[1]

    def block(c):
        h = _adaln(x_ref[...], mod_ref, gain_ref, row, 3)
        o_ref[...] = jnp.dot(h, w_ref[:, pl.ds(c * tn, tn)], preferred_element_type=F32)

    for c in range(w_ref.shape[1] // tn):
        pl.when(pl.program_id(1) == c)(functools.partial(block, c))


def _inproj(x, mods, gains, w, layer, w_idx, ncb, n_prompt_rows, seq_rows):
    t, d = x.shape
    tn = w.shape[2] // ncb
    tm = TOKEN_TILE
    kern = functools.partial(_inproj_kernel, n_prompt_tiles=n_prompt_rows // tm, tiles_per_seq=seq_rows // tm)
    return pl.pallas_call(
        kern,
        grid=(t // tm, ncb),
        in_specs=[
            pl.BlockSpec((tm, d), lambda i, j: (i, 0)),
            pl.BlockSpec((None, MOD_ROWS, N_MOD * d), lambda i, j: (layer, 0, 0)),
            pl.BlockSpec((None, 1, d), lambda i, j: (3 * layer + 1, 0, 0)),
            pl.BlockSpec((None, d, ncb * tn), lambda i, j: (w_idx, 0, 0), pipeline_mode=pl.Buffered(1)),
        ],
        out_specs=pl.BlockSpec((tm, tn), lambda i, j: (i, j)),
        out_shape=jax.ShapeDtypeStruct((t, ncb * tn), F32),
        compiler_params=_params(("parallel", "parallel"), 52),
        name="inproj",
    )(x, mods, gains, w)


def _outproj_kernel(x_ref, mod_ref, ap_ref, as_ref, bp_ref, bs_ref, w_ref, o_ref, *,
                    n_prompt_tiles, tiles_per_seq):
    i = pl.program_id(0)
    d = x_ref.shape[1]
    half = ap_ref.shape[1]
    is_prompt = i < n_prompt_tiles
    ya = jnp.where(is_prompt, ap_ref[...], as_ref[...])
    yb = jnp.where(is_prompt, bp_ref[...], bs_ref[...])
    y = jnp.dot(ya, w_ref[pl.ds(0, half), :], preferred_element_type=F32)
    y = y + jnp.dot(yb, w_ref[pl.ds(half, half), :], preferred_element_type=F32)
    row = _mod_row(i, n_prompt_tiles, tiles_per_seq)
    gate = mod_ref[pl.ds(row, 1), pl.ds(5 * d, d)]
    o_ref[...] = x_ref[...] + gate * y


def _outproj(x, mods, layer, ya_p, ya_s, yb_p, yb_s, w, w_idx, n_prompt_rows, seq_rows):
    t, d = x.shape
    half = ya_p.shape[1]
    npt = n_prompt_rows // TOKEN_TILE
    kern = functools.partial(_outproj_kernel, n_prompt_tiles=npt, tiles_per_seq=seq_rows // TOKEN_TILE)
    p_map = lambda i: (jnp.minimum(i, npt - 1), 0)
    s_map = lambda i: (jnp.maximum(i - npt, 0), 0)
    return pl.pallas_call(
        kern,
        grid=(t // TOKEN_TILE,),
        in_specs=[
            pl.BlockSpec((TOKEN_TILE, d), lambda i: (i, 0)),
            pl.BlockSpec((None, MOD_ROWS, N_MOD * d), lambda i: (layer, 0, 0)),
            pl.BlockSpec((TOKEN_TILE, half), p_map),
            pl.BlockSpec((TOKEN_TILE, half), s_map),
            pl.BlockSpec((TOKEN_TILE, half), p_map),
            pl.BlockSpec((TOKEN_TILE, half), s_map),
            pl.BlockSpec((None, 2 * half, d), lambda i: (w_idx, 0, 0)),
        ],
        out_specs=pl.BlockSpec((TOKEN_TILE, d), lambda i: (i, 0)),
        out_shape=jax.ShapeDtypeStruct((t, d), F32),
        compiler_params=_params(("parallel",), 48),
        name="outproj",
    )(x, mods, ya_p, ya_s, yb_p, yb_s, w)


def _group_roll(x, step):
    rows, w = x.shape
    x3 = x.reshape(rows // V7X_SUBLANES, V7X_SUBLANES, w)
    return pltpu.roll(x3, step, axis=1).reshape(rows, w)


def _lru_kernel(xa_ref, ga_ref, h0f_ref, h0b_ref, cw_ref, cb_ref, wa_ref, ba_ref, wx_ref, bx_ref,
                lam_ref, y_ref, lf_ref, lb_ref, af_scr, uf_scr, ab_scr, ub_scr):
    seq, w = xa_ref.shape
    groups = seq // V7X_SUBLANES
    xa = xa_ref[...]
    row = lax.broadcasted_iota(jnp.int32, (seq, w), 0)
    sub = row & (V7X_SUBLANES - 1)

    def tap(offset):
        if offset == 0:
            return xa
        shifted = pltpu.roll(xa, (-offset) % seq, axis=0)
        valid = row >= -offset if offset < 0 else row < seq - offset
        return jnp.where(valid, shifted, 0.0)

    xc = cb_ref[...]
    for j in range(CONV_W):
        xc = xc + tap(j - CONV_W // 2) * cw_ref[pl.ds(j, 1), :]
    xcb = xc.astype(BF16)

    for direction, (a_scr, u_scr) in enumerate(((af_scr, uf_scr), (ab_scr, ub_scr))):
        reverse = direction == 1
        r = jax.nn.sigmoid(jnp.dot(xcb, wa_ref[direction], preferred_element_type=F32) + ba_ref[direction])
        gi = jax.nn.sigmoid(jnp.dot(xcb, wx_ref[direction], preferred_element_type=F32) + bx_ref[direction])
        neg_lam = -lam_ref[direction]
        softplus = jnp.maximum(neg_lam, 0.0) + jnp.log1p(jnp.exp(-jnp.abs(neg_lam)))
        log_a = r * (-LRU_C * softplus)
        a = jnp.exp(log_a)
        m2 = (1.0 - a) * (1.0 + a)
        u = jnp.where(m2 > 0.0, m2 * lax.rsqrt(m2), 0.0) * (gi * xc)
        for step in (1, 2, 4):
            if reverse:
                a_nb = _group_roll(a, V7X_SUBLANES - step)
                u_nb = _group_roll(u, V7X_SUBLANES - step)
                valid = sub < V7X_SUBLANES - step
            else:
                a_nb = _group_roll(a, step)
                u_nb = _group_roll(u, step)
                valid = sub >= step
            u = a * jnp.where(valid, u_nb, 0.0) + u
            a = a * jnp.where(valid, a_nb, 1.0)
        a_scr[...] = a
        u_scr[...] = u

    def carry_step(g, carry):
        cf, cb = carry
        off_f = pl.multiple_of(g * V7X_SUBLANES, V7X_SUBLANES)
        off_b = pl.multiple_of((groups - 1 - g) * V7X_SUBLANES, V7X_SUBLANES)
        hf = af_scr[pl.ds(off_f, V7X_SUBLANES), :] * cf + uf_scr[pl.ds(off_f, V7X_SUBLANES), :]
        hb = ab_scr[pl.ds(off_b, V7X_SUBLANES), :] * cb + ub_scr[pl.ds(off_b, V7X_SUBLANES), :]
        uf_scr[pl.ds(off_f, V7X_SUBLANES), :] = hf
        ub_scr[pl.ds(off_b, V7X_SUBLANES), :] = hb
        return hf[V7X_SUBLANES - 1:V7X_SUBLANES], hb[0:1]

    lf_ref[...], lb_ref[...] = lax.fori_loop(0, groups, carry_step, (h0f_ref[...], h0b_ref[...]), unroll=4)
    y_ref[...] = ((uf_scr[...] + ub_scr[...]) * jax.nn.gelu(ga_ref[...])).astype(BF16)


def _lru(p_ab, h0f, h0b, prm, seq_len, n_seq, row_start):
    lru_w = h0f.shape[-1]
    cw = LRU_COLS
    ncb = lru_w // cw
    sb = row_start // seq_len
    vec = lambda: pl.BlockSpec((None, 1, cw), lambda s, c: (s, 0, c))
    par2 = lambda: pl.BlockSpec((None, 2, 1, cw), lambda s, c: (prm["idx"], 0, 0, c))
    gate = lambda: pl.BlockSpec((None, 2, None, cw, cw), lambda s, c: (prm["idx"], 0, c, 0, 0))
    return pl.pallas_call(
        _lru_kernel,
        grid=(n_seq, ncb),
        in_specs=[
            pl.BlockSpec((seq_len, cw), lambda s, c: (s + sb, c)),
            pl.BlockSpec((seq_len, cw), lambda s, c: (s + sb, c + ncb)),
            vec(), vec(),
            pl.BlockSpec((None, CONV_W, cw), lambda s, c: (prm["idx"], 0, c)),
            pl.BlockSpec((None, 1, cw), lambda s, c: (prm["idx"], 0, c)),
            gate(), par2(), gate(), par2(), par2(),
        ],
        out_specs=[
            pl.BlockSpec((seq_len, cw), lambda s, c: (s, c)),
            vec(), vec(),
        ],
        out_shape=[
            jax.ShapeDtypeStruct((n_seq * seq_len, lru_w), BF16),
            jax.ShapeDtypeStruct((n_seq, 1, lru_w), F32),
            jax.ShapeDtypeStruct((n_seq, 1, lru_w), F32),
        ],
        scratch_shapes=[pltpu.VMEM((seq_len, cw), F32)] * 4,
        compiler_params=_params(("parallel", "parallel"), 40),
        name="lru",
    )(p_ab, p_ab, h0f.reshape(n_seq, 1, lru_w), h0b.reshape(n_seq, 1, lru_w), prm["conv_w"],
      prm["conv_b"], prm["wa"], prm["ba"], prm["wx"], prm["bx"], prm["lam"])


def _block_diag_tiles(w, tile):
    n, two, nb, bw, _ = w.shape
    per = tile // bw
    w = w.reshape(n, two, nb // per, per, bw, bw)
    eye = jnp.eye(per, dtype=w.dtype)
    out = jnp.einsum("ndgpij,pq->ndgpiqj", w, eye)
    return out.reshape(n, two, nb // per, tile, tile)


def _softmax_attend(q, ks, vs, biases, scale):
    scores = []
    for k, b in zip(ks, biases):
        s = lax.dot_general(q, k, (((1,), (1,)), ((), ())), preferred_element_type=F32)
        if scale is not None:
            s = s * scale
        scores.append(s if b is None else s + b)
    m = scores[0].max(axis=-1, keepdims=True)
    for s in scores[1:]:
        m = jnp.maximum(m, s.max(axis=-1, keepdims=True))
    denom = None
    out = None
    for s, v in zip(scores, vs):
        e = jnp.exp(s - m)
        part = e.sum(axis=-1, keepdims=True)
        denom = part if denom is None else denom + part
        o = jnp.dot(e.astype(BF16), v, preferred_element_type=F32)
        out = o if out is None else out + o
    return out / denom


def _upper_half(shape):
    lane = lax.broadcasted_iota(jnp.int32, shape, len(shape) - 1)
    return lane % V7X_LANES >= V7X_LANES // 2


def _split_pair(q):
    up = _upper_half(q.shape)
    zero = jnp.zeros_like(q)
    return jnp.concatenate([jnp.where(up, zero, q), jnp.where(up, q, zero)], axis=0)


def _merge_pair(o):
    m = o.shape[0] // 2
    return jnp.where(_upper_half((m, o.shape[1])), o[m:], o[:m])


def _dup_head(block, use_high):
    swapped = pltpu.roll(block, V7X_LANES // 2, axis=1)
    keep = _upper_half(block.shape) == use_high
    return jnp.where(keep, block, swapped)


def _fold_scale(q, scale):
    return (q * scale, None) if _is_pow2(scale) else (q, scale)


def _na_ctx_kernel(q_ref, k_ref, v_ref, o_ref, *, scale):
    for pair in range(q_ref.shape[1] // V7X_LANES):
        sl = pl.ds(pair * V7X_LANES, V7X_LANES)
        q, sc = _fold_scale(q_ref[:, sl], scale)
        o = _softmax_attend(_split_pair(q.astype(BF16)), [k_ref[:, sl].astype(BF16)],
                            [v_ref[:, sl].astype(BF16)], [None], sc)
        o_ref[:, sl] = _merge_pair(o).astype(o_ref.dtype)


def _na_ctx(p_ab, n_seq, seq_len, q_blk, na_w, scale):
    blk = lambda col: pl.BlockSpec((seq_len, na_w), lambda b: (b, col))
    return pl.pallas_call(
        functools.partial(_na_ctx_kernel, scale=scale),
        grid=(n_seq,),
        in_specs=[blk(q_blk), blk(q_blk + 1), blk(q_blk + 2)],
        out_specs=pl.BlockSpec((seq_len, na_w), lambda b: (b, 0)),
        out_shape=jax.ShapeDtypeStruct((n_seq * seq_len, na_w), BF16),
        compiler_params=_params(("parallel",), 40),
        name="na_ctx",
    )(p_ab, p_ab, p_ab)


def _na_tile_index(seq_len):
    rows_n = seq_len // GRID_W
    kr = min(NA_KR, rows_n)
    rows_per_chunk = NA_Q_CHUNK // GRID_W
    n_chunks = rows_n // rows_per_chunk
    win_rows = NA_WIN // GRID_W
    table = []
    for r in range(rows_n):
        r_start = min(max(r - kr // 2, 0), rows_n - kr)
        base = 0 if r // rows_per_chunk < n_chunks // 2 else rows_n - win_rows
        table.append([k - r + NA_KR - 1 if r_start <= k < r_start + kr else 2 * NA_KR - 1
                      for k in range(base, base + win_rows)])
    return table


def _na_lat_kernel(q_ref, k_ref, v_ref, kc_ref, vc_ref, tiles_ref, o_ref, bias_ref, *, scale):
    lq = q_ref.shape[0]
    n_chunks = lq // NA_Q_CHUNK
    rows_per_chunk = NA_Q_CHUNK // GRID_W

    @pl.when(pl.program_id(1) == 0)
    def _():
        low = lax.broadcasted_iota(jnp.int32, (GRID_W, V7X_LANES), 1) < GRID_W
        for half in (0, 1):
            for r, slots in enumerate(_na_tile_index(lq)):
                c, rr = divmod(r, rows_per_chunk)
                rows = pl.ds(half * NA_Q_CHUNK + rr * GRID_W, GRID_W)
                for s in range(0, len(slots), 2):
                    blk = jnp.where(low, tiles_ref[half, slots[s]], tiles_ref[half, slots[s + 1]])
                    bias_ref[c, rows, pl.ds(s * GRID_W, V7X_LANES)] = blk

    kcb = kc_ref[...].astype(BF16)
    vcb = vc_ref[...].astype(BF16)
    for c in range(n_chunks):
        win = 0 if c < n_chunks // 2 else lq - NA_WIN
        kb = k_ref[pl.ds(win, NA_WIN), :].astype(BF16)
        vb = v_ref[pl.ds(win, NA_WIN), :].astype(BF16)
        rows = pl.ds(c * NA_Q_CHUNK, NA_Q_CHUNK)
        q, sc = _fold_scale(q_ref[rows, :], scale)
        o = _softmax_attend(_split_pair(q.astype(BF16)), [kb, kcb], [vb, vcb], [bias_ref[c], None], sc)
        o_ref[rows, :] = _merge_pair(o).astype(o_ref.dtype)


def _na_lat(p_ab, cache_k, cache_v, tiles, idx, n_seq, seq_len, row_start, q_col, n_pairs, scale):
    sb = row_start // seq_len
    past = cache_k.shape[2]
    blk = lambda col: pl.BlockSpec((seq_len, V7X_LANES), lambda h, b: (b + sb, col + h))
    cblk = lambda: pl.BlockSpec((None, None, past, V7X_LANES), lambda h, b: (b, idx, 0, h))
    n_chunks = seq_len // NA_Q_CHUNK
    return pl.pallas_call(
        functools.partial(_na_lat_kernel, scale=scale),
        grid=(n_pairs, n_seq),
        in_specs=[blk(q_col), blk(q_col + n_pairs), blk(q_col + 2 * n_pairs), cblk(), cblk(),
                  pl.BlockSpec((None, 2, 2 * NA_KR, GRID_W, V7X_LANES), lambda h, b: (idx, h, 0, 0, 0))],
        out_specs=pl.BlockSpec((seq_len, V7X_LANES), lambda h, b: (b, h)),
        out_shape=jax.ShapeDtypeStruct((n_seq * seq_len, n_pairs * V7X_LANES), BF16),
        scratch_shapes=[pltpu.VMEM((n_chunks, 2 * NA_Q_CHUNK, NA_WIN), F32)],
        compiler_params=_params(("parallel", "arbitrary"), 56),
        name="na_lat",
    )(p_ab, p_ab, p_ab, cache_k, cache_v, tiles)


def _na_bias_tiles(tab):
    lead = tab.shape[:-1]
    edge = GRID_W - NA_KC
    vec = jnp.pad(tab, [(0, 0)] * len(lead) + [(edge, edge + 1)], mode="edge")
    skew = jnp.broadcast_to(vec[..., None, :], lead + (GRID_W, 2 * GRID_W))
    skew = skew.reshape(lead + (2 * GRID_W * GRID_W,))[..., :GRID_W * (2 * GRID_W - 1)]
    toep = skew.reshape(lead + (GRID_W, 2 * GRID_W - 1))[..., GRID_W - 1:]
    cols = np.arange(GRID_W)
    c_start = np.clip(cols - NA_KC // 2, 0, GRID_W - NA_KC)
    col_ok = (cols[None, :] >= c_start[:, None]) & (cols[None, :] < c_start[:, None] + NA_KC)
    tiles = jnp.where(col_ok, toep, NEG_BIAS)
    masked = jnp.full(lead[:-1] + (1, GRID_W, GRID_W), NEG_BIAS, F32)
    tiles = jnp.concatenate([tiles, masked], axis=-3)
    return jnp.concatenate([tiles, tiles], axis=-1)


def _split_dot(v, m):
    hi = v.astype(BF16)
    lo = (v - hi.astype(F32)).astype(BF16)
    return jnp.dot(hi, m, preferred_element_type=F32) + jnp.dot(lo, m, preferred_element_type=F32)


def _head_rms(x, gather, spread, gain, head_dim):
    inv = lax.rsqrt(_split_dot(x * x, gather) * (1.0 / head_dim) + EPS)
    return x * _split_dot(inv, spread) * gain


def _rope(x, cos, sin_lo, sin_hi):
    w = x.shape[1]
    reps = w // V7X_LANES
    tile = lambda t: t if reps == 1 else jnp.concatenate([t] * reps, axis=1)
    quarter = 16
    return (x * tile(cos) + pltpu.roll(x, w - quarter, axis=1) * tile(sin_lo)
            + pltpu.roll(x, quarter, axis=1) * tile(sin_hi))


def _cd_post_kernel(p_ref, gat_ref, spr_ref, qg_ref, kg_ref, mqg_ref, mkg_ref, wuq_ref, wukv_ref, *rest,
                    rope, head_dim, q_w, kv_w, rank, q_scale):
    if rope:
        cos_ref, slo_ref, shi_ref = rest[:3]
        rest = rest[3:]
        tabs = (cos_ref[...], slo_ref[...], shi_ref[...])
    q_ref, k_ref, qn_ref, qr_ref, ckv_ref, kn_ref, vm_ref, kr_ref = rest
    c0 = 0
    qc = p_ref[:, pl.ds(c0, q_w)]
    c0 += q_w
    kc = p_ref[:, pl.ds(c0, kv_w)]
    c0 += 2 * kv_w
    qa = p_ref[:, pl.ds(c0, rank)]
    c0 += rank
    ckv = p_ref[:, pl.ds(c0, rank)]
    c0 += rank
    kr = p_ref[:, pl.ds(c0, V7X_LANES)]

    qc = _head_rms(qc, gat_ref[...], spr_ref[...], qg_ref[...], head_dim)
    kc = _head_rms(kc, gat_ref[pl.ds(0, kv_w), :], spr_ref[:, pl.ds(0, kv_w)], kg_ref[...], head_dim)
    qd = jnp.dot(_rms(qa, mqg_ref[...]).astype(BF16), wuq_ref[...], preferred_element_type=F32)
    n_nope = qn_ref.shape[1]
    qn = qd[:, :n_nope]
    qr = qd[:, n_nope:]
    ckv = _rms(ckv, mkg_ref[...])
    if rope:
        qc = _rope(qc, *tabs)
        kc = _rope(kc, *tabs)
        qr = _rope(qr, *tabs)
        kr = _rope(kr, *tabs)
    q_ref[...] = (qc * q_scale).astype(BF16)
    k_ref[...] = kc
    qn_ref[...] = qn.astype(BF16)
    qr_ref[...] = qr.astype(BF16)
    ckv_ref[...] = ckv
    kv = jnp.dot(ckv.astype(BF16), wukv_ref[...], preferred_element_type=F32)
    kn_ref[...] = kv[:, :n_nope].astype(BF16)
    vm_ref[...] = kv[:, n_nope:].astype(BF16)
    kr_ref[...] = kr


def _cd_post(p_cd, prm, row_start, rows, rope_tabs, dims, q_scale):
    q_w, kv_w, rank, head_dim, n_nope, n_rope, n_v = dims
    tm = 256
    off = row_start // tm
    n_in = p_cd.shape[1]
    idx = prm["idx"]
    rope = rope_tabs is not None
    const = lambda shape: pl.BlockSpec(shape, lambda i: (0,) * len(shape))
    layer = lambda shape: pl.BlockSpec((None,) + shape, lambda i: (idx,) + (0,) * len(shape))
    in_specs = [
        pl.BlockSpec((tm, n_in), lambda i: (i + off, 0)),
        const((q_w, V7X_LANES)), const((V7X_LANES, q_w)),
        layer((1, q_w)), layer((1, kv_w)), layer((1, rank)), layer((1, rank)),
        layer((rank, n_nope + n_rope)), layer((rank, n_nope + n_v)),
    ]
    args = [p_cd, prm["gather"], prm["spread"], prm["q_gain"], prm["k_gain"], prm["mla_q_gain"],
            prm["mla_kv_gain"], prm["w_uq"], prm["w_ukv"]]
    if rope:
        seq_tiles = rope_tabs[0].shape[0] // tm
        in_specs += [pl.BlockSpec((tm, V7X_LANES), lambda i: (i % seq_tiles, 0))] * 3
        args += list(rope_tabs)
    widths = [(q_w, BF16), (kv_w, F32), (n_nope, BF16), (n_rope, BF16), (rank, F32), (n_nope, BF16),
              (n_v, BF16), (V7X_LANES, F32)]
    kern = functools.partial(_cd_post_kernel, rope=rope, head_dim=head_dim, q_w=q_w, kv_w=kv_w, rank=rank,
                             q_scale=q_scale)
    return pl.pallas_call(
        kern,
        grid=(rows // tm,),
        in_specs=in_specs,
        out_specs=[pl.BlockSpec((tm, w), lambda i: (i, 0)) for w, _ in widths],
        out_shape=[jax.ShapeDtypeStruct((rows, w), dt) for w, dt in widths],
        compiler_params=_params(("parallel",), 48),
        name="cd_post",
    )(*args)


def _rope_tables(seq_len, head_dim):
    half = head_dim // 2
    nf = half // 2
    t = np.arange(seq_len)
    inv_freq = (1.0 / (ROPE_THETA ** (np.arange(nf, dtype=np.float32) / nf))).astype(np.float32)
    zeros = np.zeros((seq_len, nf), np.float32)
    cos, slo, shi = [], [], []
    for pos in (t // GRID_W, t % GRID_W):
        ang = pos.astype(np.float32)[:, None] * inv_freq[None, :]
        c, s = np.cos(ang).astype(np.float32), np.sin(ang).astype(np.float32)
        cos += [c, c]
        slo += [-s, zeros]
        shi += [zeros, s]
    reps = V7X_LANES // head_dim
    return tuple(jnp.asarray(np.tile(np.concatenate(x, axis=1), (1, reps))) for x in (cos, slo, shi))


def _ckv_up_kernel(c_ref, w_ref, kn_ref, vm_ref):
    kv = jnp.dot(c_ref[...].astype(BF16), w_ref[...], preferred_element_type=F32)
    n = kn_ref.shape[1]
    kn_ref[...] = kv[:, :n].astype(BF16)
    vm_ref[...] = kv[:, n:].astype(BF16)


def _ckv_up(ckv, w_ukv, idx, n_nope):
    n_seq, _, tm, rank = ckv.shape
    rows = n_seq * tm
    n = w_ukv.shape[2]
    return pl.pallas_call(
        _ckv_up_kernel,
        grid=(rows // tm,),
        in_specs=[pl.BlockSpec((None, None, tm, rank), lambda i: (i, idx, 0, 0)),
                  pl.BlockSpec((None, rank, n), lambda i: (idx, 0, 0))],
        out_specs=[pl.BlockSpec((tm, n_nope), lambda i: (i, 0)), pl.BlockSpec((tm, n - n_nope), lambda i: (i, 0))],
        out_shape=[jax.ShapeDtypeStruct((rows, n_nope), BF16), jax.ShapeDtypeStruct((rows, n - n_nope), BF16)],
        compiler_params=_params(("parallel",), 32),
        name="ckv_up",
    )(ckv, w_ukv)


def _gqa_kernel(q_ref, k_ref, v_ref, *rest, scale, has_cache, group_w):
    if has_cache:
        kc_ref, vc_ref, o_ref = rest
    else:
        (o_ref,) = rest
    lq = q_ref.shape[0]
    n_local = q_ref.shape[1] // group_w
    n_pairs = group_w // V7X_LANES
    for hl in range(n_local):
        if n_local == 1:
            high = (pl.program_id(1) % 2) == 1
            pick = lambda ref: ref[...]
        else:
            high = hl % 2 == 1
            pick = lambda ref, hl=hl: ref[:, pl.ds((hl // 2) * V7X_LANES, V7X_LANES)]
        ks = [_dup_head(pick(k_ref), high).astype(BF16)]
        vs = [_dup_head(pick(v_ref), high).astype(BF16)]
        if has_cache:
            ks.append(_dup_head(pick(kc_ref), high).astype(BF16))
            vs.append(_dup_head(pick(vc_ref), high).astype(BF16))
        for c in range(lq // Q_CHUNK):
            rows = pl.ds(c * Q_CHUNK, Q_CHUNK)
            cols = [pl.ds(hl * group_w + p * V7X_LANES, V7X_LANES) for p in range(n_pairs)]
            q = jnp.concatenate([_split_pair(q_ref[rows, cl]) for cl in cols], axis=0)
            o = _softmax_attend(q, ks, vs, [None] * len(ks), scale)
            for p, cl in enumerate(cols):
                o_ref[rows, cl] = _merge_pair(o[2 * p * Q_CHUNK:2 * (p + 1) * Q_CHUNK]).astype(o_ref.dtype)


def _gqa(q, k, p_cd, v_col, cache_k, cache_v, idx, n_seq, seq_len, row_start, n_kv, heads_per_step, scale):
    sb = row_start // seq_len
    group_w = q.shape[1] // n_kv
    has_cache = cache_k is not None
    if heads_per_step == 1:
        kv_w, kv_blk, v_blk = V7X_LANES, (lambda h: h // 2), (lambda h: v_col + h // 2)
    else:
        kv_w = k.shape[1]
        kv_blk, v_blk = (lambda h: 0), (lambda h: v_col * V7X_LANES // kv_w)
    in_specs = [
        pl.BlockSpec((seq_len, group_w * heads_per_step), lambda b, h: (b, h)),
        pl.BlockSpec((seq_len, kv_w), lambda b, h: (b, kv_blk(h))),
        pl.BlockSpec((seq_len, kv_w), lambda b, h: (b + sb, v_blk(h))),
    ]
    args = [q, k, p_cd]
    if has_cache:
        past = cache_k.shape[2]
        in_specs += [pl.BlockSpec((None, None, past, kv_w), lambda b, h: (b, idx, 0, kv_blk(h)))] * 2
        args += [cache_k, cache_v]
    return pl.pallas_call(
        functools.partial(_gqa_kernel, scale=scale, has_cache=has_cache, group_w=group_w),
        grid=(n_seq, n_kv // heads_per_step),
        in_specs=in_specs,
        out_specs=pl.BlockSpec((seq_len, group_w * heads_per_step), lambda b, h: (b, h)),
        out_shape=jax.ShapeDtypeStruct(q.shape, BF16),
        compiler_params=_params(("parallel", "parallel"), 56),
        name="gqa",
    )(*args)


def _mla_kernel(qn_ref, qr_ref, kn_ref, kr_ref, v_ref, *rest, scale, has_cache):
    if has_cache:
        knc_ref, krc_ref, vc_ref, o_ref = rest
        krc = krc_ref[...].astype(BF16)
    else:
        (o_ref,) = rest
    lq = qn_ref.shape[0]
    qc = min(2 * Q_CHUNK, lq)
    n_local = qn_ref.shape[1] // V7X_LANES
    kr = kr_ref[...].astype(BF16)
    for hl in range(n_local):
        sl = pl.ds(hl * V7X_LANES, V7X_LANES)
        if n_local == 1:
            high = (pl.program_id(1) % 2) == 1
            pair = pl.ds(0, V7X_LANES)
        else:
            high = hl % 2 == 1
            pair = pl.ds((hl // 2) * V7X_LANES, V7X_LANES)
        ks = [jnp.concatenate([kn_ref[:, sl], kr], axis=1)]
        vs = [v_ref[:, sl]]
        if has_cache:
            ks.append(jnp.concatenate([knc_ref[:, sl], krc], axis=1))
            vs.append(vc_ref[:, sl])
        for c in range(lq // qc):
            rows = pl.ds(c * qc, qc)
            qr = qr_ref[rows, pair]
            qr = jnp.where(_upper_half(qr.shape) == high, qr, jnp.zeros_like(qr))
            q = jnp.concatenate([qn_ref[rows, sl], qr], axis=1)
            o_ref[rows, sl] = _softmax_attend(q, ks, vs, [None] * len(ks), scale).astype(o_ref.dtype)


def _mla(qn, qr, kn, kr2, vm, cache, n_seq, seq_len, heads_per_step, scale):
    has_cache = cache is not None
    n_heads = vm.shape[1] // V7X_LANES
    hw = heads_per_step * V7X_LANES
    head = lambda rows: pl.BlockSpec((rows, hw), lambda b, h: (b, h))
    if heads_per_step == 1:
        rope_q = pl.BlockSpec((seq_len, V7X_LANES), lambda b, h: (b, h // 2))
    else:
        rope_q = pl.BlockSpec((seq_len, hw // 2), lambda b, h: (b, h))
    in_specs = [head(seq_len), rope_q, head(seq_len),
                pl.BlockSpec((seq_len, V7X_LANES), lambda b, h: (b, 0)),
                head(seq_len)]
    args = [qn, qr, kn, kr2, vm]
    if has_cache:
        knc, krc, vmc, idx = cache
        past = krc.shape[2]
        in_specs += [head(past),
                     pl.BlockSpec((None, None, past, V7X_LANES), lambda b, h: (b, idx, 0, 0)),
                     head(past)]
        args += [knc, krc, vmc]
    return pl.pallas_call(
        functools.partial(_mla_kernel, scale=scale, has_cache=has_cache),
        grid=(n_seq, n_heads // heads_per_step),
        in_specs=in_specs,
        out_specs=head(seq_len),
        out_shape=jax.ShapeDtypeStruct(vm.shape, BF16),
        compiler_params=_params(("parallel", "parallel"), 48),
        name="mla",
    )(*args)


def kernel(x_prompt, x_sample, state_lru_fwd, state_lru_bwd, cache_na_k, cache_na_v, cache_gqa_k, cache_gqa_v, cache_mla_ckv, cache_mla_krope, c, c_ctx, w_mod, b_mod, norm_gain, w_ffn_in, w_ffn_out, w_in_ab, conv_w, conv_b, lru_wa, lru_ba, lru_wx, lru_bx, lru_lambda, na_bias, w_out_ab, w_in_cd, gqa_q_gain, gqa_k_gain, mla_q_gain, mla_kv_gain, mla_w_uq, mla_w_uk, mla_w_uv, w_out_cd, final_gain):
    batch, seq, d = x_prompt.shape
    dec_batch, dec_seq, _ = x_sample.shape
    depth = w_mod.shape[0]
    n_even, n_odd = w_in_ab.shape[0], w_in_cd.shape[0]
    n_p = batch * seq
    n_s = dec_batch * dec_seq
    past = cache_na_k.shape[2]
    lru_w = state_lru_fwd.shape[-1]
    na_heads, na_dh = cache_na_k.shape[3], cache_na_k.shape[4]
    na_w = na_heads * na_dh
    gqa_kv, gqa_dh = cache_gqa_k.shape[3], cache_gqa_k.shape[4]
    kv_w = gqa_kv * gqa_dh
    rank = cache_mla_ckv.shape[-1]
    rope_w = cache_mla_krope.shape[-1]
    q_w = w_in_cd.shape[2] - 2 * kv_w - 2 * rank - rope_w
    mla_heads = mla_w_uk.shape[2] // MLA_NOPE
    n_nope = mla_heads * MLA_NOPE
    n_rope = mla_heads * rope_w
    n_v = mla_heads * MLA_V
    mla_qk = MLA_NOPE + rope_w
    assert mla_q_gain.shape[-1] == rank and 2 * rope_w == V7X_LANES
    assert n_p % dec_seq == 0 and dec_seq % TOKEN_TILE == 0 and seq % Q_CHUNK == 0
    assert dec_seq // GRID_W == 2 * NA_KR and na_dh == GRID_W and gqa_dh == GRID_W
    assert w_in_ab.shape[2] == 2 * lru_w + 3 * na_w and lru_w == na_w
    assert (q_w // gqa_kv) % V7X_LANES == 0 and q_w % kv_w == 0

    x = (x_prompt.reshape(n_p, d), x_sample.reshape(n_s, d))
    cond = jnp.concatenate([c_ctx[None, :], c, jnp.zeros((MOD_ROWS - 1 - dec_batch, d), F32)], axis=0)
    mods = _modulation(cond, w_mod, b_mod)
    gains = norm_gain.reshape(depth * 3, 1, d)

    ffn_w = (w_ffn_in[0, 0].astype(BF16), w_ffn_out[0, 0].astype(BF16))

    def ffn_step(x, ffn_w, layer, which):
        last = layer == depth - 1 and which == 1
        nxt = None if last else (w_ffn_in, w_ffn_out, layer + which, 1 - which)
        out = _ffn(x, mods, gains, ffn_w[0], ffn_w[1], layer, which, nxt, final_gain, n_p, dec_seq)
        return (tuple(out), None) if last else (out[0], tuple(out[1:]))

    w_in_ab_bf = w_in_ab.astype(BF16)
    w_out_ab_bf = w_out_ab.astype(BF16)
    w_in_cd_bf = jnp.concatenate([w_in_cd, w_in_cd[:, :, -rope_w:]], axis=2).astype(BF16)
    w_out_cd_bf = w_out_cd.astype(BF16)

    lru_prm = {
        "conv_w": conv_w, "conv_b": conv_b.reshape(n_even, 1, lru_w),
        "wa": _block_diag_tiles(lru_wa, LRU_COLS).astype(BF16),
        "wx": _block_diag_tiles(lru_wx, LRU_COLS).astype(BF16),
        "ba": lru_ba.reshape(n_even, 2, 1, lru_w), "bx": lru_bx.reshape(n_even, 2, 1, lru_w),
        "lam": lru_lambda.reshape(n_even, 2, 1, lru_w),
    }
    zeros_state = jnp.zeros((batch, lru_w), F32)
    na_k_ctx = cache_na_k.reshape(dec_batch, n_even, past, na_w)
    na_v_ctx = cache_na_v.reshape(dec_batch, n_even, past, na_w)
    na_tiles = _na_bias_tiles(na_bias)

    w_uq = mla_w_uq.reshape(n_odd, rank, mla_heads, mla_qk)
    w_uq = jnp.concatenate([w_uq[..., :MLA_NOPE].reshape(n_odd, rank, n_nope),
                            w_uq[..., MLA_NOPE:].reshape(n_odd, rank, n_rope)], axis=2).astype(BF16)
    gqa_scale = gqa_dh ** -0.5
    q_scale = gqa_scale if _is_pow2(gqa_scale) else 1.0
    head_of_lane = np.arange(q_w) // gqa_dh
    gather = (head_of_lane[:, None] == np.arange(V7X_LANES)[None, :]).astype(np.float32)
    assert q_w // gqa_dh <= V7X_LANES
    cd_prm = {
        "gather": jnp.asarray(gather, BF16), "spread": jnp.asarray(gather.T, BF16),
        "q_gain": jnp.tile(gqa_q_gain, (1, q_w // gqa_dh)).reshape(n_odd, 1, q_w),
        "k_gain": jnp.tile(gqa_k_gain, (1, gqa_kv)).reshape(n_odd, 1, kv_w),
        "mla_q_gain": mla_q_gain.reshape(n_odd, 1, rank),
        "mla_kv_gain": mla_kv_gain.reshape(n_odd, 1, rank),
        "w_uq": w_uq,
        "w_ukv": jnp.concatenate([mla_w_uk, mla_w_uv], axis=2).astype(BF16),
    }
    rope_tabs = _rope_tables(dec_seq, gqa_dh)
    gqa_k_ctx = cache_gqa_k.reshape(dec_batch, n_odd, past, kv_w)
    gqa_v_ctx = cache_gqa_v.reshape(dec_batch, n_odd, past, kv_w)
    krope_ctx = jnp.concatenate([cache_mla_krope, cache_mla_krope], axis=-1)

    st_f, st_b, na_k, na_v, gq_k, gq_v, ml_c, ml_r = [], [], [], [], [], [], [], []
    for layer in range(depth):
        jdx = layer // 2
        x, ffn_w = ffn_step(x, ffn_w, layer, 0)
        if layer % 2 == 0:
            p_ab = _inproj(x, mods, gains, w_in_ab_bf, layer, jdx, 2, n_p, dec_seq)
            prm = dict(lru_prm, idx=jdx)
            ya_p, lf, lb = _lru(p_ab, zeros_state, zeros_state, prm, seq, batch, 0)
            ya_s, _, _ = _lru(p_ab, state_lru_fwd[:, jdx], state_lru_bwd[:, jdx], prm, dec_seq, dec_batch, n_p)
            scale = na_dh ** -0.5
            yb_p = _na_ctx(p_ab, batch, seq, 2 * lru_w // na_w, na_w, scale)
            yb_s = _na_lat(p_ab, na_k_ctx, na_v_ctx, na_tiles, jdx, dec_batch, dec_seq, n_p,
                           2 * lru_w // V7X_LANES, na_w // V7X_LANES, scale)
            x = _outproj(x, mods, layer, ya_p, ya_s, yb_p, yb_s, w_out_ab_bf, jdx, n_p, dec_seq)
            st_f.append(lf.reshape(batch, lru_w))
            st_b.append(lb.reshape(batch, lru_w))
            na_k.append(p_ab[:n_p, 2 * lru_w + na_w:2 * lru_w + 2 * na_w].reshape(batch, seq, na_heads, na_dh))
            na_v.append(p_ab[:n_p, 2 * lru_w + 2 * na_w:].reshape(batch, seq, na_heads, na_dh))
        else:
            p_cd = _inproj(x, mods, gains, w_in_cd_bf, layer, jdx, 1, n_p, dec_seq)
            prm = dict(cd_prm, idx=jdx)
            dims = (q_w, kv_w, rank, gqa_dh, n_nope, n_rope, n_v)
            qp, kp, qnp_, qrp, ckvp, knp_, vmp, krp = _cd_post(p_cd, prm, 0, n_p, None, dims, q_scale)
            qs, ks_, qns, qrs, _, kns, vms, krs = _cd_post(p_cd, prm, n_p, n_s, rope_tabs, dims, q_scale)
            v_col = (q_w + kv_w) // V7X_LANES
            att_scale = None if q_scale != 1.0 else gqa_scale
            yc_p = _gqa(qp, kp, p_cd, v_col, None, None, jdx, batch, seq, 0, gqa_kv, gqa_kv, att_scale)
            yc_s = _gqa(qs, ks_, p_cd, v_col, gqa_k_ctx, gqa_v_ctx, jdx, dec_batch, dec_seq, n_p,
                        gqa_kv, 1, att_scale)
            knc, vmc = _ckv_up(cache_mla_ckv, cd_prm["w_ukv"], jdx, n_nope)
            yd_p = _mla(qnp_, qrp, knp_, krp, vmp, None, batch, seq, mla_heads, mla_qk ** -0.5)
            yd_s = _mla(qns, qrs, kns, krs, vms, (knc, krope_ctx, vmc, jdx), dec_batch, dec_seq, 2,
                        mla_qk ** -0.5)
            x = _outproj(x, mods, layer, yc_p, yc_s, yd_p, yd_s, w_out_cd_bf, jdx, n_p, dec_seq)
            gq_k.append(kp.reshape(batch, seq, gqa_kv, gqa_dh))
            gq_v.append(p_cd[:n_p, q_w + kv_w:q_w + 2 * kv_w].reshape(batch, seq, gqa_kv, gqa_dh))
            ml_c.append(ckvp.reshape(batch, seq, rank))
            ml_r.append(krp[:, :rope_w].reshape(batch, seq, rope_w))
        x, ffn_w = ffn_step(x, ffn_w, layer, 1)

    y_prompt = x[0].reshape(batch, seq, d)
    y_sample = x[1].reshape(dec_batch, dec_seq, d)
    stack = lambda xs: jnp.stack(xs, axis=1)
    return (y_prompt, y_sample, stack(st_f), stack(st_b), stack(na_k), stack(na_v), stack(gq_k), stack(gq_v),
            stack(ml_c), stack(ml_r))
```
